```python
import math
import jax, jax.numpy as jnp
from jax import lax
import numpy as np

D_MODEL = 1024
BATCH = 2
SEQ = 16384
DEPTH = 2

RET_HEADS = 4
RET_QK_DIM = 128
RET_V_DIM = 128
RET_CHUNK = 128
ROPE_BASE = 10000.0
DIL_HEADS = 8
DIL_HEAD_DIM = 64
DIL_PATTERNS = ((128, 1), (512, 4), (2048, 16))
DIL_BLOCK = 128
DIFF_HEADS = 8
DIFF_HEAD_DIM = 64
BLOCK_Q = 128
REL_BUCKETS = 32
REL_MAX_DISTANCE = 2048
REL_BIAS_HEADS = 8
D_FF = 2816
CONV_WIDTH = 3
EPS = 1e-6

A_QK = RET_HEADS * RET_QK_DIM
A_V = RET_HEADS * RET_V_DIM
B_W = DIL_HEADS * DIL_HEAD_DIM
MIX0_WIDTHS = (A_QK, A_QK, A_V, A_V, B_W, B_W, B_W)
MIX0_IN = sum(MIX0_WIDTHS)
MIX0_OUT = A_V + B_W
C_QK = DIFF_HEADS * 2 * DIFF_HEAD_DIM
C_V = DIFF_HEADS * 2 * DIFF_HEAD_DIM
MIX1_WIDTHS = (C_QK, C_QK, C_V)
MIX1_IN = sum(MIX1_WIDTHS)
MIX1_OUT = C_V
N_EVEN = (DEPTH + 1) // 2
N_ODD = DEPTH // 2

kernel_name = 'hybrid_retention_dilated_diffattn_trunk'

F32 = jnp.float32


def rmsnorm(x, g):
    xf = x.astype(F32)
    y = xf * lax.rsqrt(jnp.mean(xf * xf, axis=-1, keepdims=True) + EPS) * g.astype(F32)
    return y.astype(x.dtype)


def split_cols(t, widths):
    outs, off = [], 0
    for w in widths:
        outs.append(t[..., off:off + w])
        off += w
    return outs


def t5_bucket(dist):
    max_exact = REL_BUCKETS // 2
    d = jnp.maximum(dist.astype(F32), 1.0)
    large = max_exact + (jnp.log(d / max_exact) / math.log(REL_MAX_DISTANCE / max_exact)
                         * (REL_BUCKETS - max_exact))
    large = jnp.clip(large.astype(jnp.int32), max_exact, REL_BUCKETS - 1)
    return jnp.where(dist < max_exact, dist, large)


def rotary(t):
    s, e = t.shape[1], t.shape[-1]
    inv = ROPE_BASE ** (-jnp.arange(0, e, 2, dtype=F32) / e)
    ang = jnp.arange(s, dtype=F32)[:, None] * inv[None, :]
    cos = jnp.cos(ang)[None, :, None, :]
    sin = jnp.sin(ang)[None, :, None, :]
    t1, t2 = t[..., :e // 2].astype(F32), t[..., e // 2:].astype(F32)
    return jnp.concatenate([t1 * cos - t2 * sin, t1 * sin + t2 * cos], axis=-1)


def retention(q, k, v):
    b, s, h, dk = q.shape
    dv = v.shape[-1]
    c = RET_CHUNK
    n = s // c
    log_g = jnp.log1p(-jnp.exp2(-5.0 - jnp.arange(h, dtype=F32)))
    qc = q.astype(F32).reshape(b, n, c, h, dk)
    kc = k.astype(F32).reshape(b, n, c, h, dk) * dk ** -0.5
    vc = v.astype(F32).reshape(b, n, c, h, dv)
    pos = jnp.arange(c, dtype=F32)
    rel = pos[:, None] - pos[None, :]
    causal = rel >= 0
    decay = jnp.where(causal, jnp.exp(jnp.where(causal, rel, 0.0)[None] * log_g[:, None, None]), 0.0)
    intra = jnp.einsum('bnihd,bnjhd->bnhij', qc, kc) * decay
    intra = jnp.einsum('bnhij,bnjhe->bnihe', intra, vc)
    zeta = jnp.exp((c - 1 - pos)[None, :] * log_g[:, None])
    xi = jnp.exp((pos + 1)[None, :] * log_g[:, None])
    kv = jnp.einsum('bnjhd,hj,bnjhe->nbhde', kc, zeta, vc)
    g_chunk = jnp.exp(c * log_g)[None, :, None, None]

    def step(state, kv_n):
        return g_chunk * state + kv_n, state

    _, prev = lax.scan(step, jnp.zeros((b, h, dk, dv), F32), kv)
    cross = jnp.einsum('bnihd,nbhde,hi->bnihe', qc, prev, xi)
    return (intra + cross).reshape(b, s, h, dv)


def head_groupnorm(y, g):
    b, s, h, e = y.shape
    mu = jnp.mean(y, axis=-1, keepdims=True)
    var = jnp.mean((y - mu) ** 2, axis=-1, keepdims=True)
    y = (y - mu) * lax.rsqrt(var + EPS)
    return y.reshape(b, s, h * e) * g.astype(F32)


def dilated_attention(q, k, v, rel_bias):
    b, s, h, e = q.shape
    blk = DIL_BLOCK
    outs, lses = [], []
    for window, dil in DIL_PATTERNS:
        span = window // dil
        unit = dil * blk
        s_pad = -(-s // unit) * unit
        nb = s_pad // unit
        pad = ((0, 0), (0, s_pad - s), (0, 0), (0, 0))

        def split(t):
            return jnp.pad(t, pad).reshape(b, nb, blk, dil, h, e)

        def with_prev(t):
            prev = jnp.pad(t[:, :-1], ((0, 0), (1, 0), (0, 0), (0, 0), (0, 0), (0, 0)))
            return jnp.concatenate([prev, t], axis=2)

        qb = split(q)
        kw = with_prev(split(k))
        vw = with_prev(split(v))
        i = jnp.arange(blk)[:, None]
        j = jnp.arange(2 * blk)[None, :]
        r = i + blk - j
        band = (r >= 0) & (r <= span)
        bias = rel_bias[t5_bucket(jnp.maximum(r, 0) * dil)].astype(F32)
        bias = jnp.transpose(bias, (2, 0, 1))
        valid = band[None] & ((jnp.arange(nb)[:, None, None] > 0) | (j >= blk)[None])
        logits = jnp.einsum('bnirhe,bnjrhe->bnrhij', qb, kw).astype(F32) * e ** -0.5 + bias[None, None, None]
        logits = jnp.where(valid[None, :, None, None], logits, -jnp.inf)
        lse = jax.nn.logsumexp(logits, axis=-1)
        p = jnp.exp(logits - lse[..., None])
        o = jnp.einsum('bnrhij,bnjrhe->bnirhe', p.astype(v.dtype), vw)
        outs.append(o.reshape(b, s_pad, h, e)[:, :s].astype(F32))
        lses.append(jnp.transpose(lse, (0, 1, 4, 2, 3)).reshape(b, s_pad, h)[:, :s])
    w = jax.nn.softmax(jnp.stack(lses, axis=0), axis=0)
    out = jnp.sum(w[..., None] * jnp.stack(outs, axis=0), axis=0)
    return out.astype(q.dtype)


def diff_attention(q1, q2, k1, k2, v, rel_bias, lam):
    b, s, h, e = q1.shape
    nq = s // BLOCK_Q
    scale = e ** -0.5
    dist_bias = rel_bias[t5_bucket(jnp.arange(s))].astype(F32)
    kpos = jnp.arange(s)

    def block(args):
        q1b, q2b, start = args
        rel = start + jnp.arange(BLOCK_Q)[:, None] - kpos[None, :]
        causal = rel >= 0
        bias = jnp.transpose(dist_bias[jnp.maximum(rel, 0)], (2, 0, 1))

        def attn(qb, kk):
            lg = jnp.einsum('bqhe,bkhe->bhqk', qb, kk).astype(F32) * scale + bias[None]
            lg = jnp.where(causal[None, None], lg, -jnp.inf)
            return jax.nn.softmax(lg, axis=-1)

        a = attn(q1b, k1) - lam * attn(q2b, k2)
        return jnp.einsum('bhqk,bkhe->bqhe', a.astype(v.dtype), v)

    def blocks(t):
        return jnp.transpose(t.reshape(b, nq, BLOCK_Q, h, e), (1, 0, 2, 3, 4))

    out = lax.map(block, (blocks(q1), blocks(q2), jnp.arange(nq) * BLOCK_Q))
    return jnp.transpose(out, (1, 0, 2, 3, 4)).reshape(b, s, h, v.shape[-1])


def conv_glu_ffn(x, w_in, conv_w, conv_b, w_out):
    u = x @ w_in
    ch = u.shape[-1]
    u = lax.conv_general_dilated(u, conv_w[:, None, :].astype(u.dtype), window_strides=(1,),
                                 padding=[(CONV_WIDTH - 1, 0)],
                                 dimension_numbers=('NWC', 'WIO', 'NWC'),
                                 feature_group_count=ch) + conv_b
    gate, up = u[..., :D_FF], u[..., D_FF:]
    return (jax.nn.gelu(gate, approximate=False) * up) @ w_out


def mixer_even(h, w_in, ret_gn, w_out, rel_bias):
    b, s, _ = h.shape
    rq, rk, rv, rg, dq, dk, dv = split_cols(h @ w_in, MIX0_WIDTHS)
    rq = rotary(rq.reshape(b, s, RET_HEADS, RET_QK_DIM))
    rk = rotary(rk.reshape(b, s, RET_HEADS, RET_QK_DIM))
    ret = retention(rq, rk, rv.reshape(b, s, RET_HEADS, RET_V_DIM))
    ret = jax.nn.silu(rg.astype(F32)) * head_groupnorm(ret, ret_gn)
    dil = dilated_attention(dq.reshape(b, s, DIL_HEADS, DIL_HEAD_DIM),
                            dk.reshape(b, s, DIL_HEADS, DIL_HEAD_DIM),
                            dv.reshape(b, s, DIL_HEADS, DIL_HEAD_DIM), rel_bias)
    cat = jnp.concatenate([ret.astype(h.dtype), dil.reshape(b, s, B_W)], axis=-1)
    return cat @ w_out


def mixer_odd(h, w_in, lq1, lk1, lq2, lk2, subln, w_out, rel_bias, layer_idx):
    b, s, _ = h.shape
    q, k, v = split_cols(h @ w_in, MIX1_WIDTHS)
    q = q.reshape(b, s, DIFF_HEADS, 2, DIFF_HEAD_DIM)
    k = k.reshape(b, s, DIFF_HEADS, 2, DIFF_HEAD_DIM)
    v = v.reshape(b, s, DIFF_HEADS, 2 * DIFF_HEAD_DIM)
    lam_init = 0.8 - 0.6 * math.exp(-0.3 * layer_idx)
    lam = (jnp.exp(jnp.sum(lq1.astype(F32) * lk1.astype(F32)))
           - jnp.exp(jnp.sum(lq2.astype(F32) * lk2.astype(F32))) + lam_init)
    o = diff_attention(q[..., 0, :], q[..., 1, :], k[..., 0, :], k[..., 1, :], v, rel_bias, lam)
    o = rmsnorm(o, subln) * (1.0 - lam_init)
    return o.reshape(b, s, MIX1_OUT) @ w_out


def setup_inputs(seed: int = 0) -> dict:
    key = jax.random.key(seed)
    ks = jax.random.split(key, 20)

    def dense(k, shape, fan_in):
        return jax.random.normal(k, shape, F32) * fan_in ** -0.5

    def gain(k, shape):
        return 1.0 + 0.02 * jax.random.normal(k, shape, F32)

    return {
        'x': jax.random.normal(ks[0], (BATCH, SEQ, D_MODEL), F32),
        'rel_bias': 0.2 * jax.random.normal(ks[1], (REL_BUCKETS, REL_BIAS_HEADS), F32),
        'norm_mix': gain(ks[2], (DEPTH, D_MODEL)),
        'norm_ffn': gain(ks[3], (DEPTH, D_MODEL)),
        'norm_final': gain(ks[4], (D_MODEL,)),
        'w_in_ab': dense(ks[5], (N_EVEN, D_MODEL, MIX0_IN), D_MODEL),
        'ret_gn': gain(ks[6], (N_EVEN, A_V)),
        'w_out_ab': dense(ks[7], (N_EVEN, MIX0_OUT, D_MODEL), MIX0_OUT),
        'w_in_c': dense(ks[8], (N_ODD, D_MODEL, MIX1_IN), D_MODEL),
        'lam_q1': 0.1 * jax.random.normal(ks[9], (N_ODD, DIFF_HEAD_DIM), F32),
        'lam_k1': 0.1 * jax.random.normal(ks[10], (N_ODD, DIFF_HEAD_DIM), F32),
        'lam_q2': 0.1 * jax.random.normal(ks[11], (N_ODD, DIFF_HEAD_DIM), F32),
        'lam_k2': 0.1 * jax.random.normal(ks[12], (N_ODD, DIFF_HEAD_DIM), F32),
        'diff_subln': gain(ks[13], (N_ODD, 2 * DIFF_HEAD_DIM)),
        'w_out_c': dense(ks[14], (N_ODD, MIX1_OUT, D_MODEL), MIX1_OUT),
        'w_ffn_in': dense(ks[15], (DEPTH, D_MODEL, 2 * D_FF), D_MODEL),
        'conv_w': dense(ks[16], (DEPTH, CONV_WIDTH, 2 * D_FF), CONV_WIDTH),
        'conv_b': 0.02 * jax.random.normal(ks[17], (DEPTH, 2 * D_FF), F32),
        'w_ffn_out': dense(ks[18], (DEPTH, D_FF, D_MODEL), D_FF),
    }


def reference(x, rel_bias, norm_mix, norm_ffn, norm_final, w_in_ab, ret_gn, w_out_ab,
              w_in_c, lam_q1, lam_k1, lam_q2, lam_k2, diff_subln, w_out_c,
              w_ffn_in, conv_w, conv_b, w_ffn_out):
    for i in range(DEPTH):
        h = rmsnorm(x, norm_mix[i])
        if i % 2 == 0:
            j = i // 2
            x = x + mixer_even(h, w_in_ab[j], ret_gn[j], w_out_ab[j], rel_bias).astype(x.dtype)
        else:
            j = i // 2
            x = x + mixer_odd(h, w_in_c[j], lam_q1[j], lam_k1[j], lam_q2[j], lam_k2[j],
                              diff_subln[j], w_out_c[j], rel_bias, i).astype(x.dtype)
        h = rmsnorm(x, norm_ffn[i])
        x = x + conv_glu_ffn(h, w_ffn_in[i], conv_w[i], conv_b[i], w_ffn_out[i]).astype(x.dtype)
    return rmsnorm(x, norm_final)
```

```python
import functools
import math

import jax
import jax.numpy as jnp
from jax import lax
from jax.experimental import pallas as pl
from jax.experimental.pallas import tpu as pltpu

F32 = jnp.float32
BF16 = jnp.bfloat16

EPS = 1e-6
MASKED = -1e30

RET_HEADS = 4
RET_DIM = 128
RET_CHUNK = 128
ROPE_BASE = 10000.0
DIL_HEADS = 8
DIL_HEAD_DIM = 64
DIL_PATTERNS = ((128, 1), (512, 4), (2048, 16))
DIL_BLOCK = 128
DIFF_HEADS = 8
DIFF_HEAD_DIM = 64
REL_BUCKETS = 32
REL_MAX_DISTANCE = 2048
D_FF = 2816
CONV_WIDTH = 3

LANES = 128
SUBLANES = 8
VMEM_BYTES_V7X = 64 * 1024 * 1024
VMEM_LIMIT = VMEM_BYTES_V7X - 8 * 1024 * 1024

PROJ_TM = 512
PROJ_TN = 512
RET_ROWS = 1024
FFN_TM = 512
FFN_CF = 256
ATT_T = 512
ATT_NEAR = -(-(REL_MAX_DISTANCE + ATT_T - 1) // ATT_T)


def _params(sem):
    return pltpu.CompilerParams(dimension_semantics=sem, vmem_limit_bytes=VMEM_LIMIT)


def _rms(x, g):
    return x * lax.rsqrt(jnp.mean(x * x, axis=-1, keepdims=True) + EPS) * g


def _dot(a, b):
    return jnp.dot(a, b, preferred_element_type=F32)


def _dot_nt(a, b):
    return lax.dot_general(a, b, (((1,), (1,)), ((), ())), preferred_element_type=F32)


def _dot_tn(a, b):
    return lax.dot_general(a, b, (((0,), (0,)), ((), ())), preferred_element_type=F32)


def _proj0_kernel(x_ref, g_ref, w_ref, o_ref):
    h = _rms(x_ref[...], g_ref[...]).astype(BF16)
    n = o_ref.shape[-1]
    for c in range(0, n, PROJ_TN):
        o_ref[:, c:c + PROJ_TN] = _dot(h, w_ref[:, c:c + PROJ_TN]).astype(o_ref.dtype)


def _proj0(x2, g, w):
    t, d = x2.shape
    n = w.shape[1]
    return pl.pallas_call(
        _proj0_kernel,
        grid=(t // PROJ_TM,),
        in_specs=[pl.BlockSpec((PROJ_TM, d), lambda i: (i, 0)),
                  pl.BlockSpec((1, d), lambda i: (0, 0)),
                  pl.BlockSpec((d, n), lambda i: (0, 0))],
        out_specs=pl.BlockSpec((PROJ_TM, n), lambda i: (i, 0)),
        out_shape=jax.ShapeDtypeStruct((t, n), BF16),
        compiler_params=_params(("arbitrary",)),
        name="proj0",
    )(x2, g, w)


def _ret_kernel(q_ref, k_ref, v_ref, gate_ref, cos_ref, sin_ref, tab_ref, gn_ref, o_ref,
                state_ref):
    @pl.when(pl.program_id(2) == 0)
    def _():
        state_ref[...] = jnp.zeros_like(state_ref)

    decay = tab_ref[0, 0]
    xi = tab_ref[0, 1]
    zeta = tab_ref[0, 2]
    g_chunk = tab_ref[0, 3]
    gain = gn_ref[...]
    half = RET_DIM // 2
    for c in range(q_ref.shape[1] // RET_CHUNK):
        rows = pl.ds(c * RET_CHUNK, RET_CHUNK)
        cos = cos_ref[rows, :]
        sin = sin_ref[rows, :]
        q = q_ref[0, rows, :].astype(F32)
        k = k_ref[0, rows, :].astype(F32)
        q = q * cos + pltpu.roll(q, half, 1) * sin
        k = k * cos + pltpu.roll(k, half, 1) * sin
        v = v_ref[0, rows, :]
        state = state_ref[...]
        intra = _dot_nt(q.astype(BF16), k.astype(BF16)) * decay
        y = _dot(intra.astype(BF16), v) + _dot((q * xi).astype(BF16), state.astype(BF16))
        state_ref[...] = g_chunk * state + _dot_tn((k * zeta).astype(BF16), v)
        mu = jnp.mean(y, axis=-1, keepdims=True)
        yc = y - mu
        var = jnp.mean(yc * yc, axis=-1, keepdims=True)
        yn = yc * lax.rsqrt(var + EPS) * gain
        gate = gate_ref[0, rows, :].astype(F32)
        o_ref[0, rows, :] = (gate * jax.nn.sigmoid(gate) * yn).astype(o_ref.dtype)


def _retention_tables(s):
    c, e = RET_CHUNK, RET_DIM
    inv = ROPE_BASE ** (-jnp.arange(0, e, 2, dtype=F32) / e)
    ang = jnp.arange(s, dtype=F32)[:, None] * inv[None, :]
    cos = jnp.concatenate([jnp.cos(ang), jnp.cos(ang)], axis=-1)
    sin = jnp.concatenate([-jnp.sin(ang), jnp.sin(ang)], axis=-1)
    log_g = jnp.log1p(-jnp.exp2(-5.0 - jnp.arange(RET_HEADS, dtype=F32)))
    pos = jnp.arange(c, dtype=F32)
    rel = pos[:, None] - pos[None, :]
    causal = rel >= 0
    scale = e ** -0.5
    decay = jnp.where(causal, jnp.exp(jnp.where(causal, rel, 0.0)[None] * log_g[:, None, None]), 0.0)
    zeta = jnp.exp((c - 1 - pos)[None, :] * log_g[:, None])
    xi = jnp.exp((pos + 1)[None, :] * log_g[:, None])
    g_chunk = jnp.exp(c * log_g)
    tab = jnp.stack([
        decay * scale,
        jnp.broadcast_to(xi[:, :, None], (RET_HEADS, c, e)),
        jnp.broadcast_to(zeta[:, :, None] * scale, (RET_HEADS, c, e)),
        jnp.broadcast_to(g_chunk[:, None, None], (RET_HEADS, e, e)),
    ], axis=1)
    return cos, sin, tab


def _retention(p0, gn, b, s):
    cos, sin, tab = _retention_tables(s)
    e = RET_DIM
    col = lambda off: (lambda bi, h, t: (bi, t, off + h))
    return pl.pallas_call(
        _ret_kernel,
        grid=(b, RET_HEADS, s // RET_ROWS),
        in_specs=[pl.BlockSpec((1, RET_ROWS, e), col(0)),
                  pl.BlockSpec((1, RET_ROWS, e), col(RET_HEADS)),
                  pl.BlockSpec((1, RET_ROWS, e), col(2 * RET_HEADS)),
                  pl.BlockSpec((1, RET_ROWS, e), col(3 * RET_HEADS)),
                  pl.BlockSpec((RET_ROWS, e), lambda bi, h, t: (t, 0)),
                  pl.BlockSpec((RET_ROWS, e), lambda bi, h, t: (t, 0)),
                  pl.BlockSpec((1, 4, e, e), lambda bi, h, t: (h, 0, 0, 0)),
                  pl.BlockSpec((1, e), lambda bi, h, t: (0, h))],
        out_specs=pl.BlockSpec((1, RET_ROWS, e), lambda bi, h, t: (bi, t, h)),
        out_shape=jax.ShapeDtypeStruct((b, s, RET_HEADS * e), BF16),
        scratch_shapes=[pltpu.VMEM((e, e), F32)],
        compiler_params=_params(("arbitrary", "arbitrary", "arbitrary")),
        name="retention",
    )(p0, p0, p0, p0, cos, sin, tab, gn)


def _t5_bucket(dist):
    max_exact = REL_BUCKETS // 2
    d = jnp.maximum(dist.astype(F32), 1.0)
    large = max_exact + (jnp.log(d / max_exact) / math.log(REL_MAX_DISTANCE / max_exact)
                         * (REL_BUCKETS - max_exact))
    large = jnp.clip(large.astype(jnp.int32), max_exact, REL_BUCKETS - 1)
    return jnp.where(dist < max_exact, dist, large)


def _dil_kernel(q_ref, kp_ref, kc_ref, vp_ref, vc_ref, bias_ref, o_ref, lse_ref):
    first = pl.program_id(2) == 0
    blk, e = DIL_BLOCK, DIL_HEAD_DIM
    q = q_ref[0] * (e ** -0.5)
    kcat = jnp.concatenate([kp_ref[0], kc_ref[0]], axis=0)
    vcat = jnp.concatenate([vp_ref[0], vc_ref[0]], axis=0)
    in_prev = lax.broadcasted_iota(jnp.int32, (blk, 2 * blk), 1) < blk
    no_prev = jnp.logical_and(first, in_prev)
    for h in range(DIL_HEADS):
        cols = slice(h * e, (h + 1) * e)
        s = _dot_nt(q[:, cols], kcat[:, cols]) + bias_ref[h]
        s = jnp.where(no_prev, MASKED, s)
        m = jnp.max(s, axis=-1, keepdims=True)
        p = jnp.exp(s - m)
        l = jnp.sum(p, axis=-1, keepdims=True)
        o = _dot(p.astype(BF16), vcat[:, cols]) / l
        o_ref[0, :, cols] = o
        lse_ref[0, :, cols] = jnp.broadcast_to(m + jnp.log(l), (blk, e))


def _dil_bias(rel_bias, window, dil):
    blk = DIL_BLOCK
    span = window // dil
    i = jnp.arange(blk)[:, None]
    j = jnp.arange(2 * blk)[None, :]
    r = i + blk - j
    band = (r >= 0) & (r <= span)
    bias = rel_bias[_t5_bucket(jnp.maximum(r, 0) * dil)].astype(F32)
    bias = jnp.transpose(bias, (2, 0, 1))
    return jnp.where(band[None], bias, MASKED)


def _dilated_pattern(p0, rel_bias, b, s, window, dil):
    blk = DIL_BLOCK
    assert window // dil == blk and s % (dil * blk) == 0
    w = DIL_HEADS * DIL_HEAD_DIM
    n_in = p0.shape[-1]
    assert n_in % w == 0
    per_row = n_in // w
    base = 2 * RET_HEADS * 2 * RET_DIM // w
    nb = s // (dil * blk)
    view = p0.reshape(b, s // dil, dil * n_in)
    cur = lambda c: (lambda bi, r, n: (bi, n, r * per_row + base + c))
    prev = lambda c: (lambda bi, r, n: (bi, jnp.maximum(n - 1, 0), r * per_row + base + c))
    out_spec = pl.BlockSpec((1, blk, w), lambda bi, r, n: (bi, n, r))
    o, lse = pl.pallas_call(
        _dil_kernel,
        grid=(b, dil, nb),
        in_specs=[pl.BlockSpec((1, blk, w), cur(0)),
                  pl.BlockSpec((1, blk, w), prev(1)),
                  pl.BlockSpec((1, blk, w), cur(1)),
                  pl.BlockSpec((1, blk, w), prev(2)),
                  pl.BlockSpec((1, blk, w), cur(2)),
                  pl.BlockSpec((DIL_HEADS, blk, 2 * blk), lambda bi, r, n: (0, 0, 0))],
        out_specs=[out_spec, out_spec],
        out_shape=[jax.ShapeDtypeStruct((b, s // dil, dil * w), F32)] * 2,
        compiler_params=_params(("arbitrary", "arbitrary", "arbitrary")),
        name=f"dilated_d{dil}",
    )(view, view, view, view, view, _dil_bias(rel_bias, window, dil))
    return o.reshape(b * s, w), lse.reshape(b * s, w)


def _mix0_kernel(x_ref, ret_ref, o1_ref, o2_ref, o3_ref, l1_ref, l2_ref, l3_ref, w_ref, out_ref):
    l1, l2, l3 = l1_ref[...], l2_ref[...], l3_ref[...]
    m = jnp.maximum(jnp.maximum(l1, l2), l3)
    w1, w2, w3 = jnp.exp(l1 - m), jnp.exp(l2 - m), jnp.exp(l3 - m)
    dil = (w1 * o1_ref[...] + w2 * o2_ref[...] + w3 * o3_ref[...]) / (w1 + w2 + w3)
    na = ret_ref.shape[-1]
    y = _dot(ret_ref[...], w_ref[:na, :]) + _dot(dil.astype(BF16), w_ref[na:, :])
    out_ref[...] = x_ref[...] + y


def _mix0(x2, ret, outs, lses, w):
    t, d = x2.shape
    na, nb = ret.shape[-1], outs[0].shape[-1]
    row = lambda n: pl.BlockSpec((PROJ_TM, n), lambda i: (i, 0))
    return pl.pallas_call(
        _mix0_kernel,
        grid=(t // PROJ_TM,),
        in_specs=[row(d), row(na)] + [row(nb)] * 6 + [pl.BlockSpec((na + nb, d), lambda i: (0, 0))],
        out_specs=row(d),
        out_shape=jax.ShapeDtypeStruct((t, d), F32),
        compiler_params=_params(("arbitrary",)),
        name="mix0",
    )(x2, ret, *outs, *lses, w)


def _ffn_kernel(x_ref, g_ref, win_ref, cw_ref, cb_ref, wout_ref, fg_ref, o_ref,
                ubuf, carry, gbuf, *, final_norm):
    tm = x_ref.shape[1]
    halo = SUBLANES

    @pl.when(pl.program_id(1) == 0)
    def _():
        carry[...] = jnp.zeros_like(carry)

    x = x_ref[0]
    h = _rms(x, g_ref[...]).astype(BF16)

    def conv(part, col):
        cols = slice(col, col + FFN_CF)
        u = _dot(h, win_ref[:, cols])
        ubuf[part, 0:halo, :] = carry[:, cols]
        ubuf[part, halo:halo + tm, :] = u
        carry[:, cols] = u[tm - halo:, :]
        u1 = ubuf[part, halo - 1:halo - 1 + tm, :]
        u2 = ubuf[part, halo - 2:halo - 2 + tm, :]
        return (u * cw_ref[2:3, cols] + u1 * cw_ref[1:2, cols] + u2 * cw_ref[0:1, cols]
                + cb_ref[:, cols])

    for c in range(D_FF // FFN_CF):
        gate = conv(0, c * FFN_CF)
        up = conv(1, D_FF + c * FFN_CF)
        act = 0.5 * gate * (1.0 + lax.erf(gate * (2.0 ** -0.5)))
        gbuf[:, c * FFN_CF:(c + 1) * FFN_CF] = (act * up).astype(BF16)

    y = x + _dot(gbuf[...], wout_ref[...])
    if final_norm:
        y = _rms(y, fg_ref[...])
    o_ref[0] = y


def _ffn(x3, g, w_in, conv_w, conv_b, w_out, final_gain, final_norm):
    b, s, d = x3.shape
    f2 = w_in.shape[1]
    whole = pl.BlockSpec(memory_space=pltpu.VMEM)
    return pl.pallas_call(
        functools.partial(_ffn_kernel, final_norm=final_norm),
        grid=(b, s // FFN_TM),
        in_specs=[pl.BlockSpec((1, FFN_TM, d), lambda bi, t: (bi, t, 0)),
                  whole, whole, whole, whole, whole, whole],
        out_specs=pl.BlockSpec((1, FFN_TM, d), lambda bi, t: (bi, t, 0)),
        out_shape=jax.ShapeDtypeStruct((b, s, d), F32),
        scratch_shapes=[pltpu.VMEM((2, FFN_TM + SUBLANES, FFN_CF), F32),
                        pltpu.VMEM((SUBLANES, f2), F32),
                        pltpu.VMEM((FFN_TM, D_FF), BF16)],
        compiler_params=_params(("arbitrary", "arbitrary")),
        name="ffn_final" if final_norm else "ffn",
    )(x3, g, w_in, conv_w, conv_b, w_out, final_gain)


def _proj1_kernel(x_ref, g_ref, wq_ref, wk_ref, wvt_ref, q_ref, k_ref, vt_ref):
    h = _rms(x_ref[0], g_ref[...]).astype(BF16)
    n = q_ref.shape[-1]
    for c in range(0, n, PROJ_TN):
        cols = slice(c, c + PROJ_TN)
        q_ref[0, :, cols] = (_dot(h, wq_ref[:, cols]) * (DIFF_HEAD_DIM ** -0.5)).astype(BF16)
        k_ref[0, :, cols] = _dot(h, wk_ref[:, cols]).astype(BF16)
        vt_ref[0, 0, cols, :] = _dot_nt(wvt_ref[cols, :], h).astype(BF16)


def _proj1(x3, g, wq, wk, wvt):
    b, s, d = x3.shape
    n = wq.shape[1]
    t = ATT_T
    wspec = pl.BlockSpec((d, n), lambda bi, i: (0, 0))
    return pl.pallas_call(
        _proj1_kernel,
        grid=(b, s // t),
        in_specs=[pl.BlockSpec((1, t, d), lambda bi, i: (bi, i, 0)),
                  pl.BlockSpec((1, d), lambda bi, i: (0, 0)),
                  wspec, wspec, pl.BlockSpec((n, d), lambda bi, i: (0, 0))],
        out_specs=[pl.BlockSpec((1, t, n), lambda bi, i: (bi, i, 0)),
                   pl.BlockSpec((1, t, n), lambda bi, i: (bi, i, 0)),
                   pl.BlockSpec((1, 1, n, t), lambda bi, i: (bi, i, 0, 0))],
        out_shape=[jax.ShapeDtypeStruct((b, s, n), BF16),
                   jax.ShapeDtypeStruct((b, s, n), BF16),
                   jax.ShapeDtypeStruct((b, s // t, n, t), BF16)],
        compiler_params=_params(("arbitrary", "arbitrary")),
        name="proj1",
    )(x3, g, wq, wk, wvt)


def _diff_kernel(q_ref, k_ref, vt_ref, nb_ref, lam_ref, sub_ref, o_ref,
                 m_ref, l_ref, acc_ref, *, lam_init):
    i = pl.program_id(2)
    t = ATT_T
    e = DIFF_HEAD_DIM
    q = q_ref[0]
    lane = lax.broadcasted_iota(jnp.int32, q.shape, 1)
    zero = jnp.zeros_like(q)
    qs = (jnp.where(lane < e, q, zero), jnp.where(lane >= e, q, zero))
    m_ref[...] = jnp.full_like(m_ref, MASKED)
    l_ref[...] = jnp.zeros_like(l_ref)
    acc_ref[...] = jnp.zeros_like(acc_ref)

    def tile(j, bias):
        kb = k_ref[0, pl.ds(pl.multiple_of(j * t, t), t), :]
        vtb = vt_ref[0, j]
        for a in range(2):
            s = _dot_nt(kb, qs[a])
            if bias is not None:
                s = s + bias
            m_old = m_ref[a]
            m_new = jnp.maximum(m_old, jnp.max(s, axis=0, keepdims=True))
            alpha = jnp.exp(m_old - m_new)
            p = jnp.exp(s - m_new)
            l_ref[a] = alpha * l_ref[a] + jnp.sum(p, axis=0, keepdims=True)
            acc_ref[a] = alpha * acc_ref[a] + _dot(vtb, p.astype(BF16))
            m_ref[a] = m_new

    n_far = jnp.maximum(i - (ATT_NEAR - 1), 0)

    def far_body(j, carry):
        tile(j, None)
        return carry

    def near_body(j, carry):
        tile(j, nb_ref[0, i - j])
        return carry

    lax.fori_loop(0, n_far, far_body, 0)
    lax.fori_loop(n_far, i + 1, near_body, 0)

    lam = (jnp.exp(jnp.sum(lam_ref[0:1, :] * lam_ref[1:2, :], axis=-1, keepdims=True))
           - jnp.exp(jnp.sum(lam_ref[2:3, :] * lam_ref[3:4, :], axis=-1, keepdims=True))
           + lam_init)
    ot = acc_ref[0] / l_ref[0] - lam * (acc_ref[1] / l_ref[1])
    ot = ot * lax.rsqrt(jnp.mean(ot * ot, axis=0, keepdims=True) + EPS)
    o_ref[0] = (ot.T * sub_ref[...] * (1.0 - lam_init)).astype(o_ref.dtype)


def _diff_bias(rel_bias, s):
    t = ATT_T
    dist_bias = rel_bias[_t5_bucket(jnp.arange(s))].astype(F32)
    far = rel_bias[REL_BUCKETS - 1].astype(F32)
    o = jnp.arange(ATT_NEAR)[:, None, None]
    a = jnp.arange(t)[None, :, None]
    b = jnp.arange(t)[None, None, :]
    d = o * t + b - a
    tbl = dist_bias[jnp.clip(d, 0, s - 1)] - far
    tbl = jnp.where((d >= 0)[..., None], tbl, MASKED)
    return jnp.transpose(tbl, (3, 0, 1, 2))


def _diff_attention(q, k, vt, rel_bias, lam4, subln, layer_idx):
    b, s, n = q.shape
    t = ATT_T
    w = 2 * DIFF_HEAD_DIM
    lam_init = 0.8 - 0.6 * math.exp(-0.3 * layer_idx)
    return pl.pallas_call(
        functools.partial(_diff_kernel, lam_init=lam_init),
        grid=(b, DIFF_HEADS, s // t),
        in_specs=[pl.BlockSpec((1, t, w), lambda bi, h, i: (bi, i, h)),
                  pl.BlockSpec((1, s, w), lambda bi, h, i: (bi, 0, h)),
                  pl.BlockSpec((1, s // t, w, t), lambda bi, h, i: (bi, 0, h, 0)),
                  pl.BlockSpec((1, ATT_NEAR, t, t), lambda bi, h, i: (h, 0, 0, 0)),
                  pl.BlockSpec((4, DIFF_HEAD_DIM), lambda bi, h, i: (0, 0)),
                  pl.BlockSpec((1, w), lambda bi, h, i: (0, 0))],
        out_specs=pl.BlockSpec((1, t, w), lambda bi, h, i: (bi, i, h)),
        out_shape=jax.ShapeDtypeStruct((b, s, n), BF16),
        scratch_shapes=[pltpu.VMEM((2, 1, t), F32),
                        pltpu.VMEM((2, 1, t), F32),
                        pltpu.VMEM((2, w, t), F32)],
        compiler_params=_params(("arbitrary", "arbitrary", "arbitrary")),
        name="diff_attention",
    )(q, k, vt, _diff_bias(rel_bias, s), lam4, subln)


def _mix1_kernel(x_ref, a_ref, w_ref, out_ref):
    out_ref[...] = x_ref[...] + _dot(a_ref[...], w_ref[...])


def _mix1(x2, a, w):
    t, d = x2.shape
    n = a.shape[-1]
    return pl.pallas_call(
        _mix1_kernel,
        grid=(t // PROJ_TM,),
        in_specs=[pl.BlockSpec((PROJ_TM, d), lambda i: (i, 0)),
                  pl.BlockSpec((PROJ_TM, n), lambda i: (i, 0)),
                  pl.BlockSpec((n, d), lambda i: (0, 0))],
        out_specs=pl.BlockSpec((PROJ_TM, d), lambda i: (i, 0)),
        out_shape=jax.ShapeDtypeStruct((t, d), F32),
        compiler_params=_params(("arbitrary",)),
        name="mix1",
    )(x2, a, w)


def kernel(x, rel_bias, norm_mix, norm_ffn, norm_final, w_in_ab, ret_gn, w_out_ab, w_in_c,
           lam_q1, lam_k1, lam_q2, lam_k2, diff_subln, w_out_c, w_ffn_in, conv_w, conv_b,
           w_ffn_out):
    b, s, d = x.shape
    depth = norm_mix.shape[0]
    assert depth == 2 and s % RET_ROWS == 0 and s % FFN_TM == 0 and s >= ATT_NEAR * ATT_T
    row = lambda v: v.reshape(1, -1)
    x2 = x.reshape(b * s, d)

    p0 = _proj0(x2, row(norm_mix[0]), w_in_ab[0].astype(BF16))
    p0 = p0.reshape(b, s, -1)
    ret = _retention(p0, row(ret_gn[0]), b, s).reshape(b * s, -1)
    outs, lses = zip(*[_dilated_pattern(p0, rel_bias, b, s, wdw, dil) for wdw, dil in DIL_PATTERNS])
    x2 = _mix0(x2, ret, outs, lses, w_out_ab[0].astype(BF16))
    x3 = _ffn(x2.reshape(b, s, d), row(norm_ffn[0]), w_ffn_in[0].astype(BF16), conv_w[0],
              row(conv_b[0]), w_ffn_out[0].astype(BF16), row(norm_final), False)

    nqk = DIFF_HEADS * 2 * DIFF_HEAD_DIM
    wc = w_in_c[0].astype(BF16)
    q, k, vt = _proj1(x3, row(norm_mix[1]), wc[:, :nqk], wc[:, nqk:2 * nqk], wc[:, 2 * nqk:].T)
    lam4 = jnp.stack([lam_q1[0], lam_k1[0], lam_q2[0], lam_k2[0]]).astype(F32)
    a = _diff_attention(q, k, vt, rel_bias, lam4, row(diff_subln[0]), 1)
    x2 = _mix1(x3.reshape(b * s, d), a.reshape(b * s, -1), w_out_c[0].astype(BF16))
    return _ffn(x2.reshape(b, s, d), row(norm_ffn[1]), w_ffn_in[1].astype(BF16), conv_w[1],
                row(conv_b[1]), w_ffn_out[1].astype(BF16), row(norm_final), True)
```

```python
import functools
import math

import jax
import jax.numpy as jnp
from jax import lax
from jax.experimental import pallas as pl
from jax.experimental.pallas import tpu as pltpu

F32 = jnp.float32
BF16 = jnp.bfloat16

EPS = 1e-6
MASKED = -1e30

RET_HEADS = 4
RET_DIM = 128
RET_CHUNK = 128
ROPE_BASE = 10000.0
DIL_HEADS = 8
DIL_HEAD_DIM = 64
DIL_PATTERNS = ((128, 1), (512, 4), (2048, 16))
DIL_BLOCK = 128
DIFF_HEADS = 8
DIFF_HEAD_DIM = 64
REL_BUCKETS = 32
REL_MAX_DISTANCE = 2048
D_FF = 2816
CONV_WIDTH = 3

LANES = 128
SUBLANES = 8
VMEM_BYTES_V7X = 64 * 1024 * 1024
VMEM_LIMIT = VMEM_BYTES_V7X - 8 * 1024 * 1024

PROJ_TM = 512
PROJ_TN = 512
RET_ROWS = 1024
FFN_TM = 512
FFN_CF = 256
ATT_T = 512
ATT_NEAR = -(-(REL_MAX_DISTANCE + ATT_T - 1) // ATT_T)
LOG2E = math.log2(math.e)
ATT_QSCALE = DIFF_HEAD_DIM ** -0.5 * LOG2E


def _params(sem):
    return pltpu.CompilerParams(dimension_semantics=sem, vmem_limit_bytes=VMEM_LIMIT)


def _rms(x, g):
    return x * lax.rsqrt(jnp.mean(x * x, axis=-1, keepdims=True) + EPS) * g


def _dot(a, b):
    return jnp.dot(a, b, preferred_element_type=F32)


def _dot_nt(a, b):
    return lax.dot_general(a, b, (((1,), (1,)), ((), ())), preferred_element_type=F32)


def _dot_tn(a, b):
    return lax.dot_general(a, b, (((0,), (0,)), ((), ())), preferred_element_type=F32)


def _proj0_kernel(x_ref, g_ref, w_ref, o_ref):
    h = _rms(x_ref[...], g_ref[...]).astype(BF16)
    n = o_ref.shape[-1]
    for c in range(0, n, PROJ_TN):
        o_ref[:, c:c + PROJ_TN] = _dot(h, w_ref[:, c:c + PROJ_TN]).astype(o_ref.dtype)


def _proj0(x2, g, w):
    t, d = x2.shape
    n = w.shape[1]
    return pl.pallas_call(
        _proj0_kernel,
        grid=(t // PROJ_TM,),
        in_specs=[pl.BlockSpec((PROJ_TM, d), lambda i: (i, 0)),
                  pl.BlockSpec((1, d), lambda i: (0, 0)),
                  pl.BlockSpec((d, n), lambda i: (0, 0))],
        out_specs=pl.BlockSpec((PROJ_TM, n), lambda i: (i, 0)),
        out_shape=jax.ShapeDtypeStruct((t, n), BF16),
        compiler_params=_params(("arbitrary",)),
        name="proj0",
    )(x2, g, w)


def _ret_kernel(q_ref, k_ref, v_ref, gate_ref, cos_ref, sin_ref, tab_ref, gn_ref, o_ref,
                state_ref):
    @pl.when(pl.program_id(2) == 0)
    def _():
        state_ref[...] = jnp.zeros_like(state_ref)

    decay = tab_ref[0, 0]
    xi = tab_ref[0, 1]
    zeta = tab_ref[0, 2]
    g_chunk = tab_ref[0, 3]
    gain = gn_ref[...]
    half = RET_DIM // 2
    for c in range(q_ref.shape[1] // RET_CHUNK):
        rows = pl.ds(c * RET_CHUNK, RET_CHUNK)
        cos = cos_ref[rows, :]
        sin = sin_ref[rows, :]
        q = q_ref[0, rows, :].astype(F32)
        k = k_ref[0, rows, :].astype(F32)
        q = q * cos + pltpu.roll(q, half, 1) * sin
        k = k * cos + pltpu.roll(k, half, 1) * sin
        v = v_ref[0, rows, :]
        state = state_ref[...]
        intra = _dot_nt(q.astype(BF16), k.astype(BF16)) * decay
        y = _dot(intra.astype(BF16), v) + _dot((q * xi).astype(BF16), state.astype(BF16))
        state_ref[...] = g_chunk * state + _dot_tn((k * zeta).astype(BF16), v)
        mu = jnp.mean(y, axis=-1, keepdims=True)
        yc = y - mu
        var = jnp.mean(yc * yc, axis=-1, keepdims=True)
        yn = yc * lax.rsqrt(var + EPS) * gain
        gate = gate_ref[0, rows, :].astype(F32)
        o_ref[0, rows, :] = (gate * jax.nn.sigmoid(gate) * yn).astype(o_ref.dtype)


def _retention_tables(s):
    c, e = RET_CHUNK, RET_DIM
    inv = ROPE_BASE ** (-jnp.arange(0, e, 2, dtype=F32) / e)
    ang = jnp.arange(s, dtype=F32)[:, None] * inv[None, :]
    cos = jnp.concatenate([jnp.cos(ang), jnp.cos(ang)], axis=-1)
    sin = jnp.concatenate([-jnp.sin(ang), jnp.sin(ang)], axis=-1)
    log_g = jnp.log1p(-jnp.exp2(-5.0 - jnp.arange(RET_HEADS, dtype=F32)))
    pos = jnp.arange(c, dtype=F32)
    rel = pos[:, None] - pos[None, :]
    causal = rel >= 0
    scale = e ** -0.5
    decay = jnp.where(causal, jnp.exp(jnp.where(causal, rel, 0.0)[None] * log_g[:, None, None]), 0.0)
    zeta = jnp.exp((c - 1 - pos)[None, :] * log_g[:, None])
    xi = jnp.exp((pos + 1)[None, :] * log_g[:, None])
    g_chunk = jnp.exp(c * log_g)
    tab = jnp.stack([
        decay * scale,
        jnp.broadcast_to(xi[:, :, None], (RET_HEADS, c, e)),
        jnp.broadcast_to(zeta[:, :, None] * scale, (RET_HEADS, c, e)),
        jnp.broadcast_to(g_chunk[:, None, None], (RET_HEADS, e, e)),
    ], axis=1)
    return cos, sin, tab


def _retention(p0, gn, b, s):
    cos, sin, tab = _retention_tables(s)
    e = RET_DIM
    col = lambda off: (lambda bi, h, t: (bi, t, off + h))
    return pl.pallas_call(
        _ret_kernel,
        grid=(b, RET_HEADS, s // RET_ROWS),
        in_specs=[pl.BlockSpec((1, RET_ROWS, e), col(0)),
                  pl.BlockSpec((1, RET_ROWS, e), col(RET_HEADS)),
                  pl.BlockSpec((1, RET_ROWS, e), col(2 * RET_HEADS)),
                  pl.BlockSpec((1, RET_ROWS, e), col(3 * RET_HEADS)),
                  pl.BlockSpec((RET_ROWS, e), lambda bi, h, t: (t, 0)),
                  pl.BlockSpec((RET_ROWS, e), lambda bi, h, t: (t, 0)),
                  pl.BlockSpec((1, 4, e, e), lambda bi, h, t: (h, 0, 0, 0)),
                  pl.BlockSpec((1, e), lambda bi, h, t: (0, h))],
        out_specs=pl.BlockSpec((1, RET_ROWS, e), lambda bi, h, t: (bi, t, h)),
        out_shape=jax.ShapeDtypeStruct((b, s, RET_HEADS * e), BF16),
        scratch_shapes=[pltpu.VMEM((e, e), F32)],
        compiler_params=_params(("arbitrary", "arbitrary", "arbitrary")),
        name="retention",
    )(p0, p0, p0, p0, cos, sin, tab, gn)


def _t5_bucket(dist):
    max_exact = REL_BUCKETS // 2
    d = jnp.maximum(dist.astype(F32), 1.0)
    large = max_exact + (jnp.log(d / max_exact) / math.log(REL_MAX_DISTANCE / max_exact)
                         * (REL_BUCKETS - max_exact))
    large = jnp.clip(large.astype(jnp.int32), max_exact, REL_BUCKETS - 1)
    return jnp.where(dist < max_exact, dist, large)


def _dil_kernel(q_ref, kp_ref, kc_ref, vp_ref, vc_ref, bias_ref, o_ref, lse_ref):
    first = pl.program_id(2) == 0
    blk, e = DIL_BLOCK, DIL_HEAD_DIM
    q = q_ref[0] * (e ** -0.5)
    kcat = jnp.concatenate([kp_ref[0], kc_ref[0]], axis=0)
    vcat = jnp.concatenate([vp_ref[0], vc_ref[0]], axis=0)
    in_prev = lax.broadcasted_iota(jnp.int32, (blk, 2 * blk), 1) < blk
    no_prev = jnp.logical_and(first, in_prev)
    for h in range(DIL_HEADS):
        cols = slice(h * e, (h + 1) * e)
        s = _dot_nt(q[:, cols], kcat[:, cols]) + bias_ref[h]
        s = jnp.where(no_prev, MASKED, s)
        m = jnp.max(s, axis=-1, keepdims=True)
        p = jnp.exp(s - m)
        l = jnp.sum(p, axis=-1, keepdims=True)
        o = _dot(p.astype(BF16), vcat[:, cols]) / l
        o_ref[0, :, cols] = o
        lse_ref[0, :, cols] = jnp.broadcast_to(m + jnp.log(l), (blk, e))


def _dil_bias(rel_bias, window, dil):
    blk = DIL_BLOCK
    span = window // dil
    i = jnp.arange(blk)[:, None]
    j = jnp.arange(2 * blk)[None, :]
    r = i + blk - j
    band = (r >= 0) & (r <= span)
    bias = rel_bias[_t5_bucket(jnp.maximum(r, 0) * dil)].astype(F32)
    bias = jnp.transpose(bias, (2, 0, 1))
    return jnp.where(band[None], bias, MASKED)


def _dilated_pattern(p0, rel_bias, b, s, window, dil):
    blk = DIL_BLOCK
    assert window // dil == blk and s % (dil * blk) == 0
    w = DIL_HEADS * DIL_HEAD_DIM
    n_in = p0.shape[-1]
    assert n_in % w == 0
    per_row = n_in // w
    base = 2 * RET_HEADS * 2 * RET_DIM // w
    nb = s // (dil * blk)
    view = p0.reshape(b, s // dil, dil * n_in)
    cur = lambda c: (lambda bi, r, n: (bi, n, r * per_row + base + c))
    prev = lambda c: (lambda bi, r, n: (bi, jnp.maximum(n - 1, 0), r * per_row + base + c))
    out_spec = pl.BlockSpec((1, blk, w), lambda bi, r, n: (bi, n, r))
    o, lse = pl.pallas_call(
        _dil_kernel,
        grid=(b, dil, nb),
        in_specs=[pl.BlockSpec((1, blk, w), cur(0)),
                  pl.BlockSpec((1, blk, w), prev(1)),
                  pl.BlockSpec((1, blk, w), cur(1)),
                  pl.BlockSpec((1, blk, w), prev(2)),
                  pl.BlockSpec((1, blk, w), cur(2)),
                  pl.BlockSpec((DIL_HEADS, blk, 2 * blk), lambda bi, r, n: (0, 0, 0))],
        out_specs=[out_spec, out_spec],
        out_shape=[jax.ShapeDtypeStruct((b, s // dil, dil * w), F32)] * 2,
        compiler_params=_params(("arbitrary", "arbitrary", "arbitrary")),
        name=f"dilated_d{dil}",
    )(view, view, view, view, view, _dil_bias(rel_bias, window, dil))
    return o.reshape(b * s, w), lse.reshape(b * s, w)


def _mix0_kernel(x_ref, ret_ref, o1_ref, o2_ref, o3_ref, l1_ref, l2_ref, l3_ref, w_ref, out_ref):
    l1, l2, l3 = l1_ref[...], l2_ref[...], l3_ref[...]
    m = jnp.maximum(jnp.maximum(l1, l2), l3)
    w1, w2, w3 = jnp.exp(l1 - m), jnp.exp(l2 - m), jnp.exp(l3 - m)
    dil = (w1 * o1_ref[...] + w2 * o2_ref[...] + w3 * o3_ref[...]) / (w1 + w2 + w3)
    na = ret_ref.shape[-1]
    y = _dot(ret_ref[...], w_ref[:na, :]) + _dot(dil.astype(BF16), w_ref[na:, :])
    out_ref[...] = x_ref[...] + y


def _mix0(x2, ret, outs, lses, w):
    t, d = x2.shape
    na, nb = ret.shape[-1], outs[0].shape[-1]
    row = lambda n: pl.BlockSpec((PROJ_TM, n), lambda i: (i, 0))
    return pl.pallas_call(
        _mix0_kernel,
        grid=(t // PROJ_TM,),
        in_specs=[row(d), row(na)] + [row(nb)] * 6 + [pl.BlockSpec((na + nb, d), lambda i: (0, 0))],
        out_specs=row(d),
        out_shape=jax.ShapeDtypeStruct((t, d), F32),
        compiler_params=_params(("arbitrary",)),
        name="mix0",
    )(x2, ret, *outs, *lses, w)


def _ffn_kernel(x_ref, g_ref, win_ref, cw_ref, cb_ref, wout_ref, fg_ref, o_ref,
                ubuf, carry, gbuf, *, final_norm):
    tm = x_ref.shape[1]
    halo = SUBLANES

    @pl.when(pl.program_id(1) == 0)
    def _():
        carry[...] = jnp.zeros_like(carry)

    x = x_ref[0]
    h = _rms(x, g_ref[...]).astype(BF16)

    def conv(part, col):
        cols = slice(col, col + FFN_CF)
        u = _dot(h, win_ref[:, cols])
        ubuf[part, 0:halo, :] = carry[:, cols]
        ubuf[part, halo:halo + tm, :] = u
        carry[:, cols] = u[tm - halo:, :]
        u1 = ubuf[part, halo - 1:halo - 1 + tm, :]
        u2 = ubuf[part, halo - 2:halo - 2 + tm, :]
        return (u * cw_ref[2:3, cols] + u1 * cw_ref[1:2, cols] + u2 * cw_ref[0:1, cols]
                + cb_ref[:, cols])

    for c in range(D_FF // FFN_CF):
        gate = conv(0, c * FFN_CF)
        up = conv(1, D_FF + c * FFN_CF)
        act = 0.5 * gate * (1.0 + lax.erf(gate * (2.0 ** -0.5)))
        gbuf[:, c * FFN_CF:(c + 1) * FFN_CF] = (act * up).astype(BF16)

    y = x + _dot(gbuf[...], wout_ref[...])
    if final_norm:
        y = _rms(y, fg_ref[...])
    o_ref[0] = y


def _ffn(x3, g, w_in, conv_w, conv_b, w_out, final_gain, final_norm):
    b, s, d = x3.shape
    f2 = w_in.shape[1]
    whole = pl.BlockSpec(memory_space=pltpu.VMEM)
    return pl.pallas_call(
        functools.partial(_ffn_kernel, final_norm=final_norm),
        grid=(b, s // FFN_TM),
        in_specs=[pl.BlockSpec((1, FFN_TM, d), lambda bi, t: (bi, t, 0)),
                  whole, whole, whole, whole, whole, whole],
        out_specs=pl.BlockSpec((1, FFN_TM, d), lambda bi, t: (bi, t, 0)),
        out_shape=jax.ShapeDtypeStruct((b, s, d), F32),
        scratch_shapes=[pltpu.VMEM((2, FFN_TM + SUBLANES, FFN_CF), F32),
                        pltpu.VMEM((SUBLANES, f2), F32),
                        pltpu.VMEM((FFN_TM, D_FF), BF16)],
        compiler_params=_params(("arbitrary", "arbitrary")),
        name="ffn_final" if final_norm else "ffn",
    )(x3, g, w_in, conv_w, conv_b, w_out, final_gain)


def _proj1_kernel(x_ref, g_ref, wq_ref, wk_ref, wvt_ref, q_ref, k_ref, vt_ref):
    h = _rms(x_ref[0], g_ref[...]).astype(BF16)
    n = q_ref.shape[-1]
    for c in range(0, n, PROJ_TN):
        cols = slice(c, c + PROJ_TN)
        q_ref[0, :, cols] = (_dot(h, wq_ref[:, cols]) * ATT_QSCALE).astype(BF16)
        k_ref[0, :, cols] = _dot(h, wk_ref[:, cols]).astype(BF16)
        vt_ref[0, 0, cols, :] = _dot_nt(wvt_ref[cols, :], h).astype(BF16)


def _proj1(x3, g, wq, wk, wvt):
    b, s, d = x3.shape
    n = wq.shape[1]
    t = ATT_T
    wspec = pl.BlockSpec((d, n), lambda bi, i: (0, 0))
    return pl.pallas_call(
        _proj1_kernel,
        grid=(b, s // t),
        in_specs=[pl.BlockSpec((1, t, d), lambda bi, i: (bi, i, 0)),
                  pl.BlockSpec((1, d), lambda bi, i: (0, 0)),
                  wspec, wspec, pl.BlockSpec((n, d), lambda bi, i: (0, 0))],
        out_specs=[pl.BlockSpec((1, t, n), lambda bi, i: (bi, i, 0)),
                   pl.BlockSpec((1, t, n), lambda bi, i: (bi, i, 0)),
                   pl.BlockSpec((1, 1, n, t), lambda bi, i: (bi, i, 0, 0))],
        out_shape=[jax.ShapeDtypeStruct((b, s, n), BF16),
                   jax.ShapeDtypeStruct((b, s, n), BF16),
                   jax.ShapeDtypeStruct((b, s // t, n, t), BF16)],
        compiler_params=_params(("arbitrary", "arbitrary")),
        name="proj1",
    )(x3, g, wq, wk, wvt)


def _diff_kernel(q_ref, k_ref, vt_ref, f_ref, lam_ref, sub_ref, o_ref,
                 m_ref, l_ref, acc_ref, nb_ref, *, lam_init):
    i = pl.program_id(2)
    t = ATT_T
    e = DIFF_HEAD_DIM

    @pl.when(i == 0)
    def _():
        for o in range(ATT_NEAR):
            g = jnp.broadcast_to(f_ref[0, :, o * t:(o + 2) * t], (t, 2 * t))
            nb_ref[o] = pltpu.roll(g, 0, 1, stride=1, stride_axis=0)[:, t:]

    q = q_ref[0]
    lane = lax.broadcasted_iota(jnp.int32, q.shape, 1)
    zero = jnp.zeros_like(q)
    qs = (jnp.where(lane < e, q, zero), jnp.where(lane >= e, q, zero))
    m_ref[...] = jnp.full_like(m_ref, MASKED)
    l_ref[...] = jnp.zeros_like(l_ref)
    acc_ref[...] = jnp.zeros_like(acc_ref)

    def tile(j, bias):
        kb = k_ref[0, pl.ds(pl.multiple_of(j * t, t), t), :]
        vtb = vt_ref[0, j]
        for a in range(2):
            s = _dot_nt(kb, qs[a])
            if bias is not None:
                s = s + bias
            m_old = m_ref[a]
            m_new = jnp.maximum(m_old, jnp.max(s, axis=0, keepdims=True))
            alpha = jnp.exp2(m_old - m_new)
            p = jnp.exp2(s - m_new)
            l_ref[a] = alpha * l_ref[a] + jnp.sum(p, axis=0, keepdims=True)
            acc_ref[a] = alpha * acc_ref[a] + _dot(vtb, p.astype(BF16))
            m_ref[a] = m_new

    n_far = jnp.maximum(i - (ATT_NEAR - 1), 0)

    def far_body(j, carry):
        tile(j, None)
        return carry

    def near_body(j, carry):
        tile(j, nb_ref[i - j])
        return carry

    lax.fori_loop(0, n_far, far_body, 0)
    lax.fori_loop(n_far, i + 1, near_body, 0)

    lam = (jnp.exp(jnp.sum(lam_ref[0:1, :] * lam_ref[1:2, :], axis=-1, keepdims=True))
           - jnp.exp(jnp.sum(lam_ref[2:3, :] * lam_ref[3:4, :], axis=-1, keepdims=True))
           + lam_init)
    ot = acc_ref[0] / l_ref[0] - lam * (acc_ref[1] / l_ref[1])
    ot = ot * lax.rsqrt(jnp.mean(ot * ot, axis=0, keepdims=True) + EPS)
    o_ref[0] = (ot.T * sub_ref[...] * (1.0 - lam_init)).astype(o_ref.dtype)


def _diff_bias(rel_bias, s):
    t = ATT_T
    n = ATT_NEAR * t
    assert n >= REL_MAX_DISTANCE + t - 1 and s >= n
    dist_bias = rel_bias[_t5_bucket(jnp.arange(n))].astype(F32)
    far = rel_bias[REL_BUCKETS - 1].astype(F32)
    f = jnp.concatenate([jnp.full((t, far.shape[0]), MASKED, F32), (dist_bias - far) * LOG2E])
    return f.T[:, None, :]


def _diff_attention(q, k, vt, rel_bias, lam4, subln, layer_idx):
    b, s, n = q.shape
    t = ATT_T
    w = 2 * DIFF_HEAD_DIM
    lam_init = 0.8 - 0.6 * math.exp(-0.3 * layer_idx)
    return pl.pallas_call(
        functools.partial(_diff_kernel, lam_init=lam_init),
        grid=(b, DIFF_HEADS, s // t),
        in_specs=[pl.BlockSpec((1, t, w), lambda bi, h, i: (bi, i, h)),
                  pl.BlockSpec((1, s, w), lambda bi, h, i: (bi, 0, h)),
                  pl.BlockSpec((1, s // t, w, t), lambda bi, h, i: (bi, 0, h, 0)),
                  pl.BlockSpec((1, 1, (ATT_NEAR + 1) * t), lambda bi, h, i: (h, 0, 0)),
                  pl.BlockSpec((4, DIFF_HEAD_DIM), lambda bi, h, i: (0, 0)),
                  pl.BlockSpec((1, w), lambda bi, h, i: (0, 0))],
        out_specs=pl.BlockSpec((1, t, w), lambda bi, h, i: (bi, i, h)),
        out_shape=jax.ShapeDtypeStruct((b, s, n), BF16),
        scratch_shapes=[pltpu.VMEM((2, 1, t), F32),
                        pltpu.VMEM((2, 1, t), F32),
                        pltpu.VMEM((2, w, t), F32),
                        pltpu.VMEM((ATT_NEAR, t, t), F32)],
        compiler_params=_params(("arbitrary", "arbitrary", "arbitrary")),
        name="diff_attention",
    )(q, k, vt, _diff_bias(rel_bias, s), lam4, subln)


def _mix1_kernel(x_ref, a_ref, w_ref, out_ref):
    out_ref[...] = x_ref[...] + _dot(a_ref[...], w_ref[...])


def _mix1(x2, a, w):
    t, d = x2.shape
    n = a.shape[-1]
    return pl.pallas_call(
        _mix1_kernel,
        grid=(t // PROJ_TM,),
        in_specs=[pl.BlockSpec((PROJ_TM, d), lambda i: (i, 0)),
                  pl.BlockSpec((PROJ_TM, n), lambda i: (i, 0)),
                  pl.BlockSpec((n, d), lambda i: (0, 0))],
        out_specs=pl.BlockSpec((PROJ_TM, d), lambda i: (i, 0)),
        out_shape=jax.ShapeDtypeStruct((t, d), F32),
        compiler_params=_params(("arbitrary",)),
        name="mix1",
    )(x2, a, w)


def kernel(x, rel_bias, norm_mix, norm_ffn, norm_final, w_in_ab, ret_gn, w_out_ab, w_in_c,
           lam_q1, lam_k1, lam_q2, lam_k2, diff_subln, w_out_c, w_ffn_in, conv_w, conv_b,
           w_ffn_out):
    b, s, d = x.shape
    depth = norm_mix.shape[0]
    assert depth == 2 and s % RET_ROWS == 0 and s % FFN_TM == 0 and s >= ATT_NEAR * ATT_T
    row = lambda v: v.reshape(1, -1)
    x2 = x.reshape(b * s, d)

    p0 = _proj0(x2, row(norm_mix[0]), w_in_ab[0].astype(BF16))
    p0 = p0.reshape(b, s, -1)
    ret = _retention(p0, row(ret_gn[0]), b, s).reshape(b * s, -1)
    outs, lses = zip(*[_dilated_pattern(p0, rel_bias, b, s, wdw, dil) for wdw, dil in DIL_PATTERNS])
    x2 = _mix0(x2, ret, outs, lses, w_out_ab[0].astype(BF16))
    x3 = _ffn(x2.reshape(b, s, d), row(norm_ffn[0]), w_ffn_in[0].astype(BF16), conv_w[0],
              row(conv_b[0]), w_ffn_out[0].astype(BF16), row(norm_final), False)

    nqk = DIFF_HEADS * 2 * DIFF_HEAD_DIM
    wc = w_in_c[0].astype(BF16)
    q, k, vt = _proj1(x3, row(norm_mix[1]), wc[:, :nqk], wc[:, nqk:2 * nqk], wc[:, 2 * nqk:].T)
    lam4 = jnp.stack([lam_q1[0], lam_k1[0], lam_q2[0], lam_k2[0]]).astype(F32)
    a = _diff_attention(q, k, vt, rel_bias, lam4, row(diff_subln[0]), 1)
    x2 = _mix1(x3.reshape(b * s, d), a.reshape(b * s, -1), w_out_c[0].astype(BF16))
    return _ffn(x2.reshape(b, s, d), row(norm_ffn[1]), w_ffn_in[1].astype(BF16), conv_w[1],
                row(conv_b[1]), w_ffn_out[1].astype(BF16), row(norm_final), True)
```

```python
import functools
import math

import jax
import jax.numpy as jnp
from jax import lax
from jax.experimental import pallas as pl
from jax.experimental.pallas import tpu as pltpu

F32 = jnp.float32
BF16 = jnp.bfloat16

EPS = 1e-6
MASKED = -1e30

RET_HEADS = 4
RET_DIM = 128
RET_CHUNK = 128
ROPE_BASE = 10000.0
DIL_HEADS = 8
DIL_HEAD_DIM = 64
DIL_PATTERNS = ((128, 1), (512, 4), (2048, 16))
DIL_BLOCK = 128
DIFF_HEADS = 8
DIFF_HEAD_DIM = 64
REL_BUCKETS = 32
REL_MAX_DISTANCE = 2048
D_FF = 2816
CONV_WIDTH = 3

LANES = 128
SUBLANES = 8
VMEM_BYTES_V7X = 64 * 1024 * 1024
VMEM_LIMIT = VMEM_BYTES_V7X - 8 * 1024 * 1024

PROJ_TM = 512
PROJ_TN = 512
RET_ROWS = 1024
FFN_TM = 512
FFN_CF = 256
ATT_T = 512
ATT_KC = 512
ATT_BOUND_SLACK = 1.0 + 2.0 ** -5
ATT_MIN_DENOM = 2.0 ** -60
ATT_MAX_DENOM = 2.0 ** 100
ATT_NEAR = -(-(REL_MAX_DISTANCE + ATT_T - 1) // ATT_T)
LOG2E = math.log2(math.e)
ATT_QSCALE = DIFF_HEAD_DIM ** -0.5 * LOG2E


def _params(sem):
    return pltpu.CompilerParams(dimension_semantics=sem, vmem_limit_bytes=VMEM_LIMIT)


def _rms(x, g):
    return x * lax.rsqrt(jnp.mean(x * x, axis=-1, keepdims=True) + EPS) * g


def _dot(a, b):
    return jnp.dot(a, b, preferred_element_type=F32)


def _dot_nt(a, b):
    return lax.dot_general(a, b, (((1,), (1,)), ((), ())), preferred_element_type=F32)


def _dot_tn(a, b):
    return lax.dot_general(a, b, (((0,), (0,)), ((), ())), preferred_element_type=F32)


def _proj0_kernel(x_ref, g_ref, w_ref, o_ref):
    h = _rms(x_ref[...], g_ref[...]).astype(BF16)
    n = o_ref.shape[-1]
    for c in range(0, n, PROJ_TN):
        o_ref[:, c:c + PROJ_TN] = _dot(h, w_ref[:, c:c + PROJ_TN]).astype(o_ref.dtype)


def _proj0(x2, g, w):
    t, d = x2.shape
    n = w.shape[1]
    return pl.pallas_call(
        _proj0_kernel,
        grid=(t // PROJ_TM,),
        in_specs=[pl.BlockSpec((PROJ_TM, d), lambda i: (i, 0)),
                  pl.BlockSpec((1, d), lambda i: (0, 0)),
                  pl.BlockSpec((d, n), lambda i: (0, 0))],
        out_specs=pl.BlockSpec((PROJ_TM, n), lambda i: (i, 0)),
        out_shape=jax.ShapeDtypeStruct((t, n), BF16),
        compiler_params=_params(("arbitrary",)),
        name="proj0",
    )(x2, g, w)


def _ret_kernel(q_ref, k_ref, v_ref, gate_ref, cos_ref, sin_ref, tab_ref, gn_ref, o_ref,
                state_ref):
    @pl.when(pl.program_id(2) == 0)
    def _():
        state_ref[...] = jnp.zeros_like(state_ref)

    decay = tab_ref[0, 0]
    xi = tab_ref[0, 1]
    zeta = tab_ref[0, 2]
    g_chunk = tab_ref[0, 3]
    gain = gn_ref[...]
    half = RET_DIM // 2
    for c in range(q_ref.shape[1] // RET_CHUNK):
        rows = pl.ds(c * RET_CHUNK, RET_CHUNK)
        cos = cos_ref[rows, :]
        sin = sin_ref[rows, :]
        q = q_ref[0, rows, :].astype(F32)
        k = k_ref[0, rows, :].astype(F32)
        q = q * cos + pltpu.roll(q, half, 1) * sin
        k = k * cos + pltpu.roll(k, half, 1) * sin
        v = v_ref[0, rows, :]
        state = state_ref[...]
        intra = _dot_nt(q.astype(BF16), k.astype(BF16)) * decay
        y = _dot(intra.astype(BF16), v) + _dot((q * xi).astype(BF16), state.astype(BF16))
        state_ref[...] = g_chunk * state + _dot_tn((k * zeta).astype(BF16), v)
        mu = jnp.mean(y, axis=-1, keepdims=True)
        yc = y - mu
        var = jnp.mean(yc * yc, axis=-1, keepdims=True)
        yn = yc * lax.rsqrt(var + EPS) * gain
        gate = gate_ref[0, rows, :].astype(F32)
        o_ref[0, rows, :] = (gate * jax.nn.sigmoid(gate) * yn).astype(o_ref.dtype)


def _retention_tables(s):
    c, e = RET_CHUNK, RET_DIM
    inv = ROPE_BASE ** (-jnp.arange(0, e, 2, dtype=F32) / e)
    ang = jnp.arange(s, dtype=F32)[:, None] * inv[None, :]
    cos = jnp.concatenate([jnp.cos(ang), jnp.cos(ang)], axis=-1)
    sin = jnp.concatenate([-jnp.sin(ang), jnp.sin(ang)], axis=-1)
    log_g = jnp.log1p(-jnp.exp2(-5.0 - jnp.arange(RET_HEADS, dtype=F32)))
    pos = jnp.arange(c, dtype=F32)
    rel = pos[:, None] - pos[None, :]
    causal = rel >= 0
    scale = e ** -0.5
    decay = jnp.where(causal, jnp.exp(jnp.where(causal, rel, 0.0)[None] * log_g[:, None, None]), 0.0)
    zeta = jnp.exp((c - 1 - pos)[None, :] * log_g[:, None])
    xi = jnp.exp((pos + 1)[None, :] * log_g[:, None])
    g_chunk = jnp.exp(c * log_g)
    tab = jnp.stack([
        decay * scale,
        jnp.broadcast_to(xi[:, :, None], (RET_HEADS, c, e)),
        jnp.broadcast_to(zeta[:, :, None] * scale, (RET_HEADS, c, e)),
        jnp.broadcast_to(g_chunk[:, None, None], (RET_HEADS, e, e)),
    ], axis=1)
    return cos, sin, tab


def _retention(p0, gn, b, s):
    cos, sin, tab = _retention_tables(s)
    e = RET_DIM
    col = lambda off: (lambda bi, h, t: (bi, t, off + h))
    return pl.pallas_call(
        _ret_kernel,
        grid=(b, RET_HEADS, s // RET_ROWS),
        in_specs=[pl.BlockSpec((1, RET_ROWS, e), col(0)),
                  pl.BlockSpec((1, RET_ROWS, e), col(RET_HEADS)),
                  pl.BlockSpec((1, RET_ROWS, e), col(2 * RET_HEADS)),
                  pl.BlockSpec((1, RET_ROWS, e), col(3 * RET_HEADS)),
                  pl.BlockSpec((RET_ROWS, e), lambda bi, h, t: (t, 0)),
                  pl.BlockSpec((RET_ROWS, e), lambda bi, h, t: (t, 0)),
                  pl.BlockSpec((1, 4, e, e), lambda bi, h, t: (h, 0, 0, 0)),
                  pl.BlockSpec((1, e), lambda bi, h, t: (0, h))],
        out_specs=pl.BlockSpec((1, RET_ROWS, e), lambda bi, h, t: (bi, t, h)),
        out_shape=jax.ShapeDtypeStruct((b, s, RET_HEADS * e), BF16),
        scratch_shapes=[pltpu.VMEM((e, e), F32)],
        compiler_params=_params(("arbitrary", "arbitrary", "arbitrary")),
        name="retention",
    )(p0, p0, p0, p0, cos, sin, tab, gn)


def _t5_bucket(dist):
    max_exact = REL_BUCKETS // 2
    d = jnp.maximum(dist.astype(F32), 1.0)
    large = max_exact + (jnp.log(d / max_exact) / math.log(REL_MAX_DISTANCE / max_exact)
                         * (REL_BUCKETS - max_exact))
    large = jnp.clip(large.astype(jnp.int32), max_exact, REL_BUCKETS - 1)
    return jnp.where(dist < max_exact, dist, large)


def _dil_kernel(q_ref, kp_ref, kc_ref, vp_ref, vc_ref, bias_ref, o_ref, lse_ref):
    first = pl.program_id(2) == 0
    blk, e = DIL_BLOCK, DIL_HEAD_DIM
    q = q_ref[0] * (e ** -0.5)
    kcat = jnp.concatenate([kp_ref[0], kc_ref[0]], axis=0)
    vcat = jnp.concatenate([vp_ref[0], vc_ref[0]], axis=0)
    in_prev = lax.broadcasted_iota(jnp.int32, (blk, 2 * blk), 1) < blk
    no_prev = jnp.logical_and(first, in_prev)
    for h in range(DIL_HEADS):
        cols = slice(h * e, (h + 1) * e)
        s = _dot_nt(q[:, cols], kcat[:, cols]) + bias_ref[h]
        s = jnp.where(no_prev, MASKED, s)
        m = jnp.max(s, axis=-1, keepdims=True)
        p = jnp.exp(s - m)
        l = jnp.sum(p, axis=-1, keepdims=True)
        o = _dot(p.astype(BF16), vcat[:, cols]) / l
        o_ref[0, :, cols] = o
        lse_ref[0, :, cols] = jnp.broadcast_to(m + jnp.log(l), (blk, e))


def _dil_bias(rel_bias, window, dil):
    blk = DIL_BLOCK
    span = window // dil
    i = jnp.arange(blk)[:, None]
    j = jnp.arange(2 * blk)[None, :]
    r = i + blk - j
    band = (r >= 0) & (r <= span)
    bias = rel_bias[_t5_bucket(jnp.maximum(r, 0) * dil)].astype(F32)
    bias = jnp.transpose(bias, (2, 0, 1))
    return jnp.where(band[None], bias, MASKED)


def _dilated_pattern(p0, rel_bias, b, s, window, dil):
    blk = DIL_BLOCK
    assert window // dil == blk and s % (dil * blk) == 0
    w = DIL_HEADS * DIL_HEAD_DIM
    n_in = p0.shape[-1]
    assert n_in % w == 0
    per_row = n_in // w
    base = 2 * RET_HEADS * 2 * RET_DIM // w
    nb = s // (dil * blk)
    view = p0.reshape(b, s // dil, dil * n_in)
    cur = lambda c: (lambda bi, r, n: (bi, n, r * per_row + base + c))
    prev = lambda c: (lambda bi, r, n: (bi, jnp.maximum(n - 1, 0), r * per_row + base + c))
    out_spec = pl.BlockSpec((1, blk, w), lambda bi, r, n: (bi, n, r))
    o, lse = pl.pallas_call(
        _dil_kernel,
        grid=(b, dil, nb),
        in_specs=[pl.BlockSpec((1, blk, w), cur(0)),
                  pl.BlockSpec((1, blk, w), prev(1)),
                  pl.BlockSpec((1, blk, w), cur(1)),
                  pl.BlockSpec((1, blk, w), prev(2)),
                  pl.BlockSpec((1, blk, w), cur(2)),
                  pl.BlockSpec((DIL_HEADS, blk, 2 * blk), lambda bi, r, n: (0, 0, 0))],
        out_specs=[out_spec, out_spec],
        out_shape=[jax.ShapeDtypeStruct((b, s // dil, dil * w), F32)] * 2,
        compiler_params=_params(("arbitrary", "arbitrary", "arbitrary")),
        name=f"dilated_d{dil}",
    )(view, view, view, view, view, _dil_bias(rel_bias, window, dil))
    return o.reshape(b * s, w), lse.reshape(b * s, w)


def _mix0_kernel(x_ref, ret_ref, o1_ref, o2_ref, o3_ref, l1_ref, l2_ref, l3_ref, w_ref, out_ref):
    l1, l2, l3 = l1_ref[...], l2_ref[...], l3_ref[...]
    m = jnp.maximum(jnp.maximum(l1, l2), l3)
    w1, w2, w3 = jnp.exp(l1 - m), jnp.exp(l2 - m), jnp.exp(l3 - m)
    dil = (w1 * o1_ref[...] + w2 * o2_ref[...] + w3 * o3_ref[...]) / (w1 + w2 + w3)
    na = ret_ref.shape[-1]
    y = _dot(ret_ref[...], w_ref[:na, :]) + _dot(dil.astype(BF16), w_ref[na:, :])
    out_ref[...] = x_ref[...] + y


def _mix0(x2, ret, outs, lses, w):
    t, d = x2.shape
    na, nb = ret.shape[-1], outs[0].shape[-1]
    row = lambda n: pl.BlockSpec((PROJ_TM, n), lambda i: (i, 0))
    return pl.pallas_call(
        _mix0_kernel,
        grid=(t // PROJ_TM,),
        in_specs=[row(d), row(na)] + [row(nb)] * 6 + [pl.BlockSpec((na + nb, d), lambda i: (0, 0))],
        out_specs=row(d),
        out_shape=jax.ShapeDtypeStruct((t, d), F32),
        compiler_params=_params(("arbitrary",)),
        name="mix0",
    )(x2, ret, *outs, *lses, w)


def _ffn_kernel(x_ref, g_ref, win_ref, cw_ref, cb_ref, wout_ref, fg_ref, o_ref,
                ubuf, carry, gbuf, *, final_norm):
    tm = x_ref.shape[1]
    halo = SUBLANES

    @pl.when(pl.program_id(1) == 0)
    def _():
        carry[...] = jnp.zeros_like(carry)

    x = x_ref[0]
    h = _rms(x, g_ref[...]).astype(BF16)

    def conv(part, col):
        cols = slice(col, col + FFN_CF)
        u = _dot(h, win_ref[:, cols])
        ubuf[part, 0:halo, :] = carry[:, cols]
        ubuf[part, halo:halo + tm, :] = u
        carry[:, cols] = u[tm - halo:, :]
        u1 = ubuf[part, halo - 1:halo - 1 + tm, :]
        u2 = ubuf[part, halo - 2:halo - 2 + tm, :]
        return (u * cw_ref[2:3, cols] + u1 * cw_ref[1:2, cols] + u2 * cw_ref[0:1, cols]
                + cb_ref[:, cols])

    for c in range(D_FF // FFN_CF):
        gate = conv(0, c * FFN_CF)
        up = conv(1, D_FF + c * FFN_CF)
        act = 0.5 * gate * (1.0 + lax.erf(gate * (2.0 ** -0.5)))
        gbuf[:, c * FFN_CF:(c + 1) * FFN_CF] = (act * up).astype(BF16)

    y = x + _dot(gbuf[...], wout_ref[...])
    if final_norm:
        y = _rms(y, fg_ref[...])
    o_ref[0] = y


def _ffn(x3, g, w_in, conv_w, conv_b, w_out, final_gain, final_norm):
    b, s, d = x3.shape
    f2 = w_in.shape[1]
    whole = pl.BlockSpec(memory_space=pltpu.VMEM)
    return pl.pallas_call(
        functools.partial(_ffn_kernel, final_norm=final_norm),
        grid=(b, s // FFN_TM),
        in_specs=[pl.BlockSpec((1, FFN_TM, d), lambda bi, t: (bi, t, 0)),
                  whole, whole, whole, whole, whole, whole],
        out_specs=pl.BlockSpec((1, FFN_TM, d), lambda bi, t: (bi, t, 0)),
        out_shape=jax.ShapeDtypeStruct((b, s, d), F32),
        scratch_shapes=[pltpu.VMEM((2, FFN_TM + SUBLANES, FFN_CF), F32),
                        pltpu.VMEM((SUBLANES, f2), F32),
                        pltpu.VMEM((FFN_TM, D_FF), BF16)],
        compiler_params=_params(("arbitrary", "arbitrary")),
        name="ffn_final" if final_norm else "ffn",
    )(x3, g, w_in, conv_w, conv_b, w_out, final_gain)


def _proj1_kernel(x_ref, g_ref, wq_ref, wk_ref, wvt_ref, q_ref, k_ref, vt_ref):
    h = _rms(x_ref[0], g_ref[...]).astype(BF16)
    n = q_ref.shape[-1]
    for c in range(0, n, PROJ_TN):
        cols = slice(c, c + PROJ_TN)
        q_ref[0, :, cols] = (_dot(h, wq_ref[:, cols]) * ATT_QSCALE).astype(BF16)
        k_ref[0, :, cols] = _dot(h, wk_ref[:, cols]).astype(BF16)
        vt_ref[0, 0, cols, :] = _dot_nt(wvt_ref[cols, :], h).astype(BF16)


def _proj1(x3, g, wq, wk, wvt):
    b, s, d = x3.shape
    n = wq.shape[1]
    t = ATT_T
    wspec = pl.BlockSpec((d, n), lambda bi, i: (0, 0))
    return pl.pallas_call(
        _proj1_kernel,
        grid=(b, s // t),
        in_specs=[pl.BlockSpec((1, t, d), lambda bi, i: (bi, i, 0)),
                  pl.BlockSpec((1, d), lambda bi, i: (0, 0)),
                  wspec, wspec, pl.BlockSpec((n, d), lambda bi, i: (0, 0))],
        out_specs=[pl.BlockSpec((1, t, n), lambda bi, i: (bi, i, 0)),
                   pl.BlockSpec((1, t, n), lambda bi, i: (bi, i, 0)),
                   pl.BlockSpec((1, 1, n, t), lambda bi, i: (bi, i, 0, 0))],
        out_shape=[jax.ShapeDtypeStruct((b, s, n), BF16),
                   jax.ShapeDtypeStruct((b, s, n), BF16),
                   jax.ShapeDtypeStruct((b, s // t, n, t), BF16)],
        compiler_params=_params(("arbitrary", "arbitrary")),
        name="proj1",
    )(x3, g, wq, wk, wvt)


def _diff_kernel(q_ref, k_ref, vt_ref, f_ref, lam_ref, sub_ref, o_ref,
                 qm_ref, km_ref, m_ref, l_ref, l8_ref, acc_ref, nb_ref, *, lam_init):
    i = pl.program_id(2)
    t = ATT_T
    e = DIFF_HEAD_DIM
    half = lax.broadcasted_iota(jnp.int32, (SUBLANES, 2 * e), 1) // e
    sel = (lax.broadcasted_iota(jnp.int32, (SUBLANES, 2 * e), 0) == half).astype(BF16)

    @pl.when(i == 0)
    def _():
        for o in range(ATT_NEAR):
            g = jnp.broadcast_to(f_ref[0, 0:1, o * t:(o + 2) * t], (t, 2 * t))
            nb_ref[o] = pltpu.roll(g, 0, 1, stride=1, stride_axis=0)[:, t:]
        nb_ref[ATT_NEAR] = jnp.zeros((t, t), F32)
        kmax = jnp.zeros((SUBLANES, t), F32)
        for c in range(k_ref.shape[1] // t):
            kk = k_ref[0, c * t:(c + 1) * t, :].astype(F32)
            kmax = jnp.maximum(kmax, _dot_nt(sel, (kk * kk).astype(BF16)))
        for a in range(2):
            km_ref[a] = jnp.broadcast_to(jnp.max(kmax[a:a + 1, :], axis=1, keepdims=True), (1, t))

    q = q_ref[0]
    lane = lax.broadcasted_iota(jnp.int32, q.shape, 1)
    zero = jnp.zeros_like(q)
    qm_ref[0] = jnp.where(lane < e, q, zero)
    qm_ref[1] = jnp.where(lane >= e, q, zero)
    qf = q.astype(F32)
    qn2 = _dot_nt(sel, (qf * qf).astype(BF16))
    bias_max = f_ref[0, 1:2, 0:t]
    n_far = jnp.maximum(i - (ATT_NEAR - 1), 0)

    l8_ref[...] = jnp.zeros_like(l8_ref)
    acc_ref[...] = jnp.zeros_like(acc_ref)
    shift = [jnp.sqrt(qn2[a:a + 1, :] * km_ref[a]) * ATT_BOUND_SLACK + bias_max for a in range(2)]

    kc = ATT_KC

    def fast_tiles(j0, ntiles, biased):
        for a in range(2):
            pv = None
            l8 = None
            for tt in range(ntiles):
                j = j0 + tt
                for c in range(0, t, kc):
                    kb = k_ref[0, pl.ds(pl.multiple_of(j * t + c, kc), kc), :]
                    s = _dot_nt(kb, qm_ref[a])
                    if biased:
                        s = s + nb_ref[jnp.minimum(i - j, ATT_NEAR), c:c + kc, :]
                    p = jnp.exp2(s - shift[a])
                    part = jnp.sum(p.reshape(kc // SUBLANES, SUBLANES, t), axis=0)
                    d = _dot(vt_ref[0, j, :, c:c + kc], p.astype(BF16))
                    l8 = part if l8 is None else l8 + part
                    pv = d if pv is None else pv + d
            l8_ref[a] += l8
            acc_ref[a] += pv

    def fast_far_pair(u, carry):
        fast_tiles(2 * u, 2, False)
        return carry

    def fast_near_pair(u, carry):
        fast_tiles(2 * u, 2, True)
        return carry

    def fast_near(j, carry):
        fast_tiles(j, 1, True)
        return carry

    n = i + 1
    lax.fori_loop(0, n_far // 2, fast_far_pair, 0)
    lax.fori_loop(n_far // 2, n // 2, fast_near_pair, 0)
    lax.fori_loop(2 * (n // 2), n, fast_near, 0)
    for a in range(2):
        l_ref[a] = jnp.sum(l8_ref[a], axis=0, keepdims=True)
    l_min = jnp.min(jnp.minimum(l_ref[0], l_ref[1]))
    l_max = jnp.max(jnp.maximum(l_ref[0], l_ref[1]))
    in_range = jnp.logical_and(l_min >= ATT_MIN_DENOM, l_max <= ATT_MAX_DENOM)

    @pl.when(jnp.logical_not(in_range))
    def _():
        m_ref[...] = jnp.full_like(m_ref, MASKED)
        l_ref[...] = jnp.zeros_like(l_ref)
        acc_ref[...] = jnp.zeros_like(acc_ref)

        def online_tile(j, bias):
            kb = k_ref[0, pl.ds(pl.multiple_of(j * t, t), t), :]
            vtb = vt_ref[0, j]
            for a in range(2):
                s = _dot_nt(kb, qm_ref[a])
                if bias is not None:
                    s = s + bias
                m_old = m_ref[a]
                m_new = jnp.maximum(m_old, jnp.max(s, axis=0, keepdims=True))
                alpha = jnp.exp2(m_old - m_new)
                p = jnp.exp2(s - m_new)
                l_ref[a] = alpha * l_ref[a] + jnp.sum(p, axis=0, keepdims=True)
                acc_ref[a] = alpha * acc_ref[a] + _dot(vtb, p.astype(BF16))
                m_ref[a] = m_new

        def online_far(j, carry):
            online_tile(j, None)
            return carry

        def online_near(j, carry):
            online_tile(j, nb_ref[i - j])
            return carry

        lax.fori_loop(0, n_far, online_far, 0)
        lax.fori_loop(n_far, i + 1, online_near, 0)

    lam = (jnp.exp(jnp.sum(lam_ref[0:1, :] * lam_ref[1:2, :], axis=-1, keepdims=True))
           - jnp.exp(jnp.sum(lam_ref[2:3, :] * lam_ref[3:4, :], axis=-1, keepdims=True))
           + lam_init)
    ot = acc_ref[0] / l_ref[0] - lam * (acc_ref[1] / l_ref[1])
    ot = ot * lax.rsqrt(jnp.mean(ot * ot, axis=0, keepdims=True) + EPS)
    o_ref[0] = (ot.T * sub_ref[...] * (1.0 - lam_init)).astype(o_ref.dtype)


def _diff_bias(rel_bias, s):
    t = ATT_T
    n = ATT_NEAR * t
    assert n >= REL_MAX_DISTANCE + t - 1 and s >= n
    dist_bias = rel_bias[_t5_bucket(jnp.arange(n))].astype(F32)
    far = rel_bias[REL_BUCKETS - 1].astype(F32)
    near = (dist_bias - far) * LOG2E
    f = jnp.concatenate([jnp.full((t, far.shape[0]), MASKED, F32), near])
    top = jnp.broadcast_to(jnp.maximum(jnp.max(near, axis=0), 0.0), f.shape)
    return jnp.stack([f.T, top.T], axis=1)


def _diff_attention(q, k, vt, rel_bias, lam4, subln, layer_idx):
    b, s, n = q.shape
    t = ATT_T
    w = 2 * DIFF_HEAD_DIM
    lam_init = 0.8 - 0.6 * math.exp(-0.3 * layer_idx)
    return pl.pallas_call(
        functools.partial(_diff_kernel, lam_init=lam_init),
        grid=(b, DIFF_HEADS, s // t),
        in_specs=[pl.BlockSpec((1, t, w), lambda bi, h, i: (bi, i, h)),
                  pl.BlockSpec((1, s, w), lambda bi, h, i: (bi, 0, h)),
                  pl.BlockSpec((1, s // t, w, t), lambda bi, h, i: (bi, 0, h, 0)),
                  pl.BlockSpec((1, 2, (ATT_NEAR + 1) * t), lambda bi, h, i: (h, 0, 0)),
                  pl.BlockSpec((4, DIFF_HEAD_DIM), lambda bi, h, i: (0, 0)),
                  pl.BlockSpec((1, w), lambda bi, h, i: (0, 0))],
        out_specs=pl.BlockSpec((1, t, w), lambda bi, h, i: (bi, i, h)),
        out_shape=jax.ShapeDtypeStruct((b, s, n), BF16),
        scratch_shapes=[pltpu.VMEM((2, t, w), BF16),
                        pltpu.VMEM((2, 1, t), F32),
                        pltpu.VMEM((2, 1, t), F32),
                        pltpu.VMEM((2, 1, t), F32),
                        pltpu.VMEM((2, SUBLANES, t), F32),
                        pltpu.VMEM((2, w, t), F32),
                        pltpu.VMEM((ATT_NEAR + 1, t, t), F32)],
        compiler_params=_params(("arbitrary", "arbitrary", "arbitrary")),
        name="diff_attention",
    )(q, k, vt, _diff_bias(rel_bias, s), lam4, subln)


def _mix1_kernel(x_ref, a_ref, w_ref, out_ref):
    out_ref[...] = x_ref[...] + _dot(a_ref[...], w_ref[...])


def _mix1(x2, a, w):
    t, d = x2.shape
    n = a.shape[-1]
    return pl.pallas_call(
        _mix1_kernel,
        grid=(t // PROJ_TM,),
        in_specs=[pl.BlockSpec((PROJ_TM, d), lambda i: (i, 0)),
                  pl.BlockSpec((PROJ_TM, n), lambda i: (i, 0)),
                  pl.BlockSpec((n, d), lambda i: (0, 0))],
        out_specs=pl.BlockSpec((PROJ_TM, d), lambda i: (i, 0)),
        out_shape=jax.ShapeDtypeStruct((t, d), F32),
        compiler_params=_params(("arbitrary",)),
        name="mix1",
    )(x2, a, w)


def kernel(x, rel_bias, norm_mix, norm_ffn, norm_final, w_in_ab, ret_gn, w_out_ab, w_in_c,
           lam_q1, lam_k1, lam_q2, lam_k2, diff_subln, w_out_c, w_ffn_in, conv_w, conv_b,
           w_ffn_out):
    b, s, d = x.shape
    depth = norm_mix.shape[0]
    assert depth == 2 and s % RET_ROWS == 0 and s % FFN_TM == 0 and s >= ATT_NEAR * ATT_T
    row = lambda v: v.reshape(1, -1)
    x2 = x.reshape(b * s, d)

    p0 = _proj0(x2, row(norm_mix[0]), w_in_ab[0].astype(BF16))
    p0 = p0.reshape(b, s, -1)
    ret = _retention(p0, row(ret_gn[0]), b, s).reshape(b * s, -1)
    outs, lses = zip(*[_dilated_pattern(p0, rel_bias, b, s, wdw, dil) for wdw, dil in DIL_PATTERNS])
    x2 = _mix0(x2, ret, outs, lses, w_out_ab[0].astype(BF16))
    x3 = _ffn(x2.reshape(b, s, d), row(norm_ffn[0]), w_ffn_in[0].astype(BF16), conv_w[0],
              row(conv_b[0]), w_ffn_out[0].astype(BF16), row(norm_final), False)

    nqk = DIFF_HEADS * 2 * DIFF_HEAD_DIM
    wc = w_in_c[0].astype(BF16)
    q, k, vt = _proj1(x3, row(norm_mix[1]), wc[:, :nqk], wc[:, nqk:2 * nqk], wc[:, 2 * nqk:].T)
    lam4 = jnp.stack([lam_q1[0], lam_k1[0], lam_q2[0], lam_k2[0]]).astype(F32)
    a = _diff_attention(q, k, vt, rel_bias, lam4, row(diff_subln[0]), 1)
    x2 = _mix1(x3.reshape(b * s, d), a.reshape(b * s, -1), w_out_c[0].astype(BF16))
    return _ffn(x2.reshape(b, s, d), row(norm_ffn[1]), w_ffn_in[1].astype(BF16), conv_w[1],
                row(conv_b[1]), w_ffn_out[1].astype(BF16), row(norm_final), True)
```

```python
import functools
import math

import jax
import jax.numpy as jnp
from jax import lax
from jax.experimental import pallas as pl
from jax.experimental.pallas import tpu as pltpu

F32 = jnp.float32
BF16 = jnp.bfloat16

EPS = 1e-6
MASKED = -1e30

RET_HEADS = 4
RET_DIM = 128
RET_CHUNK = 128
ROPE_BASE = 10000.0
DIL_HEADS = 8
DIL_HEAD_DIM = 64
DIL_PATTERNS = ((128, 1), (512, 4), (2048, 16))
DIL_BLOCK = 128
DIFF_HEADS = 8
DIFF_HEAD_DIM = 64
REL_BUCKETS = 32
REL_MAX_DISTANCE = 2048
D_FF = 2816
CONV_WIDTH = 3
DIL_W = DIL_HEADS * DIL_HEAD_DIM
RET_W = 4 * RET_HEADS * RET_DIM
DIL_DILATIONS = tuple(d for _, d in DIL_PATTERNS)

LANES = 128
SUBLANES = 8
VMEM_BYTES_V7X = 64 * 1024 * 1024
VMEM_LIMIT = VMEM_BYTES_V7X - 8 * 1024 * 1024

PROJ_TM = 512
PROJ_TN = 512
RET_ROWS = 1024
FFN_TM = 512
FFN_CF = 256
ATT_T = 512
ATT_KC = 512
ATT_BOUND_SLACK = 1.0 + 2.0 ** -5
ATT_MIN_DENOM = 2.0 ** -60
ATT_MAX_DENOM = 2.0 ** 100
ATT_NEAR = -(-(REL_MAX_DISTANCE + ATT_T - 1) // ATT_T)
LOG2E = math.log2(math.e)
ATT_QSCALE = DIFF_HEAD_DIM ** -0.5 * LOG2E


def _params(sem):
    return pltpu.CompilerParams(dimension_semantics=sem, vmem_limit_bytes=VMEM_LIMIT)


def _rms(x, g):
    return x * lax.rsqrt(jnp.mean(x * x, axis=-1, keepdims=True) + EPS) * g


def _dot(a, b):
    return jnp.dot(a, b, preferred_element_type=F32)


def _dot_nt(a, b):
    return lax.dot_general(a, b, (((1,), (1,)), ((), ())), preferred_element_type=F32)


def _dot_tn(a, b):
    return lax.dot_general(a, b, (((0,), (0,)), ((), ())), preferred_element_type=F32)


def _proj0_kernel(x_ref, g_ref, w_ref, a_ref, *rest):
    d_refs, dbuf = rest[:-1], rest[-1]
    tm = x_ref.shape[1]
    h = _rms(x_ref[0], g_ref[...]).astype(BF16)
    na = a_ref.shape[-1]
    for c in range(0, na, PROJ_TN):
        a_ref[0, :, c:c + PROJ_TN] = _dot(h, w_ref[:, c:c + PROJ_TN]).astype(BF16)
    nd = w_ref.shape[1] - na
    for c in range(0, nd, PROJ_TN):
        res = _dot(h, w_ref[:, na + c:na + c + PROJ_TN])
        for s in range(PROJ_TN // LANES):
            dbuf[(c // LANES) + s] = res[:, s * LANES:(s + 1) * LANES]
    for dref, d in zip(d_refs, DIL_DILATIONS):
        rows = tm // d
        for s in range(nd // LANES):
            cols = slice(s * LANES, (s + 1) * LANES)
            if d == 1:
                dref[0, 0, :, cols] = dbuf[s].astype(BF16)
            else:
                for r in range(d):
                    dref[0, r, :, cols] = dbuf[s, pl.ds(r, rows, stride=d), :].astype(BF16)


def _proj0(x3, g, w):
    b, s, dm = x3.shape
    n = w.shape[1]
    nd = n - RET_W
    tm = PROJ_TM
    d_specs = [pl.BlockSpec((1, d, tm // d, nd), lambda bi, i: (bi, 0, i, 0)) for d in DIL_DILATIONS]
    d_shapes = [jax.ShapeDtypeStruct((b, d, s // d, nd), BF16) for d in DIL_DILATIONS]
    return pl.pallas_call(
        _proj0_kernel,
        grid=(b, s // tm),
        in_specs=[pl.BlockSpec((1, tm, dm), lambda bi, i: (bi, i, 0)),
                  pl.BlockSpec((1, dm), lambda bi, i: (0, 0)),
                  pl.BlockSpec((dm, n), lambda bi, i: (0, 0))],
        out_specs=[pl.BlockSpec((1, tm, RET_W), lambda bi, i: (bi, i, 0))] + d_specs,
        out_shape=[jax.ShapeDtypeStruct((b, s, RET_W), BF16)] + d_shapes,
        scratch_shapes=[pltpu.VMEM((nd // LANES, tm, LANES), F32)],
        compiler_params=_params(("arbitrary", "arbitrary")),
        name="proj0",
    )(x3, g, w)


def _ret_kernel(q_ref, k_ref, v_ref, gate_ref, cos_ref, sin_ref, tab_ref, gn_ref, o_ref,
                state_ref):
    @pl.when(pl.program_id(2) == 0)
    def _():
        state_ref[...] = jnp.zeros_like(state_ref)

    decay = tab_ref[0, 0]
    xi = tab_ref[0, 1]
    zeta = tab_ref[0, 2]
    g_chunk = tab_ref[0, 3]
    gain = gn_ref[...]
    half = RET_DIM // 2
    for c in range(q_ref.shape[1] // RET_CHUNK):
        rows = pl.ds(c * RET_CHUNK, RET_CHUNK)
        cos = cos_ref[rows, :]
        sin = sin_ref[rows, :]
        q = q_ref[0, rows, :].astype(F32)
        k = k_ref[0, rows, :].astype(F32)
        q = q * cos + pltpu.roll(q, half, 1) * sin
        k = k * cos + pltpu.roll(k, half, 1) * sin
        v = v_ref[0, rows, :]
        state = state_ref[...]
        intra = _dot_nt(q.astype(BF16), k.astype(BF16)) * decay
        y = _dot(intra.astype(BF16), v) + _dot((q * xi).astype(BF16), state.astype(BF16))
        state_ref[...] = g_chunk * state + _dot_tn((k * zeta).astype(BF16), v)
        mu = jnp.mean(y, axis=-1, keepdims=True)
        yc = y - mu
        var = jnp.mean(yc * yc, axis=-1, keepdims=True)
        yn = yc * lax.rsqrt(var + EPS) * gain
        gate = gate_ref[0, rows, :].astype(F32)
        o_ref[0, rows, :] = (gate * jax.nn.sigmoid(gate) * yn).astype(o_ref.dtype)


def _retention_tables(s):
    c, e = RET_CHUNK, RET_DIM
    inv = ROPE_BASE ** (-jnp.arange(0, e, 2, dtype=F32) / e)
    ang = jnp.arange(s, dtype=F32)[:, None] * inv[None, :]
    cos = jnp.concatenate([jnp.cos(ang), jnp.cos(ang)], axis=-1)
    sin = jnp.concatenate([-jnp.sin(ang), jnp.sin(ang)], axis=-1)
    log_g = jnp.log1p(-jnp.exp2(-5.0 - jnp.arange(RET_HEADS, dtype=F32)))
    pos = jnp.arange(c, dtype=F32)
    rel = pos[:, None] - pos[None, :]
    causal = rel >= 0
    scale = e ** -0.5
    decay = jnp.where(causal, jnp.exp(jnp.where(causal, rel, 0.0)[None] * log_g[:, None, None]), 0.0)
    zeta = jnp.exp((c - 1 - pos)[None, :] * log_g[:, None])
    xi = jnp.exp((pos + 1)[None, :] * log_g[:, None])
    g_chunk = jnp.exp(c * log_g)
    tab = jnp.stack([
        decay * scale,
        jnp.broadcast_to(xi[:, :, None], (RET_HEADS, c, e)),
        jnp.broadcast_to(zeta[:, :, None] * scale, (RET_HEADS, c, e)),
        jnp.broadcast_to(g_chunk[:, None, None], (RET_HEADS, e, e)),
    ], axis=1)
    return cos, sin, tab


def _retention(p0, gn, b, s):
    cos, sin, tab = _retention_tables(s)
    e = RET_DIM
    col = lambda off: (lambda bi, h, t: (bi, t, off + h))
    return pl.pallas_call(
        _ret_kernel,
        grid=(b, RET_HEADS, s // RET_ROWS),
        in_specs=[pl.BlockSpec((1, RET_ROWS, e), col(0)),
                  pl.BlockSpec((1, RET_ROWS, e), col(RET_HEADS)),
                  pl.BlockSpec((1, RET_ROWS, e), col(2 * RET_HEADS)),
                  pl.BlockSpec((1, RET_ROWS, e), col(3 * RET_HEADS)),
                  pl.BlockSpec((RET_ROWS, e), lambda bi, h, t: (t, 0)),
                  pl.BlockSpec((RET_ROWS, e), lambda bi, h, t: (t, 0)),
                  pl.BlockSpec((1, 4, e, e), lambda bi, h, t: (h, 0, 0, 0)),
                  pl.BlockSpec((1, e), lambda bi, h, t: (0, h))],
        out_specs=pl.BlockSpec((1, RET_ROWS, e), lambda bi, h, t: (bi, t, h)),
        out_shape=jax.ShapeDtypeStruct((b, s, RET_HEADS * e), BF16),
        scratch_shapes=[pltpu.VMEM((e, e), F32)],
        compiler_params=_params(("arbitrary", "arbitrary", "arbitrary")),
        name="retention",
    )(p0, p0, p0, p0, cos, sin, tab, gn)


def _t5_bucket(dist):
    max_exact = REL_BUCKETS // 2
    d = jnp.maximum(dist.astype(F32), 1.0)
    large = max_exact + (jnp.log(d / max_exact) / math.log(REL_MAX_DISTANCE / max_exact)
                         * (REL_BUCKETS - max_exact))
    large = jnp.clip(large.astype(jnp.int32), max_exact, REL_BUCKETS - 1)
    return jnp.where(dist < max_exact, dist, large)


def _dil_kernel(q_ref, kp_ref, kc_ref, vp_ref, vc_ref, g_ref, o_ref, lse_ref, bias_ref):
    first = pl.program_id(2) == 0
    blk, e = DIL_BLOCK, DIL_HEAD_DIM

    @pl.when(jnp.logical_and(jnp.logical_and(pl.program_id(0) == 0, pl.program_id(1) == 0), first))
    def _():
        for h in range(DIL_HEADS):
            gb = jnp.broadcast_to(g_ref[h:h + 1, :], (blk, 4 * blk))
            bias_ref[h] = pltpu.roll(gb, 0, 1, stride=1, stride_axis=0)[:, :2 * blk]

    q = q_ref[0, 0] * (e ** -0.5)
    kcat = jnp.concatenate([kp_ref[0, 0], kc_ref[0, 0]], axis=0)
    vcat = jnp.concatenate([vp_ref[0, 0], vc_ref[0, 0]], axis=0)
    in_prev = lax.broadcasted_iota(jnp.int32, (blk, 2 * blk), 1) < blk
    no_prev = jnp.logical_and(first, in_prev)
    lane = lax.broadcasted_iota(jnp.int32, (blk, LANES), 1)
    lse_all = jnp.zeros((blk, LANES), F32)
    for h in range(DIL_HEADS):
        cols = slice(h * e, (h + 1) * e)
        s = _dot_nt(q[:, cols], kcat[:, cols]) + bias_ref[h]
        s = jnp.where(no_prev, MASKED, s)
        m = jnp.max(s, axis=-1, keepdims=True)
        p = jnp.exp(s - m)
        l = jnp.sum(p, axis=-1, keepdims=True)
        o_ref[0, 0, :, cols] = (_dot(p.astype(BF16), vcat[:, cols]) / l).astype(o_ref.dtype)
        lse_all = jnp.where(lane == h, m + jnp.log(l), lse_all)
    lse_ref[0, 0] = lse_all


def _dil_bias_vector(rel_bias, window, dil):
    blk = DIL_BLOCK
    span = window // dil
    r = blk - jnp.arange(4 * blk)
    band = (r >= 0) & (r <= span)
    bias = rel_bias[_t5_bucket(jnp.maximum(r, 0) * dil)].astype(F32)
    return jnp.where(band[:, None], bias, MASKED).T


def _dilated_pattern(dq, rel_bias, window, dil):
    b, d, sd, _ = dq.shape
    blk = DIL_BLOCK
    assert d == dil and window // dil == blk and sd % blk == 0
    cur = lambda c: (lambda bi, r, n: (bi, r, n, c))
    prev = lambda c: (lambda bi, r, n: (bi, r, jnp.maximum(n - 1, 0), c))
    spec = lambda im: pl.BlockSpec((1, 1, blk, DIL_W), im)
    return pl.pallas_call(
        _dil_kernel,
        grid=(b, dil, sd // blk),
        in_specs=[spec(cur(0)), spec(prev(1)), spec(cur(1)), spec(prev(2)), spec(cur(2)),
                  pl.BlockSpec((DIL_HEADS, 4 * blk), lambda bi, r, n: (0, 0))],
        out_specs=[pl.BlockSpec((1, 1, blk, DIL_W), lambda bi, r, n: (bi, r, n, 0)),
                   pl.BlockSpec((1, 1, blk, LANES), lambda bi, r, n: (bi, r, n, 0))],
        out_shape=[jax.ShapeDtypeStruct((b, dil, sd, DIL_W), BF16),
                   jax.ShapeDtypeStruct((b, dil, sd, LANES), F32)],
        scratch_shapes=[pltpu.VMEM((DIL_HEADS, blk, 2 * blk), F32)],
        compiler_params=_params(("arbitrary", "arbitrary", "arbitrary")),
        name=f"dilated_d{dil}",
    )(dq, dq, dq, dq, dq, _dil_bias_vector(rel_bias, window, dil))


def _mix0_kernel(x_ref, ret_ref, *rest):
    np_ = len(DIL_DILATIONS)
    o_refs, l_refs = rest[:np_], rest[np_:2 * np_]
    w_ref, out_ref, obuf, lbuf, dbuf = rest[2 * np_:]
    tm = x_ref.shape[1]
    nslab = DIL_W // LANES
    for pi, d in enumerate(DIL_DILATIONS):
        rows = tm // d
        for r in range(d):
            dst = pl.ds(r, rows, stride=d) if d > 1 else pl.ds(0, tm)
            lbuf[pi, dst, :] = l_refs[pi][0, r]
            for s in range(nslab):
                obuf[pi, s, dst, :] = o_refs[pi][0, r, :, s * LANES:(s + 1) * LANES].astype(F32)
    lses = [lbuf[pi] for pi in range(np_)]
    m = functools.reduce(jnp.maximum, lses)
    es = [jnp.exp(l - m) for l in lses]
    inv = 1.0 / functools.reduce(lambda a, b: a + b, es)
    row = lax.broadcasted_iota(jnp.int32, (LANES, DIL_W), 0)
    col = lax.broadcasted_iota(jnp.int32, (LANES, DIL_W), 1)
    spread = (col // DIL_HEAD_DIM == row).astype(BF16)
    wide = []
    for ex in es:
        wgt = ex * inv
        hi = wgt.astype(BF16)
        lo = (wgt - hi.astype(F32)).astype(BF16)
        wide.append(_dot(hi, spread) + _dot(lo, spread))
    for s in range(nslab):
        cols = slice(s * LANES, (s + 1) * LANES)
        acc = wide[0][:, cols] * obuf[0, s]
        for pi in range(1, np_):
            acc = acc + wide[pi][:, cols] * obuf[pi, s]
        dbuf[:, cols] = acc.astype(BF16)
    na = ret_ref.shape[-1]
    y = _dot(ret_ref[0], w_ref[:na, :]) + _dot(dbuf[...], w_ref[na:, :])
    out_ref[0] = x_ref[0] + y


def _mix0(x3, ret, outs, lses, w):
    b, s, dm = x3.shape
    na = ret.shape[-1]
    tm = PROJ_TM
    o_specs = [pl.BlockSpec((1, d, tm // d, DIL_W), lambda bi, i: (bi, 0, i, 0)) for d in DIL_DILATIONS]
    l_specs = [pl.BlockSpec((1, d, tm // d, LANES), lambda bi, i: (bi, 0, i, 0)) for d in DIL_DILATIONS]
    np_ = len(DIL_DILATIONS)
    return pl.pallas_call(
        _mix0_kernel,
        grid=(b, s // tm),
        in_specs=[pl.BlockSpec((1, tm, dm), lambda bi, i: (bi, i, 0)),
                  pl.BlockSpec((1, tm, na), lambda bi, i: (bi, i, 0))] + o_specs + l_specs
                 + [pl.BlockSpec((na + DIL_W, dm), lambda bi, i: (0, 0))],
        out_specs=pl.BlockSpec((1, tm, dm), lambda bi, i: (bi, i, 0)),
        out_shape=jax.ShapeDtypeStruct((b, s, dm), F32),
        scratch_shapes=[pltpu.VMEM((np_, DIL_W // LANES, tm, LANES), F32),
                        pltpu.VMEM((np_, tm, LANES), F32),
                        pltpu.VMEM((tm, DIL_W), BF16)],
        compiler_params=_params(("arbitrary", "arbitrary")),
        name="mix0",
    )(x3, ret, *outs, *lses, w)


def _ffn_kernel(x_ref, g_ref, win_ref, cw_ref, cb_ref, wout_ref, fg_ref, o_ref,
                ubuf, carry, gbuf, *, final_norm):
    tm = x_ref.shape[1]
    halo = SUBLANES

    @pl.when(pl.program_id(1) == 0)
    def _():
        carry[...] = jnp.zeros_like(carry)

    x = x_ref[0]
    h = _rms(x, g_ref[...]).astype(BF16)

    def conv(part, col):
        cols = slice(col, col + FFN_CF)
        u = _dot(h, win_ref[:, cols])
        ubuf[part, 0:halo, :] = carry[:, cols]
        ubuf[part, halo:halo + tm, :] = u
        carry[:, cols] = u[tm - halo:, :]
        u1 = ubuf[part, halo - 1:halo - 1 + tm, :]
        u2 = ubuf[part, halo - 2:halo - 2 + tm, :]
        return (u * cw_ref[2:3, cols] + u1 * cw_ref[1:2, cols] + u2 * cw_ref[0:1, cols]
                + cb_ref[:, cols])

    for c in range(D_FF // FFN_CF):
        gate = conv(0, c * FFN_CF)
        up = conv(1, D_FF + c * FFN_CF)
        act = 0.5 * gate * (1.0 + lax.erf(gate * (2.0 ** -0.5)))
        gbuf[:, c * FFN_CF:(c + 1) * FFN_CF] = (act * up).astype(BF16)

    y = x + _dot(gbuf[...], wout_ref[...])
    if final_norm:
        y = _rms(y, fg_ref[...])
    o_ref[0] = y


def _ffn(x3, g, w_in, conv_w, conv_b, w_out, final_gain, final_norm):
    b, s, d = x3.shape
    f2 = w_in.shape[1]
    whole = pl.BlockSpec(memory_space=pltpu.VMEM)
    return pl.pallas_call(
        functools.partial(_ffn_kernel, final_norm=final_norm),
        grid=(b, s // FFN_TM),
        in_specs=[pl.BlockSpec((1, FFN_TM, d), lambda bi, t: (bi, t, 0)),
                  whole, whole, whole, whole, whole, whole],
        out_specs=pl.BlockSpec((1, FFN_TM, d), lambda bi, t: (bi, t, 0)),
        out_shape=jax.ShapeDtypeStruct((b, s, d), F32),
        scratch_shapes=[pltpu.VMEM((2, FFN_TM + SUBLANES, FFN_CF), F32),
                        pltpu.VMEM((SUBLANES, f2), F32),
                        pltpu.VMEM((FFN_TM, D_FF), BF16)],
        compiler_params=_params(("arbitrary", "arbitrary")),
        name="ffn_final" if final_norm else "ffn",
    )(x3, g, w_in, conv_w, conv_b, w_out, final_gain)


def _proj1_kernel(x_ref, g_ref, wq_ref, wk_ref, wvt_ref, q_ref, k_ref, vt_ref):
    h = _rms(x_ref[0], g_ref[...]).astype(BF16)
    n = q_ref.shape[-1]
    for c in range(0, n, PROJ_TN):
        cols = slice(c, c + PROJ_TN)
        q_ref[0, :, cols] = (_dot(h, wq_ref[:, cols]) * ATT_QSCALE).astype(BF16)
        k_ref[0, :, cols] = _dot(h, wk_ref[:, cols]).astype(BF16)
        vt_ref[0, 0, cols, :] = _dot_nt(wvt_ref[cols, :], h).astype(BF16)


def _proj1(x3, g, wq, wk, wvt):
    b, s, d = x3.shape
    n = wq.shape[1]
    t = ATT_T
    wspec = pl.BlockSpec((d, n), lambda bi, i: (0, 0))
    return pl.pallas_call(
        _proj1_kernel,
        grid=(b, s // t),
        in_specs=[pl.BlockSpec((1, t, d), lambda bi, i: (bi, i, 0)),
                  pl.BlockSpec((1, d), lambda bi, i: (0, 0)),
                  wspec, wspec, pl.BlockSpec((n, d), lambda bi, i: (0, 0))],
        out_specs=[pl.BlockSpec((1, t, n), lambda bi, i: (bi, i, 0)),
                   pl.BlockSpec((1, t, n), lambda bi, i: (bi, i, 0)),
                   pl.BlockSpec((1, 1, n, t), lambda bi, i: (bi, i, 0, 0))],
        out_shape=[jax.ShapeDtypeStruct((b, s, n), BF16),
                   jax.ShapeDtypeStruct((b, s, n), BF16),
                   jax.ShapeDtypeStruct((b, s // t, n, t), BF16)],
        compiler_params=_params(("arbitrary", "arbitrary")),
        name="proj1",
    )(x3, g, wq, wk, wvt)


def _diff_kernel(q_ref, k_ref, vt_ref, f_ref, lam_ref, sub_ref, o_ref,
                 qm_ref, km_ref, m_ref, l_ref, l8_ref, acc_ref, nb_ref, *, lam_init):
    i = pl.program_id(2)
    t = ATT_T
    e = DIFF_HEAD_DIM
    half = lax.broadcasted_iota(jnp.int32, (SUBLANES, 2 * e), 1) // e
    sel = (lax.broadcasted_iota(jnp.int32, (SUBLANES, 2 * e), 0) == half).astype(BF16)

    @pl.when(i == 0)
    def _():
        for o in range(ATT_NEAR):
            g = jnp.broadcast_to(f_ref[0, 0:1, o * t:(o + 2) * t], (t, 2 * t))
            nb_ref[o] = pltpu.roll(g, 0, 1, stride=1, stride_axis=0)[:, t:]
        nb_ref[ATT_NEAR] = jnp.zeros((t, t), F32)
        kmax = jnp.zeros((SUBLANES, t), F32)
        for c in range(k_ref.shape[1] // t):
            kk = k_ref[0, c * t:(c + 1) * t, :].astype(F32)
            kmax = jnp.maximum(kmax, _dot_nt(sel, (kk * kk).astype(BF16)))
        for a in range(2):
            km_ref[a] = jnp.broadcast_to(jnp.max(kmax[a:a + 1, :], axis=1, keepdims=True), (1, t))

    q = q_ref[0]
    lane = lax.broadcasted_iota(jnp.int32, q.shape, 1)
    zero = jnp.zeros_like(q)
    qm_ref[0] = jnp.where(lane < e, q, zero)
    qm_ref[1] = jnp.where(lane >= e, q, zero)
    qf = q.astype(F32)
    qn2 = _dot_nt(sel, (qf * qf).astype(BF16))
    bias_max = f_ref[0, 1:2, 0:t]
    n_far = jnp.maximum(i - (ATT_NEAR - 1), 0)

    l8_ref[...] = jnp.zeros_like(l8_ref)
    acc_ref[...] = jnp.zeros_like(acc_ref)
    shift = [jnp.sqrt(qn2[a:a + 1, :] * km_ref[a]) * ATT_BOUND_SLACK + bias_max for a in range(2)]

    kc = ATT_KC

    def fast_tiles(j0, ntiles, biased):
        for a in range(2):
            pv = None
            l8 = None
            for tt in range(ntiles):
                j = j0 + tt
                for c in range(0, t, kc):
                    kb = k_ref[0, pl.ds(pl.multiple_of(j * t + c, kc), kc), :]
                    s = _dot_nt(kb, qm_ref[a])
                    if biased:
                        s = s + nb_ref[jnp.minimum(i - j, ATT_NEAR), c:c + kc, :]
                    p = jnp.exp2(s - shift[a])
                    part = jnp.sum(p.reshape(kc // SUBLANES, SUBLANES, t), axis=0)
                    d = _dot(vt_ref[0, j, :, c:c + kc], p.astype(BF16))
                    l8 = part if l8 is None else l8 + part
                    pv = d if pv is None else pv + d
            l8_ref[a] += l8
            acc_ref[a] += pv

    def fast_far_pair(u, carry):
        fast_tiles(2 * u, 2, False)
        return carry

    def fast_near_pair(u, carry):
        fast_tiles(2 * u, 2, True)
        return carry

    def fast_near(j, carry):
        fast_tiles(j, 1, True)
        return carry

    n = i + 1
    lax.fori_loop(0, n_far // 2, fast_far_pair, 0)
    lax.fori_loop(n_far // 2, n // 2, fast_near_pair, 0)
    lax.fori_loop(2 * (n // 2), n, fast_near, 0)
    for a in range(2):
        l_ref[a] = jnp.sum(l8_ref[a], axis=0, keepdims=True)
    l_min = jnp.min(jnp.minimum(l_ref[0], l_ref[1]))
    l_max = jnp.max(jnp.maximum(l_ref[0], l_ref[1]))
    in_range = jnp.logical_and(l_min >= ATT_MIN_DENOM, l_max <= ATT_MAX_DENOM)

    @pl.when(jnp.logical_not(in_range))
    def _():
        m_ref[...] = jnp.full_like(m_ref, MASKED)
        l_ref[...] = jnp.zeros_like(l_ref)
        acc_ref[...] = jnp.zeros_like(acc_ref)

        def online_tile(j, bias):
            kb = k_ref[0, pl.ds(pl.multiple_of(j * t, t), t), :]
            vtb = vt_ref[0, j]
            for a in range(2):
                s = _dot_nt(kb, qm_ref[a])
                if bias is not None:
                    s = s + bias
                m_old = m_ref[a]
                m_new = jnp.maximum(m_old, jnp.max(s, axis=0, keepdims=True))
                alpha = jnp.exp2(m_old - m_new)
                p = jnp.exp2(s - m_new)
                l_ref[a] = alpha * l_ref[a] + jnp.sum(p, axis=0, keepdims=True)
                acc_ref[a] = alpha * acc_ref[a] + _dot(vtb, p.astype(BF16))
                m_ref[a] = m_new

        def online_far(j, carry):
            online_tile(j, None)
            return carry

        def online_near(j, carry):
            online_tile(j, nb_ref[i - j])
            return carry

        lax.fori_loop(0, n_far, online_far, 0)
        lax.fori_loop(n_far, i + 1, online_near, 0)

    lam = (jnp.exp(jnp.sum(lam_ref[0:1, :] * lam_ref[1:2, :], axis=-1, keepdims=True))
           - jnp.exp(jnp.sum(lam_ref[2:3, :] * lam_ref[3:4, :], axis=-1, keepdims=True))
           + lam_init)
    ot = acc_ref[0] / l_ref[0] - lam * (acc_ref[1] / l_ref[1])
    ot = ot * lax.rsqrt(jnp.mean(ot * ot, axis=0, keepdims=True) + EPS)
    o_ref[0] = (ot.T * sub_ref[...] * (1.0 - lam_init)).astype(o_ref.dtype)


def _diff_bias(rel_bias, s):
    t = ATT_T
    n = ATT_NEAR * t
    assert n >= REL_MAX_DISTANCE + t - 1 and s >= n
    dist_bias = rel_bias[_t5_bucket(jnp.arange(n))].astype(F32)
    far = rel_bias[REL_BUCKETS - 1].astype(F32)
    near = (dist_bias - far) * LOG2E
    f = jnp.concatenate([jnp.full((t, far.shape[0]), MASKED, F32), near])
    top = jnp.broadcast_to(jnp.maximum(jnp.max(near, axis=0), 0.0), f.shape)
    return jnp.stack([f.T, top.T], axis=1)


def _diff_attention(q, k, vt, rel_bias, lam4, subln, layer_idx):
    b, s, n = q.shape
    t = ATT_T
    w = 2 * DIFF_HEAD_DIM
    lam_init = 0.8 - 0.6 * math.exp(-0.3 * layer_idx)
    return pl.pallas_call(
        functools.partial(_diff_kernel, lam_init=lam_init),
        grid=(b, DIFF_HEADS, s // t),
        in_specs=[pl.BlockSpec((1, t, w), lambda bi, h, i: (bi, i, h)),
                  pl.BlockSpec((1, s, w), lambda bi, h, i: (bi, 0, h)),
                  pl.BlockSpec((1, s // t, w, t), lambda bi, h, i: (bi, 0, h, 0)),
                  pl.BlockSpec((1, 2, (ATT_NEAR + 1) * t), lambda bi, h, i: (h, 0, 0)),
                  pl.BlockSpec((4, DIFF_HEAD_DIM), lambda bi, h, i: (0, 0)),
                  pl.BlockSpec((1, w), lambda bi, h, i: (0, 0))],
        out_specs=pl.BlockSpec((1, t, w), lambda bi, h, i: (bi, i, h)),
        out_shape=jax.ShapeDtypeStruct((b, s, n), BF16),
        scratch_shapes=[pltpu.VMEM((2, t, w), BF16),
                        pltpu.VMEM((2, 1, t), F32),
                        pltpu.VMEM((2, 1, t), F32),
                        pltpu.VMEM((2, 1, t), F32),
                        pltpu.VMEM((2, SUBLANES, t), F32),
                        pltpu.VMEM((2, w, t), F32),
                        pltpu.VMEM((ATT_NEAR + 1, t, t), F32)],
        compiler_params=_params(("arbitrary", "arbitrary", "arbitrary")),
        name="diff_attention",
    )(q, k, vt, _diff_bias(rel_bias, s), lam4, subln)


def _mix1_kernel(x_ref, a_ref, w_ref, out_ref):
    out_ref[...] = x_ref[...] + _dot(a_ref[...], w_ref[...])


def _mix1(x2, a, w):
    t, d = x2.shape
    n = a.shape[-1]
    return pl.pallas_call(
        _mix1_kernel,
        grid=(t // PROJ_TM,),
        in_specs=[pl.BlockSpec((PROJ_TM, d), lambda i: (i, 0)),
                  pl.BlockSpec((PROJ_TM, n), lambda i: (i, 0)),
                  pl.BlockSpec((n, d), lambda i: (0, 0))],
        out_specs=pl.BlockSpec((PROJ_TM, d), lambda i: (i, 0)),
        out_shape=jax.ShapeDtypeStruct((t, d), F32),
        compiler_params=_params(("arbitrary",)),
        name="mix1",
    )(x2, a, w)


def kernel(x, rel_bias, norm_mix, norm_ffn, norm_final, w_in_ab, ret_gn, w_out_ab, w_in_c,
           lam_q1, lam_k1, lam_q2, lam_k2, diff_subln, w_out_c, w_ffn_in, conv_w, conv_b,
           w_ffn_out):
    b, s, d = x.shape
    depth = norm_mix.shape[0]
    assert depth == 2 and s % RET_ROWS == 0 and s % FFN_TM == 0 and s >= ATT_NEAR * ATT_T
    assert s % (max(DIL_DILATIONS) * DIL_BLOCK) == 0
    row = lambda v: v.reshape(1, -1)

    proj = _proj0(x, row(norm_mix[0]), w_in_ab[0].astype(BF16))
    ret = _retention(proj[0], row(ret_gn[0]), b, s)
    outs, lses = zip(*[_dilated_pattern(dq, rel_bias, wdw, dil)
                       for dq, (wdw, dil) in zip(proj[1:], DIL_PATTERNS)])
    x3 = _mix0(x, ret, outs, lses, w_out_ab[0].astype(BF16))
    x3 = _ffn(x3, row(norm_ffn[0]), w_ffn_in[0].astype(BF16), conv_w[0],
              row(conv_b[0]), w_ffn_out[0].astype(BF16), row(norm_final), False)

    nqk = DIFF_HEADS * 2 * DIFF_HEAD_DIM
    wc = w_in_c[0].astype(BF16)
    q, k, vt = _proj1(x3, row(norm_mix[1]), wc[:, :nqk], wc[:, nqk:2 * nqk], wc[:, 2 * nqk:].T)
    lam4 = jnp.stack([lam_q1[0], lam_k1[0], lam_q2[0], lam_k2[0]]).astype(F32)
    a = _diff_attention(q, k, vt, rel_bias, lam4, row(diff_subln[0]), 1)
    x2 = _mix1(x3.reshape(b * s, d), a.reshape(b * s, -1), w_out_c[0].astype(BF16))
    return _ffn(x2.reshape(b, s, d), row(norm_ffn[1]), w_ffn_in[1].astype(BF16), conv_w[1],
                row(conv_b[1]), w_ffn_out[1].astype(BF16), row(norm_final), True)
```

```python
import functools
import math

import jax
import jax.numpy as jnp
from jax import lax
from jax.experimental import pallas as pl
from jax.experimental.pallas import tpu as pltpu

F32 = jnp.float32
BF16 = jnp.bfloat16

EPS = 1e-6
MASKED = -1e30

RET_HEADS = 4
RET_DIM = 128
RET_CHUNK = 128
ROPE_BASE = 10000.0
DIL_HEADS = 8
DIL_HEAD_DIM = 64
DIL_PATTERNS = ((128, 1), (512, 4), (2048, 16))
DIL_BLOCK = 128
DIFF_HEADS = 8
DIFF_HEAD_DIM = 64
REL_BUCKETS = 32
REL_MAX_DISTANCE = 2048
D_FF = 2816
CONV_WIDTH = 3
DIL_W = DIL_HEADS * DIL_HEAD_DIM
RET_W = 4 * RET_HEADS * RET_DIM
DIL_DILATIONS = tuple(d for _, d in DIL_PATTERNS)

LANES = 128
SUBLANES = 8
VMEM_BYTES_V7X = 64 * 1024 * 1024
VMEM_LIMIT = VMEM_BYTES_V7X - 8 * 1024 * 1024

PROJ_TM = 512
PROJ_TN = 512
RET_ROWS = 1024
DIL_QBLOCKS = 1
FFN_TM = 512
FFN_CF = 256
ATT_T = 512
ATT_KC = 512
ATT_BOUND_SLACK = 1.0 + 2.0 ** -5
ATT_MIN_DENOM = 2.0 ** -60
ATT_MAX_DENOM = 2.0 ** 100
ATT_NEAR = -(-(REL_MAX_DISTANCE + ATT_T - 1) // ATT_T)
LOG2E = math.log2(math.e)
ATT_QSCALE = DIFF_HEAD_DIM ** -0.5 * LOG2E


def _params(sem):
    return pltpu.CompilerParams(dimension_semantics=sem, vmem_limit_bytes=VMEM_LIMIT)


def _rms(x, g):
    return x * lax.rsqrt(jnp.mean(x * x, axis=-1, keepdims=True) + EPS) * g


def _dot(a, b):
    return jnp.dot(a, b, preferred_element_type=F32)


def _dot_nt(a, b):
    return lax.dot_general(a, b, (((1,), (1,)), ((), ())), preferred_element_type=F32)


def _dot_tn(a, b):
    return lax.dot_general(a, b, (((0,), (0,)), ((), ())), preferred_element_type=F32)


def _proj0_kernel(x_ref, g_ref, w_ref, a_ref, *rest):
    d_refs, dbuf = rest[:-1], rest[-1]
    tm = x_ref.shape[1]
    h = _rms(x_ref[0], g_ref[...]).astype(BF16)
    na = a_ref.shape[-1]
    for c in range(0, na, PROJ_TN):
        a_ref[0, :, c:c + PROJ_TN] = _dot(h, w_ref[:, c:c + PROJ_TN]).astype(BF16)
    nd = w_ref.shape[1] - na
    for c in range(0, nd, PROJ_TN):
        res = _dot(h, w_ref[:, na + c:na + c + PROJ_TN])
        for s in range(PROJ_TN // LANES):
            dbuf[(c // LANES) + s] = res[:, s * LANES:(s + 1) * LANES]
    for dref, d in zip(d_refs, DIL_DILATIONS):
        rows = tm // d
        for s in range(nd // LANES):
            cols = slice(s * LANES, (s + 1) * LANES)
            if d == 1:
                dref[0, 0, :, cols] = dbuf[s].astype(BF16)
            else:
                for r in range(d):
                    dref[0, r, :, cols] = dbuf[s, pl.ds(r, rows, stride=d), :].astype(BF16)


def _proj0(x3, g, w):
    b, s, dm = x3.shape
    n = w.shape[1]
    nd = n - RET_W
    tm = PROJ_TM
    d_specs = [pl.BlockSpec((1, d, tm // d, nd), lambda bi, i: (bi, 0, i, 0)) for d in DIL_DILATIONS]
    d_shapes = [jax.ShapeDtypeStruct((b, d, s // d, nd), BF16) for d in DIL_DILATIONS]
    return pl.pallas_call(
        _proj0_kernel,
        grid=(b, s // tm),
        in_specs=[pl.BlockSpec((1, tm, dm), lambda bi, i: (bi, i, 0)),
                  pl.BlockSpec((1, dm), lambda bi, i: (0, 0)),
                  pl.BlockSpec((dm, n), lambda bi, i: (0, 0))],
        out_specs=[pl.BlockSpec((1, tm, RET_W), lambda bi, i: (bi, i, 0))] + d_specs,
        out_shape=[jax.ShapeDtypeStruct((b, s, RET_W), BF16)] + d_shapes,
        scratch_shapes=[pltpu.VMEM((nd // LANES, tm, LANES), F32)],
        compiler_params=_params(("arbitrary", "arbitrary")),
        name="proj0",
    )(x3, g, w)


def _ret_kernel(q_ref, k_ref, v_ref, gate_ref, cos_ref, sin_ref, tab_ref, gn_ref, o_ref,
                state_ref):
    @pl.when(pl.program_id(2) == 0)
    def _():
        state_ref[...] = jnp.zeros_like(state_ref)

    decay = tab_ref[0, 0]
    xi = tab_ref[0, 1]
    zeta = tab_ref[0, 2]
    g_chunk = tab_ref[0, 3]
    gain = gn_ref[...]
    half = RET_DIM // 2
    for c in range(q_ref.shape[1] // RET_CHUNK):
        rows = pl.ds(c * RET_CHUNK, RET_CHUNK)
        cos = cos_ref[rows, :]
        sin = sin_ref[rows, :]
        q = q_ref[0, rows, :].astype(F32)
        k = k_ref[0, rows, :].astype(F32)
        q = q * cos + pltpu.roll(q, half, 1) * sin
        k = k * cos + pltpu.roll(k, half, 1) * sin
        v = v_ref[0, rows, :]
        state = state_ref[...]
        intra = _dot_nt(q.astype(BF16), k.astype(BF16)) * decay
        y = _dot(intra.astype(BF16), v) + _dot((q * xi).astype(BF16), state.astype(BF16))
        state_ref[...] = g_chunk * state + _dot_tn((k * zeta).astype(BF16), v)
        mu = jnp.mean(y, axis=-1, keepdims=True)
        yc = y - mu
        var = jnp.mean(yc * yc, axis=-1, keepdims=True)
        yn = yc * lax.rsqrt(var + EPS) * gain
        gate = gate_ref[0, rows, :].astype(F32)
        o_ref[0, rows, :] = (gate * jax.nn.sigmoid(gate) * yn).astype(o_ref.dtype)


def _retention_tables(s):
    c, e = RET_CHUNK, RET_DIM
    inv = ROPE_BASE ** (-jnp.arange(0, e, 2, dtype=F32) / e)
    ang = jnp.arange(s, dtype=F32)[:, None] * inv[None, :]
    cos = jnp.concatenate([jnp.cos(ang), jnp.cos(ang)], axis=-1)
    sin = jnp.concatenate([-jnp.sin(ang), jnp.sin(ang)], axis=-1)
    log_g = jnp.log1p(-jnp.exp2(-5.0 - jnp.arange(RET_HEADS, dtype=F32)))
    pos = jnp.arange(c, dtype=F32)
    rel = pos[:, None] - pos[None, :]
    causal = rel >= 0
    scale = e ** -0.5
    decay = jnp.where(causal, jnp.exp(jnp.where(causal, rel, 0.0)[None] * log_g[:, None, None]), 0.0)
    zeta = jnp.exp((c - 1 - pos)[None, :] * log_g[:, None])
    xi = jnp.exp((pos + 1)[None, :] * log_g[:, None])
    g_chunk = jnp.exp(c * log_g)
    tab = jnp.stack([
        decay * scale,
        jnp.broadcast_to(xi[:, :, None], (RET_HEADS, c, e)),
        jnp.broadcast_to(zeta[:, :, None] * scale, (RET_HEADS, c, e)),
        jnp.broadcast_to(g_chunk[:, None, None], (RET_HEADS, e, e)),
    ], axis=1)
    return cos, sin, tab


def _retention(p0, gn, b, s):
    cos, sin, tab = _retention_tables(s)
    e = RET_DIM
    col = lambda off: (lambda bi, h, t: (bi, t, off + h))
    return pl.pallas_call(
        _ret_kernel,
        grid=(b, RET_HEADS, s // RET_ROWS),
        in_specs=[pl.BlockSpec((1, RET_ROWS, e), col(0)),
                  pl.BlockSpec((1, RET_ROWS, e), col(RET_HEADS)),
                  pl.BlockSpec((1, RET_ROWS, e), col(2 * RET_HEADS)),
                  pl.BlockSpec((1, RET_ROWS, e), col(3 * RET_HEADS)),
                  pl.BlockSpec((RET_ROWS, e), lambda bi, h, t: (t, 0)),
                  pl.BlockSpec((RET_ROWS, e), lambda bi, h, t: (t, 0)),
                  pl.BlockSpec((1, 4, e, e), lambda bi, h, t: (h, 0, 0, 0)),
                  pl.BlockSpec((1, e), lambda bi, h, t: (0, h))],
        out_specs=pl.BlockSpec((1, RET_ROWS, e), lambda bi, h, t: (bi, t, h)),
        out_shape=jax.ShapeDtypeStruct((b, s, RET_HEADS * e), BF16),
        scratch_shapes=[pltpu.VMEM((e, e), F32)],
        compiler_params=_params(("arbitrary", "arbitrary", "arbitrary")),
        name="retention",
    )(p0, p0, p0, p0, cos, sin, tab, gn)


def _t5_bucket(dist):
    max_exact = REL_BUCKETS // 2
    d = jnp.maximum(dist.astype(F32), 1.0)
    large = max_exact + (jnp.log(d / max_exact) / math.log(REL_MAX_DISTANCE / max_exact)
                         * (REL_BUCKETS - max_exact))
    large = jnp.clip(large.astype(jnp.int32), max_exact, REL_BUCKETS - 1)
    return jnp.where(dist < max_exact, dist, large)


def _dil_kernel(q_ref, kp_ref, kc_ref, vp_ref, vc_ref, g_ref, o_ref, lse_ref, bias_ref):
    first = pl.program_id(2) == 0
    blk, e = DIL_BLOCK, DIL_HEAD_DIM

    @pl.when(jnp.logical_and(jnp.logical_and(pl.program_id(0) == 0, pl.program_id(1) == 0), first))
    def _():
        for h in range(DIL_HEADS):
            gb = jnp.broadcast_to(g_ref[h:h + 1, :], (blk, 4 * blk))
            bias_ref[h] = pltpu.roll(gb, 0, 1, stride=1, stride_axis=0)[:, :2 * blk]

    q_all = q_ref[0, 0] * (e ** -0.5)
    k_all = jnp.concatenate([kp_ref[0, 0], kc_ref[0, 0]], axis=0)
    v_all = jnp.concatenate([vp_ref[0, 0], vc_ref[0, 0]], axis=0)
    in_prev = lax.broadcasted_iota(jnp.int32, (blk, 2 * blk), 1) < blk
    no_prev = jnp.logical_and(first, in_prev)
    lane = lax.broadcasted_iota(jnp.int32, (blk, LANES), 1)
    for qb in range(q_all.shape[0] // blk):
        q = q_all[qb * blk:(qb + 1) * blk]
        kcat = k_all[qb * blk:(qb + 2) * blk]
        vcat = v_all[qb * blk:(qb + 2) * blk]
        lse_all = jnp.zeros((blk, LANES), F32)
        for h in range(DIL_HEADS):
            cols = slice(h * e, (h + 1) * e)
            s = _dot_nt(q[:, cols], kcat[:, cols]) + bias_ref[h]
            if qb == 0:
                s = jnp.where(no_prev, MASKED, s)
            m = jnp.max(s, axis=-1, keepdims=True)
            p = jnp.exp(s - m)
            l = jnp.sum(p, axis=-1, keepdims=True)
            o_ref[0, 0, qb * blk:(qb + 1) * blk, cols] = (
                _dot(p.astype(BF16), vcat[:, cols]) / l).astype(o_ref.dtype)
            lse_all = jnp.where(lane == h, m + jnp.log(l), lse_all)
        lse_ref[0, 0, qb * blk:(qb + 1) * blk, :] = lse_all


def _dil_bias_vector(rel_bias, window, dil):
    blk = DIL_BLOCK
    span = window // dil
    r = blk - jnp.arange(4 * blk)
    band = (r >= 0) & (r <= span)
    bias = rel_bias[_t5_bucket(jnp.maximum(r, 0) * dil)].astype(F32)
    return jnp.where(band[:, None], bias, MASKED).T


def _dilated_pattern(dq, rel_bias, window, dil):
    b, d, sd, _ = dq.shape
    blk = DIL_BLOCK
    nq = DIL_QBLOCKS
    rows = nq * blk
    assert d == dil and window // dil == blk and sd % rows == 0
    cur = lambda c: (lambda bi, r, n: (bi, r, n, c))
    prev = lambda c: (lambda bi, r, n: (bi, r, jnp.maximum(n * nq - 1, 0), c))
    spec = lambda im: pl.BlockSpec((1, 1, rows, DIL_W), im)
    pspec = lambda im: pl.BlockSpec((1, 1, blk, DIL_W), im)
    return pl.pallas_call(
        _dil_kernel,
        grid=(b, dil, sd // rows),
        in_specs=[spec(cur(0)), pspec(prev(1)), spec(cur(1)), pspec(prev(2)), spec(cur(2)),
                  pl.BlockSpec((DIL_HEADS, 4 * blk), lambda bi, r, n: (0, 0))],
        out_specs=[pl.BlockSpec((1, 1, rows, DIL_W), lambda bi, r, n: (bi, r, n, 0)),
                   pl.BlockSpec((1, 1, rows, LANES), lambda bi, r, n: (bi, r, n, 0))],
        out_shape=[jax.ShapeDtypeStruct((b, dil, sd, DIL_W), BF16),
                   jax.ShapeDtypeStruct((b, dil, sd, LANES), F32)],
        scratch_shapes=[pltpu.VMEM((DIL_HEADS, blk, 2 * blk), F32)],
        compiler_params=_params(("arbitrary", "arbitrary", "arbitrary")),
        name=f"dilated_d{dil}",
    )(dq, dq, dq, dq, dq, _dil_bias_vector(rel_bias, window, dil))


def _mix0_kernel(x_ref, ret_ref, *rest):
    np_ = len(DIL_DILATIONS)
    o_refs, l_refs = rest[:np_], rest[np_:2 * np_]
    w_ref, out_ref, obuf, lbuf, dbuf = rest[2 * np_:]
    tm = x_ref.shape[1]
    nslab = DIL_W // LANES
    for pi, d in enumerate(DIL_DILATIONS):
        rows = tm // d
        for r in range(d):
            dst = pl.ds(r, rows, stride=d) if d > 1 else pl.ds(0, tm)
            lbuf[pi, dst, :] = l_refs[pi][0, r]
            for s in range(nslab):
                obuf[pi, s, dst, :] = o_refs[pi][0, r, :, s * LANES:(s + 1) * LANES].astype(F32)
    lses = [lbuf[pi] for pi in range(np_)]
    m = functools.reduce(jnp.maximum, lses)
    es = [jnp.exp(l - m) for l in lses]
    inv = 1.0 / functools.reduce(lambda a, b: a + b, es)
    row = lax.broadcasted_iota(jnp.int32, (LANES, DIL_W), 0)
    col = lax.broadcasted_iota(jnp.int32, (LANES, DIL_W), 1)
    spread = (col // DIL_HEAD_DIM == row).astype(BF16)
    wide = []
    for ex in es:
        wgt = ex * inv
        hi = wgt.astype(BF16)
        lo = (wgt - hi.astype(F32)).astype(BF16)
        wide.append(_dot(hi, spread) + _dot(lo, spread))
    for s in range(nslab):
        cols = slice(s * LANES, (s + 1) * LANES)
        acc = wide[0][:, cols] * obuf[0, s]
        for pi in range(1, np_):
            acc = acc + wide[pi][:, cols] * obuf[pi, s]
        dbuf[:, cols] = acc.astype(BF16)
    na = ret_ref.shape[-1]
    y = _dot(ret_ref[0], w_ref[:na, :]) + _dot(dbuf[...], w_ref[na:, :])
    out_ref[0] = x_ref[0] + y


def _mix0(x3, ret, outs, lses, w):
    b, s, dm = x3.shape
    na = ret.shape[-1]
    tm = PROJ_TM
    o_specs = [pl.BlockSpec((1, d, tm // d, DIL_W), lambda bi, i: (bi, 0, i, 0)) for d in DIL_DILATIONS]
    l_specs = [pl.BlockSpec((1, d, tm // d, LANES), lambda bi, i: (bi, 0, i, 0)) for d in DIL_DILATIONS]
    np_ = len(DIL_DILATIONS)
    return pl.pallas_call(
        _mix0_kernel,
        grid=(b, s // tm),
        in_specs=[pl.BlockSpec((1, tm, dm), lambda bi, i: (bi, i, 0)),
                  pl.BlockSpec((1, tm, na), lambda bi, i: (bi, i, 0))] + o_specs + l_specs
                 + [pl.BlockSpec((na + DIL_W, dm), lambda bi, i: (0, 0))],
        out_specs=pl.BlockSpec((1, tm, dm), lambda bi, i: (bi, i, 0)),
        out_shape=jax.ShapeDtypeStruct((b, s, dm), F32),
        scratch_shapes=[pltpu.VMEM((np_, DIL_W // LANES, tm, LANES), F32),
                        pltpu.VMEM((np_, tm, LANES), F32),
                        pltpu.VMEM((tm, DIL_W), BF16)],
        compiler_params=_params(("arbitrary", "arbitrary")),
        name="mix0",
    )(x3, ret, *outs, *lses, w)


def _ffn_kernel(x_ref, g_ref, win_ref, cw_ref, cb_ref, wout_ref, fg_ref, o_ref,
                ubuf, carry, gbuf, *, final_norm):
    tm = x_ref.shape[1]
    halo = SUBLANES

    @pl.when(pl.program_id(1) == 0)
    def _():
        carry[...] = jnp.zeros_like(carry)

    x = x_ref[0]
    h = _rms(x, g_ref[...]).astype(BF16)

    def conv(part, col):
        cols = slice(col, col + FFN_CF)
        u = _dot(h, win_ref[:, cols])
        ubuf[part, 0:halo, :] = carry[:, cols]
        ubuf[part, halo:halo + tm, :] = u
        carry[:, cols] = u[tm - halo:, :]
        u1 = ubuf[part, halo - 1:halo - 1 + tm, :]
        u2 = ubuf[part, halo - 2:halo - 2 + tm, :]
        return (u * cw_ref[2:3, cols] + u1 * cw_ref[1:2, cols] + u2 * cw_ref[0:1, cols]
                + cb_ref[:, cols])

    for c in range(D_FF // FFN_CF):
        gate = conv(0, c * FFN_CF)
        up = conv(1, D_FF + c * FFN_CF)
        act = 0.5 * gate * (1.0 + lax.erf(gate * (2.0 ** -0.5)))
        gbuf[:, c * FFN_CF:(c + 1) * FFN_CF] = (act * up).astype(BF16)

    y = x + _dot(gbuf[...], wout_ref[...])
    if final_norm:
        y = _rms(y, fg_ref[...])
    o_ref[0] = y


def _ffn(x3, g, w_in, conv_w, conv_b, w_out, final_gain, final_norm):
    b, s, d = x3.shape
    f2 = w_in.shape[1]
    whole = pl.BlockSpec(memory_space=pltpu.VMEM)
    return pl.pallas_call(
        functools.partial(_ffn_kernel, final_norm=final_norm),
        grid=(b, s // FFN_TM),
        in_specs=[pl.BlockSpec((1, FFN_TM, d), lambda bi, t: (bi, t, 0)),
                  whole, whole, whole, whole, whole, whole],
        out_specs=pl.BlockSpec((1, FFN_TM, d), lambda bi, t: (bi, t, 0)),
        out_shape=jax.ShapeDtypeStruct((b, s, d), F32),
        scratch_shapes=[pltpu.VMEM((2, FFN_TM + SUBLANES, FFN_CF), F32),
                        pltpu.VMEM((SUBLANES, f2), F32),
                        pltpu.VMEM((FFN_TM, D_FF), BF16)],
        compiler_params=_params(("arbitrary", "arbitrary")),
        name="ffn_final" if final_norm else "ffn",
    )(x3, g, w_in, conv_w, conv_b, w_out, final_gain)


def _proj1_kernel(x_ref, g_ref, wq_ref, wk_ref, wvt_ref, q_ref, k_ref, vt_ref):
    h = _rms(x_ref[0], g_ref[...]).astype(BF16)
    n = q_ref.shape[-1]
    for c in range(0, n, PROJ_TN):
        cols = slice(c, c + PROJ_TN)
        q_ref[0, :, cols] = (_dot(h, wq_ref[:, cols]) * ATT_QSCALE).astype(BF16)
        k_ref[0, :, cols] = _dot(h, wk_ref[:, cols]).astype(BF16)
        vt_ref[0, 0, cols, :] = _dot_nt(wvt_ref[cols, :], h).astype(BF16)


def _proj1(x3, g, wq, wk, wvt):
    b, s, d = x3.shape
    n = wq.shape[1]
    t = ATT_T
    wspec = pl.BlockSpec((d, n), lambda bi, i: (0, 0))
    return pl.pallas_call(
        _proj1_kernel,
        grid=(b, s // t),
        in_specs=[pl.BlockSpec((1, t, d), lambda bi, i: (bi, i, 0)),
                  pl.BlockSpec((1, d), lambda bi, i: (0, 0)),
                  wspec, wspec, pl.BlockSpec((n, d), lambda bi, i: (0, 0))],
        out_specs=[pl.BlockSpec((1, t, n), lambda bi, i: (bi, i, 0)),
                   pl.BlockSpec((1, t, n), lambda bi, i: (bi, i, 0)),
                   pl.BlockSpec((1, 1, n, t), lambda bi, i: (bi, i, 0, 0))],
        out_shape=[jax.ShapeDtypeStruct((b, s, n), BF16),
                   jax.ShapeDtypeStruct((b, s, n), BF16),
                   jax.ShapeDtypeStruct((b, s // t, n, t), BF16)],
        compiler_params=_params(("arbitrary", "arbitrary")),
        name="proj1",
    )(x3, g, wq, wk, wvt)


def _diff_kernel(q_ref, k_ref, vt_ref, f_ref, lam_ref, sub_ref, o_ref,
                 qm_ref, km_ref, m_ref, l_ref, l8_ref, acc_ref, nb_ref, p_ref, *, lam_init):
    i = pl.program_id(2)
    t = ATT_T
    e = DIFF_HEAD_DIM
    half = lax.broadcasted_iota(jnp.int32, (SUBLANES, 2 * e), 1) // e
    sel = (lax.broadcasted_iota(jnp.int32, (SUBLANES, 2 * e), 0) == half).astype(BF16)

    @pl.when(i == 0)
    def _():
        for o in range(ATT_NEAR):
            g = jnp.broadcast_to(f_ref[0, 0:1, o * t:(o + 2) * t], (t, 2 * t))
            nb_ref[o] = pltpu.roll(g, 0, 1, stride=1, stride_axis=0)[:, t:]
        nb_ref[ATT_NEAR] = jnp.zeros((t, t), F32)
        kmax = jnp.zeros((SUBLANES, t), F32)
        for c in range(k_ref.shape[1] // t):
            kk = k_ref[0, c * t:(c + 1) * t, :].astype(F32)
            kmax = jnp.maximum(kmax, _dot_nt(sel, (kk * kk).astype(BF16)))
        for a in range(2):
            km_ref[a] = jnp.broadcast_to(jnp.max(kmax[a:a + 1, :], axis=1, keepdims=True), (1, t))

    q = q_ref[0]
    lane = lax.broadcasted_iota(jnp.int32, q.shape, 1)
    zero = jnp.zeros_like(q)
    qm_ref[0] = jnp.where(lane < e, q, zero)
    qm_ref[1] = jnp.where(lane >= e, q, zero)
    qf = q.astype(F32)
    qn2 = _dot_nt(sel, (qf * qf).astype(BF16))
    bias_max = f_ref[0, 1:2, 0:t]
    n_far = jnp.maximum(i - (ATT_NEAR - 1), 0)

    l8_ref[...] = jnp.zeros_like(l8_ref)
    acc_ref[...] = jnp.zeros_like(acc_ref)
    shift = [jnp.sqrt(qn2[a:a + 1, :] * km_ref[a]) * ATT_BOUND_SLACK + bias_max for a in range(2)]

    kc = ATT_KC

    def fast_tiles(j0, ntiles, biased):
        for a in range(2):
            pv = None
            l8 = None
            for tt in range(ntiles):
                j = j0 + tt
                for c in range(0, t, kc):
                    kb = k_ref[0, pl.ds(pl.multiple_of(j * t + c, kc), kc), :]
                    s = _dot_nt(kb, qm_ref[a])
                    if biased:
                        s = s + nb_ref[jnp.minimum(i - j, ATT_NEAR), c:c + kc, :]
                    p = jnp.exp2(s - shift[a])
                    part = jnp.sum(p.reshape(kc // SUBLANES, SUBLANES, t), axis=0)
                    d = _dot(vt_ref[0, j, :, c:c + kc], p.astype(BF16))
                    l8 = part if l8 is None else l8 + part
                    pv = d if pv is None else pv + d
            l8_ref[a] += l8
            acc_ref[a] += pv

    def probs(u, biased):
        slot = u % 2
        for a in range(2):
            l8 = None
            for tt in range(2):
                j = 2 * u + tt
                kb = k_ref[0, pl.ds(pl.multiple_of(j * t, t), t), :]
                s = _dot_nt(kb, qm_ref[a])
                if biased:
                    s = s + nb_ref[jnp.minimum(i - j, ATT_NEAR)]
                p = jnp.exp2(s - shift[a])
                part = jnp.sum(p.reshape(t // SUBLANES, SUBLANES, t), axis=0)
                l8 = part if l8 is None else l8 + part
                p_ref[slot, a, tt] = p.astype(BF16)
            l8_ref[a] += l8

    def values(u):
        slot = u % 2
        for a in range(2):
            acc_ref[a] += (_dot(vt_ref[0, 2 * u], p_ref[slot, a, 0])
                           + _dot(vt_ref[0, 2 * u + 1], p_ref[slot, a, 1]))

    def far_step(u, carry):
        values(u - 1)
        probs(u, False)
        return carry

    def near_step(u, carry):
        values(u - 1)
        probs(u, True)
        return carry

    def fast_near(j, carry):
        fast_tiles(j, 1, True)
        return carry

    n = i + 1
    pairs = n // 2
    far_pairs = n_far // 2

    @pl.when(far_pairs > 0)
    def _():
        probs(0, False)

    @pl.when(jnp.logical_and(far_pairs == 0, pairs > 0))
    def _():
        probs(0, True)

    lax.fori_loop(1, far_pairs, far_step, 0)
    lax.fori_loop(jnp.maximum(far_pairs, 1), pairs, near_step, 0)

    @pl.when(pairs > 0)
    def _():
        values(pairs - 1)

    lax.fori_loop(2 * pairs, n, fast_near, 0)
    for a in range(2):
        l_ref[a] = jnp.sum(l8_ref[a], axis=0, keepdims=True)
    l_min = jnp.min(jnp.minimum(l_ref[0], l_ref[1]))
    l_max = jnp.max(jnp.maximum(l_ref[0], l_ref[1]))
    in_range = jnp.logical_and(l_min >= ATT_MIN_DENOM, l_max <= ATT_MAX_DENOM)

    @pl.when(jnp.logical_not(in_range))
    def _():
        m_ref[...] = jnp.full_like(m_ref, MASKED)
        l_ref[...] = jnp.zeros_like(l_ref)
        acc_ref[...] = jnp.zeros_like(acc_ref)

        def online_tile(j, bias):
            kb = k_ref[0, pl.ds(pl.multiple_of(j * t, t), t), :]
            vtb = vt_ref[0, j]
            for a in range(2):
                s = _dot_nt(kb, qm_ref[a])
                if bias is not None:
                    s = s + bias
                m_old = m_ref[a]
                m_new = jnp.maximum(m_old, jnp.max(s, axis=0, keepdims=True))
                alpha = jnp.exp2(m_old - m_new)
                p = jnp.exp2(s - m_new)
                l_ref[a] = alpha * l_ref[a] + jnp.sum(p, axis=0, keepdims=True)
                acc_ref[a] = alpha * acc_ref[a] + _dot(vtb, p.astype(BF16))
                m_ref[a] = m_new

        def online_far(j, carry):
            online_tile(j, None)
            return carry

        def online_near(j, carry):
            online_tile(j, nb_ref[i - j])
            return carry

        lax.fori_loop(0, n_far, online_far, 0)
        lax.fori_loop(n_far, i + 1, online_near, 0)

    lam = (jnp.exp(jnp.sum(lam_ref[0:1, :] * lam_ref[1:2, :], axis=-1, keepdims=True))
           - jnp.exp(jnp.sum(lam_ref[2:3, :] * lam_ref[3:4, :], axis=-1, keepdims=True))
           + lam_init)
    ot = acc_ref[0] / l_ref[0] - lam * (acc_ref[1] / l_ref[1])
    ot = ot * lax.rsqrt(jnp.mean(ot * ot, axis=0, keepdims=True) + EPS)
    o_ref[0] = (ot.T * sub_ref[...] * (1.0 - lam_init)).astype(o_ref.dtype)


def _diff_bias(rel_bias, s):
    t = ATT_T
    n = ATT_NEAR * t
    assert n >= REL_MAX_DISTANCE + t - 1 and s >= n
    dist_bias = rel_bias[_t5_bucket(jnp.arange(n))].astype(F32)
    far = rel_bias[REL_BUCKETS - 1].astype(F32)
    near = (dist_bias - far) * LOG2E
    f = jnp.concatenate([jnp.full((t, far.shape[0]), MASKED, F32), near])
    top = jnp.broadcast_to(jnp.maximum(jnp.max(near, axis=0), 0.0), f.shape)
    return jnp.stack([f.T, top.T], axis=1)


def _diff_attention(q, k, vt, rel_bias, lam4, subln, layer_idx):
    b, s, n = q.shape
    t = ATT_T
    w = 2 * DIFF_HEAD_DIM
    lam_init = 0.8 - 0.6 * math.exp(-0.3 * layer_idx)
    return pl.pallas_call(
        functools.partial(_diff_kernel, lam_init=lam_init),
        grid=(b, DIFF_HEADS, s // t),
        in_specs=[pl.BlockSpec((1, t, w), lambda bi, h, i: (bi, i, h)),
                  pl.BlockSpec((1, s, w), lambda bi, h, i: (bi, 0, h)),
                  pl.BlockSpec((1, s // t, w, t), lambda bi, h, i: (bi, 0, h, 0)),
                  pl.BlockSpec((1, 2, (ATT_NEAR + 1) * t), lambda bi, h, i: (h, 0, 0)),
                  pl.BlockSpec((4, DIFF_HEAD_DIM), lambda bi, h, i: (0, 0)),
                  pl.BlockSpec((1, w), lambda bi, h, i: (0, 0))],
        out_specs=pl.BlockSpec((1, t, w), lambda bi, h, i: (bi, i, h)),
        out_shape=jax.ShapeDtypeStruct((b, s, n), BF16),
        scratch_shapes=[pltpu.VMEM((2, t, w), BF16),
                        pltpu.VMEM((2, 1, t), F32),
                        pltpu.VMEM((2, 1, t), F32),
                        pltpu.VMEM((2, 1, t), F32),
                        pltpu.VMEM((2, SUBLANES, t), F32),
                        pltpu.VMEM((2, w, t), F32),
                        pltpu.VMEM((ATT_NEAR + 1, t, t), F32),
                        pltpu.VMEM((2, 2, 2, t, t), BF16)],
        compiler_params=_params(("arbitrary", "arbitrary", "arbitrary")),
        name="diff_attention",
    )(q, k, vt, _diff_bias(rel_bias, s), lam4, subln)


def _mix1_kernel(x_ref, a_ref, w_ref, out_ref):
    out_ref[...] = x_ref[...] + _dot(a_ref[...], w_ref[...])


def _mix1(x2, a, w):
    t, d = x2.shape
    n = a.shape[-1]
    return pl.pallas_call(
        _mix1_kernel,
        grid=(t // PROJ_TM,),
        in_specs=[pl.BlockSpec((PROJ_TM, d), lambda i: (i, 0)),
                  pl.BlockSpec((PROJ_TM, n), lambda i: (i, 0)),
                  pl.BlockSpec((n, d), lambda i: (0, 0))],
        out_specs=pl.BlockSpec((PROJ_TM, d), lambda i: (i, 0)),
        out_shape=jax.ShapeDtypeStruct((t, d), F32),
        compiler_params=_params(("arbitrary",)),
        name="mix1",
    )(x2, a, w)


def kernel(x, rel_bias, norm_mix, norm_ffn, norm_final, w_in_ab, ret_gn, w_out_ab, w_in_c,
           lam_q1, lam_k1, lam_q2, lam_k2, diff_subln, w_out_c, w_ffn_in, conv_w, conv_b,
           w_ffn_out):
    b, s, d = x.shape
    depth = norm_mix.shape[0]
    assert depth == 2 and s % RET_ROWS == 0 and s % FFN_TM == 0 and s >= ATT_NEAR * ATT_T
    assert s % (max(DIL_DILATIONS) * DIL_BLOCK) == 0
    row = lambda v: v.reshape(1, -1)

    proj = _proj0(x, row(norm_mix[0]), w_in_ab[0].astype(BF16))
    ret = _retention(proj[0], row(ret_gn[0]), b, s)
    outs, lses = zip(*[_dilated_pattern(dq, rel_bias, wdw, dil)
                       for dq, (wdw, dil) in zip(proj[1:], DIL_PATTERNS)])
    x3 = _mix0(x, ret, outs, lses, w_out_ab[0].astype(BF16))
    x3 = _ffn(x3, row(norm_ffn[0]), w_ffn_in[0].astype(BF16), conv_w[0],
              row(conv_b[0]), w_ffn_out[0].astype(BF16), row(norm_final), False)

    nqk = DIFF_HEADS * 2 * DIFF_HEAD_DIM
    wc = w_in_c[0].astype(BF16)
    q, k, vt = _proj1(x3, row(norm_mix[1]), wc[:, :nqk], wc[:, nqk:2 * nqk], wc[:, 2 * nqk:].T)
    lam4 = jnp.stack([lam_q1[0], lam_k1[0], lam_q2[0], lam_k2[0]]).astype(F32)
    a = _diff_attention(q, k, vt, rel_bias, lam4, row(diff_subln[0]), 1)
    x2 = _mix1(x3.reshape(b * s, d), a.reshape(b * s, -1), w_out_c[0].astype(BF16))
    return _ffn(x2.reshape(b, s, d), row(norm_ffn[1]), w_ffn_in[1].astype(BF16), conv_w[1],
                row(conv_b[1]), w_ffn_out[1].astype(BF16), row(norm_final), True)
```

```python
import functools
import math

import jax
import jax.numpy as jnp
from jax import lax
from jax.experimental import pallas as pl
from jax.experimental.pallas import tpu as pltpu

F32 = jnp.float32
BF16 = jnp.bfloat16

EPS = 1e-6
MASKED = -1e30

RET_HEADS = 4
RET_DIM = 128
RET_CHUNK = 128
ROPE_BASE = 10000.0
DIL_HEADS = 8
DIL_HEAD_DIM = 64
DIL_PATTERNS = ((128, 1), (512, 4), (2048, 16))
DIL_BLOCK = 128
DIFF_HEADS = 8
DIFF_HEAD_DIM = 64
REL_BUCKETS = 32
REL_MAX_DISTANCE = 2048
D_FF = 2816
CONV_WIDTH = 3
DIL_W = DIL_HEADS * DIL_HEAD_DIM
RET_W = 4 * RET_HEADS * RET_DIM
DIL_DILATIONS = tuple(d for _, d in DIL_PATTERNS)

LANES = 128
SUBLANES = 8
VMEM_BYTES_V7X = 64 * 1024 * 1024
VMEM_LIMIT = VMEM_BYTES_V7X - 8 * 1024 * 1024

PROJ_TM = 512
PROJ_TN = 512
RET_ROWS = 1024
DIL_QBLOCKS = 1
FFN_TM = 512
FFN_CF = 256
ATT_T = 512
ATT_KC = 512
ATT_BOUND_SLACK = 1.0 + 2.0 ** -5
ATT_MIN_DENOM = 2.0 ** -60
ATT_MAX_DENOM = 2.0 ** 100
ATT_NEAR = -(-(REL_MAX_DISTANCE + ATT_T - 1) // ATT_T)
LOG2E = math.log2(math.e)
ATT_QSCALE = DIFF_HEAD_DIM ** -0.5 * LOG2E


def _params(sem):
    return pltpu.CompilerParams(dimension_semantics=sem, vmem_limit_bytes=VMEM_LIMIT)


def _rms(x, g):
    return x * lax.rsqrt(jnp.mean(x * x, axis=-1, keepdims=True) + EPS) * g


def _dot(a, b):
    return jnp.dot(a, b, preferred_element_type=F32)


def _dot_nt(a, b):
    return lax.dot_general(a, b, (((1,), (1,)), ((), ())), preferred_element_type=F32)


def _dot_tn(a, b):
    return lax.dot_general(a, b, (((0,), (0,)), ((), ())), preferred_element_type=F32)


def _proj0_kernel(x_ref, g_ref, w_ref, a_ref, *rest):
    d_refs, dbuf = rest[:-1], rest[-1]
    tm = x_ref.shape[1]
    h = _rms(x_ref[0], g_ref[...]).astype(BF16)
    na = a_ref.shape[-1]
    for c in range(0, na, PROJ_TN):
        a_ref[0, :, c:c + PROJ_TN] = _dot(h, w_ref[:, c:c + PROJ_TN]).astype(BF16)
    nd = w_ref.shape[1] - na
    for c in range(0, nd, PROJ_TN):
        res = _dot(h, w_ref[:, na + c:na + c + PROJ_TN])
        for s in range(PROJ_TN // LANES):
            dbuf[(c // LANES) + s] = res[:, s * LANES:(s + 1) * LANES]
    for dref, d in zip(d_refs, DIL_DILATIONS):
        rows = tm // d
        for s in range(nd // LANES):
            cols = slice(s * LANES, (s + 1) * LANES)
            if d == 1:
                dref[0, 0, :, cols] = dbuf[s].astype(BF16)
            else:
                for r in range(d):
                    dref[0, r, :, cols] = dbuf[s, pl.ds(r, rows, stride=d), :].astype(BF16)


def _proj0(x3, g, w):
    b, s, dm = x3.shape
    n = w.shape[1]
    nd = n - RET_W
    tm = PROJ_TM
    d_specs = [pl.BlockSpec((1, d, tm // d, nd), lambda bi, i: (bi, 0, i, 0)) for d in DIL_DILATIONS]
    d_shapes = [jax.ShapeDtypeStruct((b, d, s // d, nd), BF16) for d in DIL_DILATIONS]
    return pl.pallas_call(
        _proj0_kernel,
        grid=(b, s // tm),
        in_specs=[pl.BlockSpec((1, tm, dm), lambda bi, i: (bi, i, 0)),
                  pl.BlockSpec((1, dm), lambda bi, i: (0, 0)),
                  pl.BlockSpec((dm, n), lambda bi, i: (0, 0))],
        out_specs=[pl.BlockSpec((1, tm, RET_W), lambda bi, i: (bi, i, 0))] + d_specs,
        out_shape=[jax.ShapeDtypeStruct((b, s, RET_W), BF16)] + d_shapes,
        scratch_shapes=[pltpu.VMEM((nd // LANES, tm, LANES), F32)],
        compiler_params=_params(("arbitrary", "arbitrary")),
        name="proj0",
    )(x3, g, w)


def _ret_kernel(q_ref, k_ref, v_ref, gate_ref, cos_ref, sin_ref, tab_ref, gn_ref, o_ref,
                state_ref):
    @pl.when(pl.program_id(2) == 0)
    def _():
        state_ref[...] = jnp.zeros_like(state_ref)

    decay = tab_ref[0, 0]
    xi = tab_ref[0, 1]
    zeta = tab_ref[0, 2]
    g_chunk = tab_ref[0, 3]
    gain = gn_ref[...]
    half = RET_DIM // 2
    for c in range(q_ref.shape[1] // RET_CHUNK):
        rows = pl.ds(c * RET_CHUNK, RET_CHUNK)
        cos = cos_ref[rows, :]
        sin = sin_ref[rows, :]
        q = q_ref[0, rows, :].astype(F32)
        k = k_ref[0, rows, :].astype(F32)
        q = q * cos + pltpu.roll(q, half, 1) * sin
        k = k * cos + pltpu.roll(k, half, 1) * sin
        v = v_ref[0, rows, :]
        state = state_ref[...]
        intra = _dot_nt(q.astype(BF16), k.astype(BF16)) * decay
        y = _dot(intra.astype(BF16), v) + _dot((q * xi).astype(BF16), state.astype(BF16))
        state_ref[...] = g_chunk * state + _dot_tn((k * zeta).astype(BF16), v)
        mu = jnp.mean(y, axis=-1, keepdims=True)
        yc = y - mu
        var = jnp.mean(yc * yc, axis=-1, keepdims=True)
        yn = yc * lax.rsqrt(var + EPS) * gain
        gate = gate_ref[0, rows, :].astype(F32)
        o_ref[0, rows, :] = (gate * jax.nn.sigmoid(gate) * yn).astype(o_ref.dtype)


def _retention_tables(s):
    c, e = RET_CHUNK, RET_DIM
    inv = ROPE_BASE ** (-jnp.arange(0, e, 2, dtype=F32) / e)
    ang = jnp.arange(s, dtype=F32)[:, None] * inv[None, :]
    cos = jnp.concatenate([jnp.cos(ang), jnp.cos(ang)], axis=-1)
    sin = jnp.concatenate([-jnp.sin(ang), jnp.sin(ang)], axis=-1)
    log_g = jnp.log1p(-jnp.exp2(-5.0 - jnp.arange(RET_HEADS, dtype=F32)))
    pos = jnp.arange(c, dtype=F32)
    rel = pos[:, None] - pos[None, :]
    causal = rel >= 0
    scale = e ** -0.5
    decay = jnp.where(causal, jnp.exp(jnp.where(causal, rel, 0.0)[None] * log_g[:, None, None]), 0.0)
    zeta = jnp.exp((c - 1 - pos)[None, :] * log_g[:, None])
    xi = jnp.exp((pos + 1)[None, :] * log_g[:, None])
    g_chunk = jnp.exp(c * log_g)
    tab = jnp.stack([
        decay * scale,
        jnp.broadcast_to(xi[:, :, None], (RET_HEADS, c, e)),
        jnp.broadcast_to(zeta[:, :, None] * scale, (RET_HEADS, c, e)),
        jnp.broadcast_to(g_chunk[:, None, None], (RET_HEADS, e, e)),
    ], axis=1)
    return cos, sin, tab


def _retention(p0, gn, b, s):
    cos, sin, tab = _retention_tables(s)
    e = RET_DIM
    col = lambda off: (lambda bi, h, t: (bi, t, off + h))
    return pl.pallas_call(
        _ret_kernel,
        grid=(b, RET_HEADS, s // RET_ROWS),
        in_specs=[pl.BlockSpec((1, RET_ROWS, e), col(0)),
                  pl.BlockSpec((1, RET_ROWS, e), col(RET_HEADS)),
                  pl.BlockSpec((1, RET_ROWS, e), col(2 * RET_HEADS)),
                  pl.BlockSpec((1, RET_ROWS, e), col(3 * RET_HEADS)),
                  pl.BlockSpec((RET_ROWS, e), lambda bi, h, t: (t, 0)),
                  pl.BlockSpec((RET_ROWS, e), lambda bi, h, t: (t, 0)),
                  pl.BlockSpec((1, 4, e, e), lambda bi, h, t: (h, 0, 0, 0)),
                  pl.BlockSpec((1, e), lambda bi, h, t: (0, h))],
        out_specs=pl.BlockSpec((1, RET_ROWS, e), lambda bi, h, t: (bi, t, h)),
        out_shape=jax.ShapeDtypeStruct((b, s, RET_HEADS * e), BF16),
        scratch_shapes=[pltpu.VMEM((e, e), F32)],
        compiler_params=_params(("arbitrary", "arbitrary", "arbitrary")),
        name="retention",
    )(p0, p0, p0, p0, cos, sin, tab, gn)


def _t5_bucket(dist):
    max_exact = REL_BUCKETS // 2
    d = jnp.maximum(dist.astype(F32), 1.0)
    large = max_exact + (jnp.log(d / max_exact) / math.log(REL_MAX_DISTANCE / max_exact)
                         * (REL_BUCKETS - max_exact))
    large = jnp.clip(large.astype(jnp.int32), max_exact, REL_BUCKETS - 1)
    return jnp.where(dist < max_exact, dist, large)


def _dil_kernel(q_ref, kp_ref, kc_ref, vp_ref, vc_ref, g_ref, o_ref, lse_ref, bias_ref):
    first = pl.program_id(2) == 0
    blk, e = DIL_BLOCK, DIL_HEAD_DIM

    @pl.when(jnp.logical_and(jnp.logical_and(pl.program_id(0) == 0, pl.program_id(1) == 0), first))
    def _():
        for h in range(DIL_HEADS):
            gb = jnp.broadcast_to(g_ref[h:h + 1, :], (blk, 4 * blk))
            bias_ref[h] = pltpu.roll(gb, 0, 1, stride=1, stride_axis=0)[:, :2 * blk]

    q_all = q_ref[0, 0] * (e ** -0.5)
    k_all = jnp.concatenate([kp_ref[0, 0], kc_ref[0, 0]], axis=0)
    v_all = jnp.concatenate([vp_ref[0, 0], vc_ref[0, 0]], axis=0)
    in_prev = lax.broadcasted_iota(jnp.int32, (blk, 2 * blk), 1) < blk
    no_prev = jnp.logical_and(first, in_prev)
    lane = lax.broadcasted_iota(jnp.int32, (blk, LANES), 1)
    for qb in range(q_all.shape[0] // blk):
        q = q_all[qb * blk:(qb + 1) * blk]
        kcat = k_all[qb * blk:(qb + 2) * blk]
        vcat = v_all[qb * blk:(qb + 2) * blk]
        lse_all = jnp.zeros((blk, LANES), F32)
        for h in range(DIL_HEADS):
            cols = slice(h * e, (h + 1) * e)
            s = _dot_nt(q[:, cols], kcat[:, cols]) + bias_ref[h]
            if qb == 0:
                s = jnp.where(no_prev, MASKED, s)
            m = jnp.max(s, axis=-1, keepdims=True)
            p = jnp.exp(s - m)
            l = jnp.sum(p, axis=-1, keepdims=True)
            o_ref[0, 0, qb * blk:(qb + 1) * blk, cols] = (
                _dot(p.astype(BF16), vcat[:, cols]) / l).astype(o_ref.dtype)
            lse_all = jnp.where(lane == h, m + jnp.log(l), lse_all)
        lse_ref[0, 0, qb * blk:(qb + 1) * blk, :] = lse_all


def _dil_bias_vector(rel_bias, window, dil):
    blk = DIL_BLOCK
    span = window // dil
    r = blk - jnp.arange(4 * blk)
    band = (r >= 0) & (r <= span)
    bias = rel_bias[_t5_bucket(jnp.maximum(r, 0) * dil)].astype(F32)
    return jnp.where(band[:, None], bias, MASKED).T


def _dilated_pattern(dq, rel_bias, window, dil):
    b, d, sd, _ = dq.shape
    blk = DIL_BLOCK
    nq = DIL_QBLOCKS
    rows = nq * blk
    assert d == dil and window // dil == blk and sd % rows == 0
    cur = lambda c: (lambda bi, r, n: (bi, r, n, c))
    prev = lambda c: (lambda bi, r, n: (bi, r, jnp.maximum(n * nq - 1, 0), c))
    spec = lambda im: pl.BlockSpec((1, 1, rows, DIL_W), im)
    pspec = lambda im: pl.BlockSpec((1, 1, blk, DIL_W), im)
    return pl.pallas_call(
        _dil_kernel,
        grid=(b, dil, sd // rows),
        in_specs=[spec(cur(0)), pspec(prev(1)), spec(cur(1)), pspec(prev(2)), spec(cur(2)),
                  pl.BlockSpec((DIL_HEADS, 4 * blk), lambda bi, r, n: (0, 0))],
        out_specs=[pl.BlockSpec((1, 1, rows, DIL_W), lambda bi, r, n: (bi, r, n, 0)),
                   pl.BlockSpec((1, 1, rows, LANES), lambda bi, r, n: (bi, r, n, 0))],
        out_shape=[jax.ShapeDtypeStruct((b, dil, sd, DIL_W), BF16),
                   jax.ShapeDtypeStruct((b, dil, sd, LANES), F32)],
        scratch_shapes=[pltpu.VMEM((DIL_HEADS, blk, 2 * blk), F32)],
        compiler_params=_params(("arbitrary", "arbitrary", "arbitrary")),
        name=f"dilated_d{dil}",
    )(dq, dq, dq, dq, dq, _dil_bias_vector(rel_bias, window, dil))


def _mix0_kernel(x_ref, ret_ref, *rest):
    np_ = len(DIL_DILATIONS)
    o_refs, l_refs = rest[:np_], rest[np_:2 * np_]
    w_ref, out_ref, obuf, lbuf, dbuf = rest[2 * np_:]
    tm = x_ref.shape[1]
    nslab = DIL_W // LANES
    for pi, d in enumerate(DIL_DILATIONS):
        rows = tm // d
        for r in range(d):
            dst = pl.ds(r, rows, stride=d) if d > 1 else pl.ds(0, tm)
            lbuf[pi, dst, :] = l_refs[pi][0, r]
            for s in range(nslab):
                obuf[pi, s, dst, :] = o_refs[pi][0, r, :, s * LANES:(s + 1) * LANES].astype(F32)
    lses = [lbuf[pi] for pi in range(np_)]
    m = functools.reduce(jnp.maximum, lses)
    es = [jnp.exp(l - m) for l in lses]
    inv = 1.0 / functools.reduce(lambda a, b: a + b, es)
    row = lax.broadcasted_iota(jnp.int32, (LANES, DIL_W), 0)
    col = lax.broadcasted_iota(jnp.int32, (LANES, DIL_W), 1)
    spread = (col // DIL_HEAD_DIM == row).astype(BF16)
    wide = []
    for ex in es:
        wgt = ex * inv
        hi = wgt.astype(BF16)
        lo = (wgt - hi.astype(F32)).astype(BF16)
        wide.append(_dot(hi, spread) + _dot(lo, spread))
    for s in range(nslab):
        cols = slice(s * LANES, (s + 1) * LANES)
        acc = wide[0][:, cols] * obuf[0, s]
        for pi in range(1, np_):
            acc = acc + wide[pi][:, cols] * obuf[pi, s]
        dbuf[:, cols] = acc.astype(BF16)
    na = ret_ref.shape[-1]
    y = _dot(ret_ref[0], w_ref[:na, :]) + _dot(dbuf[...], w_ref[na:, :])
    out_ref[0] = x_ref[0] + y


def _mix0(x3, ret, outs, lses, w):
    b, s, dm = x3.shape
    na = ret.shape[-1]
    tm = PROJ_TM
    o_specs = [pl.BlockSpec((1, d, tm // d, DIL_W), lambda bi, i: (bi, 0, i, 0)) for d in DIL_DILATIONS]
    l_specs = [pl.BlockSpec((1, d, tm // d, LANES), lambda bi, i: (bi, 0, i, 0)) for d in DIL_DILATIONS]
    np_ = len(DIL_DILATIONS)
    return pl.pallas_call(
        _mix0_kernel,
        grid=(b, s // tm),
        in_specs=[pl.BlockSpec((1, tm, dm), lambda bi, i: (bi, i, 0)),
                  pl.BlockSpec((1, tm, na), lambda bi, i: (bi, i, 0))] + o_specs + l_specs
                 + [pl.BlockSpec((na + DIL_W, dm), lambda bi, i: (0, 0))],
        out_specs=pl.BlockSpec((1, tm, dm), lambda bi, i: (bi, i, 0)),
        out_shape=jax.ShapeDtypeStruct((b, s, dm), F32),
        scratch_shapes=[pltpu.VMEM((np_, DIL_W // LANES, tm, LANES), F32),
                        pltpu.VMEM((np_, tm, LANES), F32),
                        pltpu.VMEM((tm, DIL_W), BF16)],
        compiler_params=_params(("arbitrary", "arbitrary")),
        name="mix0",
    )(x3, ret, *outs, *lses, w)


def _ffn_kernel(x_ref, *rest, final_norm, mixer_proj):
    if mixer_proj:
        a_ref, wmix_ref = rest[:2]
        rest = rest[2:]
    g_ref, win_ref, cw_ref, cb_ref, wout_ref, fg_ref, o_ref, ubuf, carry, gbuf = rest
    tm = x_ref.shape[1]
    halo = SUBLANES

    @pl.when(pl.program_id(1) == 0)
    def _():
        carry[...] = jnp.zeros_like(carry)

    x = x_ref[0]
    if mixer_proj:
        x = x + _dot(a_ref[0], wmix_ref[...])
    h = _rms(x, g_ref[...]).astype(BF16)

    def conv(part, col):
        cols = slice(col, col + FFN_CF)
        u = _dot(h, win_ref[:, cols])
        ubuf[part, 0:halo, :] = carry[:, cols]
        ubuf[part, halo:halo + tm, :] = u
        carry[:, cols] = u[tm - halo:, :]
        u1 = ubuf[part, halo - 1:halo - 1 + tm, :]
        u2 = ubuf[part, halo - 2:halo - 2 + tm, :]
        return (u * cw_ref[2:3, cols] + u1 * cw_ref[1:2, cols] + u2 * cw_ref[0:1, cols]
                + cb_ref[:, cols])

    for c in range(D_FF // FFN_CF):
        gate = conv(0, c * FFN_CF)
        up = conv(1, D_FF + c * FFN_CF)
        act = 0.5 * gate * (1.0 + lax.erf(gate * (2.0 ** -0.5)))
        gbuf[:, c * FFN_CF:(c + 1) * FFN_CF] = (act * up).astype(BF16)

    y = x + _dot(gbuf[...], wout_ref[...])
    if final_norm:
        y = _rms(y, fg_ref[...])
    o_ref[0] = y


def _ffn(x3, g, w_in, conv_w, conv_b, w_out, final_gain, final_norm, mixer=None):
    b, s, d = x3.shape
    f2 = w_in.shape[1]
    whole = pl.BlockSpec(memory_space=pltpu.VMEM)
    tile = lambda n: pl.BlockSpec((1, FFN_TM, n), lambda bi, t: (bi, t, 0))
    mix_specs = [tile(mixer[0].shape[-1]), whole] if mixer is not None else []
    mix_args = list(mixer) if mixer is not None else []
    return pl.pallas_call(
        functools.partial(_ffn_kernel, final_norm=final_norm, mixer_proj=mixer is not None),
        grid=(b, s // FFN_TM),
        in_specs=[tile(d)] + mix_specs + [whole, whole, whole, whole, whole, whole],
        out_specs=pl.BlockSpec((1, FFN_TM, d), lambda bi, t: (bi, t, 0)),
        out_shape=jax.ShapeDtypeStruct((b, s, d), F32),
        scratch_shapes=[pltpu.VMEM((2, FFN_TM + SUBLANES, FFN_CF), F32),
                        pltpu.VMEM((SUBLANES, f2), F32),
                        pltpu.VMEM((FFN_TM, D_FF), BF16)],
        compiler_params=_params(("arbitrary", "arbitrary")),
        name="ffn_final" if final_norm else "ffn",
    )(x3, *mix_args, g, w_in, conv_w, conv_b, w_out, final_gain)


def _proj1_kernel(x_ref, g_ref, wq_ref, wk_ref, wvt_ref, q_ref, k_ref, vt_ref):
    h = _rms(x_ref[0], g_ref[...]).astype(BF16)
    n = k_ref.shape[-1]
    for c in range(0, n, PROJ_TN):
        cols = slice(c, c + PROJ_TN)
        q_ref[0, 0, cols, :] = (_dot_nt(wq_ref[cols, :], h) * ATT_QSCALE).astype(BF16)
        k_ref[0, :, cols] = _dot(h, wk_ref[:, cols]).astype(BF16)
        vt_ref[0, 0, cols, :] = _dot_nt(wvt_ref[cols, :], h).astype(BF16)


def _proj1(x3, g, wq, wk, wvt):
    b, s, d = x3.shape
    n = wq.shape[1]
    t = ATT_T
    wspec = pl.BlockSpec((d, n), lambda bi, i: (0, 0))
    return pl.pallas_call(
        _proj1_kernel,
        grid=(b, s // t),
        in_specs=[pl.BlockSpec((1, t, d), lambda bi, i: (bi, i, 0)),
                  pl.BlockSpec((1, d), lambda bi, i: (0, 0)),
                  pl.BlockSpec((n, d), lambda bi, i: (0, 0)), wspec,
                  pl.BlockSpec((n, d), lambda bi, i: (0, 0))],
        out_specs=[pl.BlockSpec((1, 1, n, t), lambda bi, i: (bi, i, 0, 0)),
                   pl.BlockSpec((1, t, n), lambda bi, i: (bi, i, 0)),
                   pl.BlockSpec((1, 1, n, t), lambda bi, i: (bi, i, 0, 0))],
        out_shape=[jax.ShapeDtypeStruct((b, s // t, n, t), BF16),
                   jax.ShapeDtypeStruct((b, s, n), BF16),
                   jax.ShapeDtypeStruct((b, s // t, n, t), BF16)],
        compiler_params=_params(("arbitrary", "arbitrary")),
        name="proj1",
    )(x3, g, wq, wk, wvt)


def _diff_kernel(q_ref, k_ref, vt_ref, f_ref, lam_ref, sub_ref, o_ref,
                 qm_ref, km_ref, m_ref, l_ref, l8_ref, acc_ref, nb_ref, p_ref, *, lam_init):
    i = pl.program_id(2)
    t = ATT_T
    e = DIFF_HEAD_DIM
    half = lax.broadcasted_iota(jnp.int32, (SUBLANES, 2 * e), 1) // e
    sel = (lax.broadcasted_iota(jnp.int32, (SUBLANES, 2 * e), 0) == half).astype(BF16)

    @pl.when(i == 0)
    def _():
        for o in range(ATT_NEAR):
            g = jnp.broadcast_to(f_ref[0, 0:1, o * t:(o + 2) * t], (t, 2 * t))
            nb_ref[o] = pltpu.roll(g, 0, 1, stride=1, stride_axis=0)[:, t:]
        nb_ref[ATT_NEAR] = jnp.zeros((t, t), F32)
        kmax = jnp.zeros((SUBLANES, t), F32)
        for c in range(k_ref.shape[1] // t):
            kk = k_ref[0, c * t:(c + 1) * t, :].astype(F32)
            kmax = jnp.maximum(kmax, _dot_nt(sel, (kk * kk).astype(BF16)))
        for a in range(2):
            km_ref[a] = jnp.broadcast_to(jnp.max(kmax[a:a + 1, :], axis=1, keepdims=True), (1, t))

    qt = q_ref[0, 0]
    row = lax.broadcasted_iota(jnp.int32, qt.shape, 0)
    zero = jnp.zeros_like(qt)
    qm_ref[0] = jnp.where(row < e, qt, zero)
    qm_ref[1] = jnp.where(row >= e, qt, zero)
    qf = qt.astype(F32)
    qq = qf * qf
    qn2 = [jnp.sum(qq[a * e:(a + 1) * e], axis=0, keepdims=True) for a in range(2)]
    bias_max = f_ref[0, 1:2, 0:t]
    n_far = jnp.maximum(i - (ATT_NEAR - 1), 0)

    l8_ref[...] = jnp.zeros_like(l8_ref)
    acc_ref[...] = jnp.zeros_like(acc_ref)
    shift = [jnp.sqrt(qn2[a] * km_ref[a]) * ATT_BOUND_SLACK + bias_max for a in range(2)]

    kc = ATT_KC

    def fast_tiles(j0, ntiles, biased):
        for a in range(2):
            pv = None
            l8 = None
            for tt in range(ntiles):
                j = j0 + tt
                for c in range(0, t, kc):
                    kb = k_ref[0, pl.ds(pl.multiple_of(j * t + c, kc), kc), :]
                    s = _dot(kb, qm_ref[a])
                    if biased:
                        s = s + nb_ref[jnp.minimum(i - j, ATT_NEAR), c:c + kc, :]
                    p = jnp.exp2(s - shift[a])
                    part = jnp.sum(p.reshape(kc // SUBLANES, SUBLANES, t), axis=0)
                    d = _dot(vt_ref[0, j, :, c:c + kc], p.astype(BF16))
                    l8 = part if l8 is None else l8 + part
                    pv = d if pv is None else pv + d
            l8_ref[a] += l8
            acc_ref[a] += pv

    def probs(u, biased):
        slot = u % 2
        for a in range(2):
            l8 = None
            for tt in range(2):
                j = 2 * u + tt
                kb = k_ref[0, pl.ds(pl.multiple_of(j * t, t), t), :]
                s = _dot(kb, qm_ref[a])
                if biased:
                    s = s + nb_ref[jnp.minimum(i - j, ATT_NEAR)]
                p = jnp.exp2(s - shift[a])
                part = jnp.sum(p.reshape(t // SUBLANES, SUBLANES, t), axis=0)
                l8 = part if l8 is None else l8 + part
                p_ref[slot, a, tt] = p.astype(BF16)
            l8_ref[a] += l8

    def values(u):
        slot = u % 2
        for a in range(2):
            acc_ref[a] += (_dot(vt_ref[0, 2 * u], p_ref[slot, a, 0])
                           + _dot(vt_ref[0, 2 * u + 1], p_ref[slot, a, 1]))

    def far_step(u, carry):
        values(u - 1)
        probs(u, False)
        return carry

    def near_step(u, carry):
        values(u - 1)
        probs(u, True)
        return carry

    def fast_near(j, carry):
        fast_tiles(j, 1, True)
        return carry

    n = i + 1
    pairs = n // 2
    far_pairs = n_far // 2

    @pl.when(far_pairs > 0)
    def _():
        probs(0, False)

    @pl.when(jnp.logical_and(far_pairs == 0, pairs > 0))
    def _():
        probs(0, True)

    lax.fori_loop(1, far_pairs, far_step, 0)
    lax.fori_loop(jnp.maximum(far_pairs, 1), pairs, near_step, 0)

    @pl.when(pairs > 0)
    def _():
        values(pairs - 1)

    lax.fori_loop(2 * pairs, n, fast_near, 0)
    for a in range(2):
        l_ref[a] = jnp.sum(l8_ref[a], axis=0, keepdims=True)
    l_min = jnp.min(jnp.minimum(l_ref[0], l_ref[1]))
    l_max = jnp.max(jnp.maximum(l_ref[0], l_ref[1]))
    in_range = jnp.logical_and(l_min >= ATT_MIN_DENOM, l_max <= ATT_MAX_DENOM)

    @pl.when(jnp.logical_not(in_range))
    def _():
        m_ref[...] = jnp.full_like(m_ref, MASKED)
        l_ref[...] = jnp.zeros_like(l_ref)
        acc_ref[...] = jnp.zeros_like(acc_ref)

        def online_tile(j, bias):
            kb = k_ref[0, pl.ds(pl.multiple_of(j * t, t), t), :]
            vtb = vt_ref[0, j]
            for a in range(2):
                s = _dot(kb, qm_ref[a])
                if bias is not None:
                    s = s + bias
                m_old = m_ref[a]
                m_new = jnp.maximum(m_old, jnp.max(s, axis=0, keepdims=True))
                alpha = jnp.exp2(m_old - m_new)
                p = jnp.exp2(s - m_new)
                l_ref[a] = alpha * l_ref[a] + jnp.sum(p, axis=0, keepdims=True)
                acc_ref[a] = alpha * acc_ref[a] + _dot(vtb, p.astype(BF16))
                m_ref[a] = m_new

        def online_far(j, carry):
            online_tile(j, None)
            return carry

        def online_near(j, carry):
            online_tile(j, nb_ref[i - j])
            return carry

        lax.fori_loop(0, n_far, online_far, 0)
        lax.fori_loop(n_far, i + 1, online_near, 0)

    lam = (jnp.exp(jnp.sum(lam_ref[0:1, :] * lam_ref[1:2, :], axis=-1, keepdims=True))
           - jnp.exp(jnp.sum(lam_ref[2:3, :] * lam_ref[3:4, :], axis=-1, keepdims=True))
           + lam_init)
    ot = acc_ref[0] / l_ref[0] - lam * (acc_ref[1] / l_ref[1])
    ot = ot * lax.rsqrt(jnp.mean(ot * ot, axis=0, keepdims=True) + EPS)
    o_ref[0] = (ot.T * sub_ref[...] * (1.0 - lam_init)).astype(o_ref.dtype)


def _diff_bias(rel_bias, s):
    t = ATT_T
    n = ATT_NEAR * t
    assert n >= REL_MAX_DISTANCE + t - 1 and s >= n
    dist_bias = rel_bias[_t5_bucket(jnp.arange(n))].astype(F32)
    far = rel_bias[REL_BUCKETS - 1].astype(F32)
    near = (dist_bias - far) * LOG2E
    f = jnp.concatenate([jnp.full((t, far.shape[0]), MASKED, F32), near])
    top = jnp.broadcast_to(jnp.maximum(jnp.max(near, axis=0), 0.0), f.shape)
    return jnp.stack([f.T, top.T], axis=1)


def _diff_attention(q, k, vt, rel_bias, lam4, subln, layer_idx):
    b, s, n = k.shape
    t = ATT_T
    w = 2 * DIFF_HEAD_DIM
    lam_init = 0.8 - 0.6 * math.exp(-0.3 * layer_idx)
    return pl.pallas_call(
        functools.partial(_diff_kernel, lam_init=lam_init),
        grid=(b, DIFF_HEADS, s // t),
        in_specs=[pl.BlockSpec((1, 1, w, t), lambda bi, h, i: (bi, i, h, 0)),
                  pl.BlockSpec((1, s, w), lambda bi, h, i: (bi, 0, h)),
                  pl.BlockSpec((1, s // t, w, t), lambda bi, h, i: (bi, 0, h, 0)),
                  pl.BlockSpec((1, 2, (ATT_NEAR + 1) * t), lambda bi, h, i: (h, 0, 0)),
                  pl.BlockSpec((4, DIFF_HEAD_DIM), lambda bi, h, i: (0, 0)),
                  pl.BlockSpec((1, w), lambda bi, h, i: (0, 0))],
        out_specs=pl.BlockSpec((1, t, w), lambda bi, h, i: (bi, i, h)),
        out_shape=jax.ShapeDtypeStruct((b, s, n), BF16),
        scratch_shapes=[pltpu.VMEM((2, w, t), BF16),
                        pltpu.VMEM((2, 1, t), F32),
                        pltpu.VMEM((2, 1, t), F32),
                        pltpu.VMEM((2, 1, t), F32),
                        pltpu.VMEM((2, SUBLANES, t), F32),
                        pltpu.VMEM((2, w, t), F32),
                        pltpu.VMEM((ATT_NEAR + 1, t, t), F32),
                        pltpu.VMEM((2, 2, 2, t, t), BF16)],
        compiler_params=_params(("arbitrary", "arbitrary", "arbitrary")),
        name="diff_attention",
    )(q, k, vt, _diff_bias(rel_bias, s), lam4, subln)


def kernel(x, rel_bias, norm_mix, norm_ffn, norm_final, w_in_ab, ret_gn, w_out_ab, w_in_c,
           lam_q1, lam_k1, lam_q2, lam_k2, diff_subln, w_out_c, w_ffn_in, conv_w, conv_b,
           w_ffn_out):
    b, s, d = x.shape
    depth = norm_mix.shape[0]
    assert depth == 2 and s % RET_ROWS == 0 and s % FFN_TM == 0 and s >= ATT_NEAR * ATT_T
    assert s % (max(DIL_DILATIONS) * DIL_BLOCK) == 0
    row = lambda v: v.reshape(1, -1)

    proj = _proj0(x, row(norm_mix[0]), w_in_ab[0].astype(BF16))
    ret = _retention(proj[0], row(ret_gn[0]), b, s)
    outs, lses = zip(*[_dilated_pattern(dq, rel_bias, wdw, dil)
                       for dq, (wdw, dil) in zip(proj[1:], DIL_PATTERNS)])
    x3 = _mix0(x, ret, outs, lses, w_out_ab[0].astype(BF16))
    x3 = _ffn(x3, row(norm_ffn[0]), w_ffn_in[0].astype(BF16), conv_w[0],
              row(conv_b[0]), w_ffn_out[0].astype(BF16), row(norm_final), False)

    nqk = DIFF_HEADS * 2 * DIFF_HEAD_DIM
    wc = w_in_c[0].astype(BF16)
    q, k, vt = _proj1(x3, row(norm_mix[1]), wc[:, :nqk].T, wc[:, nqk:2 * nqk], wc[:, 2 * nqk:].T)
    lam4 = jnp.stack([lam_q1[0], lam_k1[0], lam_q2[0], lam_k2[0]]).astype(F32)
    a = _diff_attention(q, k, vt, rel_bias, lam4, row(diff_subln[0]), 1)
    return _ffn(x3, row(norm_ffn[1]), w_ffn_in[1].astype(BF16), conv_w[1], row(conv_b[1]),
                w_ffn_out[1].astype(BF16), row(norm_final), True,
                mixer=(a, w_out_c[0].astype(BF16)))
```

```python
import functools
import math

import jax
import jax.numpy as jnp
from jax import lax
from jax.experimental import pallas as pl
from jax.experimental.pallas import tpu as pltpu

F32 = jnp.float32
BF16 = jnp.bfloat16

EPS = 1e-6
MASKED = -1e30

RET_HEADS = 4
RET_DIM = 128
RET_CHUNK = 128
ROPE_BASE = 10000.0
DIL_HEADS = 8
DIL_HEAD_DIM = 64
DIL_PATTERNS = ((128, 1), (512, 4), (2048, 16))
DIL_BLOCK = 128
DIFF_HEADS = 8
DIFF_HEAD_DIM = 64
REL_BUCKETS = 32
REL_MAX_DISTANCE = 2048
D_FF = 2816
CONV_WIDTH = 3
DIL_W = DIL_HEADS * DIL_HEAD_DIM
RET_W = 4 * RET_HEADS * RET_DIM
DIL_DILATIONS = tuple(d for _, d in DIL_PATTERNS)

LANES = 128
SUBLANES = 8
VMEM_BYTES_V7X = 64 * 1024 * 1024
VMEM_LIMIT = VMEM_BYTES_V7X - 8 * 1024 * 1024

PROJ_TM = 512
PROJ_TN = 512
RET_ROWS = 1024
DIL_QBLOCKS = 2
FFN_TM = 512
FFN_CF = 256
ATT_T = 512
ATT_KC = 512
ATT_BOUND_SLACK = 1.0 + 2.0 ** -5
ATT_MIN_DENOM = 2.0 ** -60
ATT_MAX_DENOM = 2.0 ** 100
ATT_NEAR = -(-(REL_MAX_DISTANCE + ATT_T - 1) // ATT_T)
LOG2E = math.log2(math.e)
ATT_QSCALE = DIFF_HEAD_DIM ** -0.5 * LOG2E


def _params(sem):
    return pltpu.CompilerParams(dimension_semantics=sem, vmem_limit_bytes=VMEM_LIMIT)


def _rms(x, g):
    return x * lax.rsqrt(jnp.mean(x * x, axis=-1, keepdims=True) + EPS) * g


def _dot(a, b):
    return jnp.dot(a, b, preferred_element_type=F32)


def _dot_nt(a, b):
    return lax.dot_general(a, b, (((1,), (1,)), ((), ())), preferred_element_type=F32)


def _dot_tn(a, b):
    return lax.dot_general(a, b, (((0,), (0,)), ((), ())), preferred_element_type=F32)


def _proj0_kernel(x_ref, g_ref, w_ref, a_ref, *rest):
    d_refs, dbuf = rest[:-1], rest[-1]
    tm = x_ref.shape[1]
    h = _rms(x_ref[0], g_ref[...]).astype(BF16)
    na = a_ref.shape[-1]
    nd = w_ref.shape[1] - na
    for c in range(0, nd, PROJ_TN):
        res = _dot(h, w_ref[:, na + c:na + c + PROJ_TN])
        for s in range(c // LANES, (c + PROJ_TN) // LANES):
            cols = slice(s * LANES, (s + 1) * LANES)
            dbuf[0, s] = res[:, (s * LANES - c):(s * LANES - c) + LANES]
            src, d_prev = 0, 1
            for dref, d in zip(d_refs, DIL_DILATIONS):
                if d == 1:
                    dref[0, 0, :, cols] = dbuf[0, s].astype(BF16)
                    continue
                step, rows = d // d_prev, tm // d
                dst = 1 - src
                for rp in range(d_prev):
                    for q in range(step):
                        piece = dbuf[src, s, pl.ds(rp * (tm // d_prev) + q, rows, stride=step), :]
                        r = rp + q * d_prev
                        if d != DIL_DILATIONS[-1]:
                            dbuf[dst, s, r * rows:(r + 1) * rows, :] = piece
                        dref[0, r, :, cols] = piece.astype(BF16)
                src, d_prev = dst, d
    for c in range(0, na, PROJ_TN):
        a_ref[0, :, c:c + PROJ_TN] = _dot(h, w_ref[:, c:c + PROJ_TN]).astype(BF16)


def _proj0(x3, g, w):
    b, s, dm = x3.shape
    n = w.shape[1]
    nd = n - RET_W
    tm = PROJ_TM
    d_specs = [pl.BlockSpec((1, d, tm // d, nd), lambda bi, i: (bi, 0, i, 0)) for d in DIL_DILATIONS]
    d_shapes = [jax.ShapeDtypeStruct((b, d, s // d, nd), BF16) for d in DIL_DILATIONS]
    return pl.pallas_call(
        _proj0_kernel,
        grid=(b, s // tm),
        in_specs=[pl.BlockSpec((1, tm, dm), lambda bi, i: (bi, i, 0)),
                  pl.BlockSpec((1, dm), lambda bi, i: (0, 0)),
                  pl.BlockSpec((dm, n), lambda bi, i: (0, 0))],
        out_specs=[pl.BlockSpec((1, tm, RET_W), lambda bi, i: (bi, i, 0))] + d_specs,
        out_shape=[jax.ShapeDtypeStruct((b, s, RET_W), BF16)] + d_shapes,
        scratch_shapes=[pltpu.VMEM((2, nd // LANES, tm, LANES), F32)],
        compiler_params=_params(("arbitrary", "arbitrary")),
        name="proj0",
    )(x3, g, w)


def _ret_kernel(q_ref, k_ref, v_ref, gate_ref, cos_ref, sin_ref, tab_ref, gn_ref, o_ref,
                state_ref):
    @pl.when(pl.program_id(2) == 0)
    def _():
        state_ref[...] = jnp.zeros_like(state_ref)

    decay = tab_ref[0, 0]
    xi = tab_ref[0, 1]
    zeta = tab_ref[0, 2]
    g_chunk = tab_ref[0, 3]
    gain = gn_ref[...]
    half = RET_DIM // 2
    for c in range(q_ref.shape[1] // RET_CHUNK):
        rows = pl.ds(c * RET_CHUNK, RET_CHUNK)
        cos = cos_ref[rows, :]
        sin = sin_ref[rows, :]
        q = q_ref[0, rows, :].astype(F32)
        k = k_ref[0, rows, :].astype(F32)
        q = q * cos + pltpu.roll(q, half, 1) * sin
        k = k * cos + pltpu.roll(k, half, 1) * sin
        v = v_ref[0, rows, :]
        state = state_ref[...]
        intra = _dot_nt(q.astype(BF16), k.astype(BF16)) * decay
        y = _dot(intra.astype(BF16), v) + _dot((q * xi).astype(BF16), state.astype(BF16))
        state_ref[...] = g_chunk * state + _dot_tn((k * zeta).astype(BF16), v)
        mu = jnp.mean(y, axis=-1, keepdims=True)
        yc = y - mu
        var = jnp.mean(yc * yc, axis=-1, keepdims=True)
        yn = yc * lax.rsqrt(var + EPS) * gain
        gate = gate_ref[0, rows, :].astype(F32)
        o_ref[0, rows, :] = (gate * jax.nn.sigmoid(gate) * yn).astype(o_ref.dtype)


def _retention_tables(s):
    c, e = RET_CHUNK, RET_DIM
    inv = ROPE_BASE ** (-jnp.arange(0, e, 2, dtype=F32) / e)
    ang = jnp.arange(s, dtype=F32)[:, None] * inv[None, :]
    cos = jnp.concatenate([jnp.cos(ang), jnp.cos(ang)], axis=-1)
    sin = jnp.concatenate([-jnp.sin(ang), jnp.sin(ang)], axis=-1)
    log_g = jnp.log1p(-jnp.exp2(-5.0 - jnp.arange(RET_HEADS, dtype=F32)))
    pos = jnp.arange(c, dtype=F32)
    rel = pos[:, None] - pos[None, :]
    causal = rel >= 0
    scale = e ** -0.5
    decay = jnp.where(causal, jnp.exp(jnp.where(causal, rel, 0.0)[None] * log_g[:, None, None]), 0.0)
    zeta = jnp.exp((c - 1 - pos)[None, :] * log_g[:, None])
    xi = jnp.exp((pos + 1)[None, :] * log_g[:, None])
    g_chunk = jnp.exp(c * log_g)
    tab = jnp.stack([
        decay * scale,
        jnp.broadcast_to(xi[:, :, None], (RET_HEADS, c, e)),
        jnp.broadcast_to(zeta[:, :, None] * scale, (RET_HEADS, c, e)),
        jnp.broadcast_to(g_chunk[:, None, None], (RET_HEADS, e, e)),
    ], axis=1)
    return cos, sin, tab


def _retention(p0, gn, b, s):
    cos, sin, tab = _retention_tables(s)
    e = RET_DIM
    col = lambda off: (lambda bi, h, t: (bi, t, off + h))
    return pl.pallas_call(
        _ret_kernel,
        grid=(b, RET_HEADS, s // RET_ROWS),
        in_specs=[pl.BlockSpec((1, RET_ROWS, e), col(0)),
                  pl.BlockSpec((1, RET_ROWS, e), col(RET_HEADS)),
                  pl.BlockSpec((1, RET_ROWS, e), col(2 * RET_HEADS)),
                  pl.BlockSpec((1, RET_ROWS, e), col(3 * RET_HEADS)),
                  pl.BlockSpec((RET_ROWS, e), lambda bi, h, t: (t, 0)),
                  pl.BlockSpec((RET_ROWS, e), lambda bi, h, t: (t, 0)),
                  pl.BlockSpec((1, 4, e, e), lambda bi, h, t: (h, 0, 0, 0)),
                  pl.BlockSpec((1, e), lambda bi, h, t: (0, h))],
        out_specs=pl.BlockSpec((1, RET_ROWS, e), lambda bi, h, t: (bi, t, h)),
        out_shape=jax.ShapeDtypeStruct((b, s, RET_HEADS * e), BF16),
        scratch_shapes=[pltpu.VMEM((e, e), F32)],
        compiler_params=_params(("arbitrary", "arbitrary", "arbitrary")),
        name="retention",
    )(p0, p0, p0, p0, cos, sin, tab, gn)


def _t5_bucket(dist):
    max_exact = REL_BUCKETS // 2
    d = jnp.maximum(dist.astype(F32), 1.0)
    large = max_exact + (jnp.log(d / max_exact) / math.log(REL_MAX_DISTANCE / max_exact)
                         * (REL_BUCKETS - max_exact))
    large = jnp.clip(large.astype(jnp.int32), max_exact, REL_BUCKETS - 1)
    return jnp.where(dist < max_exact, dist, large)


def _dil_kernel(q_ref, kp_ref, kc_ref, vp_ref, vc_ref, g_ref, o_ref, lse_ref, bias_ref):
    first = pl.program_id(2) == 0
    blk, e = DIL_BLOCK, DIL_HEAD_DIM

    @pl.when(jnp.logical_and(jnp.logical_and(pl.program_id(0) == 0, pl.program_id(1) == 0), first))
    def _():
        for h in range(DIL_HEADS):
            gb = jnp.broadcast_to(g_ref[h:h + 1, :], (blk, 4 * blk))
            bias_ref[h] = pltpu.roll(gb, 0, 1, stride=1, stride_axis=0)[:, :2 * blk]

    q_all = q_ref[0, 0] * (e ** -0.5)
    k_all = jnp.concatenate([kp_ref[0, 0], kc_ref[0, 0]], axis=0)
    v_all = jnp.concatenate([vp_ref[0, 0], vc_ref[0, 0]], axis=0)
    in_prev = lax.broadcasted_iota(jnp.int32, (blk, 2 * blk), 1) < blk
    no_prev = jnp.logical_and(first, in_prev)
    lane = lax.broadcasted_iota(jnp.int32, (blk, LANES), 1)
    for qb in range(q_all.shape[0] // blk):
        q = q_all[qb * blk:(qb + 1) * blk]
        kcat = k_all[qb * blk:(qb + 2) * blk]
        vcat = v_all[qb * blk:(qb + 2) * blk]
        lse_all = jnp.zeros((blk, LANES), F32)
        for h in range(DIL_HEADS):
            cols = slice(h * e, (h + 1) * e)
            s = _dot_nt(q[:, cols], kcat[:, cols]) + bias_ref[h]
            if qb == 0:
                s = jnp.where(no_prev, MASKED, s)
            m = jnp.max(s, axis=-1, keepdims=True)
            p = jnp.exp(s - m)
            l = jnp.sum(p, axis=-1, keepdims=True)
            o_ref[0, 0, qb * blk:(qb + 1) * blk, cols] = (
                _dot(p.astype(BF16), vcat[:, cols]) / l).astype(o_ref.dtype)
            lse_all = jnp.where(lane == h, m + jnp.log(l), lse_all)
        lse_ref[0, 0, qb * blk:(qb + 1) * blk, :] = lse_all


def _dil_bias_vector(rel_bias, window, dil):
    blk = DIL_BLOCK
    span = window // dil
    r = blk - jnp.arange(4 * blk)
    band = (r >= 0) & (r <= span)
    bias = rel_bias[_t5_bucket(jnp.maximum(r, 0) * dil)].astype(F32)
    return jnp.where(band[:, None], bias, MASKED).T


def _dilated_pattern(dq, rel_bias, window, dil):
    b, d, sd, _ = dq.shape
    blk = DIL_BLOCK
    nq = DIL_QBLOCKS
    rows = nq * blk
    assert d == dil and window // dil == blk and sd % rows == 0
    cur = lambda c: (lambda bi, r, n: (bi, r, n, c))
    prev = lambda c: (lambda bi, r, n: (bi, r, jnp.maximum(n * nq - 1, 0), c))
    spec = lambda im: pl.BlockSpec((1, 1, rows, DIL_W), im)
    pspec = lambda im: pl.BlockSpec((1, 1, blk, DIL_W), im)
    return pl.pallas_call(
        _dil_kernel,
        grid=(b, dil, sd // rows),
        in_specs=[spec(cur(0)), pspec(prev(1)), spec(cur(1)), pspec(prev(2)), spec(cur(2)),
                  pl.BlockSpec((DIL_HEADS, 4 * blk), lambda bi, r, n: (0, 0))],
        out_specs=[pl.BlockSpec((1, 1, rows, DIL_W), lambda bi, r, n: (bi, r, n, 0)),
                   pl.BlockSpec((1, 1, rows, LANES), lambda bi, r, n: (bi, r, n, 0))],
        out_shape=[jax.ShapeDtypeStruct((b, dil, sd, DIL_W), BF16),
                   jax.ShapeDtypeStruct((b, dil, sd, LANES), F32)],
        scratch_shapes=[pltpu.VMEM((DIL_HEADS, blk, 2 * blk), F32)],
        compiler_params=_params(("arbitrary", "arbitrary", "arbitrary")),
        name=f"dilated_d{dil}",
    )(dq, dq, dq, dq, dq, _dil_bias_vector(rel_bias, window, dil))


def _mix0_kernel(x_ref, ret_ref, *rest):
    np_ = len(DIL_DILATIONS)
    o_refs, l_refs = rest[:np_], rest[np_:2 * np_]
    w_ref, out_ref, obuf, lbuf, dbuf = rest[2 * np_:]
    tm = x_ref.shape[1]
    nslab = DIL_W // LANES
    for pi, d in enumerate(DIL_DILATIONS):
        rows = tm // d
        for r in range(d):
            dst = pl.ds(r, rows, stride=d) if d > 1 else pl.ds(0, tm)
            lbuf[pi, dst, :] = l_refs[pi][0, r]
            for s in range(nslab):
                obuf[pi, s, dst, :] = o_refs[pi][0, r, :, s * LANES:(s + 1) * LANES].astype(F32)
    lses = [lbuf[pi] for pi in range(np_)]
    m = functools.reduce(jnp.maximum, lses)
    es = [jnp.exp(l - m) for l in lses]
    inv = 1.0 / functools.reduce(lambda a, b: a + b, es)
    row = lax.broadcasted_iota(jnp.int32, (LANES, DIL_W), 0)
    col = lax.broadcasted_iota(jnp.int32, (LANES, DIL_W), 1)
    spread = (col // DIL_HEAD_DIM == row).astype(BF16)
    wide = []
    for ex in es:
        wgt = ex * inv
        hi = wgt.astype(BF16)
        lo = (wgt - hi.astype(F32)).astype(BF16)
        wide.append(_dot(hi, spread) + _dot(lo, spread))
    for s in range(nslab):
        cols = slice(s * LANES, (s + 1) * LANES)
        acc = wide[0][:, cols] * obuf[0, s]
        for pi in range(1, np_):
            acc = acc + wide[pi][:, cols] * obuf[pi, s]
        dbuf[:, cols] = acc.astype(BF16)
    na = ret_ref.shape[-1]
    y = _dot(ret_ref[0], w_ref[:na, :]) + _dot(dbuf[...], w_ref[na:, :])
    out_ref[0] = x_ref[0] + y


def _mix0(x3, ret, outs, lses, w):
    b, s, dm = x3.shape
    na = ret.shape[-1]
    tm = PROJ_TM
    o_specs = [pl.BlockSpec((1, d, tm // d, DIL_W), lambda bi, i: (bi, 0, i, 0)) for d in DIL_DILATIONS]
    l_specs = [pl.BlockSpec((1, d, tm // d, LANES), lambda bi, i: (bi, 0, i, 0)) for d in DIL_DILATIONS]
    np_ = len(DIL_DILATIONS)
    return pl.pallas_call(
        _mix0_kernel,
        grid=(b, s // tm),
        in_specs=[pl.BlockSpec((1, tm, dm), lambda bi, i: (bi, i, 0)),
                  pl.BlockSpec((1, tm, na), lambda bi, i: (bi, i, 0))] + o_specs + l_specs
                 + [pl.BlockSpec((na + DIL_W, dm), lambda bi, i: (0, 0))],
        out_specs=pl.BlockSpec((1, tm, dm), lambda bi, i: (bi, i, 0)),
        out_shape=jax.ShapeDtypeStruct((b, s, dm), F32),
        scratch_shapes=[pltpu.VMEM((np_, DIL_W // LANES, tm, LANES), F32),
                        pltpu.VMEM((np_, tm, LANES), F32),
                        pltpu.VMEM((tm, DIL_W), BF16)],
        compiler_params=_params(("arbitrary", "arbitrary")),
        name="mix0",
    )(x3, ret, *outs, *lses, w)


def _ffn_kernel(x_ref, *rest, final_norm, mixer_proj):
    if mixer_proj:
        a_ref, wmix_ref = rest[:2]
        rest = rest[2:]
    g_ref, win_ref, cw_ref, cb_ref, wout_ref, fg_ref, o_ref, ubuf, carry, gbuf = rest
    tm = x_ref.shape[1]
    halo = SUBLANES

    @pl.when(pl.program_id(1) == 0)
    def _():
        carry[...] = jnp.zeros_like(carry)

    x = x_ref[0]
    if mixer_proj:
        x = x + _dot(a_ref[0], wmix_ref[...])
    h = _rms(x, g_ref[...]).astype(BF16)

    def conv(part, col):
        cols = slice(col, col + FFN_CF)
        u = _dot(h, win_ref[:, cols])
        ubuf[part, 0:halo, :] = carry[:, cols]
        ubuf[part, halo:halo + tm, :] = u
        carry[:, cols] = u[tm - halo:, :]
        u1 = ubuf[part, halo - 1:halo - 1 + tm, :]
        u2 = ubuf[part, halo - 2:halo - 2 + tm, :]
        return (u * cw_ref[2:3, cols] + u1 * cw_ref[1:2, cols] + u2 * cw_ref[0:1, cols]
                + cb_ref[:, cols])

    for c in range(D_FF // FFN_CF):
        gate = conv(0, c * FFN_CF)
        up = conv(1, D_FF + c * FFN_CF)
        act = 0.5 * gate * (1.0 + lax.erf(gate * (2.0 ** -0.5)))
        gbuf[:, c * FFN_CF:(c + 1) * FFN_CF] = (act * up).astype(BF16)

    y = x + _dot(gbuf[...], wout_ref[...])
    if final_norm:
        y = _rms(y, fg_ref[...])
    o_ref[0] = y


def _ffn(x3, g, w_in, conv_w, conv_b, w_out, final_gain, final_norm, mixer=None):
    b, s, d = x3.shape
    f2 = w_in.shape[1]
    whole = pl.BlockSpec(memory_space=pltpu.VMEM)
    tile = lambda n: pl.BlockSpec((1, FFN_TM, n), lambda bi, t: (bi, t, 0))
    mix_specs = [tile(mixer[0].shape[-1]), whole] if mixer is not None else []
    mix_args = list(mixer) if mixer is not None else []
    return pl.pallas_call(
        functools.partial(_ffn_kernel, final_norm=final_norm, mixer_proj=mixer is not None),
        grid=(b, s // FFN_TM),
        in_specs=[tile(d)] + mix_specs + [whole, whole, whole, whole, whole, whole],
        out_specs=pl.BlockSpec((1, FFN_TM, d), lambda bi, t: (bi, t, 0)),
        out_shape=jax.ShapeDtypeStruct((b, s, d), F32),
        scratch_shapes=[pltpu.VMEM((2, FFN_TM + SUBLANES, FFN_CF), F32),
                        pltpu.VMEM((SUBLANES, f2), F32),
                        pltpu.VMEM((FFN_TM, D_FF), BF16)],
        compiler_params=_params(("arbitrary", "arbitrary")),
        name="ffn_final" if final_norm else "ffn",
    )(x3, *mix_args, g, w_in, conv_w, conv_b, w_out, final_gain)


def _proj1_kernel(x_ref, g_ref, wq_ref, wk_ref, wvt_ref, q_ref, k_ref, vt_ref):
    h = _rms(x_ref[0], g_ref[...]).astype(BF16)
    n = k_ref.shape[-1]
    for c in range(0, n, PROJ_TN):
        cols = slice(c, c + PROJ_TN)
        q_ref[0, 0, cols, :] = (_dot_nt(wq_ref[cols, :], h) * ATT_QSCALE).astype(BF16)
        k_ref[0, :, cols] = _dot(h, wk_ref[:, cols]).astype(BF16)
        vt_ref[0, 0, cols, :] = _dot_nt(wvt_ref[cols, :], h).astype(BF16)


def _proj1(x3, g, wq, wk, wvt):
    b, s, d = x3.shape
    n = wq.shape[1]
    t = ATT_T
    wspec = pl.BlockSpec((d, n), lambda bi, i: (0, 0))
    return pl.pallas_call(
        _proj1_kernel,
        grid=(b, s // t),
        in_specs=[pl.BlockSpec((1, t, d), lambda bi, i: (bi, i, 0)),
                  pl.BlockSpec((1, d), lambda bi, i: (0, 0)),
                  pl.BlockSpec((n, d), lambda bi, i: (0, 0)), wspec,
                  pl.BlockSpec((n, d), lambda bi, i: (0, 0))],
        out_specs=[pl.BlockSpec((1, 1, n, t), lambda bi, i: (bi, i, 0, 0)),
                   pl.BlockSpec((1, t, n), lambda bi, i: (bi, i, 0)),
                   pl.BlockSpec((1, 1, n, t), lambda bi, i: (bi, i, 0, 0))],
        out_shape=[jax.ShapeDtypeStruct((b, s // t, n, t), BF16),
                   jax.ShapeDtypeStruct((b, s, n), BF16),
                   jax.ShapeDtypeStruct((b, s // t, n, t), BF16)],
        compiler_params=_params(("arbitrary", "arbitrary")),
        name="proj1",
    )(x3, g, wq, wk, wvt)


def _diff_kernel(q_ref, k_ref, vt_ref, f_ref, lam_ref, sub_ref, o_ref,
                 qm_ref, km_ref, m_ref, l_ref, l8_ref, acc_ref, nb_ref, p_ref, *, lam_init):
    i = pl.program_id(2)
    t = ATT_T
    e = DIFF_HEAD_DIM
    half = lax.broadcasted_iota(jnp.int32, (SUBLANES, 2 * e), 1) // e
    sel = (lax.broadcasted_iota(jnp.int32, (SUBLANES, 2 * e), 0) == half).astype(BF16)

    @pl.when(i == 0)
    def _():
        for o in range(ATT_NEAR):
            g = jnp.broadcast_to(f_ref[0, 0:1, o * t:(o + 2) * t], (t, 2 * t))
            nb_ref[o] = pltpu.roll(g, 0, 1, stride=1, stride_axis=0)[:, t:]
        nb_ref[ATT_NEAR] = jnp.zeros((t, t), F32)
        kmax = jnp.zeros((SUBLANES, t), F32)
        for c in range(k_ref.shape[1] // t):
            kk = k_ref[0, c * t:(c + 1) * t, :].astype(F32)
            kmax = jnp.maximum(kmax, _dot_nt(sel, (kk * kk).astype(BF16)))
        for a in range(2):
            km_ref[a] = jnp.broadcast_to(jnp.max(kmax[a:a + 1, :], axis=1, keepdims=True), (1, t))

    qt = q_ref[0, 0]
    row = lax.broadcasted_iota(jnp.int32, qt.shape, 0)
    zero = jnp.zeros_like(qt)
    qm_ref[0] = jnp.where(row < e, qt, zero)
    qm_ref[1] = jnp.where(row >= e, qt, zero)
    qf = qt.astype(F32)
    qq = qf * qf
    qn2 = [jnp.sum(qq[a * e:(a + 1) * e], axis=0, keepdims=True) for a in range(2)]
    bias_max = f_ref[0, 1:2, 0:t]
    n_far = jnp.maximum(i - (ATT_NEAR - 1), 0)

    l8_ref[...] = jnp.zeros_like(l8_ref)
    acc_ref[...] = jnp.zeros_like(acc_ref)
    shift = [jnp.sqrt(qn2[a] * km_ref[a]) * ATT_BOUND_SLACK + bias_max for a in range(2)]

    kc = ATT_KC

    def fast_tiles(j0, ntiles, biased):
        for a in range(2):
            pv = None
            l8 = None
            for tt in range(ntiles):
                j = j0 + tt
                for c in range(0, t, kc):
                    kb = k_ref[0, pl.ds(pl.multiple_of(j * t + c, kc), kc), :]
                    s = _dot(kb, qm_ref[a])
                    if biased:
                        s = s + nb_ref[jnp.minimum(i - j, ATT_NEAR), c:c + kc, :]
                    p = jnp.exp2(s - shift[a])
                    part = jnp.sum(p.reshape(kc // SUBLANES, SUBLANES, t), axis=0)
                    d = _dot(vt_ref[0, j, :, c:c + kc], p.astype(BF16))
                    l8 = part if l8 is None else l8 + part
                    pv = d if pv is None else pv + d
            l8_ref[a] += l8
            acc_ref[a] += pv

    def probs(u, biased):
        slot = u % 2
        for a in range(2):
            l8 = None
            for tt in range(2):
                j = 2 * u + tt
                kb = k_ref[0, pl.ds(pl.multiple_of(j * t, t), t), :]
                s = _dot(kb, qm_ref[a])
                if biased:
                    s = s + nb_ref[jnp.minimum(i - j, ATT_NEAR)]
                p = jnp.exp2(s - shift[a])
                part = jnp.sum(p.reshape(t // SUBLANES, SUBLANES, t), axis=0)
                l8 = part if l8 is None else l8 + part
                p_ref[slot, a, tt] = p.astype(BF16)
            l8_ref[a] += l8

    def values(u):
        slot = u % 2
        for a in range(2):
            acc_ref[a] += (_dot(vt_ref[0, 2 * u], p_ref[slot, a, 0])
                           + _dot(vt_ref[0, 2 * u + 1], p_ref[slot, a, 1]))

    def far_step(u, carry):
        values(u - 1)
        probs(u, False)
        return carry

    def near_step(u, carry):
        values(u - 1)
        probs(u, True)
        return carry

    def fast_near(j, carry):
        fast_tiles(j, 1, True)
        return carry

    n = i + 1
    pairs = n // 2
    far_pairs = n_far // 2

    @pl.when(far_pairs > 0)
    def _():
        probs(0, False)

    @pl.when(jnp.logical_and(far_pairs == 0, pairs > 0))
    def _():
        probs(0, True)

    lax.fori_loop(1, far_pairs, far_step, 0)
    lax.fori_loop(jnp.maximum(far_pairs, 1), pairs, near_step, 0)

    @pl.when(pairs > 0)
    def _():
        values(pairs - 1)

    lax.fori_loop(2 * pairs, n, fast_near, 0)
    for a in range(2):
        l_ref[a] = jnp.sum(l8_ref[a], axis=0, keepdims=True)
    l_min = jnp.min(jnp.minimum(l_ref[0], l_ref[1]))
    l_max = jnp.max(jnp.maximum(l_ref[0], l_ref[1]))
    in_range = jnp.logical_and(l_min >= ATT_MIN_DENOM, l_max <= ATT_MAX_DENOM)

    @pl.when(jnp.logical_not(in_range))
    def _():
        m_ref[...] = jnp.full_like(m_ref, MASKED)
        l_ref[...] = jnp.zeros_like(l_ref)
        acc_ref[...] = jnp.zeros_like(acc_ref)

        def online_tile(j, bias):
            kb = k_ref[0, pl.ds(pl.multiple_of(j * t, t), t), :]
            vtb = vt_ref[0, j]
            for a in range(2):
                s = _dot(kb, qm_ref[a])
                if bias is not None:
                    s = s + bias
                m_old = m_ref[a]
                m_new = jnp.maximum(m_old, jnp.max(s, axis=0, keepdims=True))
                alpha = jnp.exp2(m_old - m_new)
                p = jnp.exp2(s - m_new)
                l_ref[a] = alpha * l_ref[a] + jnp.sum(p, axis=0, keepdims=True)
                acc_ref[a] = alpha * acc_ref[a] + _dot(vtb, p.astype(BF16))
                m_ref[a] = m_new

        def online_far(j, carry):
            online_tile(j, None)
            return carry

        def online_near(j, carry):
            online_tile(j, nb_ref[i - j])
            return carry

        lax.fori_loop(0, n_far, online_far, 0)
        lax.fori_loop(n_far, i + 1, online_near, 0)

    lam = (jnp.exp(jnp.sum(lam_ref[0:1, :] * lam_ref[1:2, :], axis=-1, keepdims=True))
           - jnp.exp(jnp.sum(lam_ref[2:3, :] * lam_ref[3:4, :], axis=-1, keepdims=True))
           + lam_init)
    ot = acc_ref[0] / l_ref[0] - lam * (acc_ref[1] / l_ref[1])
    ot = ot * lax.rsqrt(jnp.mean(ot * ot, axis=0, keepdims=True) + EPS)
    o_ref[0] = (ot.T * sub_ref[...] * (1.0 - lam_init)).astype(o_ref.dtype)


def _diff_bias(rel_bias, s):
    t = ATT_T
    n = ATT_NEAR * t
    assert n >= REL_MAX_DISTANCE + t - 1 and s >= n
    dist_bias = rel_bias[_t5_bucket(jnp.arange(n))].astype(F32)
    far = rel_bias[REL_BUCKETS - 1].astype(F32)
    near = (dist_bias - far) * LOG2E
    f = jnp.concatenate([jnp.full((t, far.shape[0]), MASKED, F32), near])
    top = jnp.broadcast_to(jnp.maximum(jnp.max(near, axis=0), 0.0), f.shape)
    return jnp.stack([f.T, top.T], axis=1)


def _diff_attention(q, k, vt, rel_bias, lam4, subln, layer_idx):
    b, s, n = k.shape
    t = ATT_T
    w = 2 * DIFF_HEAD_DIM
    lam_init = 0.8 - 0.6 * math.exp(-0.3 * layer_idx)
    return pl.pallas_call(
        functools.partial(_diff_kernel, lam_init=lam_init),
        grid=(b, DIFF_HEADS, s // t),
        in_specs=[pl.BlockSpec((1, 1, w, t), lambda bi, h, i: (bi, i, h, 0)),
                  pl.BlockSpec((1, s, w), lambda bi, h, i: (bi, 0, h)),
                  pl.BlockSpec((1, s // t, w, t), lambda bi, h, i: (bi, 0, h, 0)),
                  pl.BlockSpec((1, 2, (ATT_NEAR + 1) * t), lambda bi, h, i: (h, 0, 0)),
                  pl.BlockSpec((4, DIFF_HEAD_DIM), lambda bi, h, i: (0, 0)),
                  pl.BlockSpec((1, w), lambda bi, h, i: (0, 0))],
        out_specs=pl.BlockSpec((1, t, w), lambda bi, h, i: (bi, i, h)),
        out_shape=jax.ShapeDtypeStruct((b, s, n), BF16),
        scratch_shapes=[pltpu.VMEM((2, w, t), BF16),
                        pltpu.VMEM((2, 1, t), F32),
                        pltpu.VMEM((2, 1, t), F32),
                        pltpu.VMEM((2, 1, t), F32),
                        pltpu.VMEM((2, SUBLANES, t), F32),
                        pltpu.VMEM((2, w, t), F32),
                        pltpu.VMEM((ATT_NEAR + 1, t, t), F32),
                        pltpu.VMEM((2, 2, 2, t, t), BF16)],
        compiler_params=_params(("arbitrary", "arbitrary", "arbitrary")),
        name="diff_attention",
    )(q, k, vt, _diff_bias(rel_bias, s), lam4, subln)


def kernel(x, rel_bias, norm_mix, norm_ffn, norm_final, w_in_ab, ret_gn, w_out_ab, w_in_c,
           lam_q1, lam_k1, lam_q2, lam_k2, diff_subln, w_out_c, w_ffn_in, conv_w, conv_b,
           w_ffn_out):
    b, s, d = x.shape
    depth = norm_mix.shape[0]
    assert depth == 2 and s % RET_ROWS == 0 and s % FFN_TM == 0 and s >= ATT_NEAR * ATT_T
    assert s % (max(DIL_DILATIONS) * DIL_BLOCK) == 0
    row = lambda v: v.reshape(1, -1)

    proj = _proj0(x, row(norm_mix[0]), w_in_ab[0].astype(BF16))
    ret = _retention(proj[0], row(ret_gn[0]), b, s)
    outs, lses = zip(*[_dilated_pattern(dq, rel_bias, wdw, dil)
                       for dq, (wdw, dil) in zip(proj[1:], DIL_PATTERNS)])
    x3 = _mix0(x, ret, outs, lses, w_out_ab[0].astype(BF16))
    x3 = _ffn(x3, row(norm_ffn[0]), w_ffn_in[0].astype(BF16), conv_w[0],
              row(conv_b[0]), w_ffn_out[0].astype(BF16), row(norm_final), False)

    nqk = DIFF_HEADS * 2 * DIFF_HEAD_DIM
    wc = w_in_c[0].astype(BF16)
    q, k, vt = _proj1(x3, row(norm_mix[1]), wc[:, :nqk].T, wc[:, nqk:2 * nqk], wc[:, 2 * nqk:].T)
    lam4 = jnp.stack([lam_q1[0], lam_k1[0], lam_q2[0], lam_k2[0]]).astype(F32)
    a = _diff_attention(q, k, vt, rel_bias, lam4, row(diff_subln[0]), 1)
    return _ffn(x3, row(norm_ffn[1]), w_ffn_in[1].astype(BF16), conv_w[1], row(conv_b[1]),
                w_ffn_out[1].astype(BF16), row(norm_final), True,
                mixer=(a, w_out_c[0].astype(BF16)))
```

```python
import functools
import math

import jax
import jax.numpy as jnp
from jax import lax
from jax.experimental import pallas as pl
from jax.experimental.pallas import tpu as pltpu

F32 = jnp.float32
BF16 = jnp.bfloat16

EPS = 1e-6
MASKED = -1e30

RET_HEADS = 4
RET_DIM = 128
RET_CHUNK = 128
ROPE_BASE = 10000.0
DIL_HEADS = 8
DIL_HEAD_DIM = 64
DIL_PATTERNS = ((128, 1), (512, 4), (2048, 16))
DIL_BLOCK = 128
DIFF_HEADS = 8
DIFF_HEAD_DIM = 64
REL_BUCKETS = 32
REL_MAX_DISTANCE = 2048
D_FF = 2816
CONV_WIDTH = 3
DIL_W = DIL_HEADS * DIL_HEAD_DIM
RET_W = 4 * RET_HEADS * RET_DIM
DIL_DILATIONS = tuple(d for _, d in DIL_PATTERNS)

LANES = 128
SUBLANES = 8
VMEM_BYTES_V7X = 64 * 1024 * 1024
VMEM_LIMIT = VMEM_BYTES_V7X - 8 * 1024 * 1024

PROJ_TM = 512
PROJ_TN = 512
RET_ROWS = 1024
DIL_QBLOCKS = 2
FFN_TM = 512
FFN_CF = 256
ATT_T = 512
ATT_KC = 512
ATT_BOUND_SLACK = 1.0 + 2.0 ** -5
ATT_MIN_DENOM = 2.0 ** -60
ATT_MAX_DENOM = 2.0 ** 100
ATT_NEAR = -(-(REL_MAX_DISTANCE + ATT_T - 1) // ATT_T)
LOG2E = math.log2(math.e)
ATT_QSCALE = DIFF_HEAD_DIM ** -0.5 * LOG2E


def _params(sem):
    return pltpu.CompilerParams(dimension_semantics=sem, vmem_limit_bytes=VMEM_LIMIT)


def _rms(x, g):
    return x * lax.rsqrt(jnp.mean(x * x, axis=-1, keepdims=True) + EPS) * g


def _dot(a, b):
    return jnp.dot(a, b, preferred_element_type=F32)


def _dot_nt(a, b):
    return lax.dot_general(a, b, (((1,), (1,)), ((), ())), preferred_element_type=F32)


def _dot_tn(a, b):
    return lax.dot_general(a, b, (((0,), (0,)), ((), ())), preferred_element_type=F32)


def _proj0_kernel(x_ref, g_ref, w_ref, a_ref, *rest):
    d_refs, dbuf = rest[:-1], rest[-1]
    tm = x_ref.shape[1]
    h = _rms(x_ref[0], g_ref[...]).astype(BF16)
    na = a_ref.shape[-1]
    nd = w_ref.shape[1] - na
    for c in range(0, nd, PROJ_TN):
        res = _dot(h, w_ref[:, na + c:na + c + PROJ_TN])
        for s in range(c // LANES, (c + PROJ_TN) // LANES):
            cols = slice(s * LANES, (s + 1) * LANES)
            dbuf[0, s] = res[:, (s * LANES - c):(s * LANES - c) + LANES]
            src, d_prev = 0, 1
            for dref, d in zip(d_refs, DIL_DILATIONS):
                if d == 1:
                    dref[0, 0, :, cols] = dbuf[0, s].astype(BF16)
                    continue
                step, rows = d // d_prev, tm // d
                dst = 1 - src
                for rp in range(d_prev):
                    for q in range(step):
                        piece = dbuf[src, s, pl.ds(rp * (tm // d_prev) + q, rows, stride=step), :]
                        r = rp + q * d_prev
                        if d != DIL_DILATIONS[-1]:
                            dbuf[dst, s, r * rows:(r + 1) * rows, :] = piece
                        dref[0, r, :, cols] = piece.astype(BF16)
                src, d_prev = dst, d
    for c in range(0, na, PROJ_TN):
        a_ref[0, :, c:c + PROJ_TN] = _dot(h, w_ref[:, c:c + PROJ_TN]).astype(BF16)


def _proj0(x3, g, w):
    b, s, dm = x3.shape
    n = w.shape[1]
    nd = n - RET_W
    tm = PROJ_TM
    d_specs = [pl.BlockSpec((1, d, tm // d, nd), lambda bi, i: (bi, 0, i, 0)) for d in DIL_DILATIONS]
    d_shapes = [jax.ShapeDtypeStruct((b, d, s // d, nd), BF16) for d in DIL_DILATIONS]
    return pl.pallas_call(
        _proj0_kernel,
        grid=(b, s // tm),
        in_specs=[pl.BlockSpec((1, tm, dm), lambda bi, i: (bi, i, 0)),
                  pl.BlockSpec((1, dm), lambda bi, i: (0, 0)),
                  pl.BlockSpec((dm, n), lambda bi, i: (0, 0))],
        out_specs=[pl.BlockSpec((1, tm, RET_W), lambda bi, i: (bi, i, 0))] + d_specs,
        out_shape=[jax.ShapeDtypeStruct((b, s, RET_W), BF16)] + d_shapes,
        scratch_shapes=[pltpu.VMEM((2, nd // LANES, tm, LANES), F32)],
        compiler_params=_params(("arbitrary", "arbitrary")),
        name="proj0",
    )(x3, g, w)


def _ret_kernel(q_ref, k_ref, v_ref, gate_ref, cos_ref, sin_ref, tab_ref, gn_ref, o_ref,
                state_ref):
    @pl.when(pl.program_id(2) == 0)
    def _():
        state_ref[...] = jnp.zeros_like(state_ref)

    decay = tab_ref[0, 0]
    xi = tab_ref[0, 1]
    zeta = tab_ref[0, 2]
    g_chunk = tab_ref[0, 3]
    gain = gn_ref[...]
    half = RET_DIM // 2
    for c in range(q_ref.shape[1] // RET_CHUNK):
        rows = pl.ds(c * RET_CHUNK, RET_CHUNK)
        cos = cos_ref[rows, :]
        sin = sin_ref[rows, :]
        q = q_ref[0, rows, :].astype(F32)
        k = k_ref[0, rows, :].astype(F32)
        q = q * cos + pltpu.roll(q, half, 1) * sin
        k = k * cos + pltpu.roll(k, half, 1) * sin
        v = v_ref[0, rows, :]
        state = state_ref[...]
        intra = _dot_nt(q.astype(BF16), k.astype(BF16)) * decay
        y = _dot(intra.astype(BF16), v) + _dot((q * xi).astype(BF16), state.astype(BF16))
        state_ref[...] = g_chunk * state + _dot_tn((k * zeta).astype(BF16), v)
        mu = jnp.mean(y, axis=-1, keepdims=True)
        yc = y - mu
        var = jnp.mean(yc * yc, axis=-1, keepdims=True)
        yn = yc * lax.rsqrt(var + EPS) * gain
        gate = gate_ref[0, rows, :].astype(F32)
        o_ref[0, rows, :] = (gate * jax.nn.sigmoid(gate) * yn).astype(o_ref.dtype)


def _retention_tables(s):
    c, e = RET_CHUNK, RET_DIM
    inv = ROPE_BASE ** (-jnp.arange(0, e, 2, dtype=F32) / e)
    ang = jnp.arange(s, dtype=F32)[:, None] * inv[None, :]
    cos = jnp.concatenate([jnp.cos(ang), jnp.cos(ang)], axis=-1)
    sin = jnp.concatenate([-jnp.sin(ang), jnp.sin(ang)], axis=-1)
    log_g = jnp.log1p(-jnp.exp2(-5.0 - jnp.arange(RET_HEADS, dtype=F32)))
    pos = jnp.arange(c, dtype=F32)
    rel = pos[:, None] - pos[None, :]
    causal = rel >= 0
    scale = e ** -0.5
    decay = jnp.where(causal, jnp.exp(jnp.where(causal, rel, 0.0)[None] * log_g[:, None, None]), 0.0)
    zeta = jnp.exp((c - 1 - pos)[None, :] * log_g[:, None])
    xi = jnp.exp((pos + 1)[None, :] * log_g[:, None])
    g_chunk = jnp.exp(c * log_g)
    tab = jnp.stack([
        decay * scale,
        jnp.broadcast_to(xi[:, :, None], (RET_HEADS, c, e)),
        jnp.broadcast_to(zeta[:, :, None] * scale, (RET_HEADS, c, e)),
        jnp.broadcast_to(g_chunk[:, None, None], (RET_HEADS, e, e)),
    ], axis=1)
    return cos, sin, tab


def _retention(p0, gn, b, s):
    cos, sin, tab = _retention_tables(s)
    e = RET_DIM
    col = lambda off: (lambda bi, h, t: (bi, t, off + h))
    return pl.pallas_call(
        _ret_kernel,
        grid=(b, RET_HEADS, s // RET_ROWS),
        in_specs=[pl.BlockSpec((1, RET_ROWS, e), col(0)),
                  pl.BlockSpec((1, RET_ROWS, e), col(RET_HEADS)),
                  pl.BlockSpec((1, RET_ROWS, e), col(2 * RET_HEADS)),
                  pl.BlockSpec((1, RET_ROWS, e), col(3 * RET_HEADS)),
                  pl.BlockSpec((RET_ROWS, e), lambda bi, h, t: (t, 0)),
                  pl.BlockSpec((RET_ROWS, e), lambda bi, h, t: (t, 0)),
                  pl.BlockSpec((1, 4, e, e), lambda bi, h, t: (h, 0, 0, 0)),
                  pl.BlockSpec((1, e), lambda bi, h, t: (0, h))],
        out_specs=pl.BlockSpec((1, RET_ROWS, e), lambda bi, h, t: (bi, t, h)),
        out_shape=jax.ShapeDtypeStruct((b, s, RET_HEADS * e), BF16),
        scratch_shapes=[pltpu.VMEM((e, e), F32)],
        compiler_params=_params(("arbitrary", "arbitrary", "arbitrary")),
        name="retention",
    )(p0, p0, p0, p0, cos, sin, tab, gn)


def _t5_bucket(dist):
    max_exact = REL_BUCKETS // 2
    d = jnp.maximum(dist.astype(F32), 1.0)
    large = max_exact + (jnp.log(d / max_exact) / math.log(REL_MAX_DISTANCE / max_exact)
                         * (REL_BUCKETS - max_exact))
    large = jnp.clip(large.astype(jnp.int32), max_exact, REL_BUCKETS - 1)
    return jnp.where(dist < max_exact, dist, large)


def _dil_kernel(q_ref, kp_ref, kc_ref, vp_ref, vc_ref, g_ref, o_ref, lse_ref, bias_ref,
                s_ref, p_ref):
    first = pl.program_id(2) == 0
    blk, e = DIL_BLOCK, DIL_HEAD_DIM

    @pl.when(jnp.logical_and(jnp.logical_and(pl.program_id(0) == 0, pl.program_id(1) == 0), first))
    def _():
        for h in range(DIL_HEADS):
            gb = jnp.broadcast_to(g_ref[h:h + 1, :], (blk, 4 * blk))
            bias_ref[h] = pltpu.roll(gb, 0, 1, stride=1, stride_axis=0)[:, :2 * blk]

    q_all = q_ref[0, 0] * (e ** -0.5)
    k_all = jnp.concatenate([kp_ref[0, 0], kc_ref[0, 0]], axis=0)
    v_all = jnp.concatenate([vp_ref[0, 0], vc_ref[0, 0]], axis=0)
    in_prev = lax.broadcasted_iota(jnp.int32, (blk, 2 * blk), 1) < blk
    no_prev = jnp.logical_and(first, in_prev)
    lane = lax.broadcasted_iota(jnp.int32, (blk, LANES), 1)
    for qb in range(q_all.shape[0] // blk):
        q = q_all[qb * blk:(qb + 1) * blk]
        kcat = k_all[qb * blk:(qb + 2) * blk]
        vcat = v_all[qb * blk:(qb + 2) * blk]
        lse_all = jnp.zeros((blk, LANES), F32)
        for h in range(DIL_HEADS):
            cols = slice(h * e, (h + 1) * e)
            s = _dot_nt(q[:, cols], kcat[:, cols]) + bias_ref[h]
            if qb == 0:
                s = jnp.where(no_prev, MASKED, s)
            s_ref[qb, h] = s
        inv_l = []
        for h in range(DIL_HEADS):
            s = s_ref[qb, h]
            m = jnp.max(s, axis=-1, keepdims=True)
            p = jnp.exp(s - m)
            l = jnp.sum(p, axis=-1, keepdims=True)
            p_ref[qb, h] = p.astype(BF16)
            inv_l.append(1.0 / l)
            lse_all = jnp.where(lane == h, m + jnp.log(l), lse_all)
        for h in range(DIL_HEADS):
            cols = slice(h * e, (h + 1) * e)
            o_ref[0, 0, qb * blk:(qb + 1) * blk, cols] = (
                _dot(p_ref[qb, h], vcat[:, cols]) * inv_l[h]).astype(o_ref.dtype)
        lse_ref[0, 0, qb * blk:(qb + 1) * blk, :] = lse_all


def _dil_bias_vector(rel_bias, window, dil):
    blk = DIL_BLOCK
    span = window // dil
    r = blk - jnp.arange(4 * blk)
    band = (r >= 0) & (r <= span)
    bias = rel_bias[_t5_bucket(jnp.maximum(r, 0) * dil)].astype(F32)
    return jnp.where(band[:, None], bias, MASKED).T


def _dilated_pattern(dq, rel_bias, window, dil):
    b, d, sd, _ = dq.shape
    blk = DIL_BLOCK
    nq = DIL_QBLOCKS
    rows = nq * blk
    assert d == dil and window // dil == blk and sd % rows == 0
    cur = lambda c: (lambda bi, r, n: (bi, r, n, c))
    prev = lambda c: (lambda bi, r, n: (bi, r, jnp.maximum(n * nq - 1, 0), c))
    spec = lambda im: pl.BlockSpec((1, 1, rows, DIL_W), im)
    pspec = lambda im: pl.BlockSpec((1, 1, blk, DIL_W), im)
    return pl.pallas_call(
        _dil_kernel,
        grid=(b, dil, sd // rows),
        in_specs=[spec(cur(0)), pspec(prev(1)), spec(cur(1)), pspec(prev(2)), spec(cur(2)),
                  pl.BlockSpec((DIL_HEADS, 4 * blk), lambda bi, r, n: (0, 0))],
        out_specs=[pl.BlockSpec((1, 1, rows, DIL_W), lambda bi, r, n: (bi, r, n, 0)),
                   pl.BlockSpec((1, 1, rows, LANES), lambda bi, r, n: (bi, r, n, 0))],
        out_shape=[jax.ShapeDtypeStruct((b, dil, sd, DIL_W), BF16),
                   jax.ShapeDtypeStruct((b, dil, sd, LANES), F32)],
        scratch_shapes=[pltpu.VMEM((DIL_HEADS, blk, 2 * blk), F32),
                        pltpu.VMEM((nq, DIL_HEADS, blk, 2 * blk), F32),
                        pltpu.VMEM((nq, DIL_HEADS, blk, 2 * blk), BF16)],
        compiler_params=_params(("arbitrary", "arbitrary", "arbitrary")),
        name=f"dilated_d{dil}",
    )(dq, dq, dq, dq, dq, _dil_bias_vector(rel_bias, window, dil))


def _mix0_kernel(x_ref, ret_ref, *rest):
    np_ = len(DIL_DILATIONS)
    o_refs, l_refs = rest[:np_], rest[np_:2 * np_]
    w_ref, out_ref, obuf, lbuf, dbuf = rest[2 * np_:]
    tm = x_ref.shape[1]
    nslab = DIL_W // LANES
    for pi, d in enumerate(DIL_DILATIONS):
        rows = tm // d
        for r in range(d):
            dst = pl.ds(r, rows, stride=d) if d > 1 else pl.ds(0, tm)
            lbuf[pi, dst, :] = l_refs[pi][0, r]
            for s in range(nslab):
                obuf[pi, s, dst, :] = o_refs[pi][0, r, :, s * LANES:(s + 1) * LANES].astype(F32)
    lses = [lbuf[pi] for pi in range(np_)]
    m = functools.reduce(jnp.maximum, lses)
    es = [jnp.exp(l - m) for l in lses]
    inv = 1.0 / functools.reduce(lambda a, b: a + b, es)
    row = lax.broadcasted_iota(jnp.int32, (LANES, DIL_W), 0)
    col = lax.broadcasted_iota(jnp.int32, (LANES, DIL_W), 1)
    spread = (col // DIL_HEAD_DIM == row).astype(BF16)
    wide = []
    for ex in es:
        wgt = ex * inv
        hi = wgt.astype(BF16)
        lo = (wgt - hi.astype(F32)).astype(BF16)
        wide.append(_dot(hi, spread) + _dot(lo, spread))
    for s in range(nslab):
        cols = slice(s * LANES, (s + 1) * LANES)
        acc = wide[0][:, cols] * obuf[0, s]
        for pi in range(1, np_):
            acc = acc + wide[pi][:, cols] * obuf[pi, s]
        dbuf[:, cols] = acc.astype(BF16)
    na = ret_ref.shape[-1]
    y = _dot(ret_ref[0], w_ref[:na, :]) + _dot(dbuf[...], w_ref[na:, :])
    out_ref[0] = x_ref[0] + y


def _mix0(x3, ret, outs, lses, w):
    b, s, dm = x3.shape
    na = ret.shape[-1]
    tm = PROJ_TM
    o_specs = [pl.BlockSpec((1, d, tm // d, DIL_W), lambda bi, i: (bi, 0, i, 0)) for d in DIL_DILATIONS]
    l_specs = [pl.BlockSpec((1, d, tm // d, LANES), lambda bi, i: (bi, 0, i, 0)) for d in DIL_DILATIONS]
    np_ = len(DIL_DILATIONS)
    return pl.pallas_call(
        _mix0_kernel,
        grid=(b, s // tm),
        in_specs=[pl.BlockSpec((1, tm, dm), lambda bi, i: (bi, i, 0)),
                  pl.BlockSpec((1, tm, na), lambda bi, i: (bi, i, 0))] + o_specs + l_specs
                 + [pl.BlockSpec((na + DIL_W, dm), lambda bi, i: (0, 0))],
        out_specs=pl.BlockSpec((1, tm, dm), lambda bi, i: (bi, i, 0)),
        out_shape=jax.ShapeDtypeStruct((b, s, dm), F32),
        scratch_shapes=[pltpu.VMEM((np_, DIL_W // LANES, tm, LANES), F32),
                        pltpu.VMEM((np_, tm, LANES), F32),
                        pltpu.VMEM((tm, DIL_W), BF16)],
        compiler_params=_params(("arbitrary", "arbitrary")),
        name="mix0",
    )(x3, ret, *outs, *lses, w)


def _ffn_kernel(x_ref, *rest, final_norm, mixer_proj):
    if mixer_proj:
        a_ref, wmix_ref = rest[:2]
        rest = rest[2:]
    g_ref, win_ref, cw_ref, cb_ref, wout_ref, fg_ref, o_ref, ubuf, carry, gbuf = rest
    tm = x_ref.shape[1]
    halo = SUBLANES

    @pl.when(pl.program_id(1) == 0)
    def _():
        carry[...] = jnp.zeros_like(carry)

    x = x_ref[0]
    if mixer_proj:
        x = x + _dot(a_ref[0], wmix_ref[...])
    h = _rms(x, g_ref[...]).astype(BF16)

    def conv(part, col):
        cols = slice(col, col + FFN_CF)
        u = _dot(h, win_ref[:, cols])
        ubuf[part, 0:halo, :] = carry[:, cols]
        ubuf[part, halo:halo + tm, :] = u
        carry[:, cols] = u[tm - halo:, :]
        u1 = ubuf[part, halo - 1:halo - 1 + tm, :]
        u2 = ubuf[part, halo - 2:halo - 2 + tm, :]
        return (u * cw_ref[2:3, cols] + u1 * cw_ref[1:2, cols] + u2 * cw_ref[0:1, cols]
                + cb_ref[:, cols])

    for c in range(D_FF // FFN_CF):
        gate = conv(0, c * FFN_CF)
        up = conv(1, D_FF + c * FFN_CF)
        act = 0.5 * gate * (1.0 + lax.erf(gate * (2.0 ** -0.5)))
        gbuf[:, c * FFN_CF:(c + 1) * FFN_CF] = (act * up).astype(BF16)

    y = x + _dot(gbuf[...], wout_ref[...])
    if final_norm:
        y = _rms(y, fg_ref[...])
    o_ref[0] = y


def _ffn(x3, g, w_in, conv_w, conv_b, w_out, final_gain, final_norm, mixer=None):
    b, s, d = x3.shape
    f2 = w_in.shape[1]
    whole = pl.BlockSpec(memory_space=pltpu.VMEM)
    tile = lambda n: pl.BlockSpec((1, FFN_TM, n), lambda bi, t: (bi, t, 0))
    mix_specs = [tile(mixer[0].shape[-1]), whole] if mixer is not None else []
    mix_args = list(mixer) if mixer is not None else []
    return pl.pallas_call(
        functools.partial(_ffn_kernel, final_norm=final_norm, mixer_proj=mixer is not None),
        grid=(b, s // FFN_TM),
        in_specs=[tile(d)] + mix_specs + [whole, whole, whole, whole, whole, whole],
        out_specs=pl.BlockSpec((1, FFN_TM, d), lambda bi, t: (bi, t, 0)),
        out_shape=jax.ShapeDtypeStruct((b, s, d), F32),
        scratch_shapes=[pltpu.VMEM((2, FFN_TM + SUBLANES, FFN_CF), F32),
                        pltpu.VMEM((SUBLANES, f2), F32),
                        pltpu.VMEM((FFN_TM, D_FF), BF16)],
        compiler_params=_params(("arbitrary", "arbitrary")),
        name="ffn_final" if final_norm else "ffn",
    )(x3, *mix_args, g, w_in, conv_w, conv_b, w_out, final_gain)


def _proj1_kernel(x_ref, g_ref, wq_ref, wk_ref, wvt_ref, q_ref, k_ref, vt_ref):
    h = _rms(x_ref[0], g_ref[...]).astype(BF16)
    n = k_ref.shape[-1]
    for c in range(0, n, PROJ_TN):
        cols = slice(c, c + PROJ_TN)
        q_ref[0, 0, cols, :] = (_dot_nt(wq_ref[cols, :], h) * ATT_QSCALE).astype(BF16)
        k_ref[0, :, cols] = _dot(h, wk_ref[:, cols]).astype(BF16)
        vt_ref[0, 0, cols, :] = _dot_nt(wvt_ref[cols, :], h).astype(BF16)


def _proj1(x3, g, wq, wk, wvt):
    b, s, d = x3.shape
    n = wq.shape[1]
    t = ATT_T
    wspec = pl.BlockSpec((d, n), lambda bi, i: (0, 0))
    return pl.pallas_call(
        _proj1_kernel,
        grid=(b, s // t),
        in_specs=[pl.BlockSpec((1, t, d), lambda bi, i: (bi, i, 0)),
                  pl.BlockSpec((1, d), lambda bi, i: (0, 0)),
                  pl.BlockSpec((n, d), lambda bi, i: (0, 0)), wspec,
                  pl.BlockSpec((n, d), lambda bi, i: (0, 0))],
        out_specs=[pl.BlockSpec((1, 1, n, t), lambda bi, i: (bi, i, 0, 0)),
                   pl.BlockSpec((1, t, n), lambda bi, i: (bi, i, 0)),
                   pl.BlockSpec((1, 1, n, t), lambda bi, i: (bi, i, 0, 0))],
        out_shape=[jax.ShapeDtypeStruct((b, s // t, n, t), BF16),
                   jax.ShapeDtypeStruct((b, s, n), BF16),
                   jax.ShapeDtypeStruct((b, s // t, n, t), BF16)],
        compiler_params=_params(("arbitrary", "arbitrary")),
        name="proj1",
    )(x3, g, wq, wk, wvt)


def _diff_kernel(q_ref, k_ref, vt_ref, f_ref, lam_ref, sub_ref, o_ref,
                 qm_ref, km_ref, m_ref, l_ref, l8_ref, acc_ref, nb_ref, p_ref, *, lam_init):
    i = pl.program_id(2)
    t = ATT_T
    e = DIFF_HEAD_DIM
    half = lax.broadcasted_iota(jnp.int32, (SUBLANES, 2 * e), 1) // e
    sel = (lax.broadcasted_iota(jnp.int32, (SUBLANES, 2 * e), 0) == half).astype(BF16)

    @pl.when(i == 0)
    def _():
        for o in range(ATT_NEAR):
            g = jnp.broadcast_to(f_ref[0, 0:1, o * t:(o + 2) * t], (t, 2 * t))
            nb_ref[o] = pltpu.roll(g, 0, 1, stride=1, stride_axis=0)[:, t:]
        nb_ref[ATT_NEAR] = jnp.zeros((t, t), F32)
        kmax = jnp.zeros((SUBLANES, t), F32)
        for c in range(k_ref.shape[1] // t):
            kk = k_ref[0, c * t:(c + 1) * t, :].astype(F32)
            kmax = jnp.maximum(kmax, _dot_nt(sel, (kk * kk).astype(BF16)))
        for a in range(2):
            km_ref[a] = jnp.broadcast_to(jnp.max(kmax[a:a + 1, :], axis=1, keepdims=True), (1, t))

    qt = q_ref[0, 0]
    row = lax.broadcasted_iota(jnp.int32, qt.shape, 0)
    zero = jnp.zeros_like(qt)
    qm_ref[0] = jnp.where(row < e, qt, zero)
    qm_ref[1] = jnp.where(row >= e, qt, zero)
    qf = qt.astype(F32)
    qq = qf * qf
    qn2 = [jnp.sum(qq[a * e:(a + 1) * e], axis=0, keepdims=True) for a in range(2)]
    bias_max = f_ref[0, 1:2, 0:t]
    n_far = jnp.maximum(i - (ATT_NEAR - 1), 0)

    l8_ref[...] = jnp.zeros_like(l8_ref)
    acc_ref[...] = jnp.zeros_like(acc_ref)
    shift = [jnp.sqrt(qn2[a] * km_ref[a]) * ATT_BOUND_SLACK + bias_max for a in range(2)]

    kc = ATT_KC

    def fast_tiles(j0, ntiles, biased):
        for a in range(2):
            pv = None
            l8 = None
            for tt in range(ntiles):
                j = j0 + tt
                for c in range(0, t, kc):
                    kb = k_ref[0, pl.ds(pl.multiple_of(j * t + c, kc), kc), :]
                    s = _dot(kb, qm_ref[a])
                    if biased:
                        s = s + nb_ref[jnp.minimum(i - j, ATT_NEAR), c:c + kc, :]
                    p = jnp.exp2(s - shift[a])
                    part = jnp.sum(p.reshape(kc // SUBLANES, SUBLANES, t), axis=0)
                    d = _dot(vt_ref[0, j, :, c:c + kc], p.astype(BF16))
                    l8 = part if l8 is None else l8 + part
                    pv = d if pv is None else pv + d
            l8_ref[a] += l8
            acc_ref[a] += pv

    def probs(u, biased):
        slot = u % 2
        for a in range(2):
            l8 = None
            for tt in range(2):
                j = 2 * u + tt
                kb = k_ref[0, pl.ds(pl.multiple_of(j * t, t), t), :]
                s = _dot(kb, qm_ref[a])
                if biased:
                    s = s + nb_ref[jnp.minimum(i - j, ATT_NEAR)]
                p = jnp.exp2(s - shift[a])
                part = jnp.sum(p.reshape(t // SUBLANES, SUBLANES, t), axis=0)
                l8 = part if l8 is None else l8 + part
                p_ref[slot, a, tt] = p.astype(BF16)
            l8_ref[a] += l8

    def values(u):
        slot = u % 2
        for a in range(2):
            acc_ref[a] += (_dot(vt_ref[0, 2 * u], p_ref[slot, a, 0])
                           + _dot(vt_ref[0, 2 * u + 1], p_ref[slot, a, 1]))

    def far_step(u, carry):
        values(u - 1)
        probs(u, False)
        return carry

    def near_step(u, carry):
        values(u - 1)
        probs(u, True)
        return carry

    def fast_near(j, carry):
        fast_tiles(j, 1, True)
        return carry

    n = i + 1
    pairs = n // 2
    far_pairs = n_far // 2

    @pl.when(far_pairs > 0)
    def _():
        probs(0, False)

    @pl.when(jnp.logical_and(far_pairs == 0, pairs > 0))
    def _():
        probs(0, True)

    lax.fori_loop(1, far_pairs, far_step, 0)
    lax.fori_loop(jnp.maximum(far_pairs, 1), pairs, near_step, 0)

    @pl.when(pairs > 0)
    def _():
        values(pairs - 1)

    lax.fori_loop(2 * pairs, n, fast_near, 0)
    for a in range(2):
        l_ref[a] = jnp.sum(l8_ref[a], axis=0, keepdims=True)
    l_min = jnp.min(jnp.minimum(l_ref[0], l_ref[1]))
    l_max = jnp.max(jnp.maximum(l_ref[0], l_ref[1]))
    in_range = jnp.logical_and(l_min >= ATT_MIN_DENOM, l_max <= ATT_MAX_DENOM)

    @pl.when(jnp.logical_not(in_range))
    def _():
        m_ref[...] = jnp.full_like(m_ref, MASKED)
        l_ref[...] = jnp.zeros_like(l_ref)
        acc_ref[...] = jnp.zeros_like(acc_ref)

        def online_tile(j, bias):
            kb = k_ref[0, pl.ds(pl.multiple_of(j * t, t), t), :]
            vtb = vt_ref[0, j]
            for a in range(2):
                s = _dot(kb, qm_ref[a])
                if bias is not None:
                    s = s + bias
                m_old = m_ref[a]
                m_new = jnp.maximum(m_old, jnp.max(s, axis=0, keepdims=True))
                alpha = jnp.exp2(m_old - m_new)
                p = jnp.exp2(s - m_new)
                l_ref[a] = alpha * l_ref[a] + jnp.sum(p, axis=0, keepdims=True)
                acc_ref[a] = alpha * acc_ref[a] + _dot(vtb, p.astype(BF16))
                m_ref[a] = m_new

        def online_far(j, carry):
            online_tile(j, None)
            return carry

        def online_near(j, carry):
            online_tile(j, nb_ref[i - j])
            return carry

        lax.fori_loop(0, n_far, online_far, 0)
        lax.fori_loop(n_far, i + 1, online_near, 0)

    lam = (jnp.exp(jnp.sum(lam_ref[0:1, :] * lam_ref[1:2, :], axis=-1, keepdims=True))
           - jnp.exp(jnp.sum(lam_ref[2:3, :] * lam_ref[3:4, :], axis=-1, keepdims=True))
           + lam_init)
    ot = acc_ref[0] / l_ref[0] - lam * (acc_ref[1] / l_ref[1])
    ot = ot * lax.rsqrt(jnp.mean(ot * ot, axis=0, keepdims=True) + EPS)
    o_ref[0] = (ot.T * sub_ref[...] * (1.0 - lam_init)).astype(o_ref.dtype)


def _diff_bias(rel_bias, s):
    t = ATT_T
    n = ATT_NEAR * t
    assert n >= REL_MAX_DISTANCE + t - 1 and s >= n
    dist_bias = rel_bias[_t5_bucket(jnp.arange(n))].astype(F32)
    far = rel_bias[REL_BUCKETS - 1].astype(F32)
    near = (dist_bias - far) * LOG2E
    f = jnp.concatenate([jnp.full((t, far.shape[0]), MASKED, F32), near])
    top = jnp.broadcast_to(jnp.maximum(jnp.max(near, axis=0), 0.0), f.shape)
    return jnp.stack([f.T, top.T], axis=1)


def _diff_attention(q, k, vt, rel_bias, lam4, subln, layer_idx):
    b, s, n = k.shape
    t = ATT_T
    w = 2 * DIFF_HEAD_DIM
    lam_init = 0.8 - 0.6 * math.exp(-0.3 * layer_idx)
    return pl.pallas_call(
        functools.partial(_diff_kernel, lam_init=lam_init),
        grid=(b, DIFF_HEADS, s // t),
        in_specs=[pl.BlockSpec((1, 1, w, t), lambda bi, h, i: (bi, i, h, 0)),
                  pl.BlockSpec((1, s, w), lambda bi, h, i: (bi, 0, h)),
                  pl.BlockSpec((1, s // t, w, t), lambda bi, h, i: (bi, 0, h, 0)),
                  pl.BlockSpec((1, 2, (ATT_NEAR + 1) * t), lambda bi, h, i: (h, 0, 0)),
                  pl.BlockSpec((4, DIFF_HEAD_DIM), lambda bi, h, i: (0, 0)),
                  pl.BlockSpec((1, w), lambda bi, h, i: (0, 0))],
        out_specs=pl.BlockSpec((1, t, w), lambda bi, h, i: (bi, i, h)),
        out_shape=jax.ShapeDtypeStruct((b, s, n), BF16),
        scratch_shapes=[pltpu.VMEM((2, w, t), BF16),
                        pltpu.VMEM((2, 1, t), F32),
                        pltpu.VMEM((2, 1, t), F32),
                        pltpu.VMEM((2, 1, t), F32),
                        pltpu.VMEM((2, SUBLANES, t), F32),
                        pltpu.VMEM((2, w, t), F32),
                        pltpu.VMEM((ATT_NEAR + 1, t, t), F32),
                        pltpu.VMEM((2, 2, 2, t, t), BF16)],
        compiler_params=_params(("arbitrary", "arbitrary", "arbitrary")),
        name="diff_attention",
    )(q, k, vt, _diff_bias(rel_bias, s), lam4, subln)


def kernel(x, rel_bias, norm_mix, norm_ffn, norm_final, w_in_ab, ret_gn, w_out_ab, w_in_c,
           lam_q1, lam_k1, lam_q2, lam_k2, diff_subln, w_out_c, w_ffn_in, conv_w, conv_b,
           w_ffn_out):
    b, s, d = x.shape
    depth = norm_mix.shape[0]
    assert depth == 2 and s % RET_ROWS == 0 and s % FFN_TM == 0 and s >= ATT_NEAR * ATT_T
    assert s % (max(DIL_DILATIONS) * DIL_BLOCK) == 0
    row = lambda v: v.reshape(1, -1)

    proj = _proj0(x, row(norm_mix[0]), w_in_ab[0].astype(BF16))
    ret = _retention(proj[0], row(ret_gn[0]), b, s)
    outs, lses = zip(*[_dilated_pattern(dq, rel_bias, wdw, dil)
                       for dq, (wdw, dil) in zip(proj[1:], DIL_PATTERNS)])
    x3 = _mix0(x, ret, outs, lses, w_out_ab[0].astype(BF16))
    x3 = _ffn(x3, row(norm_ffn[0]), w_ffn_in[0].astype(BF16), conv_w[0],
              row(conv_b[0]), w_ffn_out[0].astype(BF16), row(norm_final), False)

    nqk = DIFF_HEADS * 2 * DIFF_HEAD_DIM
    wc = w_in_c[0].astype(BF16)
    q, k, vt = _proj1(x3, row(norm_mix[1]), wc[:, :nqk].T, wc[:, nqk:2 * nqk], wc[:, 2 * nqk:].T)
    lam4 = jnp.stack([lam_q1[0], lam_k1[0], lam_q2[0], lam_k2[0]]).astype(F32)
    a = _diff_attention(q, k, vt, rel_bias, lam4, row(diff_subln[0]), 1)
    return _ffn(x3, row(norm_ffn[1]), w_ffn_in[1].astype(BF16), conv_w[1], row(conv_b[1]),
                w_ffn_out[1].astype(BF16), row(norm_final), True,
                mixer=(a, w_out_c[0].astype(BF16)))
```

```python
import functools
import math

import jax
import jax.numpy as jnp
from jax import lax
from jax.experimental import pallas as pl
from jax.experimental.pallas import tpu as pltpu

F32 = jnp.float32
BF16 = jnp.bfloat16

EPS = 1e-6
MASKED = -1e30

RET_HEADS = 4
RET_DIM = 128
RET_CHUNK = 128
ROPE_BASE = 10000.0
DIL_HEADS = 8
DIL_HEAD_DIM = 64
DIL_PATTERNS = ((128, 1), (512, 4), (2048, 16))
DIL_BLOCK = 128
DIFF_HEADS = 8
DIFF_HEAD_DIM = 64
REL_BUCKETS = 32
REL_MAX_DISTANCE = 2048
D_FF = 2816
CONV_WIDTH = 3
DIL_W = DIL_HEADS * DIL_HEAD_DIM
RET_W = 4 * RET_HEADS * RET_DIM
DIL_DILATIONS = tuple(d for _, d in DIL_PATTERNS)

LANES = 128
SUBLANES = 8
VMEM_BYTES_V7X = 64 * 1024 * 1024
VMEM_LIMIT = VMEM_BYTES_V7X - 8 * 1024 * 1024

PROJ_TM = 512
PROJ_TN = 512
RET_ROWS = 1024
DIL_QBLOCKS = 2
FFN_TM = 512
FFN_CF = 256
ATT_T = 512
ATT_KC = 512
ATT_BOUND_SLACK = 1.0 + 2.0 ** -5
ATT_MIN_DENOM = 2.0 ** -60
ATT_MAX_DENOM = 2.0 ** 100
ATT_NEAR = -(-(REL_MAX_DISTANCE + ATT_T - 1) // ATT_T)
LOG2E = math.log2(math.e)
ATT_QSCALE = DIFF_HEAD_DIM ** -0.5 * LOG2E


def _params(sem):
    return pltpu.CompilerParams(dimension_semantics=sem, vmem_limit_bytes=VMEM_LIMIT)


def _rms(x, g):
    return x * lax.rsqrt(jnp.mean(x * x, axis=-1, keepdims=True) + EPS) * g


def _dot(a, b):
    return jnp.dot(a, b, preferred_element_type=F32)


def _dot_nt(a, b):
    return lax.dot_general(a, b, (((1,), (1,)), ((), ())), preferred_element_type=F32)


def _dot_tn(a, b):
    return lax.dot_general(a, b, (((0,), (0,)), ((), ())), preferred_element_type=F32)


def _proj0_kernel(x_ref, g_ref, w_ref, a_ref, *rest):
    d_refs, dbuf = rest[:-1], rest[-1]
    tm = x_ref.shape[1]
    h = _rms(x_ref[0], g_ref[...]).astype(BF16)
    na = a_ref.shape[-1]
    nd = w_ref.shape[1] - na
    for c in range(0, nd, PROJ_TN):
        res = _dot(h, w_ref[:, na + c:na + c + PROJ_TN])
        for s in range(c // LANES, (c + PROJ_TN) // LANES):
            cols = slice(s * LANES, (s + 1) * LANES)
            dbuf[0, s] = res[:, (s * LANES - c):(s * LANES - c) + LANES]
            src, d_prev = 0, 1
            for dref, d in zip(d_refs, DIL_DILATIONS):
                if d == 1:
                    dref[0, 0, :, cols] = dbuf[0, s].astype(BF16)
                    continue
                step, rows = d // d_prev, tm // d
                dst = 1 - src
                for rp in range(d_prev):
                    for q in range(step):
                        piece = dbuf[src, s, pl.ds(rp * (tm // d_prev) + q, rows, stride=step), :]
                        r = rp + q * d_prev
                        if d != DIL_DILATIONS[-1]:
                            dbuf[dst, s, r * rows:(r + 1) * rows, :] = piece
                        dref[0, r, :, cols] = piece.astype(BF16)
                src, d_prev = dst, d
    for c in range(0, na, PROJ_TN):
        a_ref[0, :, c:c + PROJ_TN] = _dot(h, w_ref[:, c:c + PROJ_TN]).astype(BF16)


def _proj0(x3, g, w):
    b, s, dm = x3.shape
    n = w.shape[1]
    nd = n - RET_W
    tm = PROJ_TM
    d_specs = [pl.BlockSpec((1, d, tm // d, nd), lambda bi, i: (bi, 0, i, 0)) for d in DIL_DILATIONS]
    d_shapes = [jax.ShapeDtypeStruct((b, d, s // d, nd), BF16) for d in DIL_DILATIONS]
    return pl.pallas_call(
        _proj0_kernel,
        grid=(b, s // tm),
        in_specs=[pl.BlockSpec((1, tm, dm), lambda bi, i: (bi, i, 0)),
                  pl.BlockSpec((1, dm), lambda bi, i: (0, 0)),
                  pl.BlockSpec((dm, n), lambda bi, i: (0, 0))],
        out_specs=[pl.BlockSpec((1, tm, RET_W), lambda bi, i: (bi, i, 0))] + d_specs,
        out_shape=[jax.ShapeDtypeStruct((b, s, RET_W), BF16)] + d_shapes,
        scratch_shapes=[pltpu.VMEM((2, nd // LANES, tm, LANES), F32)],
        compiler_params=_params(("arbitrary", "arbitrary")),
        name="proj0",
    )(x3, g, w)


def _ret_kernel(q_ref, k_ref, v_ref, gate_ref, cos_ref, sin_ref, tab_ref, gn_ref, o_ref,
                state_ref, qr_ref, kr_ref, qx_ref, kv_ref, st_ref):
    @pl.when(pl.program_id(2) == 0)
    def _():
        state_ref[...] = jnp.zeros_like(state_ref)

    decay = tab_ref[0, 0]
    xi = tab_ref[0, 1]
    zeta = tab_ref[0, 2]
    g_chunk = tab_ref[0, 3]
    gain = gn_ref[...]
    half = RET_DIM // 2
    n_chunks = q_ref.shape[1] // RET_CHUNK
    for c in range(n_chunks):
        rows = pl.ds(c * RET_CHUNK, RET_CHUNK)
        cos = cos_ref[rows, :]
        sin = sin_ref[rows, :]
        q = q_ref[0, rows, :].astype(F32)
        k = k_ref[0, rows, :].astype(F32)
        q = q * cos + pltpu.roll(q, half, 1) * sin
        k = k * cos + pltpu.roll(k, half, 1) * sin
        qr_ref[c] = q.astype(BF16)
        kr_ref[c] = k.astype(BF16)
        qx_ref[c] = (q * xi).astype(BF16)
        kv_ref[c] = _dot_tn((k * zeta).astype(BF16), v_ref[0, rows, :])
    state = state_ref[...]
    for c in range(n_chunks):
        st_ref[c] = state.astype(BF16)
        state = g_chunk * state + kv_ref[c]
    state_ref[...] = state
    for c in range(n_chunks):
        rows = pl.ds(c * RET_CHUNK, RET_CHUNK)
        v = v_ref[0, rows, :]
        intra = _dot_nt(qr_ref[c], kr_ref[c]) * decay
        y = _dot(intra.astype(BF16), v) + _dot(qx_ref[c], st_ref[c])
        mu = jnp.mean(y, axis=-1, keepdims=True)
        yc = y - mu
        var = jnp.mean(yc * yc, axis=-1, keepdims=True)
        yn = yc * lax.rsqrt(var + EPS) * gain
        gate = gate_ref[0, rows, :].astype(F32)
        o_ref[0, rows, :] = (gate * jax.nn.sigmoid(gate) * yn).astype(o_ref.dtype)


def _retention_tables(s):
    c, e = RET_CHUNK, RET_DIM
    inv = ROPE_BASE ** (-jnp.arange(0, e, 2, dtype=F32) / e)
    ang = jnp.arange(s, dtype=F32)[:, None] * inv[None, :]
    cos = jnp.concatenate([jnp.cos(ang), jnp.cos(ang)], axis=-1)
    sin = jnp.concatenate([-jnp.sin(ang), jnp.sin(ang)], axis=-1)
    log_g = jnp.log1p(-jnp.exp2(-5.0 - jnp.arange(RET_HEADS, dtype=F32)))
    pos = jnp.arange(c, dtype=F32)
    rel = pos[:, None] - pos[None, :]
    causal = rel >= 0
    scale = e ** -0.5
    decay = jnp.where(causal, jnp.exp(jnp.where(causal, rel, 0.0)[None] * log_g[:, None, None]), 0.0)
    zeta = jnp.exp((c - 1 - pos)[None, :] * log_g[:, None])
    xi = jnp.exp((pos + 1)[None, :] * log_g[:, None])
    g_chunk = jnp.exp(c * log_g)
    tab = jnp.stack([
        decay * scale,
        jnp.broadcast_to(xi[:, :, None], (RET_HEADS, c, e)),
        jnp.broadcast_to(zeta[:, :, None] * scale, (RET_HEADS, c, e)),
        jnp.broadcast_to(g_chunk[:, None, None], (RET_HEADS, e, e)),
    ], axis=1)
    return cos, sin, tab


def _retention(p0, gn, b, s):
    cos, sin, tab = _retention_tables(s)
    e = RET_DIM
    nc = RET_ROWS // RET_CHUNK
    col = lambda off: (lambda bi, h, t: (bi, t, off + h))
    return pl.pallas_call(
        _ret_kernel,
        grid=(b, RET_HEADS, s // RET_ROWS),
        in_specs=[pl.BlockSpec((1, RET_ROWS, e), col(0)),
                  pl.BlockSpec((1, RET_ROWS, e), col(RET_HEADS)),
                  pl.BlockSpec((1, RET_ROWS, e), col(2 * RET_HEADS)),
                  pl.BlockSpec((1, RET_ROWS, e), col(3 * RET_HEADS)),
                  pl.BlockSpec((RET_ROWS, e), lambda bi, h, t: (t, 0)),
                  pl.BlockSpec((RET_ROWS, e), lambda bi, h, t: (t, 0)),
                  pl.BlockSpec((1, 4, e, e), lambda bi, h, t: (h, 0, 0, 0)),
                  pl.BlockSpec((1, e), lambda bi, h, t: (0, h))],
        out_specs=pl.BlockSpec((1, RET_ROWS, e), lambda bi, h, t: (bi, t, h)),
        out_shape=jax.ShapeDtypeStruct((b, s, RET_HEADS * e), BF16),
        scratch_shapes=[pltpu.VMEM((e, e), F32),
                        pltpu.VMEM((nc, RET_CHUNK, e), BF16),
                        pltpu.VMEM((nc, RET_CHUNK, e), BF16),
                        pltpu.VMEM((nc, RET_CHUNK, e), BF16),
                        pltpu.VMEM((nc, e, e), F32),
                        pltpu.VMEM((nc, e, e), BF16)],
        compiler_params=_params(("arbitrary", "arbitrary", "arbitrary")),
        name="retention",
    )(p0, p0, p0, p0, cos, sin, tab, gn)


def _t5_bucket(dist):
    max_exact = REL_BUCKETS // 2
    d = jnp.maximum(dist.astype(F32), 1.0)
    large = max_exact + (jnp.log(d / max_exact) / math.log(REL_MAX_DISTANCE / max_exact)
                         * (REL_BUCKETS - max_exact))
    large = jnp.clip(large.astype(jnp.int32), max_exact, REL_BUCKETS - 1)
    return jnp.where(dist < max_exact, dist, large)


def _dil_kernel(q_ref, kp_ref, kc_ref, vp_ref, vc_ref, g_ref, o_ref, lse_ref, bias_ref,
                s_ref, p_ref):
    first = pl.program_id(2) == 0
    blk, e = DIL_BLOCK, DIL_HEAD_DIM

    @pl.when(jnp.logical_and(jnp.logical_and(pl.program_id(0) == 0, pl.program_id(1) == 0), first))
    def _():
        for h in range(DIL_HEADS):
            gb = jnp.broadcast_to(g_ref[h:h + 1, :], (blk, 4 * blk))
            bias_ref[h] = pltpu.roll(gb, 0, 1, stride=1, stride_axis=0)[:, :2 * blk]

    q_all = q_ref[0, 0] * (e ** -0.5)
    k_all = jnp.concatenate([kp_ref[0, 0], kc_ref[0, 0]], axis=0)
    v_all = jnp.concatenate([vp_ref[0, 0], vc_ref[0, 0]], axis=0)
    in_prev = lax.broadcasted_iota(jnp.int32, (blk, 2 * blk), 1) < blk
    no_prev = jnp.logical_and(first, in_prev)
    lane = lax.broadcasted_iota(jnp.int32, (blk, LANES), 1)
    for qb in range(q_all.shape[0] // blk):
        q = q_all[qb * blk:(qb + 1) * blk]
        kcat = k_all[qb * blk:(qb + 2) * blk]
        vcat = v_all[qb * blk:(qb + 2) * blk]
        lse_all = jnp.zeros((blk, LANES), F32)
        for h in range(DIL_HEADS):
            cols = slice(h * e, (h + 1) * e)
            s = _dot_nt(q[:, cols], kcat[:, cols]) + bias_ref[h]
            if qb == 0:
                s = jnp.where(no_prev, MASKED, s)
            s_ref[qb, h] = s
        inv_l = []
        for h in range(DIL_HEADS):
            s = s_ref[qb, h]
            m = jnp.max(s, axis=-1, keepdims=True)
            p = jnp.exp(s - m)
            l = jnp.sum(p, axis=-1, keepdims=True)
            p_ref[qb, h] = p.astype(BF16)
            inv_l.append(1.0 / l)
            lse_all = jnp.where(lane == h, m + jnp.log(l), lse_all)
        for h in range(DIL_HEADS):
            cols = slice(h * e, (h + 1) * e)
            o_ref[0, 0, qb * blk:(qb + 1) * blk, cols] = (
                _dot(p_ref[qb, h], vcat[:, cols]) * inv_l[h]).astype(o_ref.dtype)
        lse_ref[0, 0, qb * blk:(qb + 1) * blk, :] = lse_all


def _dil_bias_vector(rel_bias, window, dil):
    blk = DIL_BLOCK
    span = window // dil
    r = blk - jnp.arange(4 * blk)
    band = (r >= 0) & (r <= span)
    bias = rel_bias[_t5_bucket(jnp.maximum(r, 0) * dil)].astype(F32)
    return jnp.where(band[:, None], bias, MASKED).T


def _dilated_pattern(dq, rel_bias, window, dil):
    b, d, sd, _ = dq.shape
    blk = DIL_BLOCK
    nq = DIL_QBLOCKS
    rows = nq * blk
    assert d == dil and window // dil == blk and sd % rows == 0
    cur = lambda c: (lambda bi, r, n: (bi, r, n, c))
    prev = lambda c: (lambda bi, r, n: (bi, r, jnp.maximum(n * nq - 1, 0), c))
    spec = lambda im: pl.BlockSpec((1, 1, rows, DIL_W), im)
    pspec = lambda im: pl.BlockSpec((1, 1, blk, DIL_W), im)
    return pl.pallas_call(
        _dil_kernel,
        grid=(b, dil, sd // rows),
        in_specs=[spec(cur(0)), pspec(prev(1)), spec(cur(1)), pspec(prev(2)), spec(cur(2)),
                  pl.BlockSpec((DIL_HEADS, 4 * blk), lambda bi, r, n: (0, 0))],
        out_specs=[pl.BlockSpec((1, 1, rows, DIL_W), lambda bi, r, n: (bi, r, n, 0)),
                   pl.BlockSpec((1, 1, rows, LANES), lambda bi, r, n: (bi, r, n, 0))],
        out_shape=[jax.ShapeDtypeStruct((b, dil, sd, DIL_W), BF16),
                   jax.ShapeDtypeStruct((b, dil, sd, LANES), F32)],
        scratch_shapes=[pltpu.VMEM((DIL_HEADS, blk, 2 * blk), F32),
                        pltpu.VMEM((nq, DIL_HEADS, blk, 2 * blk), F32),
                        pltpu.VMEM((nq, DIL_HEADS, blk, 2 * blk), BF16)],
        compiler_params=_params(("arbitrary", "arbitrary", "arbitrary")),
        name=f"dilated_d{dil}",
    )(dq, dq, dq, dq, dq, _dil_bias_vector(rel_bias, window, dil))


def _mix0_kernel(x_ref, ret_ref, *rest):
    np_ = len(DIL_DILATIONS)
    o_refs, l_refs = rest[:np_], rest[np_:2 * np_]
    w_ref, out_ref, obuf, lbuf, dbuf = rest[2 * np_:]
    tm = x_ref.shape[1]
    nslab = DIL_W // LANES
    for pi, d in enumerate(DIL_DILATIONS):
        rows = tm // d
        for r in range(d):
            dst = pl.ds(r, rows, stride=d) if d > 1 else pl.ds(0, tm)
            lbuf[pi, dst, :] = l_refs[pi][0, r]
            for s in range(nslab):
                obuf[pi, s, dst, :] = o_refs[pi][0, r, :, s * LANES:(s + 1) * LANES].astype(F32)
    lses = [lbuf[pi] for pi in range(np_)]
    m = functools.reduce(jnp.maximum, lses)
    es = [jnp.exp(l - m) for l in lses]
    inv = 1.0 / functools.reduce(lambda a, b: a + b, es)
    row = lax.broadcasted_iota(jnp.int32, (LANES, DIL_W), 0)
    col = lax.broadcasted_iota(jnp.int32, (LANES, DIL_W), 1)
    spread = (col // DIL_HEAD_DIM == row).astype(BF16)
    spread2 = jnp.concatenate([spread, spread], axis=0)
    wide = []
    for ex in es:
        wgt = ex * inv
        hi = wgt.astype(BF16)
        lo = (wgt - hi.astype(F32)).astype(BF16)
        wide.append(_dot(jnp.concatenate([hi, lo], axis=1), spread2))
    for s in range(nslab):
        cols = slice(s * LANES, (s + 1) * LANES)
        acc = wide[0][:, cols] * obuf[0, s]
        for pi in range(1, np_):
            acc = acc + wide[pi][:, cols] * obuf[pi, s]
        dbuf[:, cols] = acc.astype(BF16)
    na = ret_ref.shape[-1]
    y = _dot(ret_ref[0], w_ref[:na, :]) + _dot(dbuf[...], w_ref[na:, :])
    out_ref[0] = x_ref[0] + y


def _mix0(x3, ret, outs, lses, w):
    b, s, dm = x3.shape
    na = ret.shape[-1]
    tm = PROJ_TM
    o_specs = [pl.BlockSpec((1, d, tm // d, DIL_W), lambda bi, i: (bi, 0, i, 0)) for d in DIL_DILATIONS]
    l_specs = [pl.BlockSpec((1, d, tm // d, LANES), lambda bi, i: (bi, 0, i, 0)) for d in DIL_DILATIONS]
    np_ = len(DIL_DILATIONS)
    return pl.pallas_call(
        _mix0_kernel,
        grid=(b, s // tm),
        in_specs=[pl.BlockSpec((1, tm, dm), lambda bi, i: (bi, i, 0)),
                  pl.BlockSpec((1, tm, na), lambda bi, i: (bi, i, 0))] + o_specs + l_specs
                 + [pl.BlockSpec((na + DIL_W, dm), lambda bi, i: (0, 0))],
        out_specs=pl.BlockSpec((1, tm, dm), lambda bi, i: (bi, i, 0)),
        out_shape=jax.ShapeDtypeStruct((b, s, dm), F32),
        scratch_shapes=[pltpu.VMEM((np_, DIL_W // LANES, tm, LANES), F32),
                        pltpu.VMEM((np_, tm, LANES), F32),
                        pltpu.VMEM((tm, DIL_W), BF16)],
        compiler_params=_params(("arbitrary", "arbitrary")),
        name="mix0",
    )(x3, ret, *outs, *lses, w)


def _ffn_kernel(x_ref, *rest, final_norm, mixer_proj):
    if mixer_proj:
        a_ref, wmix_ref = rest[:2]
        rest = rest[2:]
    g_ref, win_ref, cw_ref, cb_ref, wout_ref, fg_ref, o_ref, ubuf, carry, gbuf = rest
    tm = x_ref.shape[1]
    halo = SUBLANES

    @pl.when(pl.program_id(1) == 0)
    def _():
        carry[...] = jnp.zeros_like(carry)

    x = x_ref[0]
    if mixer_proj:
        x = x + _dot(a_ref[0], wmix_ref[...])
    h = _rms(x, g_ref[...]).astype(BF16)

    def conv(part, col):
        cols = slice(col, col + FFN_CF)
        u = _dot(h, win_ref[:, cols])
        ubuf[part, 0:halo, :] = carry[:, cols]
        ubuf[part, halo:halo + tm, :] = u
        carry[:, cols] = u[tm - halo:, :]
        u1 = ubuf[part, halo - 1:halo - 1 + tm, :]
        u2 = ubuf[part, halo - 2:halo - 2 + tm, :]
        return (u * cw_ref[2:3, cols] + u1 * cw_ref[1:2, cols] + u2 * cw_ref[0:1, cols]
                + cb_ref[:, cols])

    for c in range(D_FF // FFN_CF):
        gate = conv(0, c * FFN_CF)
        up = conv(1, D_FF + c * FFN_CF)
        act = 0.5 * gate * (1.0 + lax.erf(gate * (2.0 ** -0.5)))
        gbuf[:, c * FFN_CF:(c + 1) * FFN_CF] = (act * up).astype(BF16)

    y = x + _dot(gbuf[...], wout_ref[...])
    if final_norm:
        y = _rms(y, fg_ref[...])
    o_ref[0] = y


def _ffn(x3, g, w_in, conv_w, conv_b, w_out, final_gain, final_norm, mixer=None):
    b, s, d = x3.shape
    f2 = w_in.shape[1]
    whole = pl.BlockSpec(memory_space=pltpu.VMEM)
    tile = lambda n: pl.BlockSpec((1, FFN_TM, n), lambda bi, t: (bi, t, 0))
    mix_specs = [tile(mixer[0].shape[-1]), whole] if mixer is not None else []
    mix_args = list(mixer) if mixer is not None else []
    return pl.pallas_call(
        functools.partial(_ffn_kernel, final_norm=final_norm, mixer_proj=mixer is not None),
        grid=(b, s // FFN_TM),
        in_specs=[tile(d)] + mix_specs + [whole, whole, whole, whole, whole, whole],
        out_specs=pl.BlockSpec((1, FFN_TM, d), lambda bi, t: (bi, t, 0)),
        out_shape=jax.ShapeDtypeStruct((b, s, d), F32),
        scratch_shapes=[pltpu.VMEM((2, FFN_TM + SUBLANES, FFN_CF), F32),
                        pltpu.VMEM((SUBLANES, f2), F32),
                        pltpu.VMEM((FFN_TM, D_FF), BF16)],
        compiler_params=_params(("arbitrary", "arbitrary")),
        name="ffn_final" if final_norm else "ffn",
    )(x3, *mix_args, g, w_in, conv_w, conv_b, w_out, final_gain)


def _proj1_kernel(x_ref, g_ref, wq_ref, wk_ref, wvt_ref, q_ref, k_ref, vt_ref):
    h = _rms(x_ref[0], g_ref[...]).astype(BF16)
    n = k_ref.shape[-1]
    for c in range(0, n, PROJ_TN):
        cols = slice(c, c + PROJ_TN)
        q_ref[0, 0, cols, :] = (_dot_nt(wq_ref[cols, :], h) * ATT_QSCALE).astype(BF16)
        k_ref[0, :, cols] = _dot(h, wk_ref[:, cols]).astype(BF16)
        vt_ref[0, 0, cols, :] = _dot_nt(wvt_ref[cols, :], h).astype(BF16)


def _proj1(x3, g, wq, wk, wvt):
    b, s, d = x3.shape
    n = wq.shape[1]
    t = ATT_T
    wspec = pl.BlockSpec((d, n), lambda bi, i: (0, 0))
    return pl.pallas_call(
        _proj1_kernel,
        grid=(b, s // t),
        in_specs=[pl.BlockSpec((1, t, d), lambda bi, i: (bi, i, 0)),
                  pl.BlockSpec((1, d), lambda bi, i: (0, 0)),
                  pl.BlockSpec((n, d), lambda bi, i: (0, 0)), wspec,
                  pl.BlockSpec((n, d), lambda bi, i: (0, 0))],
        out_specs=[pl.BlockSpec((1, 1, n, t), lambda bi, i: (bi, i, 0, 0)),
                   pl.BlockSpec((1, t, n), lambda bi, i: (bi, i, 0)),
                   pl.BlockSpec((1, 1, n, t), lambda bi, i: (bi, i, 0, 0))],
        out_shape=[jax.ShapeDtypeStruct((b, s // t, n, t), BF16),
                   jax.ShapeDtypeStruct((b, s, n), BF16),
                   jax.ShapeDtypeStruct((b, s // t, n, t), BF16)],
        compiler_params=_params(("arbitrary", "arbitrary")),
        name="proj1",
    )(x3, g, wq, wk, wvt)


def _diff_kernel(q_ref, k_ref, vt_ref, f_ref, lam_ref, sub_ref, o_ref,
                 qm_ref, km_ref, m_ref, l_ref, l8_ref, acc_ref, nb_ref, p_ref, *, lam_init):
    i = pl.program_id(2)
    t = ATT_T
    e = DIFF_HEAD_DIM
    half = lax.broadcasted_iota(jnp.int32, (SUBLANES, 2 * e), 1) // e
    sel = (lax.broadcasted_iota(jnp.int32, (SUBLANES, 2 * e), 0) == half).astype(BF16)

    @pl.when(i == 0)
    def _():
        for o in range(ATT_NEAR):
            g = jnp.broadcast_to(f_ref[0, 0:1, o * t:(o + 2) * t], (t, 2 * t))
            nb_ref[o] = pltpu.roll(g, 0, 1, stride=1, stride_axis=0)[:, t:]
        nb_ref[ATT_NEAR] = jnp.zeros((t, t), F32)
        kmax = jnp.zeros((SUBLANES, t), F32)
        for c in range(k_ref.shape[1] // t):
            kk = k_ref[0, c * t:(c + 1) * t, :].astype(F32)
            kmax = jnp.maximum(kmax, _dot_nt(sel, (kk * kk).astype(BF16)))
        for a in range(2):
            km_ref[a] = jnp.broadcast_to(jnp.max(kmax[a:a + 1, :], axis=1, keepdims=True), (1, t))

    qt = q_ref[0, 0]
    row = lax.broadcasted_iota(jnp.int32, qt.shape, 0)
    zero = jnp.zeros_like(qt)
    qm_ref[0] = jnp.where(row < e, qt, zero)
    qm_ref[1] = jnp.where(row >= e, qt, zero)
    qf = qt.astype(F32)
    qq = qf * qf
    qn2 = [jnp.sum(qq[a * e:(a + 1) * e], axis=0, keepdims=True) for a in range(2)]
    bias_max = f_ref[0, 1:2, 0:t]
    n_far = jnp.maximum(i - (ATT_NEAR - 1), 0)

    l8_ref[...] = jnp.zeros_like(l8_ref)
    acc_ref[...] = jnp.zeros_like(acc_ref)
    shift = [jnp.sqrt(qn2[a] * km_ref[a]) * ATT_BOUND_SLACK + bias_max for a in range(2)]

    kc = ATT_KC

    def fast_tiles(j0, ntiles, biased):
        for a in range(2):
            pv = None
            l8 = None
            for tt in range(ntiles):
                j = j0 + tt
                for c in range(0, t, kc):
                    kb = k_ref[0, pl.ds(pl.multiple_of(j * t + c, kc), kc), :]
                    s = _dot(kb, qm_ref[a])
                    if biased:
                        s = s + nb_ref[jnp.minimum(i - j, ATT_NEAR), c:c + kc, :]
                    p = jnp.exp2(s - shift[a])
                    part = jnp.sum(p.reshape(kc // SUBLANES, SUBLANES, t), axis=0)
                    d = _dot(vt_ref[0, j, :, c:c + kc], p.astype(BF16))
                    l8 = part if l8 is None else l8 + part
                    pv = d if pv is None else pv + d
            l8_ref[a] += l8
            acc_ref[a] += pv

    def probs(u, biased):
        slot = u % 2
        for a in range(2):
            l8 = None
            for tt in range(2):
                j = 2 * u + tt
                kb = k_ref[0, pl.ds(pl.multiple_of(j * t, t), t), :]
                s = _dot(kb, qm_ref[a])
                if biased:
                    s = s + nb_ref[jnp.minimum(i - j, ATT_NEAR)]
                p = jnp.exp2(s - shift[a])
                part = jnp.sum(p.reshape(t // SUBLANES, SUBLANES, t), axis=0)
                l8 = part if l8 is None else l8 + part
                p_ref[slot, a, tt] = p.astype(BF16)
            l8_ref[a] += l8

    def values(u):
        slot = u % 2
        for a in range(2):
            acc_ref[a] += (_dot(vt_ref[0, 2 * u], p_ref[slot, a, 0])
                           + _dot(vt_ref[0, 2 * u + 1], p_ref[slot, a, 1]))

    def far_step(u, carry):
        values(u - 1)
        probs(u, False)
        return carry

    def near_step(u, carry):
        values(u - 1)
        probs(u, True)
        return carry

    def fast_near(j, carry):
        fast_tiles(j, 1, True)
        return carry

    n = i + 1
    pairs = n // 2
    far_pairs = n_far // 2

    @pl.when(far_pairs > 0)
    def _():
        probs(0, False)

    @pl.when(jnp.logical_and(far_pairs == 0, pairs > 0))
    def _():
        probs(0, True)

    lax.fori_loop(1, far_pairs, far_step, 0)
    lax.fori_loop(jnp.maximum(far_pairs, 1), pairs, near_step, 0)

    @pl.when(pairs > 0)
    def _():
        values(pairs - 1)

    lax.fori_loop(2 * pairs, n, fast_near, 0)
    for a in range(2):
        l_ref[a] = jnp.sum(l8_ref[a], axis=0, keepdims=True)
    l_min = jnp.min(jnp.minimum(l_ref[0], l_ref[1]))
    l_max = jnp.max(jnp.maximum(l_ref[0], l_ref[1]))
    in_range = jnp.logical_and(l_min >= ATT_MIN_DENOM, l_max <= ATT_MAX_DENOM)

    @pl.when(jnp.logical_not(in_range))
    def _():
        m_ref[...] = jnp.full_like(m_ref, MASKED)
        l_ref[...] = jnp.zeros_like(l_ref)
        acc_ref[...] = jnp.zeros_like(acc_ref)

        def online_tile(j, bias):
            kb = k_ref[0, pl.ds(pl.multiple_of(j * t, t), t), :]
            vtb = vt_ref[0, j]
            for a in range(2):
                s = _dot(kb, qm_ref[a])
                if bias is not None:
                    s = s + bias
                m_old = m_ref[a]
                m_new = jnp.maximum(m_old, jnp.max(s, axis=0, keepdims=True))
                alpha = jnp.exp2(m_old - m_new)
                p = jnp.exp2(s - m_new)
                l_ref[a] = alpha * l_ref[a] + jnp.sum(p, axis=0, keepdims=True)
                acc_ref[a] = alpha * acc_ref[a] + _dot(vtb, p.astype(BF16))
                m_ref[a] = m_new

        def online_far(j, carry):
            online_tile(j, None)
            return carry

        def online_near(j, carry):
            online_tile(j, nb_ref[i - j])
            return carry

        lax.fori_loop(0, n_far, online_far, 0)
        lax.fori_loop(n_far, i + 1, online_near, 0)

    lam = (jnp.exp(jnp.sum(lam_ref[0:1, :] * lam_ref[1:2, :], axis=-1, keepdims=True))
           - jnp.exp(jnp.sum(lam_ref[2:3, :] * lam_ref[3:4, :], axis=-1, keepdims=True))
           + lam_init)
    ot = acc_ref[0] / l_ref[0] - lam * (acc_ref[1] / l_ref[1])
    ot = ot * lax.rsqrt(jnp.mean(ot * ot, axis=0, keepdims=True) + EPS)
    o_ref[0] = (ot.T * sub_ref[...] * (1.0 - lam_init)).astype(o_ref.dtype)


def _diff_bias(rel_bias, s):
    t = ATT_T
    n = ATT_NEAR * t
    assert n >= REL_MAX_DISTANCE + t - 1 and s >= n
    dist_bias = rel_bias[_t5_bucket(jnp.arange(n))].astype(F32)
    far = rel_bias[REL_BUCKETS - 1].astype(F32)
    near = (dist_bias - far) * LOG2E
    f = jnp.concatenate([jnp.full((t, far.shape[0]), MASKED, F32), near])
    top = jnp.broadcast_to(jnp.maximum(jnp.max(near, axis=0), 0.0), f.shape)
    return jnp.stack([f.T, top.T], axis=1)


def _diff_attention(q, k, vt, rel_bias, lam4, subln, layer_idx):
    b, s, n = k.shape
    t = ATT_T
    w = 2 * DIFF_HEAD_DIM
    lam_init = 0.8 - 0.6 * math.exp(-0.3 * layer_idx)
    return pl.pallas_call(
        functools.partial(_diff_kernel, lam_init=lam_init),
        grid=(b, DIFF_HEADS, s // t),
        in_specs=[pl.BlockSpec((1, 1, w, t), lambda bi, h, i: (bi, i, h, 0)),
                  pl.BlockSpec((1, s, w), lambda bi, h, i: (bi, 0, h)),
                  pl.BlockSpec((1, s // t, w, t), lambda bi, h, i: (bi, 0, h, 0)),
                  pl.BlockSpec((1, 2, (ATT_NEAR + 1) * t), lambda bi, h, i: (h, 0, 0)),
                  pl.BlockSpec((4, DIFF_HEAD_DIM), lambda bi, h, i: (0, 0)),
                  pl.BlockSpec((1, w), lambda bi, h, i: (0, 0))],
        out_specs=pl.BlockSpec((1, t, w), lambda bi, h, i: (bi, i, h)),
        out_shape=jax.ShapeDtypeStruct((b, s, n), BF16),
        scratch_shapes=[pltpu.VMEM((2, w, t), BF16),
                        pltpu.VMEM((2, 1, t), F32),
                        pltpu.VMEM((2, 1, t), F32),
                        pltpu.VMEM((2, 1, t), F32),
                        pltpu.VMEM((2, SUBLANES, t), F32),
                        pltpu.VMEM((2, w, t), F32),
                        pltpu.VMEM((ATT_NEAR + 1, t, t), F32),
                        pltpu.VMEM((2, 2, 2, t, t), BF16)],
        compiler_params=_params(("arbitrary", "arbitrary", "arbitrary")),
        name="diff_attention",
    )(q, k, vt, _diff_bias(rel_bias, s), lam4, subln)


def kernel(x, rel_bias, norm_mix, norm_ffn, norm_final, w_in_ab, ret_gn, w_out_ab, w_in_c,
           lam_q1, lam_k1, lam_q2, lam_k2, diff_subln, w_out_c, w_ffn_in, conv_w, conv_b,
           w_ffn_out):
    b, s, d = x.shape
    depth = norm_mix.shape[0]
    assert depth == 2 and s % RET_ROWS == 0 and s % FFN_TM == 0 and s >= ATT_NEAR * ATT_T
    assert s % (max(DIL_DILATIONS) * DIL_BLOCK) == 0
    row = lambda v: v.reshape(1, -1)

    proj = _proj0(x, row(norm_mix[0]), w_in_ab[0].astype(BF16))
    ret = _retention(proj[0], row(ret_gn[0]), b, s)
    outs, lses = zip(*[_dilated_pattern(dq, rel_bias, wdw, dil)
                       for dq, (wdw, dil) in zip(proj[1:], DIL_PATTERNS)])
    x3 = _mix0(x, ret, outs, lses, w_out_ab[0].astype(BF16))
    x3 = _ffn(x3, row(norm_ffn[0]), w_ffn_in[0].astype(BF16), conv_w[0],
              row(conv_b[0]), w_ffn_out[0].astype(BF16), row(norm_final), False)

    nqk = DIFF_HEADS * 2 * DIFF_HEAD_DIM
    wc = w_in_c[0].astype(BF16)
    q, k, vt = _proj1(x3, row(norm_mix[1]), wc[:, :nqk].T, wc[:, nqk:2 * nqk], wc[:, 2 * nqk:].T)
    lam4 = jnp.stack([lam_q1[0], lam_k1[0], lam_q2[0], lam_k2[0]]).astype(F32)
    a = _diff_attention(q, k, vt, rel_bias, lam4, row(diff_subln[0]), 1)
    return _ffn(x3, row(norm_ffn[1]), w_ffn_in[1].astype(BF16), conv_w[1], row(conv_b[1]),
                w_ffn_out[1].astype(BF16), row(norm_final), True,
                mixer=(a, w_out_c[0].astype(BF16)))
```

```python
import functools
import math

import jax
import jax.numpy as jnp
from jax import lax
from jax.experimental import pallas as pl
from jax.experimental.pallas import tpu as pltpu

F32 = jnp.float32
BF16 = jnp.bfloat16

EPS = 1e-6
MASKED = -1e30

RET_HEADS = 4
RET_DIM = 128
RET_CHUNK = 128
ROPE_BASE = 10000.0
DIL_HEADS = 8
DIL_HEAD_DIM = 64
DIL_PATTERNS = ((128, 1), (512, 4), (2048, 16))
DIL_BLOCK = 128
DIFF_HEADS = 8
DIFF_HEAD_DIM = 64
REL_BUCKETS = 32
REL_MAX_DISTANCE = 2048
D_FF = 2816
CONV_WIDTH = 3
DIL_W = DIL_HEADS * DIL_HEAD_DIM
RET_W = 4 * RET_HEADS * RET_DIM
DIL_DILATIONS = tuple(d for _, d in DIL_PATTERNS)

LANES = 128
SUBLANES = 8
VMEM_BYTES_V7X = 64 * 1024 * 1024
VMEM_LIMIT = VMEM_BYTES_V7X - 8 * 1024 * 1024

PROJ_TM = 512
PROJ_TN = 512
RET_ROWS = 1024
DIL_QBLOCKS = 2
FFN_TM = 512
FFN_CF = 256
ATT_T = 512
ATT_KC = 512
ATT_BOUND_SLACK = 1.0 + 2.0 ** -5
ATT_MIN_DENOM = 2.0 ** -60
ATT_MAX_DENOM = 2.0 ** 100
_MAX_EXACT = REL_BUCKETS // 2
BIAS_FLAT_FROM = 8 + math.ceil(_MAX_EXACT * (REL_MAX_DISTANCE / _MAX_EXACT)
                               ** ((REL_BUCKETS - 1 - _MAX_EXACT) / (REL_BUCKETS - _MAX_EXACT)))
ATT_NEAR = -(-(BIAS_FLAT_FROM + ATT_T - 1) // ATT_T)
LOG2E = math.log2(math.e)
ATT_QSCALE = DIFF_HEAD_DIM ** -0.5 * LOG2E


def _params(sem):
    return pltpu.CompilerParams(dimension_semantics=sem, vmem_limit_bytes=VMEM_LIMIT)


def _rms(x, g):
    return x * lax.rsqrt(jnp.mean(x * x, axis=-1, keepdims=True) + EPS) * g


def _dot(a, b):
    return jnp.dot(a, b, preferred_element_type=F32)


def _dot_nt(a, b):
    return lax.dot_general(a, b, (((1,), (1,)), ((), ())), preferred_element_type=F32)


def _dot_tn(a, b):
    return lax.dot_general(a, b, (((0,), (0,)), ((), ())), preferred_element_type=F32)


def _proj0_kernel(x_ref, g_ref, w_ref, a_ref, *rest):
    d_refs, dbuf = rest[:-1], rest[-1]
    tm = x_ref.shape[1]
    h = _rms(x_ref[0], g_ref[...]).astype(BF16)
    na = a_ref.shape[-1]
    nd = w_ref.shape[1] - na
    for c in range(0, nd, PROJ_TN):
        res = _dot(h, w_ref[:, na + c:na + c + PROJ_TN])
        for s in range(c // LANES, (c + PROJ_TN) // LANES):
            cols = slice(s * LANES, (s + 1) * LANES)
            dbuf[0, s] = res[:, (s * LANES - c):(s * LANES - c) + LANES]
            src, d_prev = 0, 1
            for dref, d in zip(d_refs, DIL_DILATIONS):
                if d == 1:
                    dref[0, 0, :, cols] = dbuf[0, s].astype(BF16)
                    continue
                step, rows = d // d_prev, tm // d
                dst = 1 - src
                for rp in range(d_prev):
                    for q in range(step):
                        piece = dbuf[src, s, pl.ds(rp * (tm // d_prev) + q, rows, stride=step), :]
                        r = rp + q * d_prev
                        if d != DIL_DILATIONS[-1]:
                            dbuf[dst, s, r * rows:(r + 1) * rows, :] = piece
                        dref[0, r, :, cols] = piece.astype(BF16)
                src, d_prev = dst, d
    for c in range(0, na, PROJ_TN):
        a_ref[0, :, c:c + PROJ_TN] = _dot(h, w_ref[:, c:c + PROJ_TN]).astype(BF16)


def _proj0(x3, g, w):
    b, s, dm = x3.shape
    n = w.shape[1]
    nd = n - RET_W
    tm = PROJ_TM
    d_specs = [pl.BlockSpec((1, d, tm // d, nd), lambda bi, i: (bi, 0, i, 0)) for d in DIL_DILATIONS]
    d_shapes = [jax.ShapeDtypeStruct((b, d, s // d, nd), BF16) for d in DIL_DILATIONS]
    return pl.pallas_call(
        _proj0_kernel,
        grid=(b, s // tm),
        in_specs=[pl.BlockSpec((1, tm, dm), lambda bi, i: (bi, i, 0)),
                  pl.BlockSpec((1, dm), lambda bi, i: (0, 0)),
                  pl.BlockSpec((dm, n), lambda bi, i: (0, 0))],
        out_specs=[pl.BlockSpec((1, tm, RET_W), lambda bi, i: (bi, i, 0))] + d_specs,
        out_shape=[jax.ShapeDtypeStruct((b, s, RET_W), BF16)] + d_shapes,
        scratch_shapes=[pltpu.VMEM((2, nd // LANES, tm, LANES), F32)],
        compiler_params=_params(("arbitrary", "arbitrary")),
        name="proj0",
    )(x3, g, w)


def _ret_kernel(q_ref, k_ref, v_ref, gate_ref, cos_ref, sin_ref, tab_ref, gn_ref, o_ref,
                state_ref, qr_ref, kr_ref, qx_ref, kv_ref, st_ref):
    @pl.when(pl.program_id(2) == 0)
    def _():
        state_ref[...] = jnp.zeros_like(state_ref)

    decay = tab_ref[0, 0]
    xi = tab_ref[0, 1]
    zeta = tab_ref[0, 2]
    g_chunk = tab_ref[0, 3]
    gain = gn_ref[...]
    half = RET_DIM // 2
    n_chunks = q_ref.shape[1] // RET_CHUNK
    for c in range(n_chunks):
        rows = pl.ds(c * RET_CHUNK, RET_CHUNK)
        cos = cos_ref[rows, :]
        sin = sin_ref[rows, :]
        q = q_ref[0, rows, :].astype(F32)
        k = k_ref[0, rows, :].astype(F32)
        q = q * cos + pltpu.roll(q, half, 1) * sin
        k = k * cos + pltpu.roll(k, half, 1) * sin
        qr_ref[c] = q.astype(BF16)
        kr_ref[c] = k.astype(BF16)
        qx_ref[c] = (q * xi).astype(BF16)
        kv_ref[c] = _dot_tn((k * zeta).astype(BF16), v_ref[0, rows, :])
    state = state_ref[...]
    for c in range(n_chunks):
        st_ref[c] = state.astype(BF16)
        state = g_chunk * state + kv_ref[c]
    state_ref[...] = state
    for c in range(n_chunks):
        rows = pl.ds(c * RET_CHUNK, RET_CHUNK)
        v = v_ref[0, rows, :]
        intra = _dot_nt(qr_ref[c], kr_ref[c]) * decay
        y = _dot(intra.astype(BF16), v) + _dot(qx_ref[c], st_ref[c])
        mu = jnp.mean(y, axis=-1, keepdims=True)
        yc = y - mu
        var = jnp.mean(yc * yc, axis=-1, keepdims=True)
        yn = yc * lax.rsqrt(var + EPS) * gain
        gate = gate_ref[0, rows, :].astype(F32)
        o_ref[0, rows, :] = (gate * jax.nn.sigmoid(gate) * yn).astype(o_ref.dtype)


def _retention_tables(s):
    c, e = RET_CHUNK, RET_DIM
    inv = ROPE_BASE ** (-jnp.arange(0, e, 2, dtype=F32) / e)
    ang = jnp.arange(s, dtype=F32)[:, None] * inv[None, :]
    cos = jnp.concatenate([jnp.cos(ang), jnp.cos(ang)], axis=-1)
    sin = jnp.concatenate([-jnp.sin(ang), jnp.sin(ang)], axis=-1)
    log_g = jnp.log1p(-jnp.exp2(-5.0 - jnp.arange(RET_HEADS, dtype=F32)))
    pos = jnp.arange(c, dtype=F32)
    rel = pos[:, None] - pos[None, :]
    causal = rel >= 0
    scale = e ** -0.5
    decay = jnp.where(causal, jnp.exp(jnp.where(causal, rel, 0.0)[None] * log_g[:, None, None]), 0.0)
    zeta = jnp.exp((c - 1 - pos)[None, :] * log_g[:, None])
    xi = jnp.exp((pos + 1)[None, :] * log_g[:, None])
    g_chunk = jnp.exp(c * log_g)
    tab = jnp.stack([
        decay * scale,
        jnp.broadcast_to(xi[:, :, None], (RET_HEADS, c, e)),
        jnp.broadcast_to(zeta[:, :, None] * scale, (RET_HEADS, c, e)),
        jnp.broadcast_to(g_chunk[:, None, None], (RET_HEADS, e, e)),
    ], axis=1)
    return cos, sin, tab


def _retention(p0, gn, b, s):
    cos, sin, tab = _retention_tables(s)
    e = RET_DIM
    nc = RET_ROWS // RET_CHUNK
    col = lambda off: (lambda bi, h, t: (bi, t, off + h))
    return pl.pallas_call(
        _ret_kernel,
        grid=(b, RET_HEADS, s // RET_ROWS),
        in_specs=[pl.BlockSpec((1, RET_ROWS, e), col(0)),
                  pl.BlockSpec((1, RET_ROWS, e), col(RET_HEADS)),
                  pl.BlockSpec((1, RET_ROWS, e), col(2 * RET_HEADS)),
                  pl.BlockSpec((1, RET_ROWS, e), col(3 * RET_HEADS)),
                  pl.BlockSpec((RET_ROWS, e), lambda bi, h, t: (t, 0)),
                  pl.BlockSpec((RET_ROWS, e), lambda bi, h, t: (t, 0)),
                  pl.BlockSpec((1, 4, e, e), lambda bi, h, t: (h, 0, 0, 0)),
                  pl.BlockSpec((1, e), lambda bi, h, t: (0, h))],
        out_specs=pl.BlockSpec((1, RET_ROWS, e), lambda bi, h, t: (bi, t, h)),
        out_shape=jax.ShapeDtypeStruct((b, s, RET_HEADS * e), BF16),
        scratch_shapes=[pltpu.VMEM((e, e), F32),
                        pltpu.VMEM((nc, RET_CHUNK, e), BF16),
                        pltpu.VMEM((nc, RET_CHUNK, e), BF16),
                        pltpu.VMEM((nc, RET_CHUNK, e), BF16),
                        pltpu.VMEM((nc, e, e), F32),
                        pltpu.VMEM((nc, e, e), BF16)],
        compiler_params=_params(("arbitrary", "arbitrary", "arbitrary")),
        name="retention",
    )(p0, p0, p0, p0, cos, sin, tab, gn)


def _t5_bucket(dist):
    max_exact = REL_BUCKETS // 2
    d = jnp.maximum(dist.astype(F32), 1.0)
    large = max_exact + (jnp.log(d / max_exact) / math.log(REL_MAX_DISTANCE / max_exact)
                         * (REL_BUCKETS - max_exact))
    large = jnp.clip(large.astype(jnp.int32), max_exact, REL_BUCKETS - 1)
    return jnp.where(dist < max_exact, dist, large)


def _dil_kernel(q_ref, kp_ref, kc_ref, vp_ref, vc_ref, g_ref, o_ref, lse_ref, bias_ref,
                s_ref, p_ref):
    first = pl.program_id(2) == 0
    blk, e = DIL_BLOCK, DIL_HEAD_DIM

    @pl.when(jnp.logical_and(jnp.logical_and(pl.program_id(0) == 0, pl.program_id(1) == 0), first))
    def _():
        for h in range(DIL_HEADS):
            gb = jnp.broadcast_to(g_ref[h:h + 1, :], (blk, 4 * blk))
            bias_ref[h] = pltpu.roll(gb, 0, 1, stride=1, stride_axis=0)[:, :2 * blk]

    q_all = q_ref[0, 0] * (e ** -0.5)
    k_all = jnp.concatenate([kp_ref[0, 0], kc_ref[0, 0]], axis=0)
    v_all = jnp.concatenate([vp_ref[0, 0], vc_ref[0, 0]], axis=0)
    in_prev = lax.broadcasted_iota(jnp.int32, (blk, 2 * blk), 1) < blk
    no_prev = jnp.logical_and(first, in_prev)
    lane = lax.broadcasted_iota(jnp.int32, (blk, LANES), 1)
    for qb in range(q_all.shape[0] // blk):
        q = q_all[qb * blk:(qb + 1) * blk]
        kcat = k_all[qb * blk:(qb + 2) * blk]
        vcat = v_all[qb * blk:(qb + 2) * blk]
        lse_all = jnp.zeros((blk, LANES), F32)
        for h in range(DIL_HEADS):
            cols = slice(h * e, (h + 1) * e)
            s = _dot_nt(q[:, cols], kcat[:, cols]) + bias_ref[h]
            if qb == 0:
                s = jnp.where(no_prev, MASKED, s)
            s_ref[qb, h] = s
        inv_l = []
        for h in range(DIL_HEADS):
            s = s_ref[qb, h]
            m = jnp.max(s, axis=-1, keepdims=True)
            p = jnp.exp(s - m)
            l = jnp.sum(p, axis=-1, keepdims=True)
            p_ref[qb, h] = p.astype(BF16)
            inv_l.append(1.0 / l)
            lse_all = jnp.where(lane == h, m + jnp.log(l), lse_all)
        for h in range(DIL_HEADS):
            cols = slice(h * e, (h + 1) * e)
            o_ref[0, 0, qb * blk:(qb + 1) * blk, cols] = (
                _dot(p_ref[qb, h], vcat[:, cols]) * inv_l[h]).astype(o_ref.dtype)
        lse_ref[0, 0, qb * blk:(qb + 1) * blk, :] = lse_all


def _dil_bias_vector(rel_bias, window, dil):
    blk = DIL_BLOCK
    span = window // dil
    r = blk - jnp.arange(4 * blk)
    band = (r >= 0) & (r <= span)
    bias = rel_bias[_t5_bucket(jnp.maximum(r, 0) * dil)].astype(F32)
    return jnp.where(band[:, None], bias, MASKED).T


def _dilated_pattern(dq, rel_bias, window, dil):
    b, d, sd, _ = dq.shape
    blk = DIL_BLOCK
    nq = DIL_QBLOCKS
    rows = nq * blk
    assert d == dil and window // dil == blk and sd % rows == 0
    cur = lambda c: (lambda bi, r, n: (bi, r, n, c))
    prev = lambda c: (lambda bi, r, n: (bi, r, jnp.maximum(n * nq - 1, 0), c))
    spec = lambda im: pl.BlockSpec((1, 1, rows, DIL_W), im)
    pspec = lambda im: pl.BlockSpec((1, 1, blk, DIL_W), im)
    return pl.pallas_call(
        _dil_kernel,
        grid=(b, dil, sd // rows),
        in_specs=[spec(cur(0)), pspec(prev(1)), spec(cur(1)), pspec(prev(2)), spec(cur(2)),
                  pl.BlockSpec((DIL_HEADS, 4 * blk), lambda bi, r, n: (0, 0))],
        out_specs=[pl.BlockSpec((1, 1, rows, DIL_W), lambda bi, r, n: (bi, r, n, 0)),
                   pl.BlockSpec((1, 1, rows, LANES), lambda bi, r, n: (bi, r, n, 0))],
        out_shape=[jax.ShapeDtypeStruct((b, dil, sd, DIL_W), BF16),
                   jax.ShapeDtypeStruct((b, dil, sd, LANES), F32)],
        scratch_shapes=[pltpu.VMEM((DIL_HEADS, blk, 2 * blk), F32),
                        pltpu.VMEM((nq, DIL_HEADS, blk, 2 * blk), F32),
                        pltpu.VMEM((nq, DIL_HEADS, blk, 2 * blk), BF16)],
        compiler_params=_params(("arbitrary", "arbitrary", "arbitrary")),
        name=f"dilated_d{dil}",
    )(dq, dq, dq, dq, dq, _dil_bias_vector(rel_bias, window, dil))


def _mix0_kernel(x_ref, ret_ref, *rest):
    np_ = len(DIL_DILATIONS)
    o_refs, l_refs = rest[:np_], rest[np_:2 * np_]
    w_ref, out_ref, obuf, lbuf, dbuf = rest[2 * np_:]
    tm = x_ref.shape[1]
    nslab = DIL_W // LANES
    for pi, d in enumerate(DIL_DILATIONS):
        rows = tm // d
        for r in range(d):
            dst = pl.ds(r, rows, stride=d) if d > 1 else pl.ds(0, tm)
            lbuf[pi, dst, :] = l_refs[pi][0, r]
            for s in range(nslab):
                obuf[pi, s, dst, :] = o_refs[pi][0, r, :, s * LANES:(s + 1) * LANES].astype(F32)
    lses = [lbuf[pi] for pi in range(np_)]
    m = functools.reduce(jnp.maximum, lses)
    es = [jnp.exp(l - m) for l in lses]
    inv = 1.0 / functools.reduce(lambda a, b: a + b, es)
    row = lax.broadcasted_iota(jnp.int32, (LANES, DIL_W), 0)
    col = lax.broadcasted_iota(jnp.int32, (LANES, DIL_W), 1)
    spread = (col // DIL_HEAD_DIM == row).astype(BF16)
    spread2 = jnp.concatenate([spread, spread], axis=0)
    wide = []
    for ex in es:
        wgt = ex * inv
        hi = wgt.astype(BF16)
        lo = (wgt - hi.astype(F32)).astype(BF16)
        wide.append(_dot(jnp.concatenate([hi, lo], axis=1), spread2))
    for s in range(nslab):
        cols = slice(s * LANES, (s + 1) * LANES)
        acc = wide[0][:, cols] * obuf[0, s]
        for pi in range(1, np_):
            acc = acc + wide[pi][:, cols] * obuf[pi, s]
        dbuf[:, cols] = acc.astype(BF16)
    na = ret_ref.shape[-1]
    y = _dot(ret_ref[0], w_ref[:na, :]) + _dot(dbuf[...], w_ref[na:, :])
    out_ref[0] = x_ref[0] + y


def _mix0(x3, ret, outs, lses, w):
    b, s, dm = x3.shape
    na = ret.shape[-1]
    tm = PROJ_TM
    o_specs = [pl.BlockSpec((1, d, tm // d, DIL_W), lambda bi, i: (bi, 0, i, 0)) for d in DIL_DILATIONS]
    l_specs = [pl.BlockSpec((1, d, tm // d, LANES), lambda bi, i: (bi, 0, i, 0)) for d in DIL_DILATIONS]
    np_ = len(DIL_DILATIONS)
    return pl.pallas_call(
        _mix0_kernel,
        grid=(b, s // tm),
        in_specs=[pl.BlockSpec((1, tm, dm), lambda bi, i: (bi, i, 0)),
                  pl.BlockSpec((1, tm, na), lambda bi, i: (bi, i, 0))] + o_specs + l_specs
                 + [pl.BlockSpec((na + DIL_W, dm), lambda bi, i: (0, 0))],
        out_specs=pl.BlockSpec((1, tm, dm), lambda bi, i: (bi, i, 0)),
        out_shape=jax.ShapeDtypeStruct((b, s, dm), F32),
        scratch_shapes=[pltpu.VMEM((np_, DIL_W // LANES, tm, LANES), F32),
                        pltpu.VMEM((np_, tm, LANES), F32),
                        pltpu.VMEM((tm, DIL_W), BF16)],
        compiler_params=_params(("arbitrary", "arbitrary")),
        name="mix0",
    )(x3, ret, *outs, *lses, w)


def _ffn_kernel(x_ref, *rest, final_norm, mixer_proj):
    if mixer_proj:
        a_ref, wmix_ref = rest[:2]
        rest = rest[2:]
    g_ref, win_ref, cw_ref, cb_ref, wout_ref, fg_ref, o_ref, ubuf, carry, gbuf = rest
    tm = x_ref.shape[1]
    halo = SUBLANES

    @pl.when(pl.program_id(1) == 0)
    def _():
        carry[...] = jnp.zeros_like(carry)

    x = x_ref[0]
    if mixer_proj:
        x = x + _dot(a_ref[0], wmix_ref[...])
    h = _rms(x, g_ref[...]).astype(BF16)

    def conv(part, col):
        cols = slice(col, col + FFN_CF)
        u = _dot(h, win_ref[:, cols])
        ubuf[part, 0:halo, :] = carry[:, cols]
        ubuf[part, halo:halo + tm, :] = u
        carry[:, cols] = u[tm - halo:, :]
        u1 = ubuf[part, halo - 1:halo - 1 + tm, :]
        u2 = ubuf[part, halo - 2:halo - 2 + tm, :]
        return (u * cw_ref[2:3, cols] + u1 * cw_ref[1:2, cols] + u2 * cw_ref[0:1, cols]
                + cb_ref[:, cols])

    for c in range(D_FF // FFN_CF):
        gate = conv(0, c * FFN_CF)
        up = conv(1, D_FF + c * FFN_CF)
        act = 0.5 * gate * (1.0 + lax.erf(gate * (2.0 ** -0.5)))
        gbuf[:, c * FFN_CF:(c + 1) * FFN_CF] = (act * up).astype(BF16)

    y = x + _dot(gbuf[...], wout_ref[...])
    if final_norm:
        y = _rms(y, fg_ref[...])
    o_ref[0] = y


def _ffn(x3, g, w_in, conv_w, conv_b, w_out, final_gain, final_norm, mixer=None):
    b, s, d = x3.shape
    f2 = w_in.shape[1]
    whole = pl.BlockSpec(memory_space=pltpu.VMEM)
    tile = lambda n: pl.BlockSpec((1, FFN_TM, n), lambda bi, t: (bi, t, 0))
    mix_specs = [tile(mixer[0].shape[-1]), whole] if mixer is not None else []
    mix_args = list(mixer) if mixer is not None else []
    return pl.pallas_call(
        functools.partial(_ffn_kernel, final_norm=final_norm, mixer_proj=mixer is not None),
        grid=(b, s // FFN_TM),
        in_specs=[tile(d)] + mix_specs + [whole, whole, whole, whole, whole, whole],
        out_specs=pl.BlockSpec((1, FFN_TM, d), lambda bi, t: (bi, t, 0)),
        out_shape=jax.ShapeDtypeStruct((b, s, d), F32),
        scratch_shapes=[pltpu.VMEM((2, FFN_TM + SUBLANES, FFN_CF), F32),
                        pltpu.VMEM((SUBLANES, f2), F32),
                        pltpu.VMEM((FFN_TM, D_FF), BF16)],
        compiler_params=_params(("arbitrary", "arbitrary")),
        name="ffn_final" if final_norm else "ffn",
    )(x3, *mix_args, g, w_in, conv_w, conv_b, w_out, final_gain)


def _proj1_kernel(x_ref, g_ref, wq_ref, wk_ref, wvt_ref, q_ref, k_ref, vt_ref):
    h = _rms(x_ref[0], g_ref[...]).astype(BF16)
    n = k_ref.shape[-1]
    for c in range(0, n, PROJ_TN):
        cols = slice(c, c + PROJ_TN)
        q_ref[0, 0, cols, :] = (_dot_nt(wq_ref[cols, :], h) * ATT_QSCALE).astype(BF16)
        k_ref[0, :, cols] = _dot(h, wk_ref[:, cols]).astype(BF16)
        vt_ref[0, 0, cols, :] = _dot_nt(wvt_ref[cols, :], h).astype(BF16)


def _proj1(x3, g, wq, wk, wvt):
    b, s, d = x3.shape
    n = wq.shape[1]
    t = ATT_T
    wspec = pl.BlockSpec((d, n), lambda bi, i: (0, 0))
    return pl.pallas_call(
        _proj1_kernel,
        grid=(b, s // t),
        in_specs=[pl.BlockSpec((1, t, d), lambda bi, i: (bi, i, 0)),
                  pl.BlockSpec((1, d), lambda bi, i: (0, 0)),
                  pl.BlockSpec((n, d), lambda bi, i: (0, 0)), wspec,
                  pl.BlockSpec((n, d), lambda bi, i: (0, 0))],
        out_specs=[pl.BlockSpec((1, 1, n, t), lambda bi, i: (bi, i, 0, 0)),
                   pl.BlockSpec((1, t, n), lambda bi, i: (bi, i, 0)),
                   pl.BlockSpec((1, 1, n, t), lambda bi, i: (bi, i, 0, 0))],
        out_shape=[jax.ShapeDtypeStruct((b, s // t, n, t), BF16),
                   jax.ShapeDtypeStruct((b, s, n), BF16),
                   jax.ShapeDtypeStruct((b, s // t, n, t), BF16)],
        compiler_params=_params(("arbitrary", "arbitrary")),
        name="proj1",
    )(x3, g, wq, wk, wvt)


def _diff_kernel(q_ref, k_ref, vt_ref, f_ref, lam_ref, sub_ref, o_ref,
                 qm_ref, km_ref, m_ref, l_ref, l8_ref, acc_ref, nb_ref, p_ref, *, lam_init):
    i = pl.program_id(2)
    t = ATT_T
    e = DIFF_HEAD_DIM
    half = lax.broadcasted_iota(jnp.int32, (SUBLANES, 2 * e), 1) // e
    sel = (lax.broadcasted_iota(jnp.int32, (SUBLANES, 2 * e), 0) == half).astype(BF16)

    @pl.when(i == 0)
    def _():
        for o in range(ATT_NEAR):
            g = jnp.broadcast_to(f_ref[0, 0:1, o * t:(o + 2) * t], (t, 2 * t))
            nb_ref[o] = pltpu.roll(g, 0, 1, stride=1, stride_axis=0)[:, t:]
        nb_ref[ATT_NEAR] = jnp.zeros((t, t), F32)
        kmax = jnp.zeros((SUBLANES, t), F32)
        for c in range(k_ref.shape[1] // t):
            kk = k_ref[0, c * t:(c + 1) * t, :].astype(F32)
            kmax = jnp.maximum(kmax, _dot_nt(sel, (kk * kk).astype(BF16)))
        for a in range(2):
            km_ref[a] = jnp.broadcast_to(jnp.max(kmax[a:a + 1, :], axis=1, keepdims=True), (1, t))

    qt = q_ref[0, 0]
    row = lax.broadcasted_iota(jnp.int32, qt.shape, 0)
    zero = jnp.zeros_like(qt)
    qm_ref[0] = jnp.where(row < e, qt, zero)
    qm_ref[1] = jnp.where(row >= e, qt, zero)
    qf = qt.astype(F32)
    qq = qf * qf
    qn2 = [jnp.sum(qq[a * e:(a + 1) * e], axis=0, keepdims=True) for a in range(2)]
    bias_max = f_ref[0, 1:2, 0:t]
    n_far = jnp.maximum(i - (ATT_NEAR - 1), 0)

    l8_ref[...] = jnp.zeros_like(l8_ref)
    acc_ref[...] = jnp.zeros_like(acc_ref)
    shift = [jnp.sqrt(qn2[a] * km_ref[a]) * ATT_BOUND_SLACK + bias_max for a in range(2)]

    kc = ATT_KC

    def fast_tiles(j0, ntiles, biased):
        for a in range(2):
            pv = None
            l8 = None
            for tt in range(ntiles):
                j = j0 + tt
                for c in range(0, t, kc):
                    kb = k_ref[0, pl.ds(pl.multiple_of(j * t + c, kc), kc), :]
                    s = _dot(kb, qm_ref[a])
                    if biased:
                        s = s + nb_ref[jnp.minimum(i - j, ATT_NEAR), c:c + kc, :]
                    p = jnp.exp2(s - shift[a])
                    part = jnp.sum(p.reshape(kc // SUBLANES, SUBLANES, t), axis=0)
                    d = _dot(vt_ref[0, j, :, c:c + kc], p.astype(BF16))
                    l8 = part if l8 is None else l8 + part
                    pv = d if pv is None else pv + d
            l8_ref[a] += l8
            acc_ref[a] += pv

    def probs(u, biased):
        slot = u % 2
        for a in range(2):
            l8 = None
            for tt in range(2):
                j = 2 * u + tt
                kb = k_ref[0, pl.ds(pl.multiple_of(j * t, t), t), :]
                s = _dot(kb, qm_ref[a])
                if biased:
                    s = s + nb_ref[jnp.minimum(i - j, ATT_NEAR)]
                p = jnp.exp2(s - shift[a])
                part = jnp.sum(p.reshape(t // SUBLANES, SUBLANES, t), axis=0)
                l8 = part if l8 is None else l8 + part
                p_ref[slot, a, tt] = p.astype(BF16)
            l8_ref[a] += l8

    def values(u):
        slot = u % 2
        for a in range(2):
            acc_ref[a] += (_dot(vt_ref[0, 2 * u], p_ref[slot, a, 0])
                           + _dot(vt_ref[0, 2 * u + 1], p_ref[slot, a, 1]))

    def far_step(u, carry):
        values(u - 1)
        probs(u, False)
        return carry

    def near_step(u, carry):
        values(u - 1)
        probs(u, True)
        return carry

    def fast_near(j, carry):
        fast_tiles(j, 1, True)
        return carry

    n = i + 1
    pairs = n // 2
    far_pairs = n_far // 2

    @pl.when(far_pairs > 0)
    def _():
        probs(0, False)

    @pl.when(jnp.logical_and(far_pairs == 0, pairs > 0))
    def _():
        probs(0, True)

    lax.fori_loop(1, far_pairs, far_step, 0)
    lax.fori_loop(jnp.maximum(far_pairs, 1), pairs, near_step, 0)

    @pl.when(pairs > 0)
    def _():
        values(pairs - 1)

    lax.fori_loop(2 * pairs, n, fast_near, 0)
    def finalize():
        lam = (jnp.exp(jnp.sum(lam_ref[0:1, :] * lam_ref[1:2, :], axis=-1, keepdims=True))
               - jnp.exp(jnp.sum(lam_ref[2:3, :] * lam_ref[3:4, :], axis=-1, keepdims=True))
               + lam_init)
        ot = acc_ref[0] * (1.0 / l_ref[0]) - acc_ref[1] * (lam / l_ref[1])
        ot = ot * lax.rsqrt(jnp.mean(ot * ot, axis=0, keepdims=True) + EPS)
        o_ref[0] = (ot.T * sub_ref[...] * (1.0 - lam_init)).astype(o_ref.dtype)

    for a in range(2):
        l_ref[a] = jnp.sum(l8_ref[a], axis=0, keepdims=True)
    l_min = jnp.min(jnp.minimum(l_ref[0], l_ref[1]))
    l_max = jnp.max(jnp.maximum(l_ref[0], l_ref[1]))
    in_range = jnp.logical_and(l_min >= ATT_MIN_DENOM, l_max <= ATT_MAX_DENOM)
    finalize()

    @pl.when(jnp.logical_not(in_range))
    def _():
        m_ref[...] = jnp.full_like(m_ref, MASKED)
        l_ref[...] = jnp.zeros_like(l_ref)
        acc_ref[...] = jnp.zeros_like(acc_ref)

        def online_tile(j, bias):
            kb = k_ref[0, pl.ds(pl.multiple_of(j * t, t), t), :]
            vtb = vt_ref[0, j]
            for a in range(2):
                s = _dot(kb, qm_ref[a])
                if bias is not None:
                    s = s + bias
                m_old = m_ref[a]
                m_new = jnp.maximum(m_old, jnp.max(s, axis=0, keepdims=True))
                alpha = jnp.exp2(m_old - m_new)
                p = jnp.exp2(s - m_new)
                l_ref[a] = alpha * l_ref[a] + jnp.sum(p, axis=0, keepdims=True)
                acc_ref[a] = alpha * acc_ref[a] + _dot(vtb, p.astype(BF16))
                m_ref[a] = m_new

        def online_far(j, carry):
            online_tile(j, None)
            return carry

        def online_near(j, carry):
            online_tile(j, nb_ref[i - j])
            return carry

        lax.fori_loop(0, n_far, online_far, 0)
        lax.fori_loop(n_far, i + 1, online_near, 0)
        finalize()


def _diff_bias(rel_bias, s):
    t = ATT_T
    n = ATT_NEAR * t
    assert n >= BIAS_FLAT_FROM + t - 1 and s >= n
    dist_bias = rel_bias[_t5_bucket(jnp.arange(n))].astype(F32)
    far = rel_bias[REL_BUCKETS - 1].astype(F32)
    near = (dist_bias - far) * LOG2E
    f = jnp.concatenate([jnp.full((t, far.shape[0]), MASKED, F32), near])
    top = jnp.broadcast_to(jnp.maximum(jnp.max(near, axis=0), 0.0), f.shape)
    return jnp.stack([f.T, top.T], axis=1)


def _diff_attention(q, k, vt, rel_bias, lam4, subln, layer_idx):
    b, s, n = k.shape
    t = ATT_T
    w = 2 * DIFF_HEAD_DIM
    lam_init = 0.8 - 0.6 * math.exp(-0.3 * layer_idx)
    return pl.pallas_call(
        functools.partial(_diff_kernel, lam_init=lam_init),
        grid=(b, DIFF_HEADS, s // t),
        in_specs=[pl.BlockSpec((1, 1, w, t), lambda bi, h, i: (bi, i, h, 0)),
                  pl.BlockSpec((1, s, w), lambda bi, h, i: (bi, 0, h)),
                  pl.BlockSpec((1, s // t, w, t), lambda bi, h, i: (bi, 0, h, 0)),
                  pl.BlockSpec((1, 2, (ATT_NEAR + 1) * t), lambda bi, h, i: (h, 0, 0)),
                  pl.BlockSpec((4, DIFF_HEAD_DIM), lambda bi, h, i: (0, 0)),
                  pl.BlockSpec((1, w), lambda bi, h, i: (0, 0))],
        out_specs=pl.BlockSpec((1, t, w), lambda bi, h, i: (bi, i, h)),
        out_shape=jax.ShapeDtypeStruct((b, s, n), BF16),
        scratch_shapes=[pltpu.VMEM((2, w, t), BF16),
                        pltpu.VMEM((2, 1, t), F32),
                        pltpu.VMEM((2, 1, t), F32),
                        pltpu.VMEM((2, 1, t), F32),
                        pltpu.VMEM((2, SUBLANES, t), F32),
                        pltpu.VMEM((2, w, t), F32),
                        pltpu.VMEM((ATT_NEAR + 1, t, t), F32),
                        pltpu.VMEM((2, 2, 2, t, t), BF16)],
        compiler_params=_params(("arbitrary", "arbitrary", "arbitrary")),
        name="diff_attention",
    )(q, k, vt, _diff_bias(rel_bias, s), lam4, subln)


def kernel(x, rel_bias, norm_mix, norm_ffn, norm_final, w_in_ab, ret_gn, w_out_ab, w_in_c,
           lam_q1, lam_k1, lam_q2, lam_k2, diff_subln, w_out_c, w_ffn_in, conv_w, conv_b,
           w_ffn_out):
    b, s, d = x.shape
    depth = norm_mix.shape[0]
    assert depth == 2 and s % RET_ROWS == 0 and s % FFN_TM == 0 and s >= ATT_NEAR * ATT_T
    assert s % (max(DIL_DILATIONS) * DIL_BLOCK) == 0
    row = lambda v: v.reshape(1, -1)

    proj = _proj0(x, row(norm_mix[0]), w_in_ab[0].astype(BF16))
    ret = _retention(proj[0], row(ret_gn[0]), b, s)
    outs, lses = zip(*[_dilated_pattern(dq, rel_bias, wdw, dil)
                       for dq, (wdw, dil) in zip(proj[1:], DIL_PATTERNS)])
    x3 = _mix0(x, ret, outs, lses, w_out_ab[0].astype(BF16))
    x3 = _ffn(x3, row(norm_ffn[0]), w_ffn_in[0].astype(BF16), conv_w[0],
              row(conv_b[0]), w_ffn_out[0].astype(BF16), row(norm_final), False)

    nqk = DIFF_HEADS * 2 * DIFF_HEAD_DIM
    wc = w_in_c[0].astype(BF16)
    q, k, vt = _proj1(x3, row(norm_mix[1]), wc[:, :nqk].T, wc[:, nqk:2 * nqk], wc[:, 2 * nqk:].T)
    lam4 = jnp.stack([lam_q1[0], lam_k1[0], lam_q2[0], lam_k2[0]]).astype(F32)
    a = _diff_attention(q, k, vt, rel_bias, lam4, row(diff_subln[0]), 1)
    return _ffn(x3, row(norm_ffn[1]), w_ffn_in[1].astype(BF16), conv_w[1], row(conv_b[1]),
                w_ffn_out[1].astype(BF16), row(norm_final), True,
                mixer=(a, w_out_c[0].astype(BF16)))
```

```python
import functools
import math

import jax
import jax.numpy as jnp
from jax import lax
from jax.experimental import pallas as pl
from jax.experimental.pallas import tpu as pltpu

F32 = jnp.float32
BF16 = jnp.bfloat16

EPS = 1e-6
MASKED = -1e30

RET_HEADS = 4
RET_DIM = 128
RET_CHUNK = 128
ROPE_BASE = 10000.0
DIL_HEADS = 8
DIL_HEAD_DIM = 64
DIL_PATTERNS = ((128, 1), (512, 4), (2048, 16))
DIL_BLOCK = 128
DIFF_HEADS = 8
DIFF_HEAD_DIM = 64
REL_BUCKETS = 32
REL_MAX_DISTANCE = 2048
D_FF = 2816
CONV_WIDTH = 3
DIL_W = DIL_HEADS * DIL_HEAD_DIM
RET_W = 4 * RET_HEADS * RET_DIM
DIL_DILATIONS = tuple(d for _, d in DIL_PATTERNS)

LANES = 128
SUBLANES = 8
VMEM_BYTES_V7X = 64 * 1024 * 1024
VMEM_LIMIT = VMEM_BYTES_V7X - 8 * 1024 * 1024

PROJ_TM = 512
PROJ_TN = 512
RET_ROWS = 1024
DIL_QBLOCKS = 2
FFN_TM = 1024
FFN_CF = 256
ATT_T = 512
ATT_KC = 512
ATT_BOUND_SLACK = 1.0 + 2.0 ** -5
ATT_MIN_DENOM = 2.0 ** -60
ATT_MAX_DENOM = 2.0 ** 100
_MAX_EXACT = REL_BUCKETS // 2
BIAS_FLAT_FROM = 8 + math.ceil(_MAX_EXACT * (REL_MAX_DISTANCE / _MAX_EXACT)
                               ** ((REL_BUCKETS - 1 - _MAX_EXACT) / (REL_BUCKETS - _MAX_EXACT)))
ATT_NEAR = -(-(BIAS_FLAT_FROM + ATT_T - 1) // ATT_T)
LOG2E = math.log2(math.e)
ATT_QSCALE = DIFF_HEAD_DIM ** -0.5 * LOG2E


def _params(sem):
    return pltpu.CompilerParams(dimension_semantics=sem, vmem_limit_bytes=VMEM_LIMIT)


def _rms(x, g):
    return x * lax.rsqrt(jnp.mean(x * x, axis=-1, keepdims=True) + EPS) * g


def _dot(a, b):
    return jnp.dot(a, b, preferred_element_type=F32)


def _dot_nt(a, b):
    return lax.dot_general(a, b, (((1,), (1,)), ((), ())), preferred_element_type=F32)


def _dot_tn(a, b):
    return lax.dot_general(a, b, (((0,), (0,)), ((), ())), preferred_element_type=F32)


def _proj0_kernel(x_ref, g_ref, w_ref, a_ref, *rest):
    d_refs, dbuf = rest[:-1], rest[-1]
    tm = x_ref.shape[1]
    h = _rms(x_ref[0], g_ref[...]).astype(BF16)
    na = a_ref.shape[-1]
    nd = w_ref.shape[1] - na
    for c in range(0, nd, PROJ_TN):
        res = _dot(h, w_ref[:, na + c:na + c + PROJ_TN])
        for s in range(c // LANES, (c + PROJ_TN) // LANES):
            cols = slice(s * LANES, (s + 1) * LANES)
            dbuf[0, s] = res[:, (s * LANES - c):(s * LANES - c) + LANES]
            src, d_prev = 0, 1
            for dref, d in zip(d_refs, DIL_DILATIONS):
                if d == 1:
                    dref[0, 0, :, cols] = dbuf[0, s].astype(BF16)
                    continue
                step, rows = d // d_prev, tm // d
                dst = 1 - src
                for rp in range(d_prev):
                    for q in range(step):
                        piece = dbuf[src, s, pl.ds(rp * (tm // d_prev) + q, rows, stride=step), :]
                        r = rp + q * d_prev
                        if d != DIL_DILATIONS[-1]:
                            dbuf[dst, s, r * rows:(r + 1) * rows, :] = piece
                        dref[0, r, :, cols] = piece.astype(BF16)
                src, d_prev = dst, d
    for c in range(0, na, PROJ_TN):
        a_ref[0, :, c:c + PROJ_TN] = _dot(h, w_ref[:, c:c + PROJ_TN]).astype(BF16)


def _proj0(x3, g, w):
    b, s, dm = x3.shape
    n = w.shape[1]
    nd = n - RET_W
    tm = PROJ_TM
    d_specs = [pl.BlockSpec((1, d, tm // d, nd), lambda bi, i: (bi, 0, i, 0)) for d in DIL_DILATIONS]
    d_shapes = [jax.ShapeDtypeStruct((b, d, s // d, nd), BF16) for d in DIL_DILATIONS]
    return pl.pallas_call(
        _proj0_kernel,
        grid=(b, s // tm),
        in_specs=[pl.BlockSpec((1, tm, dm), lambda bi, i: (bi, i, 0)),
                  pl.BlockSpec((1, dm), lambda bi, i: (0, 0)),
                  pl.BlockSpec((dm, n), lambda bi, i: (0, 0))],
        out_specs=[pl.BlockSpec((1, tm, RET_W), lambda bi, i: (bi, i, 0))] + d_specs,
        out_shape=[jax.ShapeDtypeStruct((b, s, RET_W), BF16)] + d_shapes,
        scratch_shapes=[pltpu.VMEM((2, nd // LANES, tm, LANES), F32)],
        compiler_params=_params(("arbitrary", "arbitrary")),
        name="proj0",
    )(x3, g, w)


def _ret_kernel(q_ref, k_ref, v_ref, gate_ref, cos_ref, sin_ref, tab_ref, gn_ref, o_ref,
                state_ref, qr_ref, kr_ref, qx_ref, kv_ref, st_ref):
    @pl.when(pl.program_id(2) == 0)
    def _():
        state_ref[...] = jnp.zeros_like(state_ref)

    decay = tab_ref[0, 0]
    xi = tab_ref[0, 1]
    zeta = tab_ref[0, 2]
    g_chunk = tab_ref[0, 3]
    gain = gn_ref[...]
    half = RET_DIM // 2
    n_chunks = q_ref.shape[1] // RET_CHUNK
    for c in range(n_chunks):
        rows = pl.ds(c * RET_CHUNK, RET_CHUNK)
        cos = cos_ref[rows, :]
        sin = sin_ref[rows, :]
        q = q_ref[0, rows, :].astype(F32)
        k = k_ref[0, rows, :].astype(F32)
        q = q * cos + pltpu.roll(q, half, 1) * sin
        k = k * cos + pltpu.roll(k, half, 1) * sin
        qr_ref[c] = q.astype(BF16)
        kr_ref[c] = k.astype(BF16)
        qx_ref[c] = (q * xi).astype(BF16)
        kv_ref[c] = _dot_tn((k * zeta).astype(BF16), v_ref[0, rows, :])
    state = state_ref[...]
    for c in range(n_chunks):
        st_ref[c] = state.astype(BF16)
        state = g_chunk * state + kv_ref[c]
    state_ref[...] = state
    for c in range(n_chunks):
        rows = pl.ds(c * RET_CHUNK, RET_CHUNK)
        v = v_ref[0, rows, :]
        intra = _dot_nt(qr_ref[c], kr_ref[c]) * decay
        y = _dot(intra.astype(BF16), v) + _dot(qx_ref[c], st_ref[c])
        mu = jnp.mean(y, axis=-1, keepdims=True)
        yc = y - mu
        var = jnp.mean(yc * yc, axis=-1, keepdims=True)
        yn = yc * lax.rsqrt(var + EPS) * gain
        gate = gate_ref[0, rows, :].astype(F32)
        o_ref[0, rows, :] = (gate * jax.nn.sigmoid(gate) * yn).astype(o_ref.dtype)


def _retention_tables(s):
    c, e = RET_CHUNK, RET_DIM
    inv = ROPE_BASE ** (-jnp.arange(0, e, 2, dtype=F32) / e)
    ang = jnp.arange(s, dtype=F32)[:, None] * inv[None, :]
    cos = jnp.concatenate([jnp.cos(ang), jnp.cos(ang)], axis=-1)
    sin = jnp.concatenate([-jnp.sin(ang), jnp.sin(ang)], axis=-1)
    log_g = jnp.log1p(-jnp.exp2(-5.0 - jnp.arange(RET_HEADS, dtype=F32)))
    pos = jnp.arange(c, dtype=F32)
    rel = pos[:, None] - pos[None, :]
    causal = rel >= 0
    scale = e ** -0.5
    decay = jnp.where(causal, jnp.exp(jnp.where(causal, rel, 0.0)[None] * log_g[:, None, None]), 0.0)
    zeta = jnp.exp((c - 1 - pos)[None, :] * log_g[:, None])
    xi = jnp.exp((pos + 1)[None, :] * log_g[:, None])
    g_chunk = jnp.exp(c * log_g)
    tab = jnp.stack([
        decay * scale,
        jnp.broadcast_to(xi[:, :, None], (RET_HEADS, c, e)),
        jnp.broadcast_to(zeta[:, :, None] * scale, (RET_HEADS, c, e)),
        jnp.broadcast_to(g_chunk[:, None, None], (RET_HEADS, e, e)),
    ], axis=1)
    return cos, sin, tab


def _retention(p0, gn, b, s):
    cos, sin, tab = _retention_tables(s)
    e = RET_DIM
    nc = RET_ROWS // RET_CHUNK
    col = lambda off: (lambda bi, h, t: (bi, t, off + h))
    return pl.pallas_call(
        _ret_kernel,
        grid=(b, RET_HEADS, s // RET_ROWS),
        in_specs=[pl.BlockSpec((1, RET_ROWS, e), col(0)),
                  pl.BlockSpec((1, RET_ROWS, e), col(RET_HEADS)),
                  pl.BlockSpec((1, RET_ROWS, e), col(2 * RET_HEADS)),
                  pl.BlockSpec((1, RET_ROWS, e), col(3 * RET_HEADS)),
                  pl.BlockSpec((RET_ROWS, e), lambda bi, h, t: (t, 0)),
                  pl.BlockSpec((RET_ROWS, e), lambda bi, h, t: (t, 0)),
                  pl.BlockSpec((1, 4, e, e), lambda bi, h, t: (h, 0, 0, 0)),
                  pl.BlockSpec((1, e), lambda bi, h, t: (0, h))],
        out_specs=pl.BlockSpec((1, RET_ROWS, e), lambda bi, h, t: (bi, t, h)),
        out_shape=jax.ShapeDtypeStruct((b, s, RET_HEADS * e), BF16),
        scratch_shapes=[pltpu.VMEM((e, e), F32),
                        pltpu.VMEM((nc, RET_CHUNK, e), BF16),
                        pltpu.VMEM((nc, RET_CHUNK, e), BF16),
                        pltpu.VMEM((nc, RET_CHUNK, e), BF16),
                        pltpu.VMEM((nc, e, e), F32),
                        pltpu.VMEM((nc, e, e), BF16)],
        compiler_params=_params(("arbitrary", "arbitrary", "arbitrary")),
        name="retention",
    )(p0, p0, p0, p0, cos, sin, tab, gn)


def _t5_bucket(dist):
    max_exact = REL_BUCKETS // 2
    d = jnp.maximum(dist.astype(F32), 1.0)
    large = max_exact + (jnp.log(d / max_exact) / math.log(REL_MAX_DISTANCE / max_exact)
                         * (REL_BUCKETS - max_exact))
    large = jnp.clip(large.astype(jnp.int32), max_exact, REL_BUCKETS - 1)
    return jnp.where(dist < max_exact, dist, large)


def _dil_kernel(q_ref, kp_ref, kc_ref, vp_ref, vc_ref, g_ref, o_ref, lse_ref, bias_ref,
                s_ref, p_ref):
    first = pl.program_id(2) == 0
    blk, e = DIL_BLOCK, DIL_HEAD_DIM

    @pl.when(jnp.logical_and(jnp.logical_and(pl.program_id(0) == 0, pl.program_id(1) == 0), first))
    def _():
        for h in range(DIL_HEADS):
            gb = jnp.broadcast_to(g_ref[h:h + 1, :], (blk, 4 * blk))
            bias_ref[h] = pltpu.roll(gb, 0, 1, stride=1, stride_axis=0)[:, :2 * blk]

    q_all = q_ref[0, 0] * (e ** -0.5)
    k_all = jnp.concatenate([kp_ref[0, 0], kc_ref[0, 0]], axis=0)
    v_all = jnp.concatenate([vp_ref[0, 0], vc_ref[0, 0]], axis=0)
    in_prev = lax.broadcasted_iota(jnp.int32, (blk, 2 * blk), 1) < blk
    no_prev = jnp.logical_and(first, in_prev)
    lane = lax.broadcasted_iota(jnp.int32, (blk, LANES), 1)
    for qb in range(q_all.shape[0] // blk):
        q = q_all[qb * blk:(qb + 1) * blk]
        kcat = k_all[qb * blk:(qb + 2) * blk]
        vcat = v_all[qb * blk:(qb + 2) * blk]
        lse_all = jnp.zeros((blk, LANES), F32)
        for h in range(DIL_HEADS):
            cols = slice(h * e, (h + 1) * e)
            s = _dot_nt(q[:, cols], kcat[:, cols]) + bias_ref[h]
            if qb == 0:
                s = jnp.where(no_prev, MASKED, s)
            s_ref[qb, h] = s
        inv_l = []
        for h in range(DIL_HEADS):
            s = s_ref[qb, h]
            m = jnp.max(s, axis=-1, keepdims=True)
            p = jnp.exp(s - m)
            l = jnp.sum(p, axis=-1, keepdims=True)
            p_ref[qb, h] = p.astype(BF16)
            inv_l.append(1.0 / l)
            lse_all = jnp.where(lane == h, m + jnp.log(l), lse_all)
        for h in range(DIL_HEADS):
            cols = slice(h * e, (h + 1) * e)
            o_ref[0, 0, qb * blk:(qb + 1) * blk, cols] = (
                _dot(p_ref[qb, h], vcat[:, cols]) * inv_l[h]).astype(o_ref.dtype)
        lse_ref[0, 0, qb * blk:(qb + 1) * blk, :] = lse_all


def _dil_bias_vector(rel_bias, window, dil):
    blk = DIL_BLOCK
    span = window // dil
    r = blk - jnp.arange(4 * blk)
    band = (r >= 0) & (r <= span)
    bias = rel_bias[_t5_bucket(jnp.maximum(r, 0) * dil)].astype(F32)
    return jnp.where(band[:, None], bias, MASKED).T


def _dilated_pattern(dq, rel_bias, window, dil):
    b, d, sd, _ = dq.shape
    blk = DIL_BLOCK
    nq = DIL_QBLOCKS
    rows = nq * blk
    assert d == dil and window // dil == blk and sd % rows == 0
    cur = lambda c: (lambda bi, r, n: (bi, r, n, c))
    prev = lambda c: (lambda bi, r, n: (bi, r, jnp.maximum(n * nq - 1, 0), c))
    spec = lambda im: pl.BlockSpec((1, 1, rows, DIL_W), im)
    pspec = lambda im: pl.BlockSpec((1, 1, blk, DIL_W), im)
    return pl.pallas_call(
        _dil_kernel,
        grid=(b, dil, sd // rows),
        in_specs=[spec(cur(0)), pspec(prev(1)), spec(cur(1)), pspec(prev(2)), spec(cur(2)),
                  pl.BlockSpec((DIL_HEADS, 4 * blk), lambda bi, r, n: (0, 0))],
        out_specs=[pl.BlockSpec((1, 1, rows, DIL_W), lambda bi, r, n: (bi, r, n, 0)),
                   pl.BlockSpec((1, 1, rows, LANES), lambda bi, r, n: (bi, r, n, 0))],
        out_shape=[jax.ShapeDtypeStruct((b, dil, sd, DIL_W), BF16),
                   jax.ShapeDtypeStruct((b, dil, sd, LANES), F32)],
        scratch_shapes=[pltpu.VMEM((DIL_HEADS, blk, 2 * blk), F32),
                        pltpu.VMEM((nq, DIL_HEADS, blk, 2 * blk), F32),
                        pltpu.VMEM((nq, DIL_HEADS, blk, 2 * blk), BF16)],
        compiler_params=_params(("arbitrary", "arbitrary", "arbitrary")),
        name=f"dilated_d{dil}",
    )(dq, dq, dq, dq, dq, _dil_bias_vector(rel_bias, window, dil))


def _mix0_kernel(x_ref, ret_ref, *rest):
    np_ = len(DIL_DILATIONS)
    o_refs, l_refs = rest[:np_], rest[np_:2 * np_]
    w_ref, out_ref, obuf, lbuf, dbuf = rest[2 * np_:]
    tm = x_ref.shape[1]
    nslab = DIL_W // LANES
    for pi, d in enumerate(DIL_DILATIONS):
        rows = tm // d
        for r in range(d):
            dst = pl.ds(r, rows, stride=d) if d > 1 else pl.ds(0, tm)
            lbuf[pi, dst, :] = l_refs[pi][0, r]
            for s in range(nslab):
                obuf[pi, s, dst, :] = o_refs[pi][0, r, :, s * LANES:(s + 1) * LANES].astype(F32)
    lses = [lbuf[pi] for pi in range(np_)]
    m = functools.reduce(jnp.maximum, lses)
    es = [jnp.exp(l - m) for l in lses]
    inv = 1.0 / functools.reduce(lambda a, b: a + b, es)
    row = lax.broadcasted_iota(jnp.int32, (LANES, DIL_W), 0)
    col = lax.broadcasted_iota(jnp.int32, (LANES, DIL_W), 1)
    spread = (col // DIL_HEAD_DIM == row).astype(BF16)
    spread2 = jnp.concatenate([spread, spread], axis=0)
    wide = []
    for ex in es:
        wgt = ex * inv
        hi = wgt.astype(BF16)
        lo = (wgt - hi.astype(F32)).astype(BF16)
        wide.append(_dot(jnp.concatenate([hi, lo], axis=1), spread2))
    for s in range(nslab):
        cols = slice(s * LANES, (s + 1) * LANES)
        acc = wide[0][:, cols] * obuf[0, s]
        for pi in range(1, np_):
            acc = acc + wide[pi][:, cols] * obuf[pi, s]
        dbuf[:, cols] = acc.astype(BF16)
    na = ret_ref.shape[-1]
    y = _dot(ret_ref[0], w_ref[:na, :]) + _dot(dbuf[...], w_ref[na:, :])
    out_ref[0] = x_ref[0] + y


def _mix0(x3, ret, outs, lses, w):
    b, s, dm = x3.shape
    na = ret.shape[-1]
    tm = PROJ_TM
    o_specs = [pl.BlockSpec((1, d, tm // d, DIL_W), lambda bi, i: (bi, 0, i, 0)) for d in DIL_DILATIONS]
    l_specs = [pl.BlockSpec((1, d, tm // d, LANES), lambda bi, i: (bi, 0, i, 0)) for d in DIL_DILATIONS]
    np_ = len(DIL_DILATIONS)
    return pl.pallas_call(
        _mix0_kernel,
        grid=(b, s // tm),
        in_specs=[pl.BlockSpec((1, tm, dm), lambda bi, i: (bi, i, 0)),
                  pl.BlockSpec((1, tm, na), lambda bi, i: (bi, i, 0))] + o_specs + l_specs
                 + [pl.BlockSpec((na + DIL_W, dm), lambda bi, i: (0, 0))],
        out_specs=pl.BlockSpec((1, tm, dm), lambda bi, i: (bi, i, 0)),
        out_shape=jax.ShapeDtypeStruct((b, s, dm), F32),
        scratch_shapes=[pltpu.VMEM((np_, DIL_W // LANES, tm, LANES), F32),
                        pltpu.VMEM((np_, tm, LANES), F32),
                        pltpu.VMEM((tm, DIL_W), BF16)],
        compiler_params=_params(("arbitrary", "arbitrary")),
        name="mix0",
    )(x3, ret, *outs, *lses, w)


def _ffn_kernel(x_ref, *rest, final_norm, mixer_proj):
    if mixer_proj:
        a_ref, wmix_ref = rest[:2]
        rest = rest[2:]
    g_ref, win_ref, cw_ref, cb_ref, wout_ref, fg_ref, o_ref, ubuf, carry, gbuf = rest
    tm = x_ref.shape[1]
    halo = SUBLANES

    @pl.when(pl.program_id(1) == 0)
    def _():
        carry[...] = jnp.zeros_like(carry)

    x = x_ref[0]
    if mixer_proj:
        x = x + _dot(a_ref[0], wmix_ref[...])
    h = _rms(x, g_ref[...]).astype(BF16)

    def conv(part, col):
        cols = slice(col, col + FFN_CF)
        u = _dot(h, win_ref[:, cols])
        ubuf[part, 0:halo, :] = carry[:, cols]
        ubuf[part, halo:halo + tm, :] = u
        carry[:, cols] = u[tm - halo:, :]
        u1 = ubuf[part, halo - 1:halo - 1 + tm, :]
        u2 = ubuf[part, halo - 2:halo - 2 + tm, :]
        return (u * cw_ref[2:3, cols] + u1 * cw_ref[1:2, cols] + u2 * cw_ref[0:1, cols]
                + cb_ref[:, cols])

    for c in range(D_FF // FFN_CF):
        gate = conv(0, c * FFN_CF)
        up = conv(1, D_FF + c * FFN_CF)
        act = 0.5 * gate * (1.0 + lax.erf(gate * (2.0 ** -0.5)))
        gbuf[:, c * FFN_CF:(c + 1) * FFN_CF] = (act * up).astype(BF16)

    y = x + _dot(gbuf[...], wout_ref[...])
    if final_norm:
        y = _rms(y, fg_ref[...])
    o_ref[0] = y


def _ffn(x3, g, w_in, conv_w, conv_b, w_out, final_gain, final_norm, mixer=None):
    b, s, d = x3.shape
    f2 = w_in.shape[1]
    whole = pl.BlockSpec(memory_space=pltpu.VMEM)
    tile = lambda n: pl.BlockSpec((1, FFN_TM, n), lambda bi, t: (bi, t, 0))
    mix_specs = [tile(mixer[0].shape[-1]), whole] if mixer is not None else []
    mix_args = list(mixer) if mixer is not None else []
    return pl.pallas_call(
        functools.partial(_ffn_kernel, final_norm=final_norm, mixer_proj=mixer is not None),
        grid=(b, s // FFN_TM),
        in_specs=[tile(d)] + mix_specs + [whole, whole, whole, whole, whole, whole],
        out_specs=pl.BlockSpec((1, FFN_TM, d), lambda bi, t: (bi, t, 0)),
        out_shape=jax.ShapeDtypeStruct((b, s, d), F32),
        scratch_shapes=[pltpu.VMEM((2, FFN_TM + SUBLANES, FFN_CF), F32),
                        pltpu.VMEM((SUBLANES, f2), F32),
                        pltpu.VMEM((FFN_TM, D_FF), BF16)],
        compiler_params=_params(("arbitrary", "arbitrary")),
        name="ffn_final" if final_norm else "ffn",
    )(x3, *mix_args, g, w_in, conv_w, conv_b, w_out, final_gain)


def _proj1_kernel(x_ref, g_ref, wq_ref, wk_ref, wvt_ref, q_ref, k_ref, vt_ref):
    h = _rms(x_ref[0], g_ref[...]).astype(BF16)
    n = k_ref.shape[-1]
    for c in range(0, n, PROJ_TN):
        cols = slice(c, c + PROJ_TN)
        q_ref[0, 0, cols, :] = (_dot_nt(wq_ref[cols, :], h) * ATT_QSCALE).astype(BF16)
        k_ref[0, :, cols] = _dot(h, wk_ref[:, cols]).astype(BF16)
        vt_ref[0, 0, cols, :] = _dot_nt(wvt_ref[cols, :], h).astype(BF16)


def _proj1(x3, g, wq, wk, wvt):
    b, s, d = x3.shape
    n = wq.shape[1]
    t = ATT_T
    wspec = pl.BlockSpec((d, n), lambda bi, i: (0, 0))
    return pl.pallas_call(
        _proj1_kernel,
        grid=(b, s // t),
        in_specs=[pl.BlockSpec((1, t, d), lambda bi, i: (bi, i, 0)),
                  pl.BlockSpec((1, d), lambda bi, i: (0, 0)),
                  pl.BlockSpec((n, d), lambda bi, i: (0, 0)), wspec,
                  pl.BlockSpec((n, d), lambda bi, i: (0, 0))],
        out_specs=[pl.BlockSpec((1, 1, n, t), lambda bi, i: (bi, i, 0, 0)),
                   pl.BlockSpec((1, t, n), lambda bi, i: (bi, i, 0)),
                   pl.BlockSpec((1, 1, n, t), lambda bi, i: (bi, i, 0, 0))],
        out_shape=[jax.ShapeDtypeStruct((b, s // t, n, t), BF16),
                   jax.ShapeDtypeStruct((b, s, n), BF16),
                   jax.ShapeDtypeStruct((b, s // t, n, t), BF16)],
        compiler_params=_params(("arbitrary", "arbitrary")),
        name="proj1",
    )(x3, g, wq, wk, wvt)


def _diff_kernel(q_ref, k_ref, vt_ref, f_ref, lam_ref, sub_ref, o_ref,
                 qm_ref, km_ref, m_ref, l_ref, l8_ref, acc_ref, nb_ref, p_ref, *, lam_init):
    i = pl.program_id(2)
    t = ATT_T
    e = DIFF_HEAD_DIM
    half = lax.broadcasted_iota(jnp.int32, (SUBLANES, 2 * e), 1) // e
    sel = (lax.broadcasted_iota(jnp.int32, (SUBLANES, 2 * e), 0) == half).astype(BF16)

    @pl.when(i == 0)
    def _():
        for o in range(ATT_NEAR):
            g = jnp.broadcast_to(f_ref[0, 0:1, o * t:(o + 2) * t], (t, 2 * t))
            nb_ref[o] = pltpu.roll(g, 0, 1, stride=1, stride_axis=0)[:, t:]
        nb_ref[ATT_NEAR] = jnp.zeros((t, t), F32)
        kmax = jnp.zeros((SUBLANES, t), F32)
        for c in range(k_ref.shape[1] // t):
            kk = k_ref[0, c * t:(c + 1) * t, :].astype(F32)
            kmax = jnp.maximum(kmax, _dot_nt(sel, (kk * kk).astype(BF16)))
        for a in range(2):
            km_ref[a] = jnp.broadcast_to(jnp.max(kmax[a:a + 1, :], axis=1, keepdims=True), (1, t))

    qt = q_ref[0, 0]
    row = lax.broadcasted_iota(jnp.int32, qt.shape, 0)
    zero = jnp.zeros_like(qt)
    qm_ref[0] = jnp.where(row < e, qt, zero)
    qm_ref[1] = jnp.where(row >= e, qt, zero)
    qf = qt.astype(F32)
    qq = qf * qf
    qn2 = [jnp.sum(qq[a * e:(a + 1) * e], axis=0, keepdims=True) for a in range(2)]
    bias_max = f_ref[0, 1:2, 0:t]
    n_far = jnp.maximum(i - (ATT_NEAR - 1), 0)

    l8_ref[...] = jnp.zeros_like(l8_ref)
    acc_ref[...] = jnp.zeros_like(acc_ref)
    shift = [jnp.sqrt(qn2[a] * km_ref[a]) * ATT_BOUND_SLACK + bias_max for a in range(2)]

    kc = ATT_KC

    def fast_tiles(j0, ntiles, biased):
        for a in range(2):
            pv = None
            l8 = None
            for tt in range(ntiles):
                j = j0 + tt
                for c in range(0, t, kc):
                    kb = k_ref[0, pl.ds(pl.multiple_of(j * t + c, kc), kc), :]
                    s = _dot(kb, qm_ref[a])
                    if biased:
                        s = s + nb_ref[jnp.minimum(i - j, ATT_NEAR), c:c + kc, :]
                    p = jnp.exp2(s - shift[a])
                    part = jnp.sum(p.reshape(kc // SUBLANES, SUBLANES, t), axis=0)
                    d = _dot(vt_ref[0, j, :, c:c + kc], p.astype(BF16))
                    l8 = part if l8 is None else l8 + part
                    pv = d if pv is None else pv + d
            l8_ref[a] += l8
            acc_ref[a] += pv

    def probs(u, biased):
        slot = u % 2
        for a in range(2):
            l8 = None
            for tt in range(2):
                j = 2 * u + tt
                kb = k_ref[0, pl.ds(pl.multiple_of(j * t, t), t), :]
                s = _dot(kb, qm_ref[a])
                if biased:
                    s = s + nb_ref[jnp.minimum(i - j, ATT_NEAR)]
                p = jnp.exp2(s - shift[a])
                part = jnp.sum(p.reshape(t // SUBLANES, SUBLANES, t), axis=0)
                l8 = part if l8 is None else l8 + part
                p_ref[slot, a, tt] = p.astype(BF16)
            l8_ref[a] += l8

    def values(u):
        slot = u % 2
        for a in range(2):
            acc_ref[a] += (_dot(vt_ref[0, 2 * u], p_ref[slot, a, 0])
                           + _dot(vt_ref[0, 2 * u + 1], p_ref[slot, a, 1]))

    def far_step(u, carry):
        values(u - 1)
        probs(u, False)
        return carry

    def near_step(u, carry):
        values(u - 1)
        probs(u, True)
        return carry

    n = i + 1
    pairs = n // 2
    far_pairs = n_far // 2

    @pl.when(far_pairs > 0)
    def _():
        probs(0, False)

    @pl.when(jnp.logical_and(far_pairs == 0, pairs > 0))
    def _():
        probs(0, True)

    lax.fori_loop(1, far_pairs, far_step, 0)
    lax.fori_loop(jnp.maximum(far_pairs, 1), pairs, near_step, 0)

    odd = n % 2 == 1

    @pl.when(jnp.logical_and(pairs > 0, jnp.logical_not(odd)))
    def _():
        values(pairs - 1)

    @pl.when(jnp.logical_and(pairs > 0, odd))
    def _():
        values(pairs - 1)
        fast_tiles(n - 1, 1, True)

    @pl.when(pairs == 0)
    def _():
        fast_tiles(0, 1, True)
    def finalize():
        lam = (jnp.exp(jnp.sum(lam_ref[0:1, :] * lam_ref[1:2, :], axis=-1, keepdims=True))
               - jnp.exp(jnp.sum(lam_ref[2:3, :] * lam_ref[3:4, :], axis=-1, keepdims=True))
               + lam_init)
        ot = acc_ref[0] * (1.0 / l_ref[0]) - acc_ref[1] * (lam / l_ref[1])
        ot = ot * lax.rsqrt(jnp.mean(ot * ot, axis=0, keepdims=True) + EPS)
        o_ref[0] = (ot.T * sub_ref[...] * (1.0 - lam_init)).astype(o_ref.dtype)

    for a in range(2):
        l_ref[a] = jnp.sum(l8_ref[a], axis=0, keepdims=True)
    l_min = jnp.min(jnp.minimum(l_ref[0], l_ref[1]))
    l_max = jnp.max(jnp.maximum(l_ref[0], l_ref[1]))
    in_range = jnp.logical_and(l_min >= ATT_MIN_DENOM, l_max <= ATT_MAX_DENOM)
    finalize()

    @pl.when(jnp.logical_not(in_range))
    def _():
        m_ref[...] = jnp.full_like(m_ref, MASKED)
        l_ref[...] = jnp.zeros_like(l_ref)
        acc_ref[...] = jnp.zeros_like(acc_ref)

        def online_tile(j, bias):
            kb = k_ref[0, pl.ds(pl.multiple_of(j * t, t), t), :]
            vtb = vt_ref[0, j]
            for a in range(2):
                s = _dot(kb, qm_ref[a])
                if bias is not None:
                    s = s + bias
                m_old = m_ref[a]
                m_new = jnp.maximum(m_old, jnp.max(s, axis=0, keepdims=True))
                alpha = jnp.exp2(m_old - m_new)
                p = jnp.exp2(s - m_new)
                l_ref[a] = alpha * l_ref[a] + jnp.sum(p, axis=0, keepdims=True)
                acc_ref[a] = alpha * acc_ref[a] + _dot(vtb, p.astype(BF16))
                m_ref[a] = m_new

        def online_far(j, carry):
            online_tile(j, None)
            return carry

        def online_near(j, carry):
            online_tile(j, nb_ref[i - j])
            return carry

        lax.fori_loop(0, n_far, online_far, 0)
        lax.fori_loop(n_far, i + 1, online_near, 0)
        finalize()


def _diff_bias(rel_bias, s):
    t = ATT_T
    n = ATT_NEAR * t
    assert n >= BIAS_FLAT_FROM + t - 1 and s >= n
    dist_bias = rel_bias[_t5_bucket(jnp.arange(n))].astype(F32)
    far = rel_bias[REL_BUCKETS - 1].astype(F32)
    near = (dist_bias - far) * LOG2E
    f = jnp.concatenate([jnp.full((t, far.shape[0]), MASKED, F32), near])
    top = jnp.broadcast_to(jnp.maximum(jnp.max(near, axis=0), 0.0), f.shape)
    return jnp.stack([f.T, top.T], axis=1)


def _diff_attention(q, k, vt, rel_bias, lam4, subln, layer_idx):
    b, s, n = k.shape
    t = ATT_T
    w = 2 * DIFF_HEAD_DIM
    lam_init = 0.8 - 0.6 * math.exp(-0.3 * layer_idx)
    return pl.pallas_call(
        functools.partial(_diff_kernel, lam_init=lam_init),
        grid=(b, DIFF_HEADS, s // t),
        in_specs=[pl.BlockSpec((1, 1, w, t), lambda bi, h, i: (bi, i, h, 0)),
                  pl.BlockSpec((1, s, w), lambda bi, h, i: (bi, 0, h)),
                  pl.BlockSpec((1, s // t, w, t), lambda bi, h, i: (bi, 0, h, 0)),
                  pl.BlockSpec((1, 2, (ATT_NEAR + 1) * t), lambda bi, h, i: (h, 0, 0)),
                  pl.BlockSpec((4, DIFF_HEAD_DIM), lambda bi, h, i: (0, 0)),
                  pl.BlockSpec((1, w), lambda bi, h, i: (0, 0))],
        out_specs=pl.BlockSpec((1, t, w), lambda bi, h, i: (bi, i, h)),
        out_shape=jax.ShapeDtypeStruct((b, s, n), BF16),
        scratch_shapes=[pltpu.VMEM((2, w, t), BF16),
                        pltpu.VMEM((2, 1, t), F32),
                        pltpu.VMEM((2, 1, t), F32),
                        pltpu.VMEM((2, 1, t), F32),
                        pltpu.VMEM((2, SUBLANES, t), F32),
                        pltpu.VMEM((2, w, t), F32),
                        pltpu.VMEM((ATT_NEAR + 1, t, t), F32),
                        pltpu.VMEM((2, 2, 2, t, t), BF16)],
        compiler_params=_params(("arbitrary", "arbitrary", "arbitrary")),
        name="diff_attention",
    )(q, k, vt, _diff_bias(rel_bias, s), lam4, subln)


def kernel(x, rel_bias, norm_mix, norm_ffn, norm_final, w_in_ab, ret_gn, w_out_ab, w_in_c,
           lam_q1, lam_k1, lam_q2, lam_k2, diff_subln, w_out_c, w_ffn_in, conv_w, conv_b,
           w_ffn_out):
    b, s, d = x.shape
    depth = norm_mix.shape[0]
    assert depth == 2 and s % RET_ROWS == 0 and s % FFN_TM == 0 and s >= ATT_NEAR * ATT_T
    assert s % (max(DIL_DILATIONS) * DIL_BLOCK) == 0
    row = lambda v: v.reshape(1, -1)

    proj = _proj0(x, row(norm_mix[0]), w_in_ab[0].astype(BF16))
    ret = _retention(proj[0], row(ret_gn[0]), b, s)
    outs, lses = zip(*[_dilated_pattern(dq, rel_bias, wdw, dil)
                       for dq, (wdw, dil) in zip(proj[1:], DIL_PATTERNS)])
    x3 = _mix0(x, ret, outs, lses, w_out_ab[0].astype(BF16))
    x3 = _ffn(x3, row(norm_ffn[0]), w_ffn_in[0].astype(BF16), conv_w[0],
              row(conv_b[0]), w_ffn_out[0].astype(BF16), row(norm_final), False)

    nqk = DIFF_HEADS * 2 * DIFF_HEAD_DIM
    wc = w_in_c[0].astype(BF16)
    q, k, vt = _proj1(x3, row(norm_mix[1]), wc[:, :nqk].T, wc[:, nqk:2 * nqk], wc[:, 2 * nqk:].T)
    lam4 = jnp.stack([lam_q1[0], lam_k1[0], lam_q2[0], lam_k2[0]]).astype(F32)
    a = _diff_attention(q, k, vt, rel_bias, lam4, row(diff_subln[0]), 1)
    return _ffn(x3, row(norm_ffn[1]), w_ffn_in[1].astype(BF16), conv_w[1], row(conv_b[1]),
                w_ffn_out[1].astype(BF16), row(norm_final), True,
                mixer=(a, w_out_c[0].astype(BF16)))
```

```python
import functools
import math

import jax
import jax.numpy as jnp
from jax import lax
from jax.experimental import pallas as pl
from jax.experimental.pallas import tpu as pltpu

F32 = jnp.float32
BF16 = jnp.bfloat16

EPS = 1e-6
MASKED = -1e30

RET_HEADS = 4
RET_DIM = 128
RET_CHUNK = 128
ROPE_BASE = 10000.0
DIL_HEADS = 8
DIL_HEAD_DIM = 64
DIL_PATTERNS = ((128, 1), (512, 4), (2048, 16))
DIL_BLOCK = 128
DIFF_HEADS = 8
DIFF_HEAD_DIM = 64
REL_BUCKETS = 32
REL_MAX_DISTANCE = 2048
D_FF = 2816
CONV_WIDTH = 3
DIL_W = DIL_HEADS * DIL_HEAD_DIM
RET_W = 4 * RET_HEADS * RET_DIM
DIL_DILATIONS = tuple(d for _, d in DIL_PATTERNS)

LANES = 128
SUBLANES = 8
VMEM_BYTES_V7X = 64 * 1024 * 1024
VMEM_LIMIT = VMEM_BYTES_V7X - 8 * 1024 * 1024

PROJ_TM = 512
PROJ_TN = 512
RET_ROWS = 1024
DIL_QBLOCKS = 8
FFN_TM = 1024
FFN_CF = 256
ATT_T = 512
ATT_KC = 512
ATT_BOUND_SLACK = 1.0 + 2.0 ** -5
ATT_MIN_DENOM = 2.0 ** -60
ATT_MAX_DENOM = 2.0 ** 100
_MAX_EXACT = REL_BUCKETS // 2
BIAS_FLAT_FROM = 8 + math.ceil(_MAX_EXACT * (REL_MAX_DISTANCE / _MAX_EXACT)
                               ** ((REL_BUCKETS - 1 - _MAX_EXACT) / (REL_BUCKETS - _MAX_EXACT)))
ATT_NEAR = -(-(BIAS_FLAT_FROM + ATT_T - 1) // ATT_T)
LOG2E = math.log2(math.e)
ATT_QSCALE = DIFF_HEAD_DIM ** -0.5 * LOG2E


def _params(sem):
    return pltpu.CompilerParams(dimension_semantics=sem, vmem_limit_bytes=VMEM_LIMIT)


def _rms(x, g):
    return x * lax.rsqrt(jnp.mean(x * x, axis=-1, keepdims=True) + EPS) * g


def _dot(a, b):
    return jnp.dot(a, b, preferred_element_type=F32)


def _dot_nt(a, b):
    return lax.dot_general(a, b, (((1,), (1,)), ((), ())), preferred_element_type=F32)


def _dot_tn(a, b):
    return lax.dot_general(a, b, (((0,), (0,)), ((), ())), preferred_element_type=F32)


def _proj0_kernel(x_ref, g_ref, w_ref, a_ref, *rest):
    d_refs, dbuf = rest[:-1], rest[-1]
    tm = x_ref.shape[1]
    h = _rms(x_ref[0], g_ref[...]).astype(BF16)
    na = a_ref.shape[-1]
    nd = w_ref.shape[1] - na
    for c in range(0, nd, PROJ_TN):
        res = _dot(h, w_ref[:, na + c:na + c + PROJ_TN])
        for s in range(c // LANES, (c + PROJ_TN) // LANES):
            cols = slice(s * LANES, (s + 1) * LANES)
            dbuf[0, s] = res[:, (s * LANES - c):(s * LANES - c) + LANES]
            src, d_prev = 0, 1
            for dref, d in zip(d_refs, DIL_DILATIONS):
                if d == 1:
                    dref[0, 0, :, cols] = dbuf[0, s].astype(BF16)
                    continue
                step, rows = d // d_prev, tm // d
                dst = 1 - src
                for rp in range(d_prev):
                    for q in range(step):
                        piece = dbuf[src, s, pl.ds(rp * (tm // d_prev) + q, rows, stride=step), :]
                        r = rp + q * d_prev
                        if d != DIL_DILATIONS[-1]:
                            dbuf[dst, s, r * rows:(r + 1) * rows, :] = piece
                        dref[0, r, :, cols] = piece.astype(BF16)
                src, d_prev = dst, d
    for c in range(0, na, PROJ_TN):
        a_ref[0, :, c:c + PROJ_TN] = _dot(h, w_ref[:, c:c + PROJ_TN]).astype(BF16)


def _proj0(x3, g, w):
    b, s, dm = x3.shape
    n = w.shape[1]
    nd = n - RET_W
    tm = PROJ_TM
    d_specs = [pl.BlockSpec((1, d, tm // d, nd), lambda bi, i: (bi, 0, i, 0)) for d in DIL_DILATIONS]
    d_shapes = [jax.ShapeDtypeStruct((b, d, s // d, nd), BF16) for d in DIL_DILATIONS]
    return pl.pallas_call(
        _proj0_kernel,
        grid=(b, s // tm),
        in_specs=[pl.BlockSpec((1, tm, dm), lambda bi, i: (bi, i, 0)),
                  pl.BlockSpec((1, dm), lambda bi, i: (0, 0)),
                  pl.BlockSpec((dm, n), lambda bi, i: (0, 0))],
        out_specs=[pl.BlockSpec((1, tm, RET_W), lambda bi, i: (bi, i, 0))] + d_specs,
        out_shape=[jax.ShapeDtypeStruct((b, s, RET_W), BF16)] + d_shapes,
        scratch_shapes=[pltpu.VMEM((2, nd // LANES, tm, LANES), F32)],
        compiler_params=_params(("arbitrary", "arbitrary")),
        name="proj0",
    )(x3, g, w)


def _ret_kernel(q_ref, k_ref, v_ref, gate_ref, cos_ref, sin_ref, tab_ref, gn_ref, o_ref,
                state_ref, qr_ref, kr_ref, qx_ref, kv_ref, st_ref):
    @pl.when(pl.program_id(2) == 0)
    def _():
        state_ref[...] = jnp.zeros_like(state_ref)

    decay = tab_ref[0, 0]
    xi = tab_ref[0, 1]
    zeta = tab_ref[0, 2]
    g_chunk = tab_ref[0, 3]
    gain = gn_ref[...]
    half = RET_DIM // 2
    n_chunks = q_ref.shape[1] // RET_CHUNK
    for c in range(n_chunks):
        rows = pl.ds(c * RET_CHUNK, RET_CHUNK)
        cos = cos_ref[rows, :]
        sin = sin_ref[rows, :]
        q = q_ref[0, rows, :].astype(F32)
        k = k_ref[0, rows, :].astype(F32)
        q = q * cos + pltpu.roll(q, half, 1) * sin
        k = k * cos + pltpu.roll(k, half, 1) * sin
        qr_ref[c] = q.astype(BF16)
        kr_ref[c] = k.astype(BF16)
        qx_ref[c] = (q * xi).astype(BF16)
        kv_ref[c] = _dot_tn((k * zeta).astype(BF16), v_ref[0, rows, :])
    state = state_ref[...]
    for c in range(n_chunks):
        st_ref[c] = state.astype(BF16)
        state = g_chunk * state + kv_ref[c]
    state_ref[...] = state
    for c in range(n_chunks):
        rows = pl.ds(c * RET_CHUNK, RET_CHUNK)
        v = v_ref[0, rows, :]
        intra = _dot_nt(qr_ref[c], kr_ref[c]) * decay
        y = _dot(intra.astype(BF16), v) + _dot(qx_ref[c], st_ref[c])
        mu = jnp.mean(y, axis=-1, keepdims=True)
        yc = y - mu
        var = jnp.mean(yc * yc, axis=-1, keepdims=True)
        yn = yc * lax.rsqrt(var + EPS) * gain
        gate = gate_ref[0, rows, :].astype(F32)
        o_ref[0, rows, :] = (gate * jax.nn.sigmoid(gate) * yn).astype(o_ref.dtype)


def _retention_tables(s):
    c, e = RET_CHUNK, RET_DIM
    inv = ROPE_BASE ** (-jnp.arange(0, e, 2, dtype=F32) / e)
    ang = jnp.arange(s, dtype=F32)[:, None] * inv[None, :]
    cos = jnp.concatenate([jnp.cos(ang), jnp.cos(ang)], axis=-1)
    sin = jnp.concatenate([-jnp.sin(ang), jnp.sin(ang)], axis=-1)
    log_g = jnp.log1p(-jnp.exp2(-5.0 - jnp.arange(RET_HEADS, dtype=F32)))
    pos = jnp.arange(c, dtype=F32)
    rel = pos[:, None] - pos[None, :]
    causal = rel >= 0
    scale = e ** -0.5
    decay = jnp.where(causal, jnp.exp(jnp.where(causal, rel, 0.0)[None] * log_g[:, None, None]), 0.0)
    zeta = jnp.exp((c - 1 - pos)[None, :] * log_g[:, None])
    xi = jnp.exp((pos + 1)[None, :] * log_g[:, None])
    g_chunk = jnp.exp(c * log_g)
    tab = jnp.stack([
        decay * scale,
        jnp.broadcast_to(xi[:, :, None], (RET_HEADS, c, e)),
        jnp.broadcast_to(zeta[:, :, None] * scale, (RET_HEADS, c, e)),
        jnp.broadcast_to(g_chunk[:, None, None], (RET_HEADS, e, e)),
    ], axis=1)
    return cos, sin, tab


def _retention(p0, gn, b, s):
    cos, sin, tab = _retention_tables(s)
    e = RET_DIM
    nc = RET_ROWS // RET_CHUNK
    col = lambda off: (lambda bi, h, t: (bi, t, off + h))
    return pl.pallas_call(
        _ret_kernel,
        grid=(b, RET_HEADS, s // RET_ROWS),
        in_specs=[pl.BlockSpec((1, RET_ROWS, e), col(0)),
                  pl.BlockSpec((1, RET_ROWS, e), col(RET_HEADS)),
                  pl.BlockSpec((1, RET_ROWS, e), col(2 * RET_HEADS)),
                  pl.BlockSpec((1, RET_ROWS, e), col(3 * RET_HEADS)),
                  pl.BlockSpec((RET_ROWS, e), lambda bi, h, t: (t, 0)),
                  pl.BlockSpec((RET_ROWS, e), lambda bi, h, t: (t, 0)),
                  pl.BlockSpec((1, 4, e, e), lambda bi, h, t: (h, 0, 0, 0)),
                  pl.BlockSpec((1, e), lambda bi, h, t: (0, h))],
        out_specs=pl.BlockSpec((1, RET_ROWS, e), lambda bi, h, t: (bi, t, h)),
        out_shape=jax.ShapeDtypeStruct((b, s, RET_HEADS * e), BF16),
        scratch_shapes=[pltpu.VMEM((e, e), F32),
                        pltpu.VMEM((nc, RET_CHUNK, e), BF16),
                        pltpu.VMEM((nc, RET_CHUNK, e), BF16),
                        pltpu.VMEM((nc, RET_CHUNK, e), BF16),
                        pltpu.VMEM((nc, e, e), F32),
                        pltpu.VMEM((nc, e, e), BF16)],
        compiler_params=_params(("arbitrary", "arbitrary", "arbitrary")),
        name="retention",
    )(p0, p0, p0, p0, cos, sin, tab, gn)


def _t5_bucket(dist):
    max_exact = REL_BUCKETS // 2
    d = jnp.maximum(dist.astype(F32), 1.0)
    large = max_exact + (jnp.log(d / max_exact) / math.log(REL_MAX_DISTANCE / max_exact)
                         * (REL_BUCKETS - max_exact))
    large = jnp.clip(large.astype(jnp.int32), max_exact, REL_BUCKETS - 1)
    return jnp.where(dist < max_exact, dist, large)


def _dil_kernel(q_ref, kp_ref, kc_ref, vp_ref, vc_ref, g_ref, o_ref, lse_ref, bias_ref,
                s_ref, p_ref):
    first = pl.program_id(2) == 0
    blk, e = DIL_BLOCK, DIL_HEAD_DIM

    @pl.when(jnp.logical_and(jnp.logical_and(pl.program_id(0) == 0, pl.program_id(1) == 0), first))
    def _():
        for h in range(DIL_HEADS):
            gb = jnp.broadcast_to(g_ref[h:h + 1, :], (blk, 4 * blk))
            bias_ref[h] = pltpu.roll(gb, 0, 1, stride=1, stride_axis=0)[:, :2 * blk]

    q_all = q_ref[0, 0] * (e ** -0.5)
    k_all = jnp.concatenate([kp_ref[0, 0], kc_ref[0, 0]], axis=0)
    v_all = jnp.concatenate([vp_ref[0, 0], vc_ref[0, 0]], axis=0)
    in_prev = lax.broadcasted_iota(jnp.int32, (blk, 2 * blk), 1) < blk
    no_prev = jnp.logical_and(first, in_prev)
    lane = lax.broadcasted_iota(jnp.int32, (blk, LANES), 1)
    for qb in range(q_all.shape[0] // blk):
        q = q_all[qb * blk:(qb + 1) * blk]
        kcat = k_all[qb * blk:(qb + 2) * blk]
        vcat = v_all[qb * blk:(qb + 2) * blk]
        lse_all = jnp.zeros((blk, LANES), F32)
        for h in range(DIL_HEADS):
            pair = slice((h // 2) * LANES, (h // 2 + 1) * LANES)
            qp = q[:, pair]
            mine = (lax.broadcasted_iota(jnp.int32, qp.shape, 1) // e) == h % 2
            s = _dot_nt(jnp.where(mine, qp, jnp.zeros_like(qp)), kcat[:, pair]) + bias_ref[h]
            if qb == 0:
                s = jnp.where(no_prev, MASKED, s)
            s_ref[qb, h] = s
        inv_l = []
        for h in range(DIL_HEADS):
            s = s_ref[qb, h]
            m = jnp.max(s, axis=-1, keepdims=True)
            p = jnp.exp(s - m)
            l = jnp.sum(p, axis=-1, keepdims=True)
            p_ref[qb, h] = p.astype(BF16)
            inv_l.append(1.0 / l)
            lse_all = jnp.where(lane == h, m + jnp.log(l), lse_all)
        for g in range(DIL_HEADS // 2):
            pair = slice(g * LANES, (g + 1) * LANES)
            vp = vcat[:, pair]
            upper = lax.broadcasted_iota(jnp.int32, vp.shape, 1) >= e
            zero = jnp.zeros_like(vp)
            o_pair = (_dot(p_ref[qb, 2 * g], jnp.where(upper, zero, vp))
                      + _dot(p_ref[qb, 2 * g + 1], jnp.where(upper, vp, zero)))
            scale = jnp.where(lane >= e, inv_l[2 * g + 1], inv_l[2 * g])
            o_ref[0, 0, qb * blk:(qb + 1) * blk, pair] = (o_pair * scale).astype(o_ref.dtype)
        lse_ref[0, 0, qb * blk:(qb + 1) * blk, :] = lse_all


def _dil_bias_vector(rel_bias, window, dil):
    blk = DIL_BLOCK
    span = window // dil
    r = blk - jnp.arange(4 * blk)
    band = (r >= 0) & (r <= span)
    bias = rel_bias[_t5_bucket(jnp.maximum(r, 0) * dil)].astype(F32)
    return jnp.where(band[:, None], bias, MASKED).T


def _dilated_pattern(dq, rel_bias, window, dil):
    b, d, sd, _ = dq.shape
    blk = DIL_BLOCK
    nq = math.gcd(DIL_QBLOCKS, sd // DIL_BLOCK)
    rows = nq * blk
    assert d == dil and window // dil == blk and sd % rows == 0
    cur = lambda c: (lambda bi, r, n: (bi, r, n, c))
    prev = lambda c: (lambda bi, r, n: (bi, r, jnp.maximum(n * nq - 1, 0), c))
    spec = lambda im: pl.BlockSpec((1, 1, rows, DIL_W), im)
    pspec = lambda im: pl.BlockSpec((1, 1, blk, DIL_W), im)
    return pl.pallas_call(
        _dil_kernel,
        grid=(b, dil, sd // rows),
        in_specs=[spec(cur(0)), pspec(prev(1)), spec(cur(1)), pspec(prev(2)), spec(cur(2)),
                  pl.BlockSpec((DIL_HEADS, 4 * blk), lambda bi, r, n: (0, 0))],
        out_specs=[pl.BlockSpec((1, 1, rows, DIL_W), lambda bi, r, n: (bi, r, n, 0)),
                   pl.BlockSpec((1, 1, rows, LANES), lambda bi, r, n: (bi, r, n, 0))],
        out_shape=[jax.ShapeDtypeStruct((b, dil, sd, DIL_W), BF16),
                   jax.ShapeDtypeStruct((b, dil, sd, LANES), F32)],
        scratch_shapes=[pltpu.VMEM((DIL_HEADS, blk, 2 * blk), F32),
                        pltpu.VMEM((nq, DIL_HEADS, blk, 2 * blk), F32),
                        pltpu.VMEM((nq, DIL_HEADS, blk, 2 * blk), BF16)],
        compiler_params=_params(("arbitrary", "arbitrary", "arbitrary")),
        name=f"dilated_d{dil}",
    )(dq, dq, dq, dq, dq, _dil_bias_vector(rel_bias, window, dil))


def _mix0_kernel(x_ref, ret_ref, *rest):
    np_ = len(DIL_DILATIONS)
    o_refs, l_refs = rest[:np_], rest[np_:2 * np_]
    w_ref, out_ref, obuf, lbuf, dbuf = rest[2 * np_:]
    tm = x_ref.shape[1]
    nslab = DIL_W // LANES
    for pi, d in enumerate(DIL_DILATIONS):
        rows = tm // d
        for r in range(d):
            dst = pl.ds(r, rows, stride=d) if d > 1 else pl.ds(0, tm)
            lbuf[pi, dst, :] = l_refs[pi][0, r]
            for s in range(nslab):
                obuf[pi, s, dst, :] = o_refs[pi][0, r, :, s * LANES:(s + 1) * LANES].astype(F32)
    lses = [lbuf[pi] for pi in range(np_)]
    m = functools.reduce(jnp.maximum, lses)
    es = [jnp.exp(l - m) for l in lses]
    inv = 1.0 / functools.reduce(lambda a, b: a + b, es)
    row = lax.broadcasted_iota(jnp.int32, (LANES, DIL_W), 0)
    col = lax.broadcasted_iota(jnp.int32, (LANES, DIL_W), 1)
    spread = (col // DIL_HEAD_DIM == row).astype(BF16)
    spread2 = jnp.concatenate([spread, spread], axis=0)
    wide = []
    for ex in es:
        wgt = ex * inv
        hi = wgt.astype(BF16)
        lo = (wgt - hi.astype(F32)).astype(BF16)
        wide.append(_dot(jnp.concatenate([hi, lo], axis=1), spread2))
    for s in range(nslab):
        cols = slice(s * LANES, (s + 1) * LANES)
        acc = wide[0][:, cols] * obuf[0, s]
        for pi in range(1, np_):
            acc = acc + wide[pi][:, cols] * obuf[pi, s]
        dbuf[:, cols] = acc.astype(BF16)
    na = ret_ref.shape[-1]
    y = _dot(ret_ref[0], w_ref[:na, :]) + _dot(dbuf[...], w_ref[na:, :])
    out_ref[0] = x_ref[0] + y


def _mix0(x3, ret, outs, lses, w):
    b, s, dm = x3.shape
    na = ret.shape[-1]
    tm = PROJ_TM
    o_specs = [pl.BlockSpec((1, d, tm // d, DIL_W), lambda bi, i: (bi, 0, i, 0)) for d in DIL_DILATIONS]
    l_specs = [pl.BlockSpec((1, d, tm // d, LANES), lambda bi, i: (bi, 0, i, 0)) for d in DIL_DILATIONS]
    np_ = len(DIL_DILATIONS)
    return pl.pallas_call(
        _mix0_kernel,
        grid=(b, s // tm),
        in_specs=[pl.BlockSpec((1, tm, dm), lambda bi, i: (bi, i, 0)),
                  pl.BlockSpec((1, tm, na), lambda bi, i: (bi, i, 0))] + o_specs + l_specs
                 + [pl.BlockSpec((na + DIL_W, dm), lambda bi, i: (0, 0))],
        out_specs=pl.BlockSpec((1, tm, dm), lambda bi, i: (bi, i, 0)),
        out_shape=jax.ShapeDtypeStruct((b, s, dm), F32),
        scratch_shapes=[pltpu.VMEM((np_, DIL_W // LANES, tm, LANES), F32),
                        pltpu.VMEM((np_, tm, LANES), F32),
                        pltpu.VMEM((tm, DIL_W), BF16)],
        compiler_params=_params(("arbitrary", "arbitrary")),
        name="mix0",
    )(x3, ret, *outs, *lses, w)


def _ffn_kernel(x_ref, *rest, final_norm, mixer_proj):
    if mixer_proj:
        a_ref, wmix_ref = rest[:2]
        rest = rest[2:]
    g_ref, win_ref, cw_ref, cb_ref, wout_ref, fg_ref, o_ref, ubuf, carry, gbuf = rest
    tm = x_ref.shape[1]
    halo = SUBLANES

    @pl.when(pl.program_id(1) == 0)
    def _():
        carry[...] = jnp.zeros_like(carry)

    x = x_ref[0]
    if mixer_proj:
        x = x + _dot(a_ref[0], wmix_ref[...])
    h = _rms(x, g_ref[...]).astype(BF16)

    def conv(part, col):
        cols = slice(col, col + FFN_CF)
        u = _dot(h, win_ref[:, cols])
        ubuf[part, 0:halo, :] = carry[:, cols]
        ubuf[part, halo:halo + tm, :] = u
        carry[:, cols] = u[tm - halo:, :]
        u1 = ubuf[part, halo - 1:halo - 1 + tm, :]
        u2 = ubuf[part, halo - 2:halo - 2 + tm, :]
        return (u * cw_ref[2:3, cols] + u1 * cw_ref[1:2, cols] + u2 * cw_ref[0:1, cols]
                + cb_ref[:, cols])

    for c in range(D_FF // FFN_CF):
        gate = conv(0, c * FFN_CF)
        up = conv(1, D_FF + c * FFN_CF)
        act = 0.5 * gate * (1.0 + lax.erf(gate * (2.0 ** -0.5)))
        gbuf[:, c * FFN_CF:(c + 1) * FFN_CF] = (act * up).astype(BF16)

    y = x + _dot(gbuf[...], wout_ref[...])
    if final_norm:
        y = _rms(y, fg_ref[...])
    o_ref[0] = y


def _ffn(x3, g, w_in, conv_w, conv_b, w_out, final_gain, final_norm, mixer=None):
    b, s, d = x3.shape
    f2 = w_in.shape[1]
    whole = pl.BlockSpec(memory_space=pltpu.VMEM)
    tile = lambda n: pl.BlockSpec((1, FFN_TM, n), lambda bi, t: (bi, t, 0))
    mix_specs = [tile(mixer[0].shape[-1]), whole] if mixer is not None else []
    mix_args = list(mixer) if mixer is not None else []
    return pl.pallas_call(
        functools.partial(_ffn_kernel, final_norm=final_norm, mixer_proj=mixer is not None),
        grid=(b, s // FFN_TM),
        in_specs=[tile(d)] + mix_specs + [whole, whole, whole, whole, whole, whole],
        out_specs=pl.BlockSpec((1, FFN_TM, d), lambda bi, t: (bi, t, 0)),
        out_shape=jax.ShapeDtypeStruct((b, s, d), F32),
        scratch_shapes=[pltpu.VMEM((2, FFN_TM + SUBLANES, FFN_CF), F32),
                        pltpu.VMEM((SUBLANES, f2), F32),
                        pltpu.VMEM((FFN_TM, D_FF), BF16)],
        compiler_params=_params(("arbitrary", "arbitrary")),
        name="ffn_final" if final_norm else "ffn",
    )(x3, *mix_args, g, w_in, conv_w, conv_b, w_out, final_gain)


def _proj1_kernel(x_ref, g_ref, wq_ref, wk_ref, wvt_ref, q_ref, k_ref, vt_ref):
    h = _rms(x_ref[0], g_ref[...]).astype(BF16)
    n = k_ref.shape[-1]
    for c in range(0, n, PROJ_TN):
        cols = slice(c, c + PROJ_TN)
        q_ref[0, 0, cols, :] = (_dot_nt(wq_ref[cols, :], h) * ATT_QSCALE).astype(BF16)
        k_ref[0, :, cols] = _dot(h, wk_ref[:, cols]).astype(BF16)
        vt_ref[0, 0, cols, :] = _dot_nt(wvt_ref[cols, :], h).astype(BF16)


def _proj1(x3, g, wq, wk, wvt):
    b, s, d = x3.shape
    n = wq.shape[1]
    t = ATT_T
    wspec = pl.BlockSpec((d, n), lambda bi, i: (0, 0))
    return pl.pallas_call(
        _proj1_kernel,
        grid=(b, s // t),
        in_specs=[pl.BlockSpec((1, t, d), lambda bi, i: (bi, i, 0)),
                  pl.BlockSpec((1, d), lambda bi, i: (0, 0)),
                  pl.BlockSpec((n, d), lambda bi, i: (0, 0)), wspec,
                  pl.BlockSpec((n, d), lambda bi, i: (0, 0))],
        out_specs=[pl.BlockSpec((1, 1, n, t), lambda bi, i: (bi, i, 0, 0)),
                   pl.BlockSpec((1, t, n), lambda bi, i: (bi, i, 0)),
                   pl.BlockSpec((1, 1, n, t), lambda bi, i: (bi, i, 0, 0))],
        out_shape=[jax.ShapeDtypeStruct((b, s // t, n, t), BF16),
                   jax.ShapeDtypeStruct((b, s, n), BF16),
                   jax.ShapeDtypeStruct((b, s // t, n, t), BF16)],
        compiler_params=_params(("arbitrary", "arbitrary")),
        name="proj1",
    )(x3, g, wq, wk, wvt)


def _diff_kernel(q_ref, k_ref, vt_ref, f_ref, lam_ref, sub_ref, o_ref,
                 qm_ref, km_ref, m_ref, l_ref, l8_ref, acc_ref, nb_ref, p_ref, *, lam_init):
    i = pl.program_id(2)
    t = ATT_T
    e = DIFF_HEAD_DIM
    half = lax.broadcasted_iota(jnp.int32, (SUBLANES, 2 * e), 1) // e
    sel = (lax.broadcasted_iota(jnp.int32, (SUBLANES, 2 * e), 0) == half).astype(BF16)

    @pl.when(i == 0)
    def _():
        for o in range(ATT_NEAR):
            g = jnp.broadcast_to(f_ref[0, 0:1, o * t:(o + 2) * t], (t, 2 * t))
            nb_ref[o] = pltpu.roll(g, 0, 1, stride=1, stride_axis=0)[:, t:]
        nb_ref[ATT_NEAR] = jnp.zeros((t, t), F32)
        kmax = jnp.zeros((SUBLANES, t), F32)
        for c in range(k_ref.shape[1] // t):
            kk = k_ref[0, c * t:(c + 1) * t, :].astype(F32)
            kmax = jnp.maximum(kmax, _dot_nt(sel, (kk * kk).astype(BF16)))
        for a in range(2):
            km_ref[a] = jnp.broadcast_to(jnp.max(kmax[a:a + 1, :], axis=1, keepdims=True), (1, t))

    qt = q_ref[0, 0]
    row = lax.broadcasted_iota(jnp.int32, qt.shape, 0)
    zero = jnp.zeros_like(qt)
    qm_ref[0] = jnp.where(row < e, qt, zero)
    qm_ref[1] = jnp.where(row >= e, qt, zero)
    qf = qt.astype(F32)
    qq = qf * qf
    qn2 = [jnp.sum(qq[a * e:(a + 1) * e], axis=0, keepdims=True) for a in range(2)]
    bias_max = f_ref[0, 1:2, 0:t]
    n_far = jnp.maximum(i - (ATT_NEAR - 1), 0)

    l8_ref[...] = jnp.zeros_like(l8_ref)
    acc_ref[...] = jnp.zeros_like(acc_ref)
    shift = [jnp.sqrt(qn2[a] * km_ref[a]) * ATT_BOUND_SLACK + bias_max for a in range(2)]

    kc = ATT_KC

    def fast_tiles(j0, ntiles, biased):
        for a in range(2):
            pv = None
            l8 = None
            for tt in range(ntiles):
                j = j0 + tt
                for c in range(0, t, kc):
                    kb = k_ref[0, pl.ds(pl.multiple_of(j * t + c, kc), kc), :]
                    s = _dot(kb, qm_ref[a])
                    if biased:
                        s = s + nb_ref[jnp.minimum(i - j, ATT_NEAR), c:c + kc, :]
                    p = jnp.exp2(s - shift[a])
                    part = jnp.sum(p.reshape(kc // SUBLANES, SUBLANES, t), axis=0)
                    d = _dot(vt_ref[0, j, :, c:c + kc], p.astype(BF16))
                    l8 = part if l8 is None else l8 + part
                    pv = d if pv is None else pv + d
            l8_ref[a] += l8
            acc_ref[a] += pv

    def probs(u, biased):
        slot = u % 2
        for a in range(2):
            l8 = None
            for tt in range(2):
                j = 2 * u + tt
                kb = k_ref[0, pl.ds(pl.multiple_of(j * t, t), t), :]
                s = _dot(kb, qm_ref[a])
                if biased:
                    s = s + nb_ref[jnp.minimum(i - j, ATT_NEAR)]
                p = jnp.exp2(s - shift[a])
                part = jnp.sum(p.reshape(t // SUBLANES, SUBLANES, t), axis=0)
                l8 = part if l8 is None else l8 + part
                p_ref[slot, a, tt] = p.astype(BF16)
            l8_ref[a] += l8

    def values(u):
        slot = u % 2
        for a in range(2):
            acc_ref[a] += (_dot(vt_ref[0, 2 * u], p_ref[slot, a, 0])
                           + _dot(vt_ref[0, 2 * u + 1], p_ref[slot, a, 1]))

    def far_step(u, carry):
        values(u - 1)
        probs(u, False)
        return carry

    def near_step(u, carry):
        values(u - 1)
        probs(u, True)
        return carry

    n = i + 1
    pairs = n // 2
    far_pairs = n_far // 2

    @pl.when(far_pairs > 0)
    def _():
        probs(0, False)

    @pl.when(jnp.logical_and(far_pairs == 0, pairs > 0))
    def _():
        probs(0, True)

    lax.fori_loop(1, far_pairs, far_step, 0)
    lax.fori_loop(jnp.maximum(far_pairs, 1), pairs, near_step, 0)

    odd = n % 2 == 1

    @pl.when(jnp.logical_and(pairs > 0, jnp.logical_not(odd)))
    def _():
        values(pairs - 1)

    @pl.when(jnp.logical_and(pairs > 0, odd))
    def _():
        values(pairs - 1)
        fast_tiles(n - 1, 1, True)

    @pl.when(pairs == 0)
    def _():
        fast_tiles(0, 1, True)
    def finalize():
        lam = (jnp.exp(jnp.sum(lam_ref[0:1, :] * lam_ref[1:2, :], axis=-1, keepdims=True))
               - jnp.exp(jnp.sum(lam_ref[2:3, :] * lam_ref[3:4, :], axis=-1, keepdims=True))
               + lam_init)
        ot = acc_ref[0] * (1.0 / l_ref[0]) - acc_ref[1] * (lam / l_ref[1])
        ot = ot * lax.rsqrt(jnp.mean(ot * ot, axis=0, keepdims=True) + EPS)
        o_ref[0] = (ot.T * sub_ref[...] * (1.0 - lam_init)).astype(o_ref.dtype)

    for a in range(2):
        l_ref[a] = jnp.sum(l8_ref[a], axis=0, keepdims=True)
    l_min = jnp.min(jnp.minimum(l_ref[0], l_ref[1]))
    l_max = jnp.max(jnp.maximum(l_ref[0], l_ref[1]))
    in_range = jnp.logical_and(l_min >= ATT_MIN_DENOM, l_max <= ATT_MAX_DENOM)
    finalize()

    @pl.when(jnp.logical_not(in_range))
    def _():
        m_ref[...] = jnp.full_like(m_ref, MASKED)
        l_ref[...] = jnp.zeros_like(l_ref)
        acc_ref[...] = jnp.zeros_like(acc_ref)

        def online_tile(j, bias):
            kb = k_ref[0, pl.ds(pl.multiple_of(j * t, t), t), :]
            vtb = vt_ref[0, j]
            for a in range(2):
                s = _dot(kb, qm_ref[a])
                if bias is not None:
                    s = s + bias
                m_old = m_ref[a]
                m_new = jnp.maximum(m_old, jnp.max(s, axis=0, keepdims=True))
                alpha = jnp.exp2(m_old - m_new)
                p = jnp.exp2(s - m_new)
                l_ref[a] = alpha * l_ref[a] + jnp.sum(p, axis=0, keepdims=True)
                acc_ref[a] = alpha * acc_ref[a] + _dot(vtb, p.astype(BF16))
                m_ref[a] = m_new

        def online_far(j, carry):
            online_tile(j, None)
            return carry

        def online_near(j, carry):
            online_tile(j, nb_ref[i - j])
            return carry

        lax.fori_loop(0, n_far, online_far, 0)
        lax.fori_loop(n_far, i + 1, online_near, 0)
        finalize()


def _diff_bias(rel_bias, s):
    t = ATT_T
    n = ATT_NEAR * t
    assert n >= BIAS_FLAT_FROM + t - 1 and s >= n
    dist_bias = rel_bias[_t5_bucket(jnp.arange(n))].astype(F32)
    far = rel_bias[REL_BUCKETS - 1].astype(F32)
    near = (dist_bias - far) * LOG2E
    f = jnp.concatenate([jnp.full((t, far.shape[0]), MASKED, F32), near])
    top = jnp.broadcast_to(jnp.maximum(jnp.max(near, axis=0), 0.0), f.shape)
    return jnp.stack([f.T, top.T], axis=1)


def _diff_attention(q, k, vt, rel_bias, lam4, subln, layer_idx):
    b, s, n = k.shape
    t = ATT_T
    w = 2 * DIFF_HEAD_DIM
    lam_init = 0.8 - 0.6 * math.exp(-0.3 * layer_idx)
    return pl.pallas_call(
        functools.partial(_diff_kernel, lam_init=lam_init),
        grid=(b, DIFF_HEADS, s // t),
        in_specs=[pl.BlockSpec((1, 1, w, t), lambda bi, h, i: (bi, i, h, 0)),
                  pl.BlockSpec((1, s, w), lambda bi, h, i: (bi, 0, h)),
                  pl.BlockSpec((1, s // t, w, t), lambda bi, h, i: (bi, 0, h, 0)),
                  pl.BlockSpec((1, 2, (ATT_NEAR + 1) * t), lambda bi, h, i: (h, 0, 0)),
                  pl.BlockSpec((4, DIFF_HEAD_DIM), lambda bi, h, i: (0, 0)),
                  pl.BlockSpec((1, w), lambda bi, h, i: (0, 0))],
        out_specs=pl.BlockSpec((1, t, w), lambda bi, h, i: (bi, i, h)),
        out_shape=jax.ShapeDtypeStruct((b, s, n), BF16),
        scratch_shapes=[pltpu.VMEM((2, w, t), BF16),
                        pltpu.VMEM((2, 1, t), F32),
                        pltpu.VMEM((2, 1, t), F32),
                        pltpu.VMEM((2, 1, t), F32),
                        pltpu.VMEM((2, SUBLANES, t), F32),
                        pltpu.VMEM((2, w, t), F32),
                        pltpu.VMEM((ATT_NEAR + 1, t, t), F32),
                        pltpu.VMEM((2, 2, 2, t, t), BF16)],
        compiler_params=_params(("arbitrary", "arbitrary", "arbitrary")),
        name="diff_attention",
    )(q, k, vt, _diff_bias(rel_bias, s), lam4, subln)


def kernel(x, rel_bias, norm_mix, norm_ffn, norm_final, w_in_ab, ret_gn, w_out_ab, w_in_c,
           lam_q1, lam_k1, lam_q2, lam_k2, diff_subln, w_out_c, w_ffn_in, conv_w, conv_b,
           w_ffn_out):
    b, s, d = x.shape
    depth = norm_mix.shape[0]
    assert depth == 2 and s % RET_ROWS == 0 and s % FFN_TM == 0 and s >= ATT_NEAR * ATT_T
    assert s % (max(DIL_DILATIONS) * DIL_BLOCK) == 0
    row = lambda v: v.reshape(1, -1)

    proj = _proj0(x, row(norm_mix[0]), w_in_ab[0].astype(BF16))
    ret = _retention(proj[0], row(ret_gn[0]), b, s)
    outs, lses = zip(*[_dilated_pattern(dq, rel_bias, wdw, dil)
                       for dq, (wdw, dil) in zip(proj[1:], DIL_PATTERNS)])
    x3 = _mix0(x, ret, outs, lses, w_out_ab[0].astype(BF16))
    x3 = _ffn(x3, row(norm_ffn[0]), w_ffn_in[0].astype(BF16), conv_w[0],
              row(conv_b[0]), w_ffn_out[0].astype(BF16), row(norm_final), False)

    nqk = DIFF_HEADS * 2 * DIFF_HEAD_DIM
    wc = w_in_c[0].astype(BF16)
    q, k, vt = _proj1(x3, row(norm_mix[1]), wc[:, :nqk].T, wc[:, nqk:2 * nqk], wc[:, 2 * nqk:].T)
    lam4 = jnp.stack([lam_q1[0], lam_k1[0], lam_q2[0], lam_k2[0]]).astype(F32)
    a = _diff_attention(q, k, vt, rel_bias, lam4, row(diff_subln[0]), 1)
    return _ffn(x3, row(norm_ffn[1]), w_ffn_in[1].astype(BF16), conv_w[1], row(conv_b[1]),
                w_ffn_out[1].astype(BF16), row(norm_final), True,
                mixer=(a, w_out_c[0].astype(BF16)))
```

```python
import functools
import math

import jax
import jax.numpy as jnp
from jax import lax
from jax.experimental import pallas as pl
from jax.experimental.pallas import tpu as pltpu

F32 = jnp.float32
BF16 = jnp.bfloat16

EPS = 1e-6
MASKED = -1e30

RET_HEADS = 4
RET_DIM = 128
RET_CHUNK = 128
ROPE_BASE = 10000.0
DIL_HEADS = 8
DIL_HEAD_DIM = 64
DIL_PATTERNS = ((128, 1), (512, 4), (2048, 16))
DIL_BLOCK = 128
DIFF_HEADS = 8
DIFF_HEAD_DIM = 64
REL_BUCKETS = 32
REL_MAX_DISTANCE = 2048
D_FF = 2816
CONV_WIDTH = 3
DIL_W = DIL_HEADS * DIL_HEAD_DIM
RET_W = 4 * RET_HEADS * RET_DIM
DIL_DILATIONS = tuple(d for _, d in DIL_PATTERNS)

LANES = 128
SUBLANES = 8
VMEM_BYTES_V7X = 64 * 1024 * 1024
VMEM_LIMIT = VMEM_BYTES_V7X - 8 * 1024 * 1024

PROJ_TM = 512
PROJ_TN = 512
RET_ROWS = 1024
DIL_QBLOCKS = 8
FFN_TM = 1024
FFN_CF = 256
ATT_T = 512
ATT_KC = 512
ATT_BOUND_SLACK = 1.0 + 2.0 ** -5
ATT_MIN_DENOM = 2.0 ** -60
ATT_MAX_DENOM = 2.0 ** 100
_MAX_EXACT = REL_BUCKETS // 2
BIAS_FLAT_FROM = 8 + math.ceil(_MAX_EXACT * (REL_MAX_DISTANCE / _MAX_EXACT)
                               ** ((REL_BUCKETS - 1 - _MAX_EXACT) / (REL_BUCKETS - _MAX_EXACT)))
ATT_NEAR = -(-(BIAS_FLAT_FROM + ATT_T - 1) // ATT_T)
LOG2E = math.log2(math.e)
ATT_QSCALE = DIFF_HEAD_DIM ** -0.5 * LOG2E


def _params(sem):
    return pltpu.CompilerParams(dimension_semantics=sem, vmem_limit_bytes=VMEM_LIMIT)


def _rms(x, g):
    return x * lax.rsqrt(jnp.mean(x * x, axis=-1, keepdims=True) + EPS) * g


def _dot(a, b):
    return jnp.dot(a, b, preferred_element_type=F32)


def _dot_nt(a, b):
    return lax.dot_general(a, b, (((1,), (1,)), ((), ())), preferred_element_type=F32)


def _dot_tn(a, b):
    return lax.dot_general(a, b, (((0,), (0,)), ((), ())), preferred_element_type=F32)


def _proj0_kernel(x_ref, g_ref, w_ref, a_ref, *rest):
    d_refs, dbuf = rest[:-1], rest[-1]
    tm = x_ref.shape[1]
    h = _rms(x_ref[0], g_ref[...]).astype(BF16)
    na = a_ref.shape[-1]
    nd = w_ref.shape[1] - na
    for c in range(0, nd, PROJ_TN):
        res = _dot(h, w_ref[:, na + c:na + c + PROJ_TN])
        for s in range(c // LANES, (c + PROJ_TN) // LANES):
            cols = slice(s * LANES, (s + 1) * LANES)
            dbuf[0, s] = res[:, (s * LANES - c):(s * LANES - c) + LANES]
            src, d_prev = 0, 1
            for dref, d in zip(d_refs, DIL_DILATIONS):
                if d == 1:
                    dref[0, 0, :, cols] = dbuf[0, s].astype(BF16)
                    continue
                step, rows = d // d_prev, tm // d
                dst = 1 - src
                for rp in range(d_prev):
                    for q in range(step):
                        piece = dbuf[src, s, pl.ds(rp * (tm // d_prev) + q, rows, stride=step), :]
                        r = rp + q * d_prev
                        if d != DIL_DILATIONS[-1]:
                            dbuf[dst, s, r * rows:(r + 1) * rows, :] = piece
                        dref[0, r, :, cols] = piece.astype(BF16)
                src, d_prev = dst, d
    for c in range(0, na, PROJ_TN):
        a_ref[0, :, c:c + PROJ_TN] = _dot(h, w_ref[:, c:c + PROJ_TN]).astype(BF16)


def _proj0(x3, g, w):
    b, s, dm = x3.shape
    n = w.shape[1]
    nd = n - RET_W
    tm = PROJ_TM
    d_specs = [pl.BlockSpec((1, d, tm // d, nd), lambda bi, i: (bi, 0, i, 0)) for d in DIL_DILATIONS]
    d_shapes = [jax.ShapeDtypeStruct((b, d, s // d, nd), BF16) for d in DIL_DILATIONS]
    return pl.pallas_call(
        _proj0_kernel,
        grid=(b, s // tm),
        in_specs=[pl.BlockSpec((1, tm, dm), lambda bi, i: (bi, i, 0)),
                  pl.BlockSpec((1, dm), lambda bi, i: (0, 0)),
                  pl.BlockSpec((dm, n), lambda bi, i: (0, 0))],
        out_specs=[pl.BlockSpec((1, tm, RET_W), lambda bi, i: (bi, i, 0))] + d_specs,
        out_shape=[jax.ShapeDtypeStruct((b, s, RET_W), BF16)] + d_shapes,
        scratch_shapes=[pltpu.VMEM((2, nd // LANES, tm, LANES), F32)],
        compiler_params=_params(("arbitrary", "arbitrary")),
        name="proj0",
    )(x3, g, w)


def _ret_kernel(q_ref, k_ref, v_ref, gate_ref, start_ref, off_ref, tab_ref, gn_ref, o_ref,
                state_ref, qr_ref, kr_ref, qx_ref, kv_ref, st_ref):
    @pl.when(pl.program_id(2) == 0)
    def _():
        state_ref[...] = jnp.zeros_like(state_ref)

    decay = tab_ref[0, 0]
    xi = tab_ref[0, 1]
    zeta = tab_ref[0, 2]
    g_chunk = tab_ref[0, 3]
    gain = gn_ref[...]
    half = RET_DIM // 2
    n_chunks = q_ref.shape[1] // RET_CHUNK
    for c in range(n_chunks):
        rows = pl.ds(c * RET_CHUNK, RET_CHUNK)
        ca, sa = start_ref[c, 0:1, :], start_ref[c, 1:2, :]
        cos = ca * off_ref[0] - sa * off_ref[1]
        sin = sa * off_ref[2] + ca * off_ref[3]
        q = q_ref[0, rows, :].astype(F32)
        k = k_ref[0, rows, :].astype(F32)
        q = q * cos + pltpu.roll(q, half, 1) * sin
        k = k * cos + pltpu.roll(k, half, 1) * sin
        qr_ref[c] = q.astype(BF16)
        kr_ref[c] = k.astype(BF16)
        qx_ref[c] = (q * xi).astype(BF16)
        kv_ref[c] = _dot_tn((k * zeta).astype(BF16), v_ref[0, rows, :])
    state = state_ref[...]
    for c in range(n_chunks):
        st_ref[c] = state.astype(BF16)
        state = g_chunk * state + kv_ref[c]
    state_ref[...] = state
    for c in range(n_chunks):
        rows = pl.ds(c * RET_CHUNK, RET_CHUNK)
        v = v_ref[0, rows, :]
        intra = _dot_nt(qr_ref[c], kr_ref[c]) * decay
        y = _dot(intra.astype(BF16), v) + _dot(qx_ref[c], st_ref[c])
        mu = jnp.mean(y, axis=-1, keepdims=True)
        yc = y - mu
        var = jnp.mean(yc * yc, axis=-1, keepdims=True)
        yn = yc * lax.rsqrt(var + EPS) * gain
        gate = gate_ref[0, rows, :].astype(F32)
        o_ref[0, rows, :] = (gate * jax.nn.sigmoid(gate) * yn).astype(o_ref.dtype)


def _retention_tables(s):
    c, e = RET_CHUNK, RET_DIM
    inv = ROPE_BASE ** (-jnp.arange(0, e, 2, dtype=F32) / e)
    inv = jnp.concatenate([inv, inv])
    sign = jnp.where(jnp.arange(e) < e // 2, -1.0, 1.0).astype(F32)
    a = (jnp.arange(s // c, dtype=F32) * c)[:, None] * inv[None, :]
    b_ = jnp.arange(c, dtype=F32)[:, None] * inv[None, :]
    start = jnp.stack([jnp.cos(a), jnp.sin(a)], axis=1)
    off = jnp.stack([jnp.cos(b_), jnp.sin(b_), jnp.cos(b_) * sign, jnp.sin(b_) * sign])
    log_g = jnp.log1p(-jnp.exp2(-5.0 - jnp.arange(RET_HEADS, dtype=F32)))
    pos = jnp.arange(c, dtype=F32)
    rel = pos[:, None] - pos[None, :]
    causal = rel >= 0
    scale = e ** -0.5
    decay = jnp.where(causal, jnp.exp(jnp.where(causal, rel, 0.0)[None] * log_g[:, None, None]), 0.0)
    zeta = jnp.exp((c - 1 - pos)[None, :] * log_g[:, None])
    xi = jnp.exp((pos + 1)[None, :] * log_g[:, None])
    g_chunk = jnp.exp(c * log_g)
    tab = jnp.stack([
        decay * scale,
        jnp.broadcast_to(xi[:, :, None], (RET_HEADS, c, e)),
        jnp.broadcast_to(zeta[:, :, None] * scale, (RET_HEADS, c, e)),
        jnp.broadcast_to(g_chunk[:, None, None], (RET_HEADS, e, e)),
    ], axis=1)
    return start, off, tab


def _retention(p0, gn, b, s):
    start, off, tab = _retention_tables(s)
    e = RET_DIM
    nc = RET_ROWS // RET_CHUNK
    col = lambda off: (lambda bi, h, t: (bi, t, off + h))
    return pl.pallas_call(
        _ret_kernel,
        grid=(b, RET_HEADS, s // RET_ROWS),
        in_specs=[pl.BlockSpec((1, RET_ROWS, e), col(0)),
                  pl.BlockSpec((1, RET_ROWS, e), col(RET_HEADS)),
                  pl.BlockSpec((1, RET_ROWS, e), col(2 * RET_HEADS)),
                  pl.BlockSpec((1, RET_ROWS, e), col(3 * RET_HEADS)),
                  pl.BlockSpec((nc, 2, e), lambda bi, h, t: (t, 0, 0)),
                  pl.BlockSpec((4, RET_CHUNK, e), lambda bi, h, t: (0, 0, 0)),
                  pl.BlockSpec((1, 4, e, e), lambda bi, h, t: (h, 0, 0, 0)),
                  pl.BlockSpec((1, e), lambda bi, h, t: (0, h))],
        out_specs=pl.BlockSpec((1, RET_ROWS, e), lambda bi, h, t: (bi, t, h)),
        out_shape=jax.ShapeDtypeStruct((b, s, RET_HEADS * e), BF16),
        scratch_shapes=[pltpu.VMEM((e, e), F32),
                        pltpu.VMEM((nc, RET_CHUNK, e), BF16),
                        pltpu.VMEM((nc, RET_CHUNK, e), BF16),
                        pltpu.VMEM((nc, RET_CHUNK, e), BF16),
                        pltpu.VMEM((nc, e, e), F32),
                        pltpu.VMEM((nc, e, e), BF16)],
        compiler_params=_params(("arbitrary", "arbitrary", "arbitrary")),
        name="retention",
    )(p0, p0, p0, p0, start, off, tab, gn)


def _t5_bucket(dist):
    max_exact = REL_BUCKETS // 2
    d = jnp.maximum(dist.astype(F32), 1.0)
    large = max_exact + (jnp.log(d / max_exact) / math.log(REL_MAX_DISTANCE / max_exact)
                         * (REL_BUCKETS - max_exact))
    large = jnp.clip(large.astype(jnp.int32), max_exact, REL_BUCKETS - 1)
    return jnp.where(dist < max_exact, dist, large)


def _dil_kernel(q_ref, kp_ref, kc_ref, vp_ref, vc_ref, g_ref, o_ref, lse_ref, bias_ref,
                s_ref, p_ref):
    first = pl.program_id(2) == 0
    blk, e = DIL_BLOCK, DIL_HEAD_DIM

    @pl.when(jnp.logical_and(jnp.logical_and(pl.program_id(0) == 0, pl.program_id(1) == 0), first))
    def _():
        for h in range(DIL_HEADS):
            gb = jnp.broadcast_to(g_ref[h:h + 1, :], (blk, 4 * blk))
            bias_ref[h] = pltpu.roll(gb, 0, 1, stride=1, stride_axis=0)[:, :2 * blk]

    q_all = q_ref[0, 0] * (e ** -0.5)
    k_all = jnp.concatenate([kp_ref[0, 0], kc_ref[0, 0]], axis=0)
    v_all = jnp.concatenate([vp_ref[0, 0], vc_ref[0, 0]], axis=0)
    in_prev = lax.broadcasted_iota(jnp.int32, (blk, 2 * blk), 1) < blk
    no_prev = jnp.logical_and(first, in_prev)
    lane = lax.broadcasted_iota(jnp.int32, (blk, LANES), 1)
    for qb in range(q_all.shape[0] // blk):
        q = q_all[qb * blk:(qb + 1) * blk]
        kcat = k_all[qb * blk:(qb + 2) * blk]
        vcat = v_all[qb * blk:(qb + 2) * blk]
        lse_all = jnp.zeros((blk, LANES), F32)
        for h in range(DIL_HEADS):
            pair = slice((h // 2) * LANES, (h // 2 + 1) * LANES)
            qp = q[:, pair]
            mine = (lax.broadcasted_iota(jnp.int32, qp.shape, 1) // e) == h % 2
            s = _dot_nt(jnp.where(mine, qp, jnp.zeros_like(qp)), kcat[:, pair]) + bias_ref[h]
            if qb == 0:
                s = jnp.where(no_prev, MASKED, s)
            s_ref[qb, h] = s
        inv_l = []
        for h in range(DIL_HEADS):
            s = s_ref[qb, h]
            m = jnp.max(s, axis=-1, keepdims=True)
            p = jnp.exp(s - m)
            l = jnp.sum(p, axis=-1, keepdims=True)
            p_ref[qb, h] = p.astype(BF16)
            inv_l.append(1.0 / l)
            lse_all = jnp.where(lane == h, m + jnp.log(l), lse_all)
        for g in range(DIL_HEADS // 2):
            pair = slice(g * LANES, (g + 1) * LANES)
            vp = vcat[:, pair]
            upper = lax.broadcasted_iota(jnp.int32, vp.shape, 1) >= e
            zero = jnp.zeros_like(vp)
            o_pair = (_dot(p_ref[qb, 2 * g], jnp.where(upper, zero, vp))
                      + _dot(p_ref[qb, 2 * g + 1], jnp.where(upper, vp, zero)))
            scale = jnp.where(lane >= e, inv_l[2 * g + 1], inv_l[2 * g])
            o_ref[0, 0, qb * blk:(qb + 1) * blk, pair] = (o_pair * scale).astype(o_ref.dtype)
        lse_ref[0, 0, qb * blk:(qb + 1) * blk, :] = lse_all


def _dil_bias_vector(rel_bias, window, dil):
    blk = DIL_BLOCK
    span = window // dil
    r = blk - jnp.arange(4 * blk)
    band = (r >= 0) & (r <= span)
    bias = rel_bias[_t5_bucket(jnp.maximum(r, 0) * dil)].astype(F32)
    return jnp.where(band[:, None], bias, MASKED).T


def _dilated_pattern(dq, rel_bias, window, dil):
    b, d, sd, _ = dq.shape
    blk = DIL_BLOCK
    nq = math.gcd(DIL_QBLOCKS, sd // DIL_BLOCK)
    rows = nq * blk
    assert d == dil and window // dil == blk and sd % rows == 0
    cur = lambda c: (lambda bi, r, n: (bi, r, n, c))
    prev = lambda c: (lambda bi, r, n: (bi, r, jnp.maximum(n * nq - 1, 0), c))
    spec = lambda im: pl.BlockSpec((1, 1, rows, DIL_W), im)
    pspec = lambda im: pl.BlockSpec((1, 1, blk, DIL_W), im)
    return pl.pallas_call(
        _dil_kernel,
        grid=(b, dil, sd // rows),
        in_specs=[spec(cur(0)), pspec(prev(1)), spec(cur(1)), pspec(prev(2)), spec(cur(2)),
                  pl.BlockSpec((DIL_HEADS, 4 * blk), lambda bi, r, n: (0, 0))],
        out_specs=[pl.BlockSpec((1, 1, rows, DIL_W), lambda bi, r, n: (bi, r, n, 0)),
                   pl.BlockSpec((1, 1, rows, LANES), lambda bi, r, n: (bi, r, n, 0))],
        out_shape=[jax.ShapeDtypeStruct((b, dil, sd, DIL_W), BF16),
                   jax.ShapeDtypeStruct((b, dil, sd, LANES), F32)],
        scratch_shapes=[pltpu.VMEM((DIL_HEADS, blk, 2 * blk), F32),
                        pltpu.VMEM((nq, DIL_HEADS, blk, 2 * blk), F32),
                        pltpu.VMEM((nq, DIL_HEADS, blk, 2 * blk), BF16)],
        compiler_params=_params(("arbitrary", "arbitrary", "arbitrary")),
        name=f"dilated_d{dil}",
    )(dq, dq, dq, dq, dq, _dil_bias_vector(rel_bias, window, dil))


def _mix0_kernel(x_ref, ret_ref, *rest):
    np_ = len(DIL_DILATIONS)
    o_refs, l_refs = rest[:np_], rest[np_:2 * np_]
    w_ref, out_ref, obuf, lbuf, dbuf = rest[2 * np_:]
    tm = x_ref.shape[1]
    nslab = DIL_W // LANES
    for pi, d in enumerate(DIL_DILATIONS):
        rows = tm // d
        for r in range(d):
            dst = pl.ds(r, rows, stride=d) if d > 1 else pl.ds(0, tm)
            lbuf[pi, dst, :] = l_refs[pi][0, r]
            for s in range(nslab):
                obuf[pi, s, dst, :] = o_refs[pi][0, r, :, s * LANES:(s + 1) * LANES].astype(F32)
    lses = [lbuf[pi] for pi in range(np_)]
    m = functools.reduce(jnp.maximum, lses)
    es = [jnp.exp(l - m) for l in lses]
    inv = 1.0 / functools.reduce(lambda a, b: a + b, es)
    row = lax.broadcasted_iota(jnp.int32, (LANES, DIL_W), 0)
    col = lax.broadcasted_iota(jnp.int32, (LANES, DIL_W), 1)
    spread = (col // DIL_HEAD_DIM == row).astype(BF16)
    spread2 = jnp.concatenate([spread, spread], axis=0)
    wide = []
    for ex in es:
        wgt = ex * inv
        hi = wgt.astype(BF16)
        lo = (wgt - hi.astype(F32)).astype(BF16)
        wide.append(_dot(jnp.concatenate([hi, lo], axis=1), spread2))
    for s in range(nslab):
        cols = slice(s * LANES, (s + 1) * LANES)
        acc = wide[0][:, cols] * obuf[0, s]
        for pi in range(1, np_):
            acc = acc + wide[pi][:, cols] * obuf[pi, s]
        dbuf[:, cols] = acc.astype(BF16)
    na = ret_ref.shape[-1]
    y = _dot(ret_ref[0], w_ref[:na, :]) + _dot(dbuf[...], w_ref[na:, :])
    out_ref[0] = x_ref[0] + y


def _mix0(x3, ret, outs, lses, w):
    b, s, dm = x3.shape
    na = ret.shape[-1]
    tm = PROJ_TM
    o_specs = [pl.BlockSpec((1, d, tm // d, DIL_W), lambda bi, i: (bi, 0, i, 0)) for d in DIL_DILATIONS]
    l_specs = [pl.BlockSpec((1, d, tm // d, LANES), lambda bi, i: (bi, 0, i, 0)) for d in DIL_DILATIONS]
    np_ = len(DIL_DILATIONS)
    return pl.pallas_call(
        _mix0_kernel,
        grid=(b, s // tm),
        in_specs=[pl.BlockSpec((1, tm, dm), lambda bi, i: (bi, i, 0)),
                  pl.BlockSpec((1, tm, na), lambda bi, i: (bi, i, 0))] + o_specs + l_specs
                 + [pl.BlockSpec((na + DIL_W, dm), lambda bi, i: (0, 0))],
        out_specs=pl.BlockSpec((1, tm, dm), lambda bi, i: (bi, i, 0)),
        out_shape=jax.ShapeDtypeStruct((b, s, dm), F32),
        scratch_shapes=[pltpu.VMEM((np_, DIL_W // LANES, tm, LANES), F32),
                        pltpu.VMEM((np_, tm, LANES), F32),
                        pltpu.VMEM((tm, DIL_W), BF16)],
        compiler_params=_params(("arbitrary", "arbitrary")),
        name="mix0",
    )(x3, ret, *outs, *lses, w)


def _ffn_kernel(x_ref, *rest, final_norm, mixer_proj):
    if mixer_proj:
        a_ref, wmix_ref = rest[:2]
        rest = rest[2:]
    g_ref, win_ref, cw_ref, cb_ref, wout_ref, fg_ref, o_ref, ubuf, carry, gbuf = rest
    tm = x_ref.shape[1]
    halo = SUBLANES

    @pl.when(pl.program_id(1) == 0)
    def _():
        carry[...] = jnp.zeros_like(carry)

    x = x_ref[0]
    if mixer_proj:
        x = x + _dot(a_ref[0], wmix_ref[...])
    h = _rms(x, g_ref[...]).astype(BF16)

    def conv(part, col):
        cols = slice(col, col + FFN_CF)
        u = _dot(h, win_ref[:, cols])
        ubuf[part, 0:halo, :] = carry[:, cols]
        ubuf[part, halo:halo + tm, :] = u
        carry[:, cols] = u[tm - halo:, :]
        u1 = ubuf[part, halo - 1:halo - 1 + tm, :]
        u2 = ubuf[part, halo - 2:halo - 2 + tm, :]
        return (u * cw_ref[2:3, cols] + u1 * cw_ref[1:2, cols] + u2 * cw_ref[0:1, cols]
                + cb_ref[:, cols])

    for c in range(D_FF // FFN_CF):
        gate = conv(0, c * FFN_CF)
        up = conv(1, D_FF + c * FFN_CF)
        act = 0.5 * gate * (1.0 + lax.erf(gate * (2.0 ** -0.5)))
        gbuf[:, c * FFN_CF:(c + 1) * FFN_CF] = (act * up).astype(BF16)

    y = x + _dot(gbuf[...], wout_ref[...])
    if final_norm:
        y = _rms(y, fg_ref[...])
    o_ref[0] = y


def _ffn(x3, g, w_in, conv_w, conv_b, w_out, final_gain, final_norm, mixer=None):
    b, s, d = x3.shape
    f2 = w_in.shape[1]
    whole = pl.BlockSpec(memory_space=pltpu.VMEM)
    tile = lambda n: pl.BlockSpec((1, FFN_TM, n), lambda bi, t: (bi, t, 0))
    mix_specs = [tile(mixer[0].shape[-1]), whole] if mixer is not None else []
    mix_args = list(mixer) if mixer is not None else []
    return pl.pallas_call(
        functools.partial(_ffn_kernel, final_norm=final_norm, mixer_proj=mixer is not None),
        grid=(b, s // FFN_TM),
        in_specs=[tile(d)] + mix_specs + [whole, whole, whole, whole, whole, whole],
        out_specs=pl.BlockSpec((1, FFN_TM, d), lambda bi, t: (bi, t, 0)),
        out_shape=jax.ShapeDtypeStruct((b, s, d), F32),
        scratch_shapes=[pltpu.VMEM((2, FFN_TM + SUBLANES, FFN_CF), F32),
                        pltpu.VMEM((SUBLANES, f2), F32),
                        pltpu.VMEM((FFN_TM, D_FF), BF16)],
        compiler_params=_params(("arbitrary", "arbitrary")),
        name="ffn_final" if final_norm else "ffn",
    )(x3, *mix_args, g, w_in, conv_w, conv_b, w_out, final_gain)


def _proj1_kernel(x_ref, g_ref, wq_ref, wk_ref, wvt_ref, q_ref, k_ref, vt_ref):
    h = _rms(x_ref[0], g_ref[...]).astype(BF16)
    n = k_ref.shape[-1]
    for c in range(0, n, PROJ_TN):
        cols = slice(c, c + PROJ_TN)
        q_ref[0, 0, cols, :] = (_dot_nt(wq_ref[cols, :], h) * ATT_QSCALE).astype(BF16)
        k_ref[0, :, cols] = _dot(h, wk_ref[:, cols]).astype(BF16)
        vt_ref[0, 0, cols, :] = _dot_nt(wvt_ref[cols, :], h).astype(BF16)


def _proj1(x3, g, wq, wk, wvt):
    b, s, d = x3.shape
    n = wq.shape[1]
    t = ATT_T
    wspec = pl.BlockSpec((d, n), lambda bi, i: (0, 0))
    return pl.pallas_call(
        _proj1_kernel,
        grid=(b, s // t),
        in_specs=[pl.BlockSpec((1, t, d), lambda bi, i: (bi, i, 0)),
                  pl.BlockSpec((1, d), lambda bi, i: (0, 0)),
                  pl.BlockSpec((n, d), lambda bi, i: (0, 0)), wspec,
                  pl.BlockSpec((n, d), lambda bi, i: (0, 0))],
        out_specs=[pl.BlockSpec((1, 1, n, t), lambda bi, i: (bi, i, 0, 0)),
                   pl.BlockSpec((1, t, n), lambda bi, i: (bi, i, 0)),
                   pl.BlockSpec((1, 1, n, t), lambda bi, i: (bi, i, 0, 0))],
        out_shape=[jax.ShapeDtypeStruct((b, s // t, n, t), BF16),
                   jax.ShapeDtypeStruct((b, s, n), BF16),
                   jax.ShapeDtypeStruct((b, s // t, n, t), BF16)],
        compiler_params=_params(("arbitrary", "arbitrary")),
        name="proj1",
    )(x3, g, wq, wk, wvt)


def _diff_kernel(q_ref, k_ref, vt_ref, f_ref, lam_ref, sub_ref, o_ref,
                 qm_ref, km_ref, m_ref, l_ref, l8_ref, acc_ref, nb_ref, p_ref, *, lam_init):
    i = pl.program_id(2)
    t = ATT_T
    e = DIFF_HEAD_DIM
    half = lax.broadcasted_iota(jnp.int32, (SUBLANES, 2 * e), 1) // e
    sel = (lax.broadcasted_iota(jnp.int32, (SUBLANES, 2 * e), 0) == half).astype(BF16)

    @pl.when(i == 0)
    def _():
        for o in range(ATT_NEAR):
            g = jnp.broadcast_to(f_ref[0, 0:1, o * t:(o + 2) * t], (t, 2 * t))
            nb_ref[o] = pltpu.roll(g, 0, 1, stride=1, stride_axis=0)[:, t:]
        nb_ref[ATT_NEAR] = jnp.zeros((t, t), F32)
        kmax = jnp.zeros((SUBLANES, t), F32)
        for c in range(k_ref.shape[1] // t):
            kk = k_ref[0, c * t:(c + 1) * t, :].astype(F32)
            kmax = jnp.maximum(kmax, _dot_nt(sel, (kk * kk).astype(BF16)))
        for a in range(2):
            km_ref[a] = jnp.broadcast_to(jnp.max(kmax[a:a + 1, :], axis=1, keepdims=True), (1, t))

    qt = q_ref[0, 0]
    row = lax.broadcasted_iota(jnp.int32, qt.shape, 0)
    zero = jnp.zeros_like(qt)
    qm_ref[0] = jnp.where(row < e, qt, zero)
    qm_ref[1] = jnp.where(row >= e, qt, zero)
    qf = qt.astype(F32)
    qq = qf * qf
    qn2 = [jnp.sum(qq[a * e:(a + 1) * e], axis=0, keepdims=True) for a in range(2)]
    bias_max = f_ref[0, 1:2, 0:t]
    n_far = jnp.maximum(i - (ATT_NEAR - 1), 0)

    l8_ref[...] = jnp.zeros_like(l8_ref)
    acc_ref[...] = jnp.zeros_like(acc_ref)
    shift = [jnp.sqrt(qn2[a] * km_ref[a]) * ATT_BOUND_SLACK + bias_max for a in range(2)]

    kc = ATT_KC

    def fast_tiles(j0, ntiles, biased):
        for a in range(2):
            pv = None
            l8 = None
            for tt in range(ntiles):
                j = j0 + tt
                for c in range(0, t, kc):
                    kb = k_ref[0, pl.ds(pl.multiple_of(j * t + c, kc), kc), :]
                    s = _dot(kb, qm_ref[a])
                    if biased:
                        s = s + nb_ref[jnp.minimum(i - j, ATT_NEAR), c:c + kc, :]
                    p = jnp.exp2(s - shift[a])
                    part = jnp.sum(p.reshape(kc // SUBLANES, SUBLANES, t), axis=0)
                    d = _dot(vt_ref[0, j, :, c:c + kc], p.astype(BF16))
                    l8 = part if l8 is None else l8 + part
                    pv = d if pv is None else pv + d
            l8_ref[a] += l8
            acc_ref[a] += pv

    def probs(u, biased):
        slot = u % 2
        for a in range(2):
            l8 = None
            for tt in range(2):
                j = 2 * u + tt
                kb = k_ref[0, pl.ds(pl.multiple_of(j * t, t), t), :]
                s = _dot(kb, qm_ref[a])
                if biased:
                    s = s + nb_ref[jnp.minimum(i - j, ATT_NEAR)]
                p = jnp.exp2(s - shift[a])
                part = jnp.sum(p.reshape(t // SUBLANES, SUBLANES, t), axis=0)
                l8 = part if l8 is None else l8 + part
                p_ref[slot, a, tt] = p.astype(BF16)
            l8_ref[a] += l8

    def values(u):
        slot = u % 2
        for a in range(2):
            acc_ref[a] += (_dot(vt_ref[0, 2 * u], p_ref[slot, a, 0])
                           + _dot(vt_ref[0, 2 * u + 1], p_ref[slot, a, 1]))

    def far_step(u, carry):
        values(u - 1)
        probs(u, False)
        return carry

    def near_step(u, carry):
        values(u - 1)
        probs(u, True)
        return carry

    n = i + 1
    pairs = n // 2
    far_pairs = n_far // 2

    @pl.when(far_pairs > 0)
    def _():
        probs(0, False)

    @pl.when(jnp.logical_and(far_pairs == 0, pairs > 0))
    def _():
        probs(0, True)

    lax.fori_loop(1, far_pairs, far_step, 0)
    lax.fori_loop(jnp.maximum(far_pairs, 1), pairs, near_step, 0)

    odd = n % 2 == 1

    @pl.when(jnp.logical_and(pairs > 0, jnp.logical_not(odd)))
    def _():
        values(pairs - 1)

    @pl.when(jnp.logical_and(pairs > 0, odd))
    def _():
        values(pairs - 1)
        fast_tiles(n - 1, 1, True)

    @pl.when(pairs == 0)
    def _():
        fast_tiles(0, 1, True)
    def finalize():
        lam = (jnp.exp(jnp.sum(lam_ref[0:1, :] * lam_ref[1:2, :], axis=-1, keepdims=True))
               - jnp.exp(jnp.sum(lam_ref[2:3, :] * lam_ref[3:4, :], axis=-1, keepdims=True))
               + lam_init)
        ot = acc_ref[0] * (1.0 / l_ref[0]) - acc_ref[1] * (lam / l_ref[1])
        ot = ot * lax.rsqrt(jnp.mean(ot * ot, axis=0, keepdims=True) + EPS)
        o_ref[0] = (ot.T * sub_ref[...] * (1.0 - lam_init)).astype(o_ref.dtype)

    for a in range(2):
        l_ref[a] = jnp.sum(l8_ref[a], axis=0, keepdims=True)
    l_min = jnp.min(jnp.minimum(l_ref[0], l_ref[1]))
    l_max = jnp.max(jnp.maximum(l_ref[0], l_ref[1]))
    in_range = jnp.logical_and(l_min >= ATT_MIN_DENOM, l_max <= ATT_MAX_DENOM)
    finalize()

    @pl.when(jnp.logical_not(in_range))
    def _():
        m_ref[...] = jnp.full_like(m_ref, MASKED)
        l_ref[...] = jnp.zeros_like(l_ref)
        acc_ref[...] = jnp.zeros_like(acc_ref)

        def online_tile(j, bias):
            kb = k_ref[0, pl.ds(pl.multiple_of(j * t, t), t), :]
            vtb = vt_ref[0, j]
            for a in range(2):
                s = _dot(kb, qm_ref[a])
                if bias is not None:
                    s = s + bias
                m_old = m_ref[a]
                m_new = jnp.maximum(m_old, jnp.max(s, axis=0, keepdims=True))
                alpha = jnp.exp2(m_old - m_new)
                p = jnp.exp2(s - m_new)
                l_ref[a] = alpha * l_ref[a] + jnp.sum(p, axis=0, keepdims=True)
                acc_ref[a] = alpha * acc_ref[a] + _dot(vtb, p.astype(BF16))
                m_ref[a] = m_new

        def online_far(j, carry):
            online_tile(j, None)
            return carry

        def online_near(j, carry):
            online_tile(j, nb_ref[i - j])
            return carry

        lax.fori_loop(0, n_far, online_far, 0)
        lax.fori_loop(n_far, i + 1, online_near, 0)
        finalize()


def _diff_bias(rel_bias, s):
    t = ATT_T
    n = ATT_NEAR * t
    assert n >= BIAS_FLAT_FROM + t - 1 and s >= n
    dist_bias = rel_bias[_t5_bucket(jnp.arange(n))].astype(F32)
    far = rel_bias[REL_BUCKETS - 1].astype(F32)
    near = (dist_bias - far) * LOG2E
    f = jnp.concatenate([jnp.full((t, far.shape[0]), MASKED, F32), near])
    top = jnp.broadcast_to(jnp.maximum(jnp.max(near, axis=0), 0.0), f.shape)
    return jnp.stack([f.T, top.T], axis=1)


def _diff_attention(q, k, vt, rel_bias, lam4, subln, layer_idx):
    b, s, n = k.shape
    t = ATT_T
    w = 2 * DIFF_HEAD_DIM
    lam_init = 0.8 - 0.6 * math.exp(-0.3 * layer_idx)
    return pl.pallas_call(
        functools.partial(_diff_kernel, lam_init=lam_init),
        grid=(b, DIFF_HEADS, s // t),
        in_specs=[pl.BlockSpec((1, 1, w, t), lambda bi, h, i: (bi, i, h, 0)),
                  pl.BlockSpec((1, s, w), lambda bi, h, i: (bi, 0, h)),
                  pl.BlockSpec((1, s // t, w, t), lambda bi, h, i: (bi, 0, h, 0)),
                  pl.BlockSpec((1, 2, (ATT_NEAR + 1) * t), lambda bi, h, i: (h, 0, 0)),
                  pl.BlockSpec((4, DIFF_HEAD_DIM), lambda bi, h, i: (0, 0)),
                  pl.BlockSpec((1, w), lambda bi, h, i: (0, 0))],
        out_specs=pl.BlockSpec((1, t, w), lambda bi, h, i: (bi, i, h)),
        out_shape=jax.ShapeDtypeStruct((b, s, n), BF16),
        scratch_shapes=[pltpu.VMEM((2, w, t), BF16),
                        pltpu.VMEM((2, 1, t), F32),
                        pltpu.VMEM((2, 1, t), F32),
                        pltpu.VMEM((2, 1, t), F32),
                        pltpu.VMEM((2, SUBLANES, t), F32),
                        pltpu.VMEM((2, w, t), F32),
                        pltpu.VMEM((ATT_NEAR + 1, t, t), F32),
                        pltpu.VMEM((2, 2, 2, t, t), BF16)],
        compiler_params=_params(("arbitrary", "arbitrary", "arbitrary")),
        name="diff_attention",
    )(q, k, vt, _diff_bias(rel_bias, s), lam4, subln)


def kernel(x, rel_bias, norm_mix, norm_ffn, norm_final, w_in_ab, ret_gn, w_out_ab, w_in_c,
           lam_q1, lam_k1, lam_q2, lam_k2, diff_subln, w_out_c, w_ffn_in, conv_w, conv_b,
           w_ffn_out):
    b, s, d = x.shape
    depth = norm_mix.shape[0]
    assert depth == 2 and s % RET_ROWS == 0 and s % FFN_TM == 0 and s >= ATT_NEAR * ATT_T
    assert s % (max(DIL_DILATIONS) * DIL_BLOCK) == 0
    row = lambda v: v.reshape(1, -1)

    proj = _proj0(x, row(norm_mix[0]), w_in_ab[0].astype(BF16))
    ret = _retention(proj[0], row(ret_gn[0]), b, s)
    outs, lses = zip(*[_dilated_pattern(dq, rel_bias, wdw, dil)
                       for dq, (wdw, dil) in zip(proj[1:], DIL_PATTERNS)])
    x3 = _mix0(x, ret, outs, lses, w_out_ab[0].astype(BF16))
    x3 = _ffn(x3, row(norm_ffn[0]), w_ffn_in[0].astype(BF16), conv_w[0],
              row(conv_b[0]), w_ffn_out[0].astype(BF16), row(norm_final), False)

    nqk = DIFF_HEADS * 2 * DIFF_HEAD_DIM
    wc = w_in_c[0].astype(BF16)
    q, k, vt = _proj1(x3, row(norm_mix[1]), wc[:, :nqk].T, wc[:, nqk:2 * nqk], wc[:, 2 * nqk:].T)
    lam4 = jnp.stack([lam_q1[0], lam_k1[0], lam_q2[0], lam_k2[0]]).astype(F32)
    a = _diff_attention(q, k, vt, rel_bias, lam4, row(diff_subln[0]), 1)
    return _ffn(x3, row(norm_ffn[1]), w_ffn_in[1].astype(BF16), conv_w[1], row(conv_b[1]),
                w_ffn_out[1].astype(BF16), row(norm_final), True,
                mixer=(a, w_out_c[0].astype(BF16)))
```

```python
import functools
import math

import jax
import jax.numpy as jnp
from jax import lax
from jax.experimental import pallas as pl
from jax.experimental.pallas import tpu as pltpu

F32 = jnp.float32
BF16 = jnp.bfloat16

EPS = 1e-6
MASKED = -1e30

RET_HEADS = 4
RET_DIM = 128
RET_CHUNK = 128
ROPE_BASE = 10000.0
DIL_HEADS = 8
DIL_HEAD_DIM = 64
DIL_PATTERNS = ((128, 1), (512, 4), (2048, 16))
DIL_BLOCK = 128
DIFF_HEADS = 8
DIFF_HEAD_DIM = 64
REL_BUCKETS = 32
REL_MAX_DISTANCE = 2048
D_FF = 2816
CONV_WIDTH = 3
DIL_W = DIL_HEADS * DIL_HEAD_DIM
RET_W = 4 * RET_HEADS * RET_DIM
DIL_DILATIONS = tuple(d for _, d in DIL_PATTERNS)

LANES = 128
SUBLANES = 8
VMEM_BYTES_V7X = 64 * 1024 * 1024
VMEM_LIMIT = VMEM_BYTES_V7X - 8 * 1024 * 1024

PROJ_TM = 512
PROJ_TN = 512
RET_ROWS = 1024
DIL_QBLOCKS = 8
FFN_TM = 1024
FFN_CF = 256
ATT_T = 512
ATT_BOUND_SLACK = 1.0 + 2.0 ** -5
ATT_MIN_DENOM = 2.0 ** -60
ATT_MAX_DENOM = 2.0 ** 100
_MAX_EXACT = REL_BUCKETS // 2
BIAS_FLAT_FROM = 8 + math.ceil(_MAX_EXACT * (REL_MAX_DISTANCE / _MAX_EXACT)
                               ** ((REL_BUCKETS - 1 - _MAX_EXACT) / (REL_BUCKETS - _MAX_EXACT)))
ATT_NEAR = -(-(BIAS_FLAT_FROM + ATT_T - 1) // ATT_T)
LOG2E = math.log2(math.e)
ATT_QSCALE = DIFF_HEAD_DIM ** -0.5 * LOG2E


def _params(sem):
    return pltpu.CompilerParams(dimension_semantics=sem, vmem_limit_bytes=VMEM_LIMIT)


def _rms(x, g):
    return x * lax.rsqrt(jnp.mean(x * x, axis=-1, keepdims=True) + EPS) * g


def _dot(a, b):
    return jnp.dot(a, b, preferred_element_type=F32)


def _dot_nt(a, b):
    return lax.dot_general(a, b, (((1,), (1,)), ((), ())), preferred_element_type=F32)


def _dot_tn(a, b):
    return lax.dot_general(a, b, (((0,), (0,)), ((), ())), preferred_element_type=F32)


def _proj0_kernel(x_ref, g_ref, w_ref, a_ref, *rest):
    d_refs, dbuf = rest[:-1], rest[-1]
    tm = x_ref.shape[1]
    h = _rms(x_ref[0], g_ref[...]).astype(BF16)
    na = a_ref.shape[-1]
    nd = w_ref.shape[1] - na
    for c in range(0, nd, PROJ_TN):
        res = _dot(h, w_ref[:, na + c:na + c + PROJ_TN])
        for s in range(c // LANES, (c + PROJ_TN) // LANES):
            cols = slice(s * LANES, (s + 1) * LANES)
            dbuf[0, s] = res[:, (s * LANES - c):(s * LANES - c) + LANES]
            src, d_prev = 0, 1
            for dref, d in zip(d_refs, DIL_DILATIONS):
                if d == 1:
                    dref[0, 0, :, cols] = dbuf[0, s].astype(BF16)
                    continue
                step, rows = d // d_prev, tm // d
                dst = 1 - src
                for rp in range(d_prev):
                    for q in range(step):
                        piece = dbuf[src, s, pl.ds(rp * (tm // d_prev) + q, rows, stride=step), :]
                        r = rp + q * d_prev
                        if d != DIL_DILATIONS[-1]:
                            dbuf[dst, s, r * rows:(r + 1) * rows, :] = piece
                        dref[0, r, :, cols] = piece.astype(BF16)
                src, d_prev = dst, d
    for c in range(0, na, PROJ_TN):
        a_ref[0, :, c:c + PROJ_TN] = _dot(h, w_ref[:, c:c + PROJ_TN]).astype(BF16)


def _proj0(x3, g, w):
    b, s, dm = x3.shape
    n = w.shape[1]
    nd = n - RET_W
    tm = PROJ_TM
    d_specs = [pl.BlockSpec((1, d, tm // d, nd), lambda bi, i: (bi, 0, i, 0)) for d in DIL_DILATIONS]
    d_shapes = [jax.ShapeDtypeStruct((b, d, s // d, nd), BF16) for d in DIL_DILATIONS]
    return pl.pallas_call(
        _proj0_kernel,
        grid=(b, s // tm),
        in_specs=[pl.BlockSpec((1, tm, dm), lambda bi, i: (bi, i, 0)),
                  pl.BlockSpec((1, dm), lambda bi, i: (0, 0)),
                  pl.BlockSpec((dm, n), lambda bi, i: (0, 0))],
        out_specs=[pl.BlockSpec((1, tm, RET_W), lambda bi, i: (bi, i, 0))] + d_specs,
        out_shape=[jax.ShapeDtypeStruct((b, s, RET_W), BF16)] + d_shapes,
        scratch_shapes=[pltpu.VMEM((2, nd // LANES, tm, LANES), F32)],
        compiler_params=_params(("arbitrary", "arbitrary")),
        name="proj0",
    )(x3, g, w)


def _ret_kernel(q_ref, k_ref, v_ref, gate_ref, start_ref, off_ref, tab_ref, gn_ref, o_ref,
                state_ref, qr_ref, kr_ref, qx_ref, kv_ref, st_ref):
    @pl.when(pl.program_id(2) == 0)
    def _():
        state_ref[...] = jnp.zeros_like(state_ref)

    decay = tab_ref[0, 0]
    xi = tab_ref[0, 1]
    zeta = tab_ref[0, 2]
    g_chunk = tab_ref[0, 3]
    gain = gn_ref[...]
    half = RET_DIM // 2
    n_chunks = q_ref.shape[1] // RET_CHUNK
    for c in range(n_chunks):
        rows = pl.ds(c * RET_CHUNK, RET_CHUNK)
        ca, sa = start_ref[c, 0:1, :], start_ref[c, 1:2, :]
        cos = ca * off_ref[0] - sa * off_ref[1]
        sin = sa * off_ref[2] + ca * off_ref[3]
        q = q_ref[0, rows, :].astype(F32)
        k = k_ref[0, rows, :].astype(F32)
        q = q * cos + pltpu.roll(q, half, 1) * sin
        k = k * cos + pltpu.roll(k, half, 1) * sin
        qr_ref[c] = q.astype(BF16)
        kr_ref[c] = k.astype(BF16)
        qx_ref[c] = (q * xi).astype(BF16)
        kv_ref[c] = _dot_tn((k * zeta).astype(BF16), v_ref[0, rows, :])
    state = state_ref[...]
    for c in range(n_chunks):
        st_ref[c] = state.astype(BF16)
        state = g_chunk * state + kv_ref[c]
    state_ref[...] = state
    for c in range(n_chunks):
        rows = pl.ds(c * RET_CHUNK, RET_CHUNK)
        v = v_ref[0, rows, :]
        intra = _dot_nt(qr_ref[c], kr_ref[c]) * decay
        y = _dot(intra.astype(BF16), v) + _dot(qx_ref[c], st_ref[c])
        mu = jnp.mean(y, axis=-1, keepdims=True)
        yc = y - mu
        var = jnp.mean(yc * yc, axis=-1, keepdims=True)
        yn = yc * lax.rsqrt(var + EPS) * gain
        gate = gate_ref[0, rows, :].astype(F32)
        o_ref[0, rows, :] = (gate * jax.nn.sigmoid(gate) * yn).astype(o_ref.dtype)


def _retention_tables(s):
    c, e = RET_CHUNK, RET_DIM
    inv = ROPE_BASE ** (-jnp.arange(0, e, 2, dtype=F32) / e)
    inv = jnp.concatenate([inv, inv])
    sign = jnp.where(jnp.arange(e) < e // 2, -1.0, 1.0).astype(F32)
    a = (jnp.arange(s // c, dtype=F32) * c)[:, None] * inv[None, :]
    b_ = jnp.arange(c, dtype=F32)[:, None] * inv[None, :]
    start = jnp.stack([jnp.cos(a), jnp.sin(a)], axis=1)
    off = jnp.stack([jnp.cos(b_), jnp.sin(b_), jnp.cos(b_) * sign, jnp.sin(b_) * sign])
    log_g = jnp.log1p(-jnp.exp2(-5.0 - jnp.arange(RET_HEADS, dtype=F32)))
    pos = jnp.arange(c, dtype=F32)
    rel = pos[:, None] - pos[None, :]
    causal = rel >= 0
    scale = e ** -0.5
    decay = jnp.where(causal, jnp.exp(jnp.where(causal, rel, 0.0)[None] * log_g[:, None, None]), 0.0)
    zeta = jnp.exp((c - 1 - pos)[None, :] * log_g[:, None])
    xi = jnp.exp((pos + 1)[None, :] * log_g[:, None])
    g_chunk = jnp.exp(c * log_g)
    tab = jnp.stack([
        decay * scale,
        jnp.broadcast_to(xi[:, :, None], (RET_HEADS, c, e)),
        jnp.broadcast_to(zeta[:, :, None] * scale, (RET_HEADS, c, e)),
        jnp.broadcast_to(g_chunk[:, None, None], (RET_HEADS, e, e)),
    ], axis=1)
    return start, off, tab


def _retention(p0, gn, b, s):
    start, off, tab = _retention_tables(s)
    e = RET_DIM
    nc = RET_ROWS // RET_CHUNK
    col = lambda off: (lambda bi, h, t: (bi, t, off + h))
    return pl.pallas_call(
        _ret_kernel,
        grid=(b, RET_HEADS, s // RET_ROWS),
        in_specs=[pl.BlockSpec((1, RET_ROWS, e), col(0)),
                  pl.BlockSpec((1, RET_ROWS, e), col(RET_HEADS)),
                  pl.BlockSpec((1, RET_ROWS, e), col(2 * RET_HEADS)),
                  pl.BlockSpec((1, RET_ROWS, e), col(3 * RET_HEADS)),
                  pl.BlockSpec((nc, 2, e), lambda bi, h, t: (t, 0, 0)),
                  pl.BlockSpec((4, RET_CHUNK, e), lambda bi, h, t: (0, 0, 0)),
                  pl.BlockSpec((1, 4, e, e), lambda bi, h, t: (h, 0, 0, 0)),
                  pl.BlockSpec((1, e), lambda bi, h, t: (0, h))],
        out_specs=pl.BlockSpec((1, RET_ROWS, e), lambda bi, h, t: (bi, t, h)),
        out_shape=jax.ShapeDtypeStruct((b, s, RET_HEADS * e), BF16),
        scratch_shapes=[pltpu.VMEM((e, e), F32),
                        pltpu.VMEM((nc, RET_CHUNK, e), BF16),
                        pltpu.VMEM((nc, RET_CHUNK, e), BF16),
                        pltpu.VMEM((nc, RET_CHUNK, e), BF16),
                        pltpu.VMEM((nc, e, e), F32),
                        pltpu.VMEM((nc, e, e), BF16)],
        compiler_params=_params(("arbitrary", "arbitrary", "arbitrary")),
        name="retention",
    )(p0, p0, p0, p0, start, off, tab, gn)


def _t5_bucket(dist):
    max_exact = REL_BUCKETS // 2
    d = jnp.maximum(dist.astype(F32), 1.0)
    large = max_exact + (jnp.log(d / max_exact) / math.log(REL_MAX_DISTANCE / max_exact)
                         * (REL_BUCKETS - max_exact))
    large = jnp.clip(large.astype(jnp.int32), max_exact, REL_BUCKETS - 1)
    return jnp.where(dist < max_exact, dist, large)


def _dil_kernel(q_ref, kp_ref, kc_ref, vp_ref, vc_ref, g_ref, o_ref, lse_ref, bias_ref,
                s_ref, p_ref):
    first = pl.program_id(2) == 0
    blk, e = DIL_BLOCK, DIL_HEAD_DIM

    @pl.when(jnp.logical_and(jnp.logical_and(pl.program_id(0) == 0, pl.program_id(1) == 0), first))
    def _():
        for h in range(DIL_HEADS):
            gb = jnp.broadcast_to(g_ref[h:h + 1, :], (blk, 4 * blk))
            bias_ref[h] = pltpu.roll(gb, 0, 1, stride=1, stride_axis=0)[:, :2 * blk]

    q_all = q_ref[0, 0] * (e ** -0.5)
    k_all = jnp.concatenate([kp_ref[0, 0], kc_ref[0, 0]], axis=0)
    v_all = jnp.concatenate([vp_ref[0, 0], vc_ref[0, 0]], axis=0)
    in_prev = lax.broadcasted_iota(jnp.int32, (blk, 2 * blk), 1) < blk
    no_prev = jnp.logical_and(first, in_prev)
    lane = lax.broadcasted_iota(jnp.int32, (blk, LANES), 1)
    for qb in range(q_all.shape[0] // blk):
        q = q_all[qb * blk:(qb + 1) * blk]
        kcat = k_all[qb * blk:(qb + 2) * blk]
        vcat = v_all[qb * blk:(qb + 2) * blk]
        lse_all = jnp.zeros((blk, LANES), F32)
        for h in range(DIL_HEADS):
            pair = slice((h // 2) * LANES, (h // 2 + 1) * LANES)
            qp = q[:, pair]
            mine = (lax.broadcasted_iota(jnp.int32, qp.shape, 1) // e) == h % 2
            s = _dot_nt(jnp.where(mine, qp, jnp.zeros_like(qp)), kcat[:, pair]) + bias_ref[h]
            if qb == 0:
                s = jnp.where(no_prev, MASKED, s)
            s_ref[qb, h] = s
        inv_l = []
        for h in range(DIL_HEADS):
            s = s_ref[qb, h]
            m = jnp.max(s, axis=-1, keepdims=True)
            p = jnp.exp(s - m)
            l = jnp.sum(p, axis=-1, keepdims=True)
            p_ref[qb, h] = p.astype(BF16)
            inv_l.append(1.0 / l)
            lse_all = jnp.where(lane == h, m + jnp.log(l), lse_all)
        for g in range(DIL_HEADS // 2):
            pair = slice(g * LANES, (g + 1) * LANES)
            vp = vcat[:, pair]
            upper = lax.broadcasted_iota(jnp.int32, vp.shape, 1) >= e
            zero = jnp.zeros_like(vp)
            o_pair = (_dot(p_ref[qb, 2 * g], jnp.where(upper, zero, vp))
                      + _dot(p_ref[qb, 2 * g + 1], jnp.where(upper, vp, zero)))
            scale = jnp.where(lane >= e, inv_l[2 * g + 1], inv_l[2 * g])
            o_ref[0, 0, qb * blk:(qb + 1) * blk, pair] = (o_pair * scale).astype(o_ref.dtype)
        lse_ref[0, 0, qb * blk:(qb + 1) * blk, :] = lse_all


def _dil_bias_vector(rel_bias, window, dil):
    blk = DIL_BLOCK
    span = window // dil
    r = blk - jnp.arange(4 * blk)
    band = (r >= 0) & (r <= span)
    bias = rel_bias[_t5_bucket(jnp.maximum(r, 0) * dil)].astype(F32)
    return jnp.where(band[:, None], bias, MASKED).T


def _dilated_pattern(dq, rel_bias, window, dil):
    b, d, sd, _ = dq.shape
    blk = DIL_BLOCK
    nq = math.gcd(DIL_QBLOCKS, sd // DIL_BLOCK)
    rows = nq * blk
    assert d == dil and window // dil == blk and sd % rows == 0
    cur = lambda c: (lambda bi, r, n: (bi, r, n, c))
    prev = lambda c: (lambda bi, r, n: (bi, r, jnp.maximum(n * nq - 1, 0), c))
    spec = lambda im: pl.BlockSpec((1, 1, rows, DIL_W), im)
    pspec = lambda im: pl.BlockSpec((1, 1, blk, DIL_W), im)
    return pl.pallas_call(
        _dil_kernel,
        grid=(b, dil, sd // rows),
        in_specs=[spec(cur(0)), pspec(prev(1)), spec(cur(1)), pspec(prev(2)), spec(cur(2)),
                  pl.BlockSpec((DIL_HEADS, 4 * blk), lambda bi, r, n: (0, 0))],
        out_specs=[pl.BlockSpec((1, 1, rows, DIL_W), lambda bi, r, n: (bi, r, n, 0)),
                   pl.BlockSpec((1, 1, rows, LANES), lambda bi, r, n: (bi, r, n, 0))],
        out_shape=[jax.ShapeDtypeStruct((b, dil, sd, DIL_W), BF16),
                   jax.ShapeDtypeStruct((b, dil, sd, LANES), F32)],
        scratch_shapes=[pltpu.VMEM((DIL_HEADS, blk, 2 * blk), F32),
                        pltpu.VMEM((nq, DIL_HEADS, blk, 2 * blk), F32),
                        pltpu.VMEM((nq, DIL_HEADS, blk, 2 * blk), BF16)],
        compiler_params=_params(("arbitrary", "arbitrary", "arbitrary")),
        name=f"dilated_d{dil}",
    )(dq, dq, dq, dq, dq, _dil_bias_vector(rel_bias, window, dil))


def _mix0_kernel(x_ref, ret_ref, *rest):
    np_ = len(DIL_DILATIONS)
    o_refs, l_refs = rest[:np_], rest[np_:2 * np_]
    w_ref, out_ref, obuf, lbuf, dbuf = rest[2 * np_:]
    tm = x_ref.shape[1]
    nslab = DIL_W // LANES
    for pi, d in enumerate(DIL_DILATIONS):
        rows = tm // d
        for r in range(d):
            dst = pl.ds(r, rows, stride=d) if d > 1 else pl.ds(0, tm)
            lbuf[pi, dst, :] = l_refs[pi][0, r]
            for s in range(nslab):
                obuf[pi, s, dst, :] = o_refs[pi][0, r, :, s * LANES:(s + 1) * LANES].astype(F32)
    lses = [lbuf[pi] for pi in range(np_)]
    m = functools.reduce(jnp.maximum, lses)
    es = [jnp.exp(l - m) for l in lses]
    inv = 1.0 / functools.reduce(lambda a, b: a + b, es)
    row = lax.broadcasted_iota(jnp.int32, (LANES, DIL_W), 0)
    col = lax.broadcasted_iota(jnp.int32, (LANES, DIL_W), 1)
    spread = (col // DIL_HEAD_DIM == row).astype(BF16)
    spread2 = jnp.concatenate([spread, spread], axis=0)
    wide = []
    for ex in es:
        wgt = ex * inv
        hi = wgt.astype(BF16)
        lo = (wgt - hi.astype(F32)).astype(BF16)
        wide.append(_dot(jnp.concatenate([hi, lo], axis=1), spread2))
    for s in range(nslab):
        cols = slice(s * LANES, (s + 1) * LANES)
        acc = wide[0][:, cols] * obuf[0, s]
        for pi in range(1, np_):
            acc = acc + wide[pi][:, cols] * obuf[pi, s]
        dbuf[:, cols] = acc.astype(BF16)
    na = ret_ref.shape[-1]
    y = _dot(ret_ref[0], w_ref[:na, :]) + _dot(dbuf[...], w_ref[na:, :])
    out_ref[0] = x_ref[0] + y


def _mix0(x3, ret, outs, lses, w):
    b, s, dm = x3.shape
    na = ret.shape[-1]
    tm = PROJ_TM
    o_specs = [pl.BlockSpec((1, d, tm // d, DIL_W), lambda bi, i: (bi, 0, i, 0)) for d in DIL_DILATIONS]
    l_specs = [pl.BlockSpec((1, d, tm // d, LANES), lambda bi, i: (bi, 0, i, 0)) for d in DIL_DILATIONS]
    np_ = len(DIL_DILATIONS)
    return pl.pallas_call(
        _mix0_kernel,
        grid=(b, s // tm),
        in_specs=[pl.BlockSpec((1, tm, dm), lambda bi, i: (bi, i, 0)),
                  pl.BlockSpec((1, tm, na), lambda bi, i: (bi, i, 0))] + o_specs + l_specs
                 + [pl.BlockSpec((na + DIL_W, dm), lambda bi, i: (0, 0))],
        out_specs=pl.BlockSpec((1, tm, dm), lambda bi, i: (bi, i, 0)),
        out_shape=jax.ShapeDtypeStruct((b, s, dm), F32),
        scratch_shapes=[pltpu.VMEM((np_, DIL_W // LANES, tm, LANES), F32),
                        pltpu.VMEM((np_, tm, LANES), F32),
                        pltpu.VMEM((tm, DIL_W), BF16)],
        compiler_params=_params(("arbitrary", "arbitrary")),
        name="mix0",
    )(x3, ret, *outs, *lses, w)


def _ffn_kernel(x_ref, *rest, final_norm, mixer_proj):
    if mixer_proj:
        a_ref, wmix_ref = rest[:2]
        rest = rest[2:]
    g_ref, win_ref, cw_ref, cb_ref, wout_ref, fg_ref, o_ref, ubuf, carry, gbuf = rest
    tm = x_ref.shape[1]
    halo = SUBLANES

    @pl.when(pl.program_id(1) == 0)
    def _():
        carry[...] = jnp.zeros_like(carry)

    x = x_ref[0]
    if mixer_proj:
        x = x + _dot(a_ref[0], wmix_ref[...])
    h = _rms(x, g_ref[...]).astype(BF16)

    def conv(part, col):
        cols = slice(col, col + FFN_CF)
        u = _dot(h, win_ref[:, cols])
        ubuf[part, 0:halo, :] = carry[:, cols]
        ubuf[part, halo:halo + tm, :] = u
        carry[:, cols] = u[tm - halo:, :]
        u1 = ubuf[part, halo - 1:halo - 1 + tm, :]
        u2 = ubuf[part, halo - 2:halo - 2 + tm, :]
        return (u * cw_ref[2:3, cols] + u1 * cw_ref[1:2, cols] + u2 * cw_ref[0:1, cols]
                + cb_ref[:, cols])

    for c in range(D_FF // FFN_CF):
        gate = conv(0, c * FFN_CF)
        up = conv(1, D_FF + c * FFN_CF)
        act = 0.5 * gate * (1.0 + lax.erf(gate * (2.0 ** -0.5)))
        gbuf[:, c * FFN_CF:(c + 1) * FFN_CF] = (act * up).astype(BF16)

    y = x + _dot(gbuf[...], wout_ref[...])
    if final_norm:
        y = _rms(y, fg_ref[...])
    o_ref[0] = y


def _ffn(x3, g, w_in, conv_w, conv_b, w_out, final_gain, final_norm, mixer=None):
    b, s, d = x3.shape
    f2 = w_in.shape[1]
    whole = pl.BlockSpec(memory_space=pltpu.VMEM)
    tile = lambda n: pl.BlockSpec((1, FFN_TM, n), lambda bi, t: (bi, t, 0))
    mix_specs = [tile(mixer[0].shape[-1]), whole] if mixer is not None else []
    mix_args = list(mixer) if mixer is not None else []
    return pl.pallas_call(
        functools.partial(_ffn_kernel, final_norm=final_norm, mixer_proj=mixer is not None),
        grid=(b, s // FFN_TM),
        in_specs=[tile(d)] + mix_specs + [whole, whole, whole, whole, whole, whole],
        out_specs=pl.BlockSpec((1, FFN_TM, d), lambda bi, t: (bi, t, 0)),
        out_shape=jax.ShapeDtypeStruct((b, s, d), F32),
        scratch_shapes=[pltpu.VMEM((2, FFN_TM + SUBLANES, FFN_CF), F32),
                        pltpu.VMEM((SUBLANES, f2), F32),
                        pltpu.VMEM((FFN_TM, D_FF), BF16)],
        compiler_params=_params(("arbitrary", "arbitrary")),
        name="ffn_final" if final_norm else "ffn",
    )(x3, *mix_args, g, w_in, conv_w, conv_b, w_out, final_gain)


def _proj1_kernel(x_ref, g_ref, wq_ref, wk_ref, wvt_ref, q_ref, k_ref, vt_ref):
    h = _rms(x_ref[0], g_ref[...]).astype(BF16)
    n = k_ref.shape[-1]
    for c in range(0, n, PROJ_TN):
        cols = slice(c, c + PROJ_TN)
        q_ref[0, 0, cols, :] = (_dot_nt(wq_ref[cols, :], h) * ATT_QSCALE).astype(BF16)
        k_ref[0, :, cols] = _dot(h, wk_ref[:, cols]).astype(BF16)
        vt_ref[0, 0, cols, :] = _dot_nt(wvt_ref[cols, :], h).astype(BF16)


def _proj1(x3, g, wq, wk, wvt):
    b, s, d = x3.shape
    n = wq.shape[1]
    t = ATT_T
    wspec = pl.BlockSpec((d, n), lambda bi, i: (0, 0))
    return pl.pallas_call(
        _proj1_kernel,
        grid=(b, s // t),
        in_specs=[pl.BlockSpec((1, t, d), lambda bi, i: (bi, i, 0)),
                  pl.BlockSpec((1, d), lambda bi, i: (0, 0)),
                  pl.BlockSpec((n, d), lambda bi, i: (0, 0)), wspec,
                  pl.BlockSpec((n, d), lambda bi, i: (0, 0))],
        out_specs=[pl.BlockSpec((1, 1, n, t), lambda bi, i: (bi, i, 0, 0)),
                   pl.BlockSpec((1, t, n), lambda bi, i: (bi, i, 0)),
                   pl.BlockSpec((1, 1, n, t), lambda bi, i: (bi, i, 0, 0))],
        out_shape=[jax.ShapeDtypeStruct((b, s // t, n, t), BF16),
                   jax.ShapeDtypeStruct((b, s, n), BF16),
                   jax.ShapeDtypeStruct((b, s // t, n, t), BF16)],
        compiler_params=_params(("arbitrary", "arbitrary")),
        name="proj1",
    )(x3, g, wq, wk, wvt)


def _diff_kernel(q_ref, k_ref, vt_ref, f_ref, lam_ref, sub_ref, o_ref,
                 qm_ref, km_ref, m_ref, l_ref, l8_ref, acc_ref, nb_ref, p_ref, *, lam_init):
    i = pl.program_id(2)
    t = ATT_T
    e = DIFF_HEAD_DIM
    half = lax.broadcasted_iota(jnp.int32, (SUBLANES, 2 * e), 1) // e
    sel = (lax.broadcasted_iota(jnp.int32, (SUBLANES, 2 * e), 0) == half).astype(BF16)

    @pl.when(i == 0)
    def _():
        for o in range(ATT_NEAR):
            g = jnp.broadcast_to(f_ref[0, 0:1, o * t:(o + 2) * t], (t, 2 * t))
            nb_ref[o] = pltpu.roll(g, 0, 1, stride=1, stride_axis=0)[:, t:]
        nb_ref[ATT_NEAR] = jnp.zeros((t, t), F32)
        kmax = jnp.zeros((SUBLANES, t), F32)
        for c in range(k_ref.shape[1] // t):
            kk = k_ref[0, c * t:(c + 1) * t, :].astype(F32)
            kmax = jnp.maximum(kmax, _dot_nt(sel, (kk * kk).astype(BF16)))
        for a in range(2):
            km_ref[a] = jnp.broadcast_to(jnp.max(kmax[a:a + 1, :], axis=1, keepdims=True), (1, t))

    qt = q_ref[0, 0]
    row = lax.broadcasted_iota(jnp.int32, qt.shape, 0)
    zero = jnp.zeros_like(qt)
    qm_ref[0] = jnp.where(row < e, qt, zero)
    qm_ref[1] = jnp.where(row >= e, qt, zero)
    qf = qt.astype(F32)
    qq = qf * qf
    qn2 = [jnp.sum(qq[a * e:(a + 1) * e], axis=0, keepdims=True) for a in range(2)]
    bias_max = f_ref[0, 1:2, 0:t]
    n_far = jnp.maximum(i - (ATT_NEAR - 1), 0)

    l8_ref[...] = jnp.zeros_like(l8_ref)
    acc_ref[...] = jnp.zeros_like(acc_ref)
    shift = [jnp.sqrt(qn2[a] * km_ref[a]) * ATT_BOUND_SLACK + bias_max for a in range(2)]

    def single_tile(j):
        kb = k_ref[0, pl.ds(pl.multiple_of(j * t, t), t), :]
        for a in range(2):
            s = _dot(kb, qm_ref[a]) + nb_ref[jnp.minimum(i - j, ATT_NEAR)]
            p = jnp.exp2(s - shift[a])
            l8_ref[a] += jnp.sum(p.reshape(t // SUBLANES, SUBLANES, t), axis=0)
            acc_ref[a] += _dot(vt_ref[0, j], p.astype(BF16))

    def probs(u, biased):
        slot = u % 2
        for a in range(2):
            l8 = None
            for tt in range(2):
                j = 2 * u + tt
                kb = k_ref[0, pl.ds(pl.multiple_of(j * t, t), t), :]
                s = _dot(kb, qm_ref[a])
                if biased:
                    s = s + nb_ref[jnp.minimum(i - j, ATT_NEAR)]
                p = jnp.exp2(s - shift[a])
                part = jnp.sum(p.reshape(t // SUBLANES, SUBLANES, t), axis=0)
                l8 = part if l8 is None else l8 + part
                p_ref[slot, a, tt] = p.astype(BF16)
            l8_ref[a] += l8

    def values(u):
        slot = u % 2
        for a in range(2):
            acc_ref[a] += (_dot(vt_ref[0, 2 * u], p_ref[slot, a, 0])
                           + _dot(vt_ref[0, 2 * u + 1], p_ref[slot, a, 1]))

    def far_step(u, carry):
        values(u - 1)
        probs(u, False)
        return carry

    def near_step(u, carry):
        values(u - 1)
        probs(u, True)
        return carry

    n = i + 1
    pairs = n // 2
    far_pairs = n_far // 2

    @pl.when(far_pairs > 0)
    def _():
        probs(0, False)

    @pl.when(jnp.logical_and(far_pairs == 0, pairs > 0))
    def _():
        probs(0, True)

    lax.fori_loop(1, far_pairs, far_step, 0)
    lax.fori_loop(jnp.maximum(far_pairs, 1), pairs, near_step, 0)

    odd = n % 2 == 1

    @pl.when(jnp.logical_and(pairs > 0, jnp.logical_not(odd)))
    def _():
        values(pairs - 1)

    @pl.when(jnp.logical_and(pairs > 0, odd))
    def _():
        values(pairs - 1)
        single_tile(n - 1)

    @pl.when(pairs == 0)
    def _():
        single_tile(0)

    def finalize():
        lam = (jnp.exp(jnp.sum(lam_ref[0:1, :] * lam_ref[1:2, :], axis=-1, keepdims=True))
               - jnp.exp(jnp.sum(lam_ref[2:3, :] * lam_ref[3:4, :], axis=-1, keepdims=True))
               + lam_init)
        ot = acc_ref[0] * (1.0 / l_ref[0]) - acc_ref[1] * (lam / l_ref[1])
        ot = ot * lax.rsqrt(jnp.mean(ot * ot, axis=0, keepdims=True) + EPS)
        o_ref[0] = (ot.T * sub_ref[...] * (1.0 - lam_init)).astype(o_ref.dtype)

    for a in range(2):
        l_ref[a] = jnp.sum(l8_ref[a], axis=0, keepdims=True)
    l_min = jnp.min(jnp.minimum(l_ref[0], l_ref[1]))
    l_max = jnp.max(jnp.maximum(l_ref[0], l_ref[1]))
    in_range = jnp.logical_and(l_min >= ATT_MIN_DENOM, l_max <= ATT_MAX_DENOM)
    finalize()

    @pl.when(jnp.logical_not(in_range))
    def _():
        m_ref[...] = jnp.full_like(m_ref, MASKED)
        l_ref[...] = jnp.zeros_like(l_ref)
        acc_ref[...] = jnp.zeros_like(acc_ref)

        def online_tile(j, bias):
            kb = k_ref[0, pl.ds(pl.multiple_of(j * t, t), t), :]
            vtb = vt_ref[0, j]
            for a in range(2):
                s = _dot(kb, qm_ref[a])
                if bias is not None:
                    s = s + bias
                m_old = m_ref[a]
                m_new = jnp.maximum(m_old, jnp.max(s, axis=0, keepdims=True))
                alpha = jnp.exp2(m_old - m_new)
                p = jnp.exp2(s - m_new)
                l_ref[a] = alpha * l_ref[a] + jnp.sum(p, axis=0, keepdims=True)
                acc_ref[a] = alpha * acc_ref[a] + _dot(vtb, p.astype(BF16))
                m_ref[a] = m_new

        def online_far(j, carry):
            online_tile(j, None)
            return carry

        def online_near(j, carry):
            online_tile(j, nb_ref[i - j])
            return carry

        lax.fori_loop(0, n_far, online_far, 0)
        lax.fori_loop(n_far, i + 1, online_near, 0)
        finalize()


def _diff_bias(rel_bias, s):
    t = ATT_T
    n = ATT_NEAR * t
    assert n >= BIAS_FLAT_FROM + t - 1 and s >= n
    dist_bias = rel_bias[_t5_bucket(jnp.arange(n))].astype(F32)
    far = rel_bias[REL_BUCKETS - 1].astype(F32)
    near = (dist_bias - far) * LOG2E
    f = jnp.concatenate([jnp.full((t, far.shape[0]), MASKED, F32), near])
    top = jnp.broadcast_to(jnp.maximum(jnp.max(near, axis=0), 0.0), f.shape)
    return jnp.stack([f.T, top.T], axis=1)


def _diff_attention(q, k, vt, rel_bias, lam4, subln, layer_idx):
    b, s, n = k.shape
    t = ATT_T
    w = 2 * DIFF_HEAD_DIM
    lam_init = 0.8 - 0.6 * math.exp(-0.3 * layer_idx)
    return pl.pallas_call(
        functools.partial(_diff_kernel, lam_init=lam_init),
        grid=(b, DIFF_HEADS, s // t),
        in_specs=[pl.BlockSpec((1, 1, w, t), lambda bi, h, i: (bi, i, h, 0)),
                  pl.BlockSpec((1, s, w), lambda bi, h, i: (bi, 0, h)),
                  pl.BlockSpec((1, s // t, w, t), lambda bi, h, i: (bi, 0, h, 0)),
                  pl.BlockSpec((1, 2, (ATT_NEAR + 1) * t), lambda bi, h, i: (h, 0, 0)),
                  pl.BlockSpec((4, DIFF_HEAD_DIM), lambda bi, h, i: (0, 0)),
                  pl.BlockSpec((1, w), lambda bi, h, i: (0, 0))],
        out_specs=pl.BlockSpec((1, t, w), lambda bi, h, i: (bi, i, h)),
        out_shape=jax.ShapeDtypeStruct((b, s, n), BF16),
        scratch_shapes=[pltpu.VMEM((2, w, t), BF16),
                        pltpu.VMEM((2, 1, t), F32),
                        pltpu.VMEM((2, 1, t), F32),
                        pltpu.VMEM((2, 1, t), F32),
                        pltpu.VMEM((2, SUBLANES, t), F32),
                        pltpu.VMEM((2, w, t), F32),
                        pltpu.VMEM((ATT_NEAR + 1, t, t), F32),
                        pltpu.VMEM((2, 2, 2, t, t), BF16)],
        compiler_params=_params(("arbitrary", "arbitrary", "arbitrary")),
        name="diff_attention",
    )(q, k, vt, _diff_bias(rel_bias, s), lam4, subln)


def kernel(x, rel_bias, norm_mix, norm_ffn, norm_final, w_in_ab, ret_gn, w_out_ab, w_in_c,
           lam_q1, lam_k1, lam_q2, lam_k2, diff_subln, w_out_c, w_ffn_in, conv_w, conv_b,
           w_ffn_out):
    b, s, d = x.shape
    depth = norm_mix.shape[0]
    assert depth == 2 and s >= ATT_NEAR * ATT_T and conv_w.shape[1] == CONV_WIDTH == 3
    assert all(s % n == 0 for n in (RET_ROWS, FFN_TM, PROJ_TM, ATT_T, max(DIL_DILATIONS) * DIL_BLOCK))
    assert DIL_DILATIONS[0] == 1 and all(hi % lo == 0 for lo, hi in zip(DIL_DILATIONS, DIL_DILATIONS[1:]))
    row = lambda v: v.reshape(1, -1)

    proj = _proj0(x, row(norm_mix[0]), w_in_ab[0].astype(BF16))
    ret = _retention(proj[0], row(ret_gn[0]), b, s)
    outs, lses = zip(*[_dilated_pattern(dq, rel_bias, wdw, dil)
                       for dq, (wdw, dil) in zip(proj[1:], DIL_PATTERNS)])
    x3 = _mix0(x, ret, outs, lses, w_out_ab[0].astype(BF16))
    x3 = _ffn(x3, row(norm_ffn[0]), w_ffn_in[0].astype(BF16), conv_w[0],
              row(conv_b[0]), w_ffn_out[0].astype(BF16), row(norm_final), False)

    nqk = DIFF_HEADS * 2 * DIFF_HEAD_DIM
    wc = w_in_c[0].astype(BF16)
    q, k, vt = _proj1(x3, row(norm_mix[1]), wc[:, :nqk].T, wc[:, nqk:2 * nqk], wc[:, 2 * nqk:].T)
    lam4 = jnp.stack([lam_q1[0], lam_k1[0], lam_q2[0], lam_k2[0]]).astype(F32)
    a = _diff_attention(q, k, vt, rel_bias, lam4, row(diff_subln[0]), 1)
    return _ffn(x3, row(norm_ffn[1]), w_ffn_in[1].astype(BF16), conv_w[1], row(conv_b[1]),
                w_ffn_out[1].astype(BF16), row(norm_final), True,
                mixer=(a, w_out_c[0].astype(BF16)))
```

```python
import functools
import math

import jax
import jax.numpy as jnp
from jax import lax
from jax.experimental import pallas as pl
from jax.experimental.pallas import tpu as pltpu

F32 = jnp.float32
BF16 = jnp.bfloat16

EPS = 1e-6
MASKED = -1e30

RET_HEADS = 4
RET_DIM = 128
RET_CHUNK = 128
ROPE_BASE = 10000.0
DIL_HEADS = 8
DIL_HEAD_DIM = 64
DIL_PATTERNS = ((128, 1), (512, 4), (2048, 16))
DIL_BLOCK = 128
DIFF_HEADS = 8
DIFF_HEAD_DIM = 64
REL_BUCKETS = 32
REL_MAX_DISTANCE = 2048
D_FF = 2816
CONV_WIDTH = 3
DIL_W = DIL_HEADS * DIL_HEAD_DIM
RET_W = 4 * RET_HEADS * RET_DIM
DIL_DILATIONS = tuple(d for _, d in DIL_PATTERNS)

LANES = 128
SUBLANES = 8
VMEM_BYTES_V7X = 64 * 1024 * 1024
VMEM_LIMIT = VMEM_BYTES_V7X - 8 * 1024 * 1024

PROJ_TM = 512
MIX_TM = 1024
PROJ_TN = 512
RET_ROWS = 2048
DIL_QBLOCKS = 8
FFN_TM = 1024
FFN_CF = 256
ATT_T = 512
ATT_BOUND_SLACK = 1.0 + 2.0 ** -5
ATT_MIN_DENOM = 2.0 ** -60
ATT_MAX_DENOM = 2.0 ** 100
_MAX_EXACT = REL_BUCKETS // 2
BIAS_FLAT_FROM = 8 + math.ceil(_MAX_EXACT * (REL_MAX_DISTANCE / _MAX_EXACT)
                               ** ((REL_BUCKETS - 1 - _MAX_EXACT) / (REL_BUCKETS - _MAX_EXACT)))
ATT_NEAR = -(-(BIAS_FLAT_FROM + ATT_T - 1) // ATT_T)
LOG2E = math.log2(math.e)
ATT_QSCALE = DIFF_HEAD_DIM ** -0.5 * LOG2E


def _params(sem):
    return pltpu.CompilerParams(dimension_semantics=sem, vmem_limit_bytes=VMEM_LIMIT)


def _rms(x, g):
    return x * lax.rsqrt(jnp.mean(x * x, axis=-1, keepdims=True) + EPS) * g


def _dot(a, b):
    return jnp.dot(a, b, preferred_element_type=F32)


def _dot_nt(a, b):
    return lax.dot_general(a, b, (((1,), (1,)), ((), ())), preferred_element_type=F32)


def _dot_tn(a, b):
    return lax.dot_general(a, b, (((0,), (0,)), ((), ())), preferred_element_type=F32)


def _proj0_kernel(x_ref, g_ref, w_ref, a_ref, *rest):
    d_refs, dbuf = rest[:-1], rest[-1]
    tm = x_ref.shape[1]
    h = _rms(x_ref[0], g_ref[...]).astype(BF16)
    na = a_ref.shape[-1]
    nd = w_ref.shape[1] - na
    for c in range(0, nd, PROJ_TN):
        res = _dot(h, w_ref[:, na + c:na + c + PROJ_TN])
        for s in range(c // LANES, (c + PROJ_TN) // LANES):
            cols = slice(s * LANES, (s + 1) * LANES)
            dbuf[0, s] = res[:, (s * LANES - c):(s * LANES - c) + LANES]
            src, d_prev = 0, 1
            for dref, d in zip(d_refs, DIL_DILATIONS):
                if d == 1:
                    dref[0, 0, :, cols] = dbuf[0, s].astype(BF16)
                    continue
                step, rows = d // d_prev, tm // d
                dst = 1 - src
                for rp in range(d_prev):
                    for q in range(step):
                        piece = dbuf[src, s, pl.ds(rp * (tm // d_prev) + q, rows, stride=step), :]
                        r = rp + q * d_prev
                        if d != DIL_DILATIONS[-1]:
                            dbuf[dst, s, r * rows:(r + 1) * rows, :] = piece
                        dref[0, r, :, cols] = piece.astype(BF16)
                src, d_prev = dst, d
    for c in range(0, na, PROJ_TN):
        a_ref[0, :, c:c + PROJ_TN] = _dot(h, w_ref[:, c:c + PROJ_TN]).astype(BF16)


def _proj0(x3, g, w):
    b, s, dm = x3.shape
    n = w.shape[1]
    nd = n - RET_W
    tm = PROJ_TM
    d_specs = [pl.BlockSpec((1, d, tm // d, nd), lambda bi, i: (bi, 0, i, 0)) for d in DIL_DILATIONS]
    d_shapes = [jax.ShapeDtypeStruct((b, d, s // d, nd), BF16) for d in DIL_DILATIONS]
    return pl.pallas_call(
        _proj0_kernel,
        grid=(b, s // tm),
        in_specs=[pl.BlockSpec((1, tm, dm), lambda bi, i: (bi, i, 0)),
                  pl.BlockSpec((1, dm), lambda bi, i: (0, 0)),
                  pl.BlockSpec((dm, n), lambda bi, i: (0, 0))],
        out_specs=[pl.BlockSpec((1, tm, RET_W), lambda bi, i: (bi, i, 0))] + d_specs,
        out_shape=[jax.ShapeDtypeStruct((b, s, RET_W), BF16)] + d_shapes,
        scratch_shapes=[pltpu.VMEM((2, nd // LANES, tm, LANES), F32)],
        compiler_params=_params(("arbitrary", "arbitrary")),
        name="proj0",
    )(x3, g, w)


def _ret_kernel(q_ref, k_ref, v_ref, gate_ref, start_ref, off_ref, tab_ref, gn_ref, o_ref,
                state_ref, qr_ref, kr_ref, qx_ref, kv_ref, st_ref):
    @pl.when(pl.program_id(2) == 0)
    def _():
        state_ref[...] = jnp.zeros_like(state_ref)

    decay = tab_ref[0, 0]
    xi = tab_ref[0, 1]
    zeta = tab_ref[0, 2]
    g_chunk = tab_ref[0, 3]
    gain = gn_ref[...]
    half = RET_DIM // 2
    n_chunks = q_ref.shape[1] // RET_CHUNK
    for c in range(n_chunks):
        rows = pl.ds(c * RET_CHUNK, RET_CHUNK)
        ca, sa = start_ref[c, 0:1, :], start_ref[c, 1:2, :]
        cos = ca * off_ref[0] - sa * off_ref[1]
        sin = sa * off_ref[2] + ca * off_ref[3]
        q = q_ref[0, rows, :].astype(F32)
        k = k_ref[0, rows, :].astype(F32)
        q = q * cos + pltpu.roll(q, half, 1) * sin
        k = k * cos + pltpu.roll(k, half, 1) * sin
        qr_ref[c] = q.astype(BF16)
        kr_ref[c] = k.astype(BF16)
        qx_ref[c] = (q * xi).astype(BF16)
        kv_ref[c] = _dot_tn((k * zeta).astype(BF16), v_ref[0, rows, :])
    state = state_ref[...]
    for c in range(n_chunks):
        st_ref[c] = state.astype(BF16)
        state = g_chunk * state + kv_ref[c]
    state_ref[...] = state
    for c in range(n_chunks):
        rows = pl.ds(c * RET_CHUNK, RET_CHUNK)
        v = v_ref[0, rows, :]
        intra = _dot_nt(qr_ref[c], kr_ref[c]) * decay
        y = _dot(intra.astype(BF16), v) + _dot(qx_ref[c], st_ref[c])
        mu = jnp.mean(y, axis=-1, keepdims=True)
        yc = y - mu
        var = jnp.mean(yc * yc, axis=-1, keepdims=True)
        yn = yc * lax.rsqrt(var + EPS) * gain
        gate = gate_ref[0, rows, :].astype(F32)
        o_ref[0, rows, :] = (gate * jax.nn.sigmoid(gate) * yn).astype(o_ref.dtype)


def _retention_tables(s):
    c, e = RET_CHUNK, RET_DIM
    inv = ROPE_BASE ** (-jnp.arange(0, e, 2, dtype=F32) / e)
    inv = jnp.concatenate([inv, inv])
    sign = jnp.where(jnp.arange(e) < e // 2, -1.0, 1.0).astype(F32)
    a = (jnp.arange(s // c, dtype=F32) * c)[:, None] * inv[None, :]
    b_ = jnp.arange(c, dtype=F32)[:, None] * inv[None, :]
    start = jnp.stack([jnp.cos(a), jnp.sin(a)], axis=1)
    off = jnp.stack([jnp.cos(b_), jnp.sin(b_), jnp.cos(b_) * sign, jnp.sin(b_) * sign])
    log_g = jnp.log1p(-jnp.exp2(-5.0 - jnp.arange(RET_HEADS, dtype=F32)))
    pos = jnp.arange(c, dtype=F32)
    rel = pos[:, None] - pos[None, :]
    causal = rel >= 0
    scale = e ** -0.5
    decay = jnp.where(causal, jnp.exp(jnp.where(causal, rel, 0.0)[None] * log_g[:, None, None]), 0.0)
    zeta = jnp.exp((c - 1 - pos)[None, :] * log_g[:, None])
    xi = jnp.exp((pos + 1)[None, :] * log_g[:, None])
    g_chunk = jnp.exp(c * log_g)
    tab = jnp.stack([
        decay * scale,
        jnp.broadcast_to(xi[:, :, None], (RET_HEADS, c, e)),
        jnp.broadcast_to(zeta[:, :, None] * scale, (RET_HEADS, c, e)),
        jnp.broadcast_to(g_chunk[:, None, None], (RET_HEADS, e, e)),
    ], axis=1)
    return start, off, tab


def _retention(p0, gn, b, s):
    start, off, tab = _retention_tables(s)
    e = RET_DIM
    nc = RET_ROWS // RET_CHUNK
    col = lambda off: (lambda bi, h, t: (bi, t, off + h))
    return pl.pallas_call(
        _ret_kernel,
        grid=(b, RET_HEADS, s // RET_ROWS),
        in_specs=[pl.BlockSpec((1, RET_ROWS, e), col(0)),
                  pl.BlockSpec((1, RET_ROWS, e), col(RET_HEADS)),
                  pl.BlockSpec((1, RET_ROWS, e), col(2 * RET_HEADS)),
                  pl.BlockSpec((1, RET_ROWS, e), col(3 * RET_HEADS)),
                  pl.BlockSpec((nc, 2, e), lambda bi, h, t: (t, 0, 0)),
                  pl.BlockSpec((4, RET_CHUNK, e), lambda bi, h, t: (0, 0, 0)),
                  pl.BlockSpec((1, 4, e, e), lambda bi, h, t: (h, 0, 0, 0)),
                  pl.BlockSpec((1, e), lambda bi, h, t: (0, h))],
        out_specs=pl.BlockSpec((1, RET_ROWS, e), lambda bi, h, t: (bi, t, h)),
        out_shape=jax.ShapeDtypeStruct((b, s, RET_HEADS * e), BF16),
        scratch_shapes=[pltpu.VMEM((e, e), F32),
                        pltpu.VMEM((nc, RET_CHUNK, e), BF16),
                        pltpu.VMEM((nc, RET_CHUNK, e), BF16),
                        pltpu.VMEM((nc, RET_CHUNK, e), BF16),
                        pltpu.VMEM((nc, e, e), F32),
                        pltpu.VMEM((nc, e, e), BF16)],
        compiler_params=_params(("arbitrary", "arbitrary", "arbitrary")),
        name="retention",
    )(p0, p0, p0, p0, start, off, tab, gn)


def _t5_bucket(dist):
    max_exact = REL_BUCKETS // 2
    d = jnp.maximum(dist.astype(F32), 1.0)
    large = max_exact + (jnp.log(d / max_exact) / math.log(REL_MAX_DISTANCE / max_exact)
                         * (REL_BUCKETS - max_exact))
    large = jnp.clip(large.astype(jnp.int32), max_exact, REL_BUCKETS - 1)
    return jnp.where(dist < max_exact, dist, large)


def _dil_kernel(q_ref, kp_ref, kc_ref, vp_ref, vc_ref, g_ref, o_ref, lse_ref, bias_ref,
                s_ref, p_ref):
    first = pl.program_id(2) == 0
    blk, e = DIL_BLOCK, DIL_HEAD_DIM

    @pl.when(jnp.logical_and(jnp.logical_and(pl.program_id(0) == 0, pl.program_id(1) == 0), first))
    def _():
        for h in range(DIL_HEADS):
            gb = jnp.broadcast_to(g_ref[h:h + 1, :], (blk, 4 * blk))
            bias_ref[h] = pltpu.roll(gb, 0, 1, stride=1, stride_axis=0)[:, :2 * blk]

    q_all = q_ref[0, 0] * (e ** -0.5)
    k_all = jnp.concatenate([kp_ref[0, 0], kc_ref[0, 0]], axis=0)
    v_all = jnp.concatenate([vp_ref[0, 0], vc_ref[0, 0]], axis=0)
    in_prev = lax.broadcasted_iota(jnp.int32, (blk, 2 * blk), 1) < blk
    no_prev = jnp.logical_and(first, in_prev)
    lane = lax.broadcasted_iota(jnp.int32, (blk, LANES), 1)
    for qb in range(q_all.shape[0] // blk):
        q = q_all[qb * blk:(qb + 1) * blk]
        kcat = k_all[qb * blk:(qb + 2) * blk]
        vcat = v_all[qb * blk:(qb + 2) * blk]
        lse_all = jnp.zeros((blk, LANES), F32)
        for h in range(DIL_HEADS):
            pair = slice((h // 2) * LANES, (h // 2 + 1) * LANES)
            qp = q[:, pair]
            mine = (lax.broadcasted_iota(jnp.int32, qp.shape, 1) // e) == h % 2
            s = _dot_nt(jnp.where(mine, qp, jnp.zeros_like(qp)), kcat[:, pair]) + bias_ref[h]
            if qb == 0:
                s = jnp.where(no_prev, MASKED, s)
            s_ref[qb, h] = s
        inv_l = []
        for h in range(DIL_HEADS):
            s = s_ref[qb, h]
            m = jnp.max(s, axis=-1, keepdims=True)
            p = jnp.exp(s - m)
            l = jnp.sum(p, axis=-1, keepdims=True)
            p_ref[qb, h] = p.astype(BF16)
            inv_l.append(1.0 / l)
            lse_all = jnp.where(lane == h, m + jnp.log(l), lse_all)
        for g in range(DIL_HEADS // 2):
            pair = slice(g * LANES, (g + 1) * LANES)
            vp = vcat[:, pair]
            upper = lax.broadcasted_iota(jnp.int32, vp.shape, 1) >= e
            zero = jnp.zeros_like(vp)
            o_pair = (_dot(p_ref[qb, 2 * g], jnp.where(upper, zero, vp))
                      + _dot(p_ref[qb, 2 * g + 1], jnp.where(upper, vp, zero)))
            scale = jnp.where(lane >= e, inv_l[2 * g + 1], inv_l[2 * g])
            o_ref[0, 0, qb * blk:(qb + 1) * blk, pair] = (o_pair * scale).astype(o_ref.dtype)
        lse_ref[0, 0, qb * blk:(qb + 1) * blk, :] = lse_all


def _dil_bias_vector(rel_bias, window, dil):
    blk = DIL_BLOCK
    span = window // dil
    r = blk - jnp.arange(4 * blk)
    band = (r >= 0) & (r <= span)
    bias = rel_bias[_t5_bucket(jnp.maximum(r, 0) * dil)].astype(F32)
    return jnp.where(band[:, None], bias, MASKED).T


def _dilated_pattern(dq, rel_bias, window, dil):
    b, d, sd, _ = dq.shape
    blk = DIL_BLOCK
    nq = math.gcd(DIL_QBLOCKS, sd // DIL_BLOCK)
    rows = nq * blk
    assert d == dil and window // dil == blk and sd % rows == 0
    cur = lambda c: (lambda bi, r, n: (bi, r, n, c))
    prev = lambda c: (lambda bi, r, n: (bi, r, jnp.maximum(n * nq - 1, 0), c))
    spec = lambda im: pl.BlockSpec((1, 1, rows, DIL_W), im)
    pspec = lambda im: pl.BlockSpec((1, 1, blk, DIL_W), im)
    return pl.pallas_call(
        _dil_kernel,
        grid=(b, dil, sd // rows),
        in_specs=[spec(cur(0)), pspec(prev(1)), spec(cur(1)), pspec(prev(2)), spec(cur(2)),
                  pl.BlockSpec((DIL_HEADS, 4 * blk), lambda bi, r, n: (0, 0))],
        out_specs=[pl.BlockSpec((1, 1, rows, DIL_W), lambda bi, r, n: (bi, r, n, 0)),
                   pl.BlockSpec((1, 1, rows, LANES), lambda bi, r, n: (bi, r, n, 0))],
        out_shape=[jax.ShapeDtypeStruct((b, dil, sd, DIL_W), BF16),
                   jax.ShapeDtypeStruct((b, dil, sd, LANES), F32)],
        scratch_shapes=[pltpu.VMEM((DIL_HEADS, blk, 2 * blk), F32),
                        pltpu.VMEM((nq, DIL_HEADS, blk, 2 * blk), F32),
                        pltpu.VMEM((nq, DIL_HEADS, blk, 2 * blk), BF16)],
        compiler_params=_params(("arbitrary", "arbitrary", "arbitrary")),
        name=f"dilated_d{dil}",
    )(dq, dq, dq, dq, dq, _dil_bias_vector(rel_bias, window, dil))


def _mix0_kernel(x_ref, ret_ref, *rest):
    np_ = len(DIL_DILATIONS)
    o_refs, l_refs = rest[:np_], rest[np_:2 * np_]
    w_ref, out_ref, obuf, lbuf, dbuf = rest[2 * np_:]
    tm = x_ref.shape[1]
    nslab = DIL_W // LANES
    for pi, d in enumerate(DIL_DILATIONS):
        rows = tm // d
        for r in range(d):
            dst = pl.ds(r, rows, stride=d) if d > 1 else pl.ds(0, tm)
            lbuf[pi, dst, :] = l_refs[pi][0, r]
            for s in range(nslab):
                obuf[pi, s, dst, :] = o_refs[pi][0, r, :, s * LANES:(s + 1) * LANES].astype(F32)
    lses = [lbuf[pi] for pi in range(np_)]
    m = functools.reduce(jnp.maximum, lses)
    es = [jnp.exp(l - m) for l in lses]
    inv = 1.0 / functools.reduce(lambda a, b: a + b, es)
    row = lax.broadcasted_iota(jnp.int32, (LANES, DIL_W), 0)
    col = lax.broadcasted_iota(jnp.int32, (LANES, DIL_W), 1)
    spread = (col // DIL_HEAD_DIM == row).astype(BF16)
    spread2 = jnp.concatenate([spread, spread], axis=0)
    wide = []
    for ex in es:
        wgt = ex * inv
        hi = wgt.astype(BF16)
        lo = (wgt - hi.astype(F32)).astype(BF16)
        wide.append(_dot(jnp.concatenate([hi, lo], axis=1), spread2))
    for s in range(nslab):
        cols = slice(s * LANES, (s + 1) * LANES)
        acc = wide[0][:, cols] * obuf[0, s]
        for pi in range(1, np_):
            acc = acc + wide[pi][:, cols] * obuf[pi, s]
        dbuf[:, cols] = acc.astype(BF16)
    na = ret_ref.shape[-1]
    y = _dot(ret_ref[0], w_ref[:na, :]) + _dot(dbuf[...], w_ref[na:, :])
    out_ref[0] = x_ref[0] + y


def _mix0(x3, ret, outs, lses, w):
    b, s, dm = x3.shape
    na = ret.shape[-1]
    tm = MIX_TM
    o_specs = [pl.BlockSpec((1, d, tm // d, DIL_W), lambda bi, i: (bi, 0, i, 0)) for d in DIL_DILATIONS]
    l_specs = [pl.BlockSpec((1, d, tm // d, LANES), lambda bi, i: (bi, 0, i, 0)) for d in DIL_DILATIONS]
    np_ = len(DIL_DILATIONS)
    return pl.pallas_call(
        _mix0_kernel,
        grid=(b, s // tm),
        in_specs=[pl.BlockSpec((1, tm, dm), lambda bi, i: (bi, i, 0)),
                  pl.BlockSpec((1, tm, na), lambda bi, i: (bi, i, 0))] + o_specs + l_specs
                 + [pl.BlockSpec((na + DIL_W, dm), lambda bi, i: (0, 0))],
        out_specs=pl.BlockSpec((1, tm, dm), lambda bi, i: (bi, i, 0)),
        out_shape=jax.ShapeDtypeStruct((b, s, dm), F32),
        scratch_shapes=[pltpu.VMEM((np_, DIL_W // LANES, tm, LANES), F32),
                        pltpu.VMEM((np_, tm, LANES), F32),
                        pltpu.VMEM((tm, DIL_W), BF16)],
        compiler_params=_params(("arbitrary", "arbitrary")),
        name="mix0",
    )(x3, ret, *outs, *lses, w)


def _ffn_kernel(x_ref, *rest, final_norm, mixer_proj):
    if mixer_proj:
        a_ref, wmix_ref = rest[:2]
        rest = rest[2:]
    g_ref, win_ref, cw_ref, cb_ref, wout_ref, fg_ref, o_ref, ubuf, carry, gbuf = rest
    tm = x_ref.shape[1]
    halo = SUBLANES

    @pl.when(pl.program_id(1) == 0)
    def _():
        carry[...] = jnp.zeros_like(carry)

    x = x_ref[0]
    if mixer_proj:
        x = x + _dot(a_ref[0], wmix_ref[...])
    h = _rms(x, g_ref[...]).astype(BF16)

    def conv(part, col):
        cols = slice(col, col + FFN_CF)
        u = _dot(h, win_ref[:, cols])
        ubuf[part, 0:halo, :] = carry[:, cols]
        ubuf[part, halo:halo + tm, :] = u
        carry[:, cols] = u[tm - halo:, :]
        u1 = ubuf[part, halo - 1:halo - 1 + tm, :]
        u2 = ubuf[part, halo - 2:halo - 2 + tm, :]
        return (u * cw_ref[2:3, cols] + u1 * cw_ref[1:2, cols] + u2 * cw_ref[0:1, cols]
                + cb_ref[:, cols])

    for c in range(D_FF // FFN_CF):
        gate = conv(0, c * FFN_CF)
        up = conv(1, D_FF + c * FFN_CF)
        act = 0.5 * gate * (1.0 + lax.erf(gate * (2.0 ** -0.5)))
        gbuf[:, c * FFN_CF:(c + 1) * FFN_CF] = (act * up).astype(BF16)

    y = x + _dot(gbuf[...], wout_ref[...])
    if final_norm:
        y = _rms(y, fg_ref[...])
    o_ref[0] = y


def _ffn(x3, g, w_in, conv_w, conv_b, w_out, final_gain, final_norm, mixer=None):
    b, s, d = x3.shape
    f2 = w_in.shape[1]
    whole = pl.BlockSpec(memory_space=pltpu.VMEM)
    tile = lambda n: pl.BlockSpec((1, FFN_TM, n), lambda bi, t: (bi, t, 0))
    mix_specs = [tile(mixer[0].shape[-1]), whole] if mixer is not None else []
    mix_args = list(mixer) if mixer is not None else []
    return pl.pallas_call(
        functools.partial(_ffn_kernel, final_norm=final_norm, mixer_proj=mixer is not None),
        grid=(b, s // FFN_TM),
        in_specs=[tile(d)] + mix_specs + [whole, whole, whole, whole, whole, whole],
        out_specs=pl.BlockSpec((1, FFN_TM, d), lambda bi, t: (bi, t, 0)),
        out_shape=jax.ShapeDtypeStruct((b, s, d), F32),
        scratch_shapes=[pltpu.VMEM((2, FFN_TM + SUBLANES, FFN_CF), F32),
                        pltpu.VMEM((SUBLANES, f2), F32),
                        pltpu.VMEM((FFN_TM, D_FF), BF16)],
        compiler_params=_params(("arbitrary", "arbitrary")),
        name="ffn_final" if final_norm else "ffn",
    )(x3, *mix_args, g, w_in, conv_w, conv_b, w_out, final_gain)


def _proj1_kernel(x_ref, g_ref, wq_ref, wk_ref, wvt_ref, q_ref, k_ref, vt_ref):
    h = _rms(x_ref[0], g_ref[...]).astype(BF16)
    n = k_ref.shape[-1]
    for c in range(0, n, PROJ_TN):
        cols = slice(c, c + PROJ_TN)
        q_ref[0, 0, cols, :] = (_dot_nt(wq_ref[cols, :], h) * ATT_QSCALE).astype(BF16)
        k_ref[0, :, cols] = _dot(h, wk_ref[:, cols]).astype(BF16)
        vt_ref[0, 0, cols, :] = _dot_nt(wvt_ref[cols, :], h).astype(BF16)


def _proj1(x3, g, wq, wk, wvt):
    b, s, d = x3.shape
    n = wq.shape[1]
    t = ATT_T
    wspec = pl.BlockSpec((d, n), lambda bi, i: (0, 0))
    return pl.pallas_call(
        _proj1_kernel,
        grid=(b, s // t),
        in_specs=[pl.BlockSpec((1, t, d), lambda bi, i: (bi, i, 0)),
                  pl.BlockSpec((1, d), lambda bi, i: (0, 0)),
                  pl.BlockSpec((n, d), lambda bi, i: (0, 0)), wspec,
                  pl.BlockSpec((n, d), lambda bi, i: (0, 0))],
        out_specs=[pl.BlockSpec((1, 1, n, t), lambda bi, i: (bi, i, 0, 0)),
                   pl.BlockSpec((1, t, n), lambda bi, i: (bi, i, 0)),
                   pl.BlockSpec((1, 1, n, t), lambda bi, i: (bi, i, 0, 0))],
        out_shape=[jax.ShapeDtypeStruct((b, s // t, n, t), BF16),
                   jax.ShapeDtypeStruct((b, s, n), BF16),
                   jax.ShapeDtypeStruct((b, s // t, n, t), BF16)],
        compiler_params=_params(("arbitrary", "arbitrary")),
        name="proj1",
    )(x3, g, wq, wk, wvt)


def _diff_kernel(q_ref, k_ref, vt_ref, f_ref, lam_ref, sub_ref, o_ref,
                 qm_ref, km_ref, m_ref, l_ref, l8_ref, acc_ref, nb_ref, p_ref, lamb_ref, *, lam_init):
    i = pl.program_id(2)
    t = ATT_T
    e = DIFF_HEAD_DIM
    half = lax.broadcasted_iota(jnp.int32, (SUBLANES, 2 * e), 1) // e
    sel = (lax.broadcasted_iota(jnp.int32, (SUBLANES, 2 * e), 0) == half).astype(BF16)

    @pl.when(i == 0)
    def _():
        for o in range(ATT_NEAR):
            g = jnp.broadcast_to(f_ref[0, 0:1, o * t:(o + 2) * t], (t, 2 * t))
            nb_ref[o] = pltpu.roll(g, 0, 1, stride=1, stride_axis=0)[:, t:]
        nb_ref[ATT_NEAR] = jnp.zeros((t, t), F32)
        kmax = jnp.zeros((SUBLANES, t), F32)
        for c in range(k_ref.shape[1] // t):
            kk = k_ref[0, c * t:(c + 1) * t, :].astype(F32)
            kmax = jnp.maximum(kmax, _dot_nt(sel, (kk * kk).astype(BF16)))
        for a in range(2):
            km_ref[a] = jnp.broadcast_to(jnp.max(kmax[a:a + 1, :], axis=1, keepdims=True), (1, t))
        lam = (jnp.exp(jnp.sum(lam_ref[0:1, :] * lam_ref[1:2, :], axis=-1, keepdims=True))
               - jnp.exp(jnp.sum(lam_ref[2:3, :] * lam_ref[3:4, :], axis=-1, keepdims=True))
               + lam_init)
        lamb_ref[...] = jnp.broadcast_to(lam, (1, t))

    qt = q_ref[0, 0]
    row = lax.broadcasted_iota(jnp.int32, qt.shape, 0)
    zero = jnp.zeros_like(qt)
    qm_ref[0] = jnp.where(row < e, qt, zero)
    qm_ref[1] = jnp.where(row >= e, qt, zero)
    qf = qt.astype(F32)
    qq = qf * qf
    qn2 = [jnp.sum(qq[a * e:(a + 1) * e], axis=0, keepdims=True) for a in range(2)]
    bias_max = f_ref[0, 1:2, 0:t]
    n_far = jnp.maximum(i - (ATT_NEAR - 1), 0)

    l8_ref[...] = jnp.zeros_like(l8_ref)
    acc_ref[...] = jnp.zeros_like(acc_ref)
    shift = [jnp.sqrt(qn2[a] * km_ref[a]) * ATT_BOUND_SLACK + bias_max for a in range(2)]

    def single_tile(j):
        kb = k_ref[0, pl.ds(pl.multiple_of(j * t, t), t), :]
        for a in range(2):
            s = _dot(kb, qm_ref[a]) + nb_ref[jnp.minimum(i - j, ATT_NEAR)]
            p = jnp.exp2(s - shift[a])
            l8_ref[a] += jnp.sum(p.reshape(t // SUBLANES, SUBLANES, t), axis=0)
            acc_ref[a] += _dot(vt_ref[0, j], p.astype(BF16))

    def probs(u, biased):
        slot = u % 2
        for a in range(2):
            l8 = None
            for tt in range(2):
                j = 2 * u + tt
                kb = k_ref[0, pl.ds(pl.multiple_of(j * t, t), t), :]
                s = _dot(kb, qm_ref[a])
                if biased:
                    s = s + nb_ref[jnp.minimum(i - j, ATT_NEAR)]
                p = jnp.exp2(s - shift[a])
                part = jnp.sum(p.reshape(t // SUBLANES, SUBLANES, t), axis=0)
                l8 = part if l8 is None else l8 + part
                p_ref[slot, a, tt] = p.astype(BF16)
            l8_ref[a] += l8

    def values(u):
        slot = u % 2
        for a in range(2):
            acc_ref[a] += (_dot(vt_ref[0, 2 * u], p_ref[slot, a, 0])
                           + _dot(vt_ref[0, 2 * u + 1], p_ref[slot, a, 1]))

    def far_step(u, carry):
        values(u - 1)
        probs(u, False)
        return carry

    def near_step(u, carry):
        values(u - 1)
        probs(u, True)
        return carry

    n = i + 1
    pairs = n // 2
    far_pairs = n_far // 2

    @pl.when(far_pairs > 0)
    def _():
        probs(0, False)

    @pl.when(jnp.logical_and(far_pairs == 0, pairs > 0))
    def _():
        probs(0, True)

    lax.fori_loop(1, far_pairs, far_step, 0)
    lax.fori_loop(jnp.maximum(far_pairs, 1), pairs, near_step, 0)

    odd = n % 2 == 1

    @pl.when(jnp.logical_and(pairs > 0, jnp.logical_not(odd)))
    def _():
        values(pairs - 1)

    @pl.when(jnp.logical_and(pairs > 0, odd))
    def _():
        values(pairs - 1)
        single_tile(n - 1)

    @pl.when(pairs == 0)
    def _():
        single_tile(0)

    def finalize():
        ot = acc_ref[0] * (1.0 / l_ref[0]) - acc_ref[1] * (lamb_ref[...] / l_ref[1])
        ot = ot * lax.rsqrt(jnp.mean(ot * ot, axis=0, keepdims=True) + EPS)
        o_ref[0] = (ot.T * sub_ref[...] * (1.0 - lam_init)).astype(o_ref.dtype)

    for a in range(2):
        l_ref[a] = jnp.sum(l8_ref[a], axis=0, keepdims=True)
    l_min = jnp.min(jnp.minimum(l_ref[0], l_ref[1]))
    l_max = jnp.max(jnp.maximum(l_ref[0], l_ref[1]))
    in_range = jnp.logical_and(l_min >= ATT_MIN_DENOM, l_max <= ATT_MAX_DENOM)
    finalize()

    @pl.when(jnp.logical_not(in_range))
    def _():
        m_ref[...] = jnp.full_like(m_ref, MASKED)
        l_ref[...] = jnp.zeros_like(l_ref)
        acc_ref[...] = jnp.zeros_like(acc_ref)

        def online_tile(j, bias):
            kb = k_ref[0, pl.ds(pl.multiple_of(j * t, t), t), :]
            vtb = vt_ref[0, j]
            for a in range(2):
                s = _dot(kb, qm_ref[a])
                if bias is not None:
                    s = s + bias
                m_old = m_ref[a]
                m_new = jnp.maximum(m_old, jnp.max(s, axis=0, keepdims=True))
                alpha = jnp.exp2(m_old - m_new)
                p = jnp.exp2(s - m_new)
                l_ref[a] = alpha * l_ref[a] + jnp.sum(p, axis=0, keepdims=True)
                acc_ref[a] = alpha * acc_ref[a] + _dot(vtb, p.astype(BF16))
                m_ref[a] = m_new

        def online_far(j, carry):
            online_tile(j, None)
            return carry

        def online_near(j, carry):
            online_tile(j, nb_ref[i - j])
            return carry

        lax.fori_loop(0, n_far, online_far, 0)
        lax.fori_loop(n_far, i + 1, online_near, 0)
        finalize()


def _diff_bias(rel_bias, s):
    t = ATT_T
    n = ATT_NEAR * t
    assert n >= BIAS_FLAT_FROM + t - 1 and s >= n
    dist_bias = rel_bias[_t5_bucket(jnp.arange(n))].astype(F32)
    far = rel_bias[REL_BUCKETS - 1].astype(F32)
    near = (dist_bias - far) * LOG2E
    f = jnp.concatenate([jnp.full((t, far.shape[0]), MASKED, F32), near])
    top = jnp.broadcast_to(jnp.maximum(jnp.max(near, axis=0), 0.0), f.shape)
    return jnp.stack([f.T, top.T], axis=1)


def _diff_attention(q, k, vt, rel_bias, lam4, subln, layer_idx):
    b, s, n = k.shape
    t = ATT_T
    w = 2 * DIFF_HEAD_DIM
    lam_init = 0.8 - 0.6 * math.exp(-0.3 * layer_idx)
    return pl.pallas_call(
        functools.partial(_diff_kernel, lam_init=lam_init),
        grid=(b, DIFF_HEADS, s // t),
        in_specs=[pl.BlockSpec((1, 1, w, t), lambda bi, h, i: (bi, i, h, 0)),
                  pl.BlockSpec((1, s, w), lambda bi, h, i: (bi, 0, h)),
                  pl.BlockSpec((1, s // t, w, t), lambda bi, h, i: (bi, 0, h, 0)),
                  pl.BlockSpec((1, 2, (ATT_NEAR + 1) * t), lambda bi, h, i: (h, 0, 0)),
                  pl.BlockSpec((4, DIFF_HEAD_DIM), lambda bi, h, i: (0, 0)),
                  pl.BlockSpec((1, w), lambda bi, h, i: (0, 0))],
        out_specs=pl.BlockSpec((1, t, w), lambda bi, h, i: (bi, i, h)),
        out_shape=jax.ShapeDtypeStruct((b, s, n), BF16),
        scratch_shapes=[pltpu.VMEM((2, w, t), BF16),
                        pltpu.VMEM((2, 1, t), F32),
                        pltpu.VMEM((2, 1, t), F32),
                        pltpu.VMEM((2, 1, t), F32),
                        pltpu.VMEM((2, SUBLANES, t), F32),
                        pltpu.VMEM((2, w, t), F32),
                        pltpu.VMEM((ATT_NEAR + 1, t, t), F32),
                        pltpu.VMEM((2, 2, 2, t, t), BF16),
                        pltpu.VMEM((1, t), F32)],
        compiler_params=_params(("arbitrary", "arbitrary", "arbitrary")),
        name="diff_attention",
    )(q, k, vt, _diff_bias(rel_bias, s), lam4, subln)


def kernel(x, rel_bias, norm_mix, norm_ffn, norm_final, w_in_ab, ret_gn, w_out_ab, w_in_c,
           lam_q1, lam_k1, lam_q2, lam_k2, diff_subln, w_out_c, w_ffn_in, conv_w, conv_b,
           w_ffn_out):
    b, s, d = x.shape
    depth = norm_mix.shape[0]
    assert depth == 2 and s >= ATT_NEAR * ATT_T and conv_w.shape[1] == CONV_WIDTH == 3
    assert all(s % n == 0 for n in (RET_ROWS, FFN_TM, PROJ_TM, MIX_TM, ATT_T, max(DIL_DILATIONS) * DIL_BLOCK))
    assert DIL_DILATIONS[0] == 1 and all(hi % lo == 0 for lo, hi in zip(DIL_DILATIONS, DIL_DILATIONS[1:]))
    row = lambda v: v.reshape(1, -1)

    proj = _proj0(x, row(norm_mix[0]), w_in_ab[0].astype(BF16))
    ret = _retention(proj[0], row(ret_gn[0]), b, s)
    outs, lses = zip(*[_dilated_pattern(dq, rel_bias, wdw, dil)
                       for dq, (wdw, dil) in zip(proj[1:], DIL_PATTERNS)])
    x3 = _mix0(x, ret, outs, lses, w_out_ab[0].astype(BF16))
    x3 = _ffn(x3, row(norm_ffn[0]), w_ffn_in[0].astype(BF16), conv_w[0],
              row(conv_b[0]), w_ffn_out[0].astype(BF16), row(norm_final), False)

    nqk = DIFF_HEADS * 2 * DIFF_HEAD_DIM
    wc = w_in_c[0].astype(BF16)
    q, k, vt = _proj1(x3, row(norm_mix[1]), wc[:, :nqk].T, wc[:, nqk:2 * nqk], wc[:, 2 * nqk:].T)
    lam4 = jnp.stack([lam_q1[0], lam_k1[0], lam_q2[0], lam_k2[0]]).astype(F32)
    a = _diff_attention(q, k, vt, rel_bias, lam4, row(diff_subln[0]), 1)
    return _ffn(x3, row(norm_ffn[1]), w_ffn_in[1].astype(BF16), conv_w[1], row(conv_b[1]),
                w_ffn_out[1].astype(BF16), row(norm_final), True,
                mixer=(a, w_out_c[0].astype(BF16)))
```

```python
import functools
import math

import jax
import jax.numpy as jnp
from jax import lax
from jax.experimental import pallas as pl
from jax.experimental.pallas import tpu as pltpu

F32 = jnp.float32
BF16 = jnp.bfloat16

EPS = 1e-6
MASKED = -1e30

RET_HEADS = 4
RET_DIM = 128
RET_CHUNK = 128
ROPE_BASE = 10000.0
DIL_HEADS = 8
DIL_HEAD_DIM = 64
DIL_PATTERNS = ((128, 1), (512, 4), (2048, 16))
DIL_BLOCK = 128
DIFF_HEADS = 8
DIFF_HEAD_DIM = 64
REL_BUCKETS = 32
REL_MAX_DISTANCE = 2048
D_FF = 2816
CONV_WIDTH = 3
DIL_W = DIL_HEADS * DIL_HEAD_DIM
RET_W = 4 * RET_HEADS * RET_DIM
DIL_DILATIONS = tuple(d for _, d in DIL_PATTERNS)

LANES = 128
SUBLANES = 8
VMEM_BYTES_V7X = 64 * 1024 * 1024
VMEM_LIMIT = VMEM_BYTES_V7X - 8 * 1024 * 1024

PROJ_TM = 512
MIX_TM = 1024
PROJ_TN = 512
RET_ROWS = 2048
DIL_QBLOCKS = 8
FFN_TM = 1024
FFN_CF = 256
ATT_T = 512
ATT_BOUND_SLACK = 1.0 + 2.0 ** -5
ATT_MIN_DENOM = 2.0 ** -60
ATT_MAX_DENOM = 2.0 ** 100
_MAX_EXACT = REL_BUCKETS // 2
BIAS_FLAT_FROM = 8 + math.ceil(_MAX_EXACT * (REL_MAX_DISTANCE / _MAX_EXACT)
                               ** ((REL_BUCKETS - 1 - _MAX_EXACT) / (REL_BUCKETS - _MAX_EXACT)))
ATT_NEAR = -(-(BIAS_FLAT_FROM + ATT_T - 1) // ATT_T)
LOG2E = math.log2(math.e)
ATT_QSCALE = DIFF_HEAD_DIM ** -0.5 * LOG2E


def _params(sem):
    return pltpu.CompilerParams(dimension_semantics=sem, vmem_limit_bytes=VMEM_LIMIT)


def _rms(x, g):
    return x * lax.rsqrt(jnp.mean(x * x, axis=-1, keepdims=True) + EPS) * g


def _dot(a, b):
    return jnp.dot(a, b, preferred_element_type=F32)


def _dot_nt(a, b):
    return lax.dot_general(a, b, (((1,), (1,)), ((), ())), preferred_element_type=F32)


def _dot_tn(a, b):
    return lax.dot_general(a, b, (((0,), (0,)), ((), ())), preferred_element_type=F32)


def _proj0_kernel(x_ref, g_ref, w_ref, a_ref, *rest):
    d_refs, dbuf = rest[:-1], rest[-1]
    tm = x_ref.shape[1]
    h = _rms(x_ref[0], g_ref[...]).astype(BF16)
    na = a_ref.shape[-1]
    nd = w_ref.shape[1] - na
    for c in range(0, nd, PROJ_TN):
        res = _dot(h, w_ref[:, na + c:na + c + PROJ_TN])
        for s in range(c // LANES, (c + PROJ_TN) // LANES):
            cols = slice(s * LANES, (s + 1) * LANES)
            dbuf[0, s] = res[:, (s * LANES - c):(s * LANES - c) + LANES]
            src, d_prev = 0, 1
            for dref, d in zip(d_refs, DIL_DILATIONS):
                if d == 1:
                    dref[0, 0, :, cols] = dbuf[0, s].astype(BF16)
                    continue
                step, rows = d // d_prev, tm // d
                dst = 1 - src
                for rp in range(d_prev):
                    for q in range(step):
                        piece = dbuf[src, s, pl.ds(rp * (tm // d_prev) + q, rows, stride=step), :]
                        r = rp + q * d_prev
                        if d != DIL_DILATIONS[-1]:
                            dbuf[dst, s, r * rows:(r + 1) * rows, :] = piece
                        dref[0, r, :, cols] = piece.astype(BF16)
                src, d_prev = dst, d
    for c in range(0, na, PROJ_TN):
        a_ref[0, :, c:c + PROJ_TN] = _dot(h, w_ref[:, c:c + PROJ_TN]).astype(BF16)


def _proj0(x3, g, w):
    b, s, dm = x3.shape
    n = w.shape[1]
    nd = n - RET_W
    tm = PROJ_TM
    d_specs = [pl.BlockSpec((1, d, tm // d, nd), lambda bi, i: (bi, 0, i, 0)) for d in DIL_DILATIONS]
    d_shapes = [jax.ShapeDtypeStruct((b, d, s // d, nd), BF16) for d in DIL_DILATIONS]
    return pl.pallas_call(
        _proj0_kernel,
        grid=(b, s // tm),
        in_specs=[pl.BlockSpec((1, tm, dm), lambda bi, i: (bi, i, 0)),
                  pl.BlockSpec((1, dm), lambda bi, i: (0, 0)),
                  pl.BlockSpec((dm, n), lambda bi, i: (0, 0))],
        out_specs=[pl.BlockSpec((1, tm, RET_W), lambda bi, i: (bi, i, 0))] + d_specs,
        out_shape=[jax.ShapeDtypeStruct((b, s, RET_W), BF16)] + d_shapes,
        scratch_shapes=[pltpu.VMEM((2, nd // LANES, tm, LANES), F32)],
        compiler_params=_params(("arbitrary", "arbitrary")),
        name="proj0",
    )(x3, g, w)


def _ret_kernel(q_ref, k_ref, v_ref, gate_ref, start_ref, off_ref, tab_ref, gn_ref, o_ref,
                state_ref, qr_ref, kr_ref, qx_ref, kv_ref, st_ref):
    @pl.when(pl.program_id(2) == 0)
    def _():
        state_ref[...] = jnp.zeros_like(state_ref)

    decay = tab_ref[0, 0]
    xi = tab_ref[0, 1]
    zeta = tab_ref[0, 2]
    g_chunk = tab_ref[0, 3]
    gain = gn_ref[...]
    half = RET_DIM // 2
    n_chunks = q_ref.shape[1] // RET_CHUNK
    for c in range(n_chunks):
        rows = pl.ds(c * RET_CHUNK, RET_CHUNK)
        ca, sa = start_ref[c, 0:1, :], start_ref[c, 1:2, :]
        cos = ca * off_ref[0] - sa * off_ref[1]
        sin = sa * off_ref[2] + ca * off_ref[3]
        q = q_ref[0, rows, :].astype(F32)
        k = k_ref[0, rows, :].astype(F32)
        q = q * cos + pltpu.roll(q, half, 1) * sin
        k = k * cos + pltpu.roll(k, half, 1) * sin
        qr_ref[c] = q.astype(BF16)
        kr_ref[c] = k.astype(BF16)
        qx_ref[c] = (q * xi).astype(BF16)
        kv_ref[c] = _dot_tn((k * zeta).astype(BF16), v_ref[0, rows, :])
    state = state_ref[...]
    for c in range(n_chunks):
        st_ref[c] = state.astype(BF16)
        state = g_chunk * state + kv_ref[c]
    state_ref[...] = state
    for c in range(n_chunks):
        rows = pl.ds(c * RET_CHUNK, RET_CHUNK)
        v = v_ref[0, rows, :]
        intra = _dot_nt(qr_ref[c], kr_ref[c]) * decay
        y = _dot(intra.astype(BF16), v) + _dot(qx_ref[c], st_ref[c])
        mu = jnp.mean(y, axis=-1, keepdims=True)
        yc = y - mu
        var = jnp.mean(yc * yc, axis=-1, keepdims=True)
        yn = yc * lax.rsqrt(var + EPS) * gain
        gate = gate_ref[0, rows, :].astype(F32)
        o_ref[0, rows, :] = (gate * jax.nn.sigmoid(gate) * yn).astype(o_ref.dtype)


def _retention_tables(s):
    c, e = RET_CHUNK, RET_DIM
    inv = ROPE_BASE ** (-jnp.arange(0, e, 2, dtype=F32) / e)
    inv = jnp.concatenate([inv, inv])
    sign = jnp.where(jnp.arange(e) < e // 2, -1.0, 1.0).astype(F32)
    a = (jnp.arange(s // c, dtype=F32) * c)[:, None] * inv[None, :]
    b_ = jnp.arange(c, dtype=F32)[:, None] * inv[None, :]
    start = jnp.stack([jnp.cos(a), jnp.sin(a)], axis=1)
    off = jnp.stack([jnp.cos(b_), jnp.sin(b_), jnp.cos(b_) * sign, jnp.sin(b_) * sign])
    log_g = jnp.log1p(-jnp.exp2(-5.0 - jnp.arange(RET_HEADS, dtype=F32)))
    pos = jnp.arange(c, dtype=F32)
    rel = pos[:, None] - pos[None, :]
    causal = rel >= 0
    scale = e ** -0.5
    decay = jnp.where(causal, jnp.exp(jnp.where(causal, rel, 0.0)[None] * log_g[:, None, None]), 0.0)
    zeta = jnp.exp((c - 1 - pos)[None, :] * log_g[:, None])
    xi = jnp.exp((pos + 1)[None, :] * log_g[:, None])
    g_chunk = jnp.exp(c * log_g)
    tab = jnp.stack([
        decay * scale,
        jnp.broadcast_to(xi[:, :, None], (RET_HEADS, c, e)),
        jnp.broadcast_to(zeta[:, :, None] * scale, (RET_HEADS, c, e)),
        jnp.broadcast_to(g_chunk[:, None, None], (RET_HEADS, e, e)),
    ], axis=1)
    return start, off, tab


def _retention(p0, gn, b, s):
    start, off, tab = _retention_tables(s)
    e = RET_DIM
    nc = RET_ROWS // RET_CHUNK
    col = lambda off: (lambda bi, h, t: (bi, t, off + h))
    return pl.pallas_call(
        _ret_kernel,
        grid=(b, RET_HEADS, s // RET_ROWS),
        in_specs=[pl.BlockSpec((1, RET_ROWS, e), col(0)),
                  pl.BlockSpec((1, RET_ROWS, e), col(RET_HEADS)),
                  pl.BlockSpec((1, RET_ROWS, e), col(2 * RET_HEADS)),
                  pl.BlockSpec((1, RET_ROWS, e), col(3 * RET_HEADS)),
                  pl.BlockSpec((nc, 2, e), lambda bi, h, t: (t, 0, 0)),
                  pl.BlockSpec((4, RET_CHUNK, e), lambda bi, h, t: (0, 0, 0)),
                  pl.BlockSpec((1, 4, e, e), lambda bi, h, t: (h, 0, 0, 0)),
                  pl.BlockSpec((1, e), lambda bi, h, t: (0, h))],
        out_specs=pl.BlockSpec((1, RET_ROWS, e), lambda bi, h, t: (bi, t, h)),
        out_shape=jax.ShapeDtypeStruct((b, s, RET_HEADS * e), BF16),
        scratch_shapes=[pltpu.VMEM((e, e), F32),
                        pltpu.VMEM((nc, RET_CHUNK, e), BF16),
                        pltpu.VMEM((nc, RET_CHUNK, e), BF16),
                        pltpu.VMEM((nc, RET_CHUNK, e), BF16),
                        pltpu.VMEM((nc, e, e), F32),
                        pltpu.VMEM((nc, e, e), BF16)],
        compiler_params=_params(("arbitrary", "arbitrary", "arbitrary")),
        name="retention",
    )(p0, p0, p0, p0, start, off, tab, gn)


def _t5_bucket(dist):
    max_exact = REL_BUCKETS // 2
    d = jnp.maximum(dist.astype(F32), 1.0)
    large = max_exact + (jnp.log(d / max_exact) / math.log(REL_MAX_DISTANCE / max_exact)
                         * (REL_BUCKETS - max_exact))
    large = jnp.clip(large.astype(jnp.int32), max_exact, REL_BUCKETS - 1)
    return jnp.where(dist < max_exact, dist, large)


def _dil_kernel(q_ref, kp_ref, kc_ref, vp_ref, vc_ref, g_ref, o_ref, lse_ref, bias_ref,
                s_ref, p_ref):
    first = pl.program_id(2) == 0
    blk, e = DIL_BLOCK, DIL_HEAD_DIM

    @pl.when(jnp.logical_and(jnp.logical_and(pl.program_id(0) == 0, pl.program_id(1) == 0), first))
    def _():
        for h in range(DIL_HEADS):
            gb = jnp.broadcast_to(g_ref[h:h + 1, :], (blk, 4 * blk))
            bias_ref[h] = pltpu.roll(gb, 0, 1, stride=1, stride_axis=0)[:, :2 * blk]

    q_all = q_ref[0, 0] * (e ** -0.5)
    k_all = jnp.concatenate([kp_ref[0, 0], kc_ref[0, 0]], axis=0)
    v_all = jnp.concatenate([vp_ref[0, 0], vc_ref[0, 0]], axis=0)
    in_prev = lax.broadcasted_iota(jnp.int32, (blk, 2 * blk), 1) < blk
    no_prev = jnp.logical_and(first, in_prev)
    lane = lax.broadcasted_iota(jnp.int32, (blk, LANES), 1)
    for qb in range(q_all.shape[0] // blk):
        q = q_all[qb * blk:(qb + 1) * blk]
        kcat = k_all[qb * blk:(qb + 2) * blk]
        vcat = v_all[qb * blk:(qb + 2) * blk]
        lse_all = jnp.zeros((blk, LANES), F32)
        for h in range(DIL_HEADS):
            pair = slice((h // 2) * LANES, (h // 2 + 1) * LANES)
            qp = q[:, pair]
            mine = (lax.broadcasted_iota(jnp.int32, qp.shape, 1) // e) == h % 2
            s = _dot_nt(jnp.where(mine, qp, jnp.zeros_like(qp)), kcat[:, pair]) + bias_ref[h]
            if qb == 0:
                s = jnp.where(no_prev, MASKED, s)
            s_ref[qb, h] = s
        inv_l = []
        for h in range(DIL_HEADS):
            s = s_ref[qb, h]
            m = jnp.max(s, axis=-1, keepdims=True)
            p = jnp.exp(s - m)
            l = jnp.sum(p, axis=-1, keepdims=True)
            p_ref[qb, h] = p.astype(BF16)
            inv_l.append(1.0 / l)
            lse_all = jnp.where(lane == h, m + jnp.log(l), lse_all)
        for g in range(DIL_HEADS // 2):
            pair = slice(g * LANES, (g + 1) * LANES)
            vp = vcat[:, pair]
            upper = lax.broadcasted_iota(jnp.int32, vp.shape, 1) >= e
            zero = jnp.zeros_like(vp)
            o_pair = (_dot(p_ref[qb, 2 * g], jnp.where(upper, zero, vp))
                      + _dot(p_ref[qb, 2 * g + 1], jnp.where(upper, vp, zero)))
            scale = jnp.where(lane >= e, inv_l[2 * g + 1], inv_l[2 * g])
            o_ref[0, 0, qb * blk:(qb + 1) * blk, pair] = (o_pair * scale).astype(o_ref.dtype)
        lse_ref[0, 0, qb * blk:(qb + 1) * blk, :] = lse_all


def _dil_bias_vector(rel_bias, window, dil):
    blk = DIL_BLOCK
    span = window // dil
    r = blk - jnp.arange(4 * blk)
    band = (r >= 0) & (r <= span)
    bias = rel_bias[_t5_bucket(jnp.maximum(r, 0) * dil)].astype(F32)
    return jnp.where(band[:, None], bias, MASKED).T


def _dilated_pattern(dq, rel_bias, window, dil):
    b, d, sd, _ = dq.shape
    blk = DIL_BLOCK
    nq = math.gcd(DIL_QBLOCKS, sd // DIL_BLOCK)
    rows = nq * blk
    assert d == dil and window // dil == blk and sd % rows == 0
    cur = lambda c: (lambda bi, r, n: (bi, r, n, c))
    prev = lambda c: (lambda bi, r, n: (bi, r, jnp.maximum(n * nq - 1, 0), c))
    spec = lambda im: pl.BlockSpec((1, 1, rows, DIL_W), im)
    pspec = lambda im: pl.BlockSpec((1, 1, blk, DIL_W), im)
    return pl.pallas_call(
        _dil_kernel,
        grid=(b, dil, sd // rows),
        in_specs=[spec(cur(0)), pspec(prev(1)), spec(cur(1)), pspec(prev(2)), spec(cur(2)),
                  pl.BlockSpec((DIL_HEADS, 4 * blk), lambda bi, r, n: (0, 0))],
        out_specs=[pl.BlockSpec((1, 1, rows, DIL_W), lambda bi, r, n: (bi, r, n, 0)),
                   pl.BlockSpec((1, 1, rows, LANES), lambda bi, r, n: (bi, r, n, 0))],
        out_shape=[jax.ShapeDtypeStruct((b, dil, sd, DIL_W), BF16),
                   jax.ShapeDtypeStruct((b, dil, sd, LANES), F32)],
        scratch_shapes=[pltpu.VMEM((DIL_HEADS, blk, 2 * blk), F32),
                        pltpu.VMEM((nq, DIL_HEADS, blk, 2 * blk), F32),
                        pltpu.VMEM((nq, DIL_HEADS, blk, 2 * blk), BF16)],
        compiler_params=_params(("arbitrary", "arbitrary", "arbitrary")),
        name=f"dilated_d{dil}",
    )(dq, dq, dq, dq, dq, _dil_bias_vector(rel_bias, window, dil))


def _mix0_kernel(x_ref, ret_ref, *rest):
    np_ = len(DIL_DILATIONS)
    o_refs, l_refs = rest[:np_], rest[np_:2 * np_]
    w_ref, out_ref, obuf, lbuf, dbuf = rest[2 * np_:]
    tm = x_ref.shape[1]
    nslab = DIL_W // LANES
    for pi, d in enumerate(DIL_DILATIONS):
        rows = tm // d
        for r in range(d):
            dst = pl.ds(r, rows, stride=d) if d > 1 else pl.ds(0, tm)
            lbuf[pi, dst, :] = l_refs[pi][0, r]
            for s in range(nslab):
                obuf[pi, s, dst, :] = o_refs[pi][0, r, :, s * LANES:(s + 1) * LANES].astype(F32)
    lses = [lbuf[pi] for pi in range(np_)]
    m = functools.reduce(jnp.maximum, lses)
    es = [jnp.exp(l - m) for l in lses]
    inv = 1.0 / functools.reduce(lambda a, b: a + b, es)
    row = lax.broadcasted_iota(jnp.int32, (LANES, DIL_W), 0)
    col = lax.broadcasted_iota(jnp.int32, (LANES, DIL_W), 1)
    spread = (col // DIL_HEAD_DIM == row).astype(BF16)
    spread2 = jnp.concatenate([spread, spread], axis=0)
    wide = []
    for ex in es:
        wgt = ex * inv
        hi = wgt.astype(BF16)
        lo = (wgt - hi.astype(F32)).astype(BF16)
        wide.append(_dot(jnp.concatenate([hi, lo], axis=1), spread2))
    for s in range(nslab):
        cols = slice(s * LANES, (s + 1) * LANES)
        acc = wide[0][:, cols] * obuf[0, s]
        for pi in range(1, np_):
            acc = acc + wide[pi][:, cols] * obuf[pi, s]
        dbuf[:, cols] = acc.astype(BF16)
    na = ret_ref.shape[-1]
    y = _dot(ret_ref[0], w_ref[:na, :]) + _dot(dbuf[...], w_ref[na:, :])
    out_ref[0] = x_ref[0] + y


def _mix0(x3, ret, outs, lses, w):
    b, s, dm = x3.shape
    na = ret.shape[-1]
    tm = MIX_TM
    o_specs = [pl.BlockSpec((1, d, tm // d, DIL_W), lambda bi, i: (bi, 0, i, 0)) for d in DIL_DILATIONS]
    l_specs = [pl.BlockSpec((1, d, tm // d, LANES), lambda bi, i: (bi, 0, i, 0)) for d in DIL_DILATIONS]
    np_ = len(DIL_DILATIONS)
    return pl.pallas_call(
        _mix0_kernel,
        grid=(b, s // tm),
        in_specs=[pl.BlockSpec((1, tm, dm), lambda bi, i: (bi, i, 0)),
                  pl.BlockSpec((1, tm, na), lambda bi, i: (bi, i, 0))] + o_specs + l_specs
                 + [pl.BlockSpec((na + DIL_W, dm), lambda bi, i: (0, 0))],
        out_specs=pl.BlockSpec((1, tm, dm), lambda bi, i: (bi, i, 0)),
        out_shape=jax.ShapeDtypeStruct((b, s, dm), F32),
        scratch_shapes=[pltpu.VMEM((np_, DIL_W // LANES, tm, LANES), F32),
                        pltpu.VMEM((np_, tm, LANES), F32),
                        pltpu.VMEM((tm, DIL_W), BF16)],
        compiler_params=_params(("arbitrary", "arbitrary")),
        name="mix0",
    )(x3, ret, *outs, *lses, w)


def _ffn_kernel(x_ref, *rest, final_norm, mixer_proj):
    if mixer_proj:
        a_ref, wmix_ref = rest[:2]
        rest = rest[2:]
    g_ref, win_ref, cw_ref, cb_ref, wout_ref, fg_ref, o_ref, ubuf, carry, gbuf = rest
    tm = x_ref.shape[1]
    halo = SUBLANES

    @pl.when(pl.program_id(1) == 0)
    def _():
        carry[...] = jnp.zeros_like(carry)

    x = x_ref[0]
    if mixer_proj:
        x = x + _dot(a_ref[0], wmix_ref[...])
    h = _rms(x, g_ref[...]).astype(BF16)

    def conv(part, col):
        cols = slice(col, col + FFN_CF)
        u = _dot(h, win_ref[:, cols])
        ubuf[part, 0:halo, :] = carry[:, cols]
        ubuf[part, halo:halo + tm, :] = u
        carry[:, cols] = u[tm - halo:, :]
        u1 = ubuf[part, halo - 1:halo - 1 + tm, :]
        u2 = ubuf[part, halo - 2:halo - 2 + tm, :]
        return (u * cw_ref[2:3, cols] + u1 * cw_ref[1:2, cols] + u2 * cw_ref[0:1, cols]
                + cb_ref[:, cols])

    for c in range(D_FF // FFN_CF):
        gate = conv(0, c * FFN_CF)
        up = conv(1, D_FF + c * FFN_CF)
        act = 0.5 * gate * (1.0 + lax.erf(gate * (2.0 ** -0.5)))
        gbuf[:, c * FFN_CF:(c + 1) * FFN_CF] = (act * up).astype(BF16)

    y = x + _dot(gbuf[...], wout_ref[...])
    if final_norm:
        y = _rms(y, fg_ref[...])
    o_ref[0] = y


def _ffn(x3, g, w_in, conv_w, conv_b, w_out, final_gain, final_norm, mixer=None):
    b, s, d = x3.shape
    f2 = w_in.shape[1]
    whole = pl.BlockSpec(memory_space=pltpu.VMEM)
    tile = lambda n: pl.BlockSpec((1, FFN_TM, n), lambda bi, t: (bi, t, 0))
    mix_specs = [tile(mixer[0].shape[-1]), whole] if mixer is not None else []
    mix_args = list(mixer) if mixer is not None else []
    return pl.pallas_call(
        functools.partial(_ffn_kernel, final_norm=final_norm, mixer_proj=mixer is not None),
        grid=(b, s // FFN_TM),
        in_specs=[tile(d)] + mix_specs + [whole, whole, whole, whole, whole, whole],
        out_specs=pl.BlockSpec((1, FFN_TM, d), lambda bi, t: (bi, t, 0)),
        out_shape=jax.ShapeDtypeStruct((b, s, d), F32),
        scratch_shapes=[pltpu.VMEM((2, FFN_TM + SUBLANES, FFN_CF), F32),
                        pltpu.VMEM((SUBLANES, f2), F32),
                        pltpu.VMEM((FFN_TM, D_FF), BF16)],
        compiler_params=_params(("arbitrary", "arbitrary")),
        name="ffn_final" if final_norm else "ffn",
    )(x3, *mix_args, g, w_in, conv_w, conv_b, w_out, final_gain)


def _proj1_kernel(x_ref, g_ref, wq_ref, wk_ref, wvt_ref, q_ref, k_ref, vt_ref):
    h = _rms(x_ref[0], g_ref[...]).astype(BF16)
    n = k_ref.shape[-1]
    for c in range(0, n, PROJ_TN):
        cols = slice(c, c + PROJ_TN)
        q_ref[0, 0, cols, :] = (_dot_nt(wq_ref[cols, :], h) * ATT_QSCALE).astype(BF16)
        k_ref[0, :, cols] = _dot(h, wk_ref[:, cols]).astype(BF16)
        vt_ref[0, 0, cols, :] = _dot_nt(wvt_ref[cols, :], h).astype(BF16)


def _proj1(x3, g, wq, wk, wvt):
    b, s, d = x3.shape
    n = wq.shape[1]
    t = ATT_T
    wspec = pl.BlockSpec((d, n), lambda bi, i: (0, 0))
    return pl.pallas_call(
        _proj1_kernel,
        grid=(b, s // t),
        in_specs=[pl.BlockSpec((1, t, d), lambda bi, i: (bi, i, 0)),
                  pl.BlockSpec((1, d), lambda bi, i: (0, 0)),
                  pl.BlockSpec((n, d), lambda bi, i: (0, 0)), wspec,
                  pl.BlockSpec((n, d), lambda bi, i: (0, 0))],
        out_specs=[pl.BlockSpec((1, 1, n, t), lambda bi, i: (bi, i, 0, 0)),
                   pl.BlockSpec((1, t, n), lambda bi, i: (bi, i, 0)),
                   pl.BlockSpec((1, 1, n, t), lambda bi, i: (bi, i, 0, 0))],
        out_shape=[jax.ShapeDtypeStruct((b, s // t, n, t), BF16),
                   jax.ShapeDtypeStruct((b, s, n), BF16),
                   jax.ShapeDtypeStruct((b, s // t, n, t), BF16)],
        compiler_params=_params(("arbitrary", "arbitrary")),
        name="proj1",
    )(x3, g, wq, wk, wvt)


def _diff_kernel(q_ref, k_ref, vt_ref, f_ref, lam_ref, sub_ref, o_ref,
                 qm_ref, km_ref, m_ref, l_ref, l8_ref, acc_ref, nb_ref, p_ref, lamb_ref, *, lam_init):
    i = pl.program_id(2)
    t = ATT_T
    e = DIFF_HEAD_DIM
    half = lax.broadcasted_iota(jnp.int32, (SUBLANES, 2 * e), 1) // e
    sel = (lax.broadcasted_iota(jnp.int32, (SUBLANES, 2 * e), 0) == half).astype(BF16)

    @pl.when(i == 0)
    def _():
        for o in range(ATT_NEAR):
            g = jnp.broadcast_to(f_ref[0, 0:1, o * t:(o + 2) * t], (t, 2 * t))
            nb_ref[o] = pltpu.roll(g, 0, 1, stride=1, stride_axis=0)[:, t:]
        nb_ref[ATT_NEAR] = jnp.zeros((t, t), F32)
        kmax = jnp.zeros((SUBLANES, t), F32)
        for c in range(k_ref.shape[1] // t):
            kk = k_ref[0, c * t:(c + 1) * t, :].astype(F32)
            kmax = jnp.maximum(kmax, _dot_nt(sel, (kk * kk).astype(BF16)))
        for a in range(2):
            km_ref[a] = jnp.broadcast_to(jnp.max(kmax[a:a + 1, :], axis=1, keepdims=True), (1, t))
        lam = (jnp.exp(jnp.sum(lam_ref[0:1, :] * lam_ref[1:2, :], axis=-1, keepdims=True))
               - jnp.exp(jnp.sum(lam_ref[2:3, :] * lam_ref[3:4, :], axis=-1, keepdims=True))
               + lam_init)
        lamb_ref[...] = jnp.broadcast_to(lam, (1, t))

    qt = q_ref[0, 0]
    row = lax.broadcasted_iota(jnp.int32, qt.shape, 0)
    zero = jnp.zeros_like(qt)
    qm_ref[0] = jnp.where(row < e, qt, zero)
    qm_ref[1] = jnp.where(row >= e, qt, zero)
    qf = qt.astype(F32)
    qq = qf * qf
    qn2 = [jnp.sum(qq[a * e:(a + 1) * e], axis=0, keepdims=True) for a in range(2)]
    bias_max = f_ref[0, 1:2, 0:t]
    n_far = jnp.maximum(i - (ATT_NEAR - 1), 0)

    l8_ref[...] = jnp.zeros_like(l8_ref)
    acc_ref[...] = jnp.zeros_like(acc_ref)
    shift = [jnp.sqrt(qn2[a] * km_ref[a]) * ATT_BOUND_SLACK + bias_max for a in range(2)]

    def single_tile(j, drain_pair=None):
        kb = k_ref[0, pl.ds(pl.multiple_of(j * t, t), t), :]
        for a in range(2):
            s = _dot(kb, qm_ref[a]) + nb_ref[jnp.minimum(i - j, ATT_NEAR)]
            p = jnp.exp2(s - shift[a])
            l8_ref[a] += jnp.sum(p.reshape(t // SUBLANES, SUBLANES, t), axis=0)
            pv = _dot(vt_ref[0, j], p.astype(BF16))
            if drain_pair is not None:
                slot = drain_pair % 2
                pv = (pv + _dot(vt_ref[0, 2 * drain_pair], p_ref[slot, a, 0])
                      + _dot(vt_ref[0, 2 * drain_pair + 1], p_ref[slot, a, 1]))
            acc_ref[a] += pv

    def probs(u, biased):
        slot = u % 2
        for a in range(2):
            l8 = None
            for tt in range(2):
                j = 2 * u + tt
                kb = k_ref[0, pl.ds(pl.multiple_of(j * t, t), t), :]
                s = _dot(kb, qm_ref[a])
                if biased:
                    s = s + nb_ref[jnp.minimum(i - j, ATT_NEAR)]
                p = jnp.exp2(s - shift[a])
                part = jnp.sum(p.reshape(t // SUBLANES, SUBLANES, t), axis=0)
                l8 = part if l8 is None else l8 + part
                p_ref[slot, a, tt] = p.astype(BF16)
            l8_ref[a] += l8

    def values(u):
        slot = u % 2
        for a in range(2):
            acc_ref[a] += (_dot(vt_ref[0, 2 * u], p_ref[slot, a, 0])
                           + _dot(vt_ref[0, 2 * u + 1], p_ref[slot, a, 1]))

    def far_step(u, carry):
        values(u - 1)
        probs(u, False)
        return carry

    def near_step(u, carry):
        values(u - 1)
        probs(u, True)
        return carry

    n = i + 1
    pairs = n // 2
    far_pairs = n_far // 2

    @pl.when(far_pairs > 0)
    def _():
        probs(0, False)

    @pl.when(jnp.logical_and(far_pairs == 0, pairs > 0))
    def _():
        probs(0, True)

    lax.fori_loop(1, far_pairs, far_step, 0)
    lax.fori_loop(jnp.maximum(far_pairs, 1), pairs, near_step, 0)

    odd = n % 2 == 1

    @pl.when(jnp.logical_and(pairs > 0, jnp.logical_not(odd)))
    def _():
        values(pairs - 1)

    @pl.when(jnp.logical_and(pairs > 0, odd))
    def _():
        single_tile(n - 1, drain_pair=pairs - 1)

    @pl.when(pairs == 0)
    def _():
        single_tile(0)

    def finalize():
        ot = acc_ref[0] * (1.0 / l_ref[0]) - acc_ref[1] * (lamb_ref[...] / l_ref[1])
        ot = ot * lax.rsqrt(jnp.mean(ot * ot, axis=0, keepdims=True) + EPS)
        o_ref[0] = (ot.T * sub_ref[...] * (1.0 - lam_init)).astype(o_ref.dtype)

    for a in range(2):
        l_ref[a] = jnp.sum(l8_ref[a], axis=0, keepdims=True)
    l_min = jnp.min(jnp.minimum(l_ref[0], l_ref[1]))
    l_max = jnp.max(jnp.maximum(l_ref[0], l_ref[1]))
    in_range = jnp.logical_and(l_min >= ATT_MIN_DENOM, l_max <= ATT_MAX_DENOM)
    finalize()

    @pl.when(jnp.logical_not(in_range))
    def _():
        m_ref[...] = jnp.full_like(m_ref, MASKED)
        l_ref[...] = jnp.zeros_like(l_ref)
        acc_ref[...] = jnp.zeros_like(acc_ref)

        def online_tile(j, bias):
            kb = k_ref[0, pl.ds(pl.multiple_of(j * t, t), t), :]
            vtb = vt_ref[0, j]
            for a in range(2):
                s = _dot(kb, qm_ref[a])
                if bias is not None:
                    s = s + bias
                m_old = m_ref[a]
                m_new = jnp.maximum(m_old, jnp.max(s, axis=0, keepdims=True))
                alpha = jnp.exp2(m_old - m_new)
                p = jnp.exp2(s - m_new)
                l_ref[a] = alpha * l_ref[a] + jnp.sum(p, axis=0, keepdims=True)
                acc_ref[a] = alpha * acc_ref[a] + _dot(vtb, p.astype(BF16))
                m_ref[a] = m_new

        def online_far(j, carry):
            online_tile(j, None)
            return carry

        def online_near(j, carry):
            online_tile(j, nb_ref[i - j])
            return carry

        lax.fori_loop(0, n_far, online_far, 0)
        lax.fori_loop(n_far, i + 1, online_near, 0)
        finalize()


def _diff_bias(rel_bias, s):
    t = ATT_T
    n = ATT_NEAR * t
    assert n >= BIAS_FLAT_FROM + t - 1 and s >= n
    dist_bias = rel_bias[_t5_bucket(jnp.arange(n))].astype(F32)
    far = rel_bias[REL_BUCKETS - 1].astype(F32)
    near = (dist_bias - far) * LOG2E
    f = jnp.concatenate([jnp.full((t, far.shape[0]), MASKED, F32), near])
    top = jnp.broadcast_to(jnp.maximum(jnp.max(near, axis=0), 0.0), f.shape)
    return jnp.stack([f.T, top.T], axis=1)


def _diff_attention(q, k, vt, rel_bias, lam4, subln, layer_idx):
    b, s, n = k.shape
    t = ATT_T
    w = 2 * DIFF_HEAD_DIM
    lam_init = 0.8 - 0.6 * math.exp(-0.3 * layer_idx)
    return pl.pallas_call(
        functools.partial(_diff_kernel, lam_init=lam_init),
        grid=(b, DIFF_HEADS, s // t),
        in_specs=[pl.BlockSpec((1, 1, w, t), lambda bi, h, i: (bi, i, h, 0)),
                  pl.BlockSpec((1, s, w), lambda bi, h, i: (bi, 0, h)),
                  pl.BlockSpec((1, s // t, w, t), lambda bi, h, i: (bi, 0, h, 0)),
                  pl.BlockSpec((1, 2, (ATT_NEAR + 1) * t), lambda bi, h, i: (h, 0, 0)),
                  pl.BlockSpec((4, DIFF_HEAD_DIM), lambda bi, h, i: (0, 0)),
                  pl.BlockSpec((1, w), lambda bi, h, i: (0, 0))],
        out_specs=pl.BlockSpec((1, t, w), lambda bi, h, i: (bi, i, h)),
        out_shape=jax.ShapeDtypeStruct((b, s, n), BF16),
        scratch_shapes=[pltpu.VMEM((2, w, t), BF16),
                        pltpu.VMEM((2, 1, t), F32),
                        pltpu.VMEM((2, 1, t), F32),
                        pltpu.VMEM((2, 1, t), F32),
                        pltpu.VMEM((2, SUBLANES, t), F32),
                        pltpu.VMEM((2, w, t), F32),
                        pltpu.VMEM((ATT_NEAR + 1, t, t), F32),
                        pltpu.VMEM((2, 2, 2, t, t), BF16),
                        pltpu.VMEM((1, t), F32)],
        compiler_params=_params(("arbitrary", "arbitrary", "arbitrary")),
        name="diff_attention",
    )(q, k, vt, _diff_bias(rel_bias, s), lam4, subln)


def kernel(x, rel_bias, norm_mix, norm_ffn, norm_final, w_in_ab, ret_gn, w_out_ab, w_in_c,
           lam_q1, lam_k1, lam_q2, lam_k2, diff_subln, w_out_c, w_ffn_in, conv_w, conv_b,
           w_ffn_out):
    b, s, d = x.shape
    depth = norm_mix.shape[0]
    assert depth == 2 and s >= ATT_NEAR * ATT_T and conv_w.shape[1] == CONV_WIDTH == 3
    assert all(s % n == 0 for n in (RET_ROWS, FFN_TM, PROJ_TM, MIX_TM, ATT_T, max(DIL_DILATIONS) * DIL_BLOCK))
    assert DIL_DILATIONS[0] == 1 and all(hi % lo == 0 for lo, hi in zip(DIL_DILATIONS, DIL_DILATIONS[1:]))
    row = lambda v: v.reshape(1, -1)

    proj = _proj0(x, row(norm_mix[0]), w_in_ab[0].astype(BF16))
    ret = _retention(proj[0], row(ret_gn[0]), b, s)
    outs, lses = zip(*[_dilated_pattern(dq, rel_bias, wdw, dil)
                       for dq, (wdw, dil) in zip(proj[1:], DIL_PATTERNS)])
    x3 = _mix0(x, ret, outs, lses, w_out_ab[0].astype(BF16))
    x3 = _ffn(x3, row(norm_ffn[0]), w_ffn_in[0].astype(BF16), conv_w[0],
              row(conv_b[0]), w_ffn_out[0].astype(BF16), row(norm_final), False)

    nqk = DIFF_HEADS * 2 * DIFF_HEAD_DIM
    wc = w_in_c[0].astype(BF16)
    q, k, vt = _proj1(x3, row(norm_mix[1]), wc[:, :nqk].T, wc[:, nqk:2 * nqk], wc[:, 2 * nqk:].T)
    lam4 = jnp.stack([lam_q1[0], lam_k1[0], lam_q2[0], lam_k2[0]]).astype(F32)
    a = _diff_attention(q, k, vt, rel_bias, lam4, row(diff_subln[0]), 1)
    return _ffn(x3, row(norm_ffn[1]), w_ffn_in[1].astype(BF16), conv_w[1], row(conv_b[1]),
                w_ffn_out[1].astype(BF16), row(norm_final), True,
                mixer=(a, w_out_c[0].astype(BF16)))
```

```python
import functools
import math

import jax
import jax.numpy as jnp
from jax import lax
from jax.experimental import pallas as pl
from jax.experimental.pallas import tpu as pltpu

F32 = jnp.float32
BF16 = jnp.bfloat16

EPS = 1e-6
MASKED = -1e30

RET_HEADS = 4
RET_DIM = 128
RET_CHUNK = 128
ROPE_BASE = 10000.0
DIL_HEADS = 8
DIL_HEAD_DIM = 64
DIL_PATTERNS = ((128, 1), (512, 4), (2048, 16))
DIL_BLOCK = 128
DIFF_HEADS = 8
DIFF_HEAD_DIM = 64
REL_BUCKETS = 32
REL_MAX_DISTANCE = 2048
D_FF = 2816
CONV_WIDTH = 3
DIL_W = DIL_HEADS * DIL_HEAD_DIM
RET_W = 4 * RET_HEADS * RET_DIM
DIL_DILATIONS = tuple(d for _, d in DIL_PATTERNS)

LANES = 128
SUBLANES = 8
VMEM_BYTES_V7X = 64 * 1024 * 1024
VMEM_LIMIT = VMEM_BYTES_V7X - 8 * 1024 * 1024

PROJ_TM = 512
MIX_TM = 1024
PROJ_TN = 512
RET_ROWS = 2048
DIL_QBLOCKS = 8
FFN_TM = 1024
FFN_CF = 256
ATT_T = 512
ATT_BOUND_SLACK = 1.0 + 2.0 ** -5
ATT_MIN_DENOM = 2.0 ** -60
ATT_MAX_DENOM = 2.0 ** 100
_MAX_EXACT = REL_BUCKETS // 2
BIAS_FLAT_FROM = 8 + math.ceil(_MAX_EXACT * (REL_MAX_DISTANCE / _MAX_EXACT)
                               ** ((REL_BUCKETS - 1 - _MAX_EXACT) / (REL_BUCKETS - _MAX_EXACT)))
ATT_NEAR = -(-(BIAS_FLAT_FROM + ATT_T - 1) // ATT_T)
LOG2E = math.log2(math.e)
ATT_QSCALE = DIFF_HEAD_DIM ** -0.5 * LOG2E


def _params(sem):
    return pltpu.CompilerParams(dimension_semantics=sem, vmem_limit_bytes=VMEM_LIMIT)


def _rms(x, g):
    return x * lax.rsqrt(jnp.mean(x * x, axis=-1, keepdims=True) + EPS) * g


def _dot(a, b):
    return jnp.dot(a, b, preferred_element_type=F32)


def _dot_nt(a, b):
    return lax.dot_general(a, b, (((1,), (1,)), ((), ())), preferred_element_type=F32)


def _dot_tn(a, b):
    return lax.dot_general(a, b, (((0,), (0,)), ((), ())), preferred_element_type=F32)


def _proj0_kernel(x_ref, g_ref, w_ref, a_ref, *rest):
    d_refs, dbuf = rest[:-1], rest[-1]
    tm = x_ref.shape[1]
    h = _rms(x_ref[0], g_ref[...]).astype(BF16)
    na = a_ref.shape[-1]
    nd = w_ref.shape[1] - na
    for c in range(0, nd, PROJ_TN):
        res = _dot(h, w_ref[:, na + c:na + c + PROJ_TN])
        for s in range(c // LANES, (c + PROJ_TN) // LANES):
            cols = slice(s * LANES, (s + 1) * LANES)
            dbuf[0, s] = res[:, (s * LANES - c):(s * LANES - c) + LANES]
            src, d_prev = 0, 1
            for dref, d in zip(d_refs, DIL_DILATIONS):
                if d == 1:
                    dref[0, 0, :, cols] = dbuf[0, s].astype(BF16)
                    continue
                step, rows = d // d_prev, tm // d
                dst = 1 - src
                for rp in range(d_prev):
                    for q in range(step):
                        piece = dbuf[src, s, pl.ds(rp * (tm // d_prev) + q, rows, stride=step), :]
                        r = rp + q * d_prev
                        if d != DIL_DILATIONS[-1]:
                            dbuf[dst, s, r * rows:(r + 1) * rows, :] = piece
                        dref[0, r, :, cols] = piece.astype(BF16)
                src, d_prev = dst, d
    for c in range(0, na, PROJ_TN):
        a_ref[0, :, c:c + PROJ_TN] = _dot(h, w_ref[:, c:c + PROJ_TN]).astype(BF16)


def _proj0(x3, g, w):
    b, s, dm = x3.shape
    n = w.shape[1]
    nd = n - RET_W
    tm = PROJ_TM
    d_specs = [pl.BlockSpec((1, d, tm // d, nd), lambda bi, i: (bi, 0, i, 0)) for d in DIL_DILATIONS]
    d_shapes = [jax.ShapeDtypeStruct((b, d, s // d, nd), BF16) for d in DIL_DILATIONS]
    return pl.pallas_call(
        _proj0_kernel,
        grid=(b, s // tm),
        in_specs=[pl.BlockSpec((1, tm, dm), lambda bi, i: (bi, i, 0)),
                  pl.BlockSpec((1, dm), lambda bi, i: (0, 0)),
                  pl.BlockSpec((dm, n), lambda bi, i: (0, 0))],
        out_specs=[pl.BlockSpec((1, tm, RET_W), lambda bi, i: (bi, i, 0))] + d_specs,
        out_shape=[jax.ShapeDtypeStruct((b, s, RET_W), BF16)] + d_shapes,
        scratch_shapes=[pltpu.VMEM((2, nd // LANES, tm, LANES), F32)],
        compiler_params=_params(("arbitrary", "arbitrary")),
        name="proj0",
    )(x3, g, w)


def _ret_kernel(q_ref, k_ref, v_ref, gate_ref, start_ref, off_ref, tab_ref, gn_ref, o_ref,
                state_ref, qr_ref, kr_ref, qx_ref, kv_ref, st_ref):
    @pl.when(pl.program_id(2) == 0)
    def _():
        state_ref[...] = jnp.zeros_like(state_ref)

    decay = tab_ref[0, 0]
    xi = tab_ref[0, 1]
    zeta = tab_ref[0, 2]
    g_chunk = tab_ref[0, 3]
    gain = gn_ref[...]
    half = RET_DIM // 2
    n_chunks = q_ref.shape[1] // RET_CHUNK
    for c in range(n_chunks):
        rows = pl.ds(c * RET_CHUNK, RET_CHUNK)
        ca, sa = start_ref[c, 0:1, :], start_ref[c, 1:2, :]
        cos = ca * off_ref[0] - sa * off_ref[1]
        sin = sa * off_ref[2] + ca * off_ref[3]
        q = q_ref[0, rows, :].astype(F32)
        k = k_ref[0, rows, :].astype(F32)
        q = q * cos + pltpu.roll(q, half, 1) * sin
        k = k * cos + pltpu.roll(k, half, 1) * sin
        qr_ref[c] = q.astype(BF16)
        kr_ref[c] = k.astype(BF16)
        qx_ref[c] = (q * xi).astype(BF16)
        kv_ref[c] = _dot_tn((k * zeta).astype(BF16), v_ref[0, rows, :])
    state = state_ref[...]
    for c in range(n_chunks):
        st_ref[c] = state.astype(BF16)
        state = g_chunk * state + kv_ref[c]
    state_ref[...] = state
    for c in range(n_chunks):
        rows = pl.ds(c * RET_CHUNK, RET_CHUNK)
        v = v_ref[0, rows, :]
        intra = _dot_nt(qr_ref[c], kr_ref[c]) * decay
        y = _dot(intra.astype(BF16), v) + _dot(qx_ref[c], st_ref[c])
        mu = jnp.mean(y, axis=-1, keepdims=True)
        yc = y - mu
        var = jnp.mean(yc * yc, axis=-1, keepdims=True)
        yn = yc * lax.rsqrt(var + EPS) * gain
        gate = gate_ref[0, rows, :].astype(F32)
        o_ref[0, rows, :] = (gate * jax.nn.sigmoid(gate) * yn).astype(o_ref.dtype)


def _retention_tables(s):
    c, e = RET_CHUNK, RET_DIM
    inv = ROPE_BASE ** (-jnp.arange(0, e, 2, dtype=F32) / e)
    inv = jnp.concatenate([inv, inv])
    sign = jnp.where(jnp.arange(e) < e // 2, -1.0, 1.0).astype(F32)
    a = (jnp.arange(s // c, dtype=F32) * c)[:, None] * inv[None, :]
    b_ = jnp.arange(c, dtype=F32)[:, None] * inv[None, :]
    start = jnp.stack([jnp.cos(a), jnp.sin(a)], axis=1)
    off = jnp.stack([jnp.cos(b_), jnp.sin(b_), jnp.cos(b_) * sign, jnp.sin(b_) * sign])
    log_g = jnp.log1p(-jnp.exp2(-5.0 - jnp.arange(RET_HEADS, dtype=F32)))
    pos = jnp.arange(c, dtype=F32)
    rel = pos[:, None] - pos[None, :]
    causal = rel >= 0
    scale = e ** -0.5
    decay = jnp.where(causal, jnp.exp(jnp.where(causal, rel, 0.0)[None] * log_g[:, None, None]), 0.0)
    zeta = jnp.exp((c - 1 - pos)[None, :] * log_g[:, None])
    xi = jnp.exp((pos + 1)[None, :] * log_g[:, None])
    g_chunk = jnp.exp(c * log_g)
    tab = jnp.stack([
        decay * scale,
        jnp.broadcast_to(xi[:, :, None], (RET_HEADS, c, e)),
        jnp.broadcast_to(zeta[:, :, None] * scale, (RET_HEADS, c, e)),
        jnp.broadcast_to(g_chunk[:, None, None], (RET_HEADS, e, e)),
    ], axis=1)
    return start, off, tab


def _retention(p0, gn, b, s):
    start, off, tab = _retention_tables(s)
    e = RET_DIM
    nc = RET_ROWS // RET_CHUNK
    col = lambda off: (lambda bi, h, t: (bi, t, off + h))
    return pl.pallas_call(
        _ret_kernel,
        grid=(b, RET_HEADS, s // RET_ROWS),
        in_specs=[pl.BlockSpec((1, RET_ROWS, e), col(0)),
                  pl.BlockSpec((1, RET_ROWS, e), col(RET_HEADS)),
                  pl.BlockSpec((1, RET_ROWS, e), col(2 * RET_HEADS)),
                  pl.BlockSpec((1, RET_ROWS, e), col(3 * RET_HEADS)),
                  pl.BlockSpec((nc, 2, e), lambda bi, h, t: (t, 0, 0)),
                  pl.BlockSpec((4, RET_CHUNK, e), lambda bi, h, t: (0, 0, 0)),
                  pl.BlockSpec((1, 4, e, e), lambda bi, h, t: (h, 0, 0, 0)),
                  pl.BlockSpec((1, e), lambda bi, h, t: (0, h))],
        out_specs=pl.BlockSpec((1, RET_ROWS, e), lambda bi, h, t: (bi, t, h)),
        out_shape=jax.ShapeDtypeStruct((b, s, RET_HEADS * e), BF16),
        scratch_shapes=[pltpu.VMEM((e, e), F32),
                        pltpu.VMEM((nc, RET_CHUNK, e), BF16),
                        pltpu.VMEM((nc, RET_CHUNK, e), BF16),
                        pltpu.VMEM((nc, RET_CHUNK, e), BF16),
                        pltpu.VMEM((nc, e, e), F32),
                        pltpu.VMEM((nc, e, e), BF16)],
        compiler_params=_params(("arbitrary", "arbitrary", "arbitrary")),
        name="retention",
    )(p0, p0, p0, p0, start, off, tab, gn)


def _t5_bucket(dist):
    max_exact = REL_BUCKETS // 2
    d = jnp.maximum(dist.astype(F32), 1.0)
    large = max_exact + (jnp.log(d / max_exact) / math.log(REL_MAX_DISTANCE / max_exact)
                         * (REL_BUCKETS - max_exact))
    large = jnp.clip(large.astype(jnp.int32), max_exact, REL_BUCKETS - 1)
    return jnp.where(dist < max_exact, dist, large)


def _dil_kernel(q_ref, kp_ref, kc_ref, vp_ref, vc_ref, g_ref, o_ref, lse_ref, bias_ref,
                s_ref, p_ref):
    first = pl.program_id(2) == 0
    blk, e = DIL_BLOCK, DIL_HEAD_DIM

    @pl.when(jnp.logical_and(jnp.logical_and(pl.program_id(0) == 0, pl.program_id(1) == 0), first))
    def _():
        for h in range(DIL_HEADS):
            gb = jnp.broadcast_to(g_ref[h:h + 1, :], (blk, 4 * blk))
            bias_ref[h] = pltpu.roll(gb, 0, 1, stride=1, stride_axis=0)[:, :2 * blk]

    q_all = q_ref[0, 0] * (e ** -0.5)
    k_all = jnp.concatenate([kp_ref[0, 0], kc_ref[0, 0]], axis=0)
    v_all = jnp.concatenate([vp_ref[0, 0], vc_ref[0, 0]], axis=0)
    in_prev = lax.broadcasted_iota(jnp.int32, (blk, 2 * blk), 1) < blk
    no_prev = jnp.logical_and(first, in_prev)
    lane = lax.broadcasted_iota(jnp.int32, (blk, LANES), 1)
    for qb in range(q_all.shape[0] // blk):
        q = q_all[qb * blk:(qb + 1) * blk]
        kcat = k_all[qb * blk:(qb + 2) * blk]
        vcat = v_all[qb * blk:(qb + 2) * blk]
        lse_all = jnp.zeros((blk, LANES), F32)
        for h in range(DIL_HEADS):
            pair = slice((h // 2) * LANES, (h // 2 + 1) * LANES)
            qp = q[:, pair]
            mine = (lax.broadcasted_iota(jnp.int32, qp.shape, 1) // e) == h % 2
            s = _dot_nt(jnp.where(mine, qp, jnp.zeros_like(qp)), kcat[:, pair]) + bias_ref[h]
            if qb == 0:
                s = jnp.where(no_prev, MASKED, s)
            s_ref[qb, h] = s
        inv_l = []
        for h in range(DIL_HEADS):
            s = s_ref[qb, h]
            m = jnp.max(s, axis=-1, keepdims=True)
            p = jnp.exp(s - m)
            l = jnp.sum(p, axis=-1, keepdims=True)
            p_ref[qb, h] = p.astype(BF16)
            inv_l.append(1.0 / l)
            lse_all = jnp.where(lane == h, m + jnp.log(l), lse_all)
        for g in range(DIL_HEADS // 2):
            pair = slice(g * LANES, (g + 1) * LANES)
            vp = vcat[:, pair]
            upper = lax.broadcasted_iota(jnp.int32, vp.shape, 1) >= e
            zero = jnp.zeros_like(vp)
            o_pair = (_dot(p_ref[qb, 2 * g], jnp.where(upper, zero, vp))
                      + _dot(p_ref[qb, 2 * g + 1], jnp.where(upper, vp, zero)))
            scale = jnp.where(lane >= e, inv_l[2 * g + 1], inv_l[2 * g])
            o_ref[0, 0, qb * blk:(qb + 1) * blk, pair] = (o_pair * scale).astype(o_ref.dtype)
        lse_ref[0, 0, qb * blk:(qb + 1) * blk, :] = lse_all


def _dil_bias_vector(rel_bias, window, dil):
    blk = DIL_BLOCK
    span = window // dil
    r = blk - jnp.arange(4 * blk)
    band = (r >= 0) & (r <= span)
    bias = rel_bias[_t5_bucket(jnp.maximum(r, 0) * dil)].astype(F32)
    return jnp.where(band[:, None], bias, MASKED).T


def _dilated_pattern(dq, rel_bias, window, dil):
    b, d, sd, _ = dq.shape
    blk = DIL_BLOCK
    nq = math.gcd(DIL_QBLOCKS, sd // DIL_BLOCK)
    rows = nq * blk
    assert d == dil and window // dil == blk and sd % rows == 0
    cur = lambda c: (lambda bi, r, n: (bi, r, n, c))
    prev = lambda c: (lambda bi, r, n: (bi, r, jnp.maximum(n * nq - 1, 0), c))
    spec = lambda im: pl.BlockSpec((1, 1, rows, DIL_W), im)
    pspec = lambda im: pl.BlockSpec((1, 1, blk, DIL_W), im)
    return pl.pallas_call(
        _dil_kernel,
        grid=(b, dil, sd // rows),
        in_specs=[spec(cur(0)), pspec(prev(1)), spec(cur(1)), pspec(prev(2)), spec(cur(2)),
                  pl.BlockSpec((DIL_HEADS, 4 * blk), lambda bi, r, n: (0, 0))],
        out_specs=[pl.BlockSpec((1, 1, rows, DIL_W), lambda bi, r, n: (bi, r, n, 0)),
                   pl.BlockSpec((1, 1, rows, LANES), lambda bi, r, n: (bi, r, n, 0))],
        out_shape=[jax.ShapeDtypeStruct((b, dil, sd, DIL_W), BF16),
                   jax.ShapeDtypeStruct((b, dil, sd, LANES), F32)],
        scratch_shapes=[pltpu.VMEM((DIL_HEADS, blk, 2 * blk), F32),
                        pltpu.VMEM((nq, DIL_HEADS, blk, 2 * blk), F32),
                        pltpu.VMEM((nq, DIL_HEADS, blk, 2 * blk), BF16)],
        compiler_params=_params(("arbitrary", "arbitrary", "arbitrary")),
        name=f"dilated_d{dil}",
    )(dq, dq, dq, dq, dq, _dil_bias_vector(rel_bias, window, dil))


def _mix0_kernel(x_ref, ret_ref, *rest):
    np_ = len(DIL_DILATIONS)
    o_refs, l_refs = rest[:np_], rest[np_:2 * np_]
    w_ref, out_ref, obuf, lbuf, dbuf = rest[2 * np_:]
    tm = x_ref.shape[1]
    nslab = DIL_W // LANES
    for pi, d in enumerate(DIL_DILATIONS):
        rows = tm // d
        for r in range(d):
            dst = pl.ds(r, rows, stride=d) if d > 1 else pl.ds(0, tm)
            lbuf[pi, dst, :] = l_refs[pi][0, r]
            for s in range(nslab):
                obuf[pi, s, dst, :] = o_refs[pi][0, r, :, s * LANES:(s + 1) * LANES].astype(F32)
    lses = [lbuf[pi] for pi in range(np_)]
    m = functools.reduce(jnp.maximum, lses)
    es = [jnp.exp(l - m) for l in lses]
    inv = 1.0 / functools.reduce(lambda a, b: a + b, es)
    row = lax.broadcasted_iota(jnp.int32, (LANES, DIL_W), 0)
    col = lax.broadcasted_iota(jnp.int32, (LANES, DIL_W), 1)
    spread = (col // DIL_HEAD_DIM == row).astype(BF16)
    spread2 = jnp.concatenate([spread, spread], axis=0)
    wide = []
    for ex in es:
        wgt = ex * inv
        hi = wgt.astype(BF16)
        lo = (wgt - hi.astype(F32)).astype(BF16)
        wide.append(_dot(jnp.concatenate([hi, lo], axis=1), spread2))
    for s in range(nslab):
        cols = slice(s * LANES, (s + 1) * LANES)
        acc = wide[0][:, cols] * obuf[0, s]
        for pi in range(1, np_):
            acc = acc + wide[pi][:, cols] * obuf[pi, s]
        dbuf[:, cols] = acc.astype(BF16)
    na = ret_ref.shape[-1]
    y = _dot(ret_ref[0], w_ref[:na, :]) + _dot(dbuf[...], w_ref[na:, :])
    out_ref[0] = x_ref[0] + y


def _mix0(x3, ret, outs, lses, w):
    b, s, dm = x3.shape
    na = ret.shape[-1]
    tm = MIX_TM
    o_specs = [pl.BlockSpec((1, d, tm // d, DIL_W), lambda bi, i: (bi, 0, i, 0)) for d in DIL_DILATIONS]
    l_specs = [pl.BlockSpec((1, d, tm // d, LANES), lambda bi, i: (bi, 0, i, 0)) for d in DIL_DILATIONS]
    np_ = len(DIL_DILATIONS)
    return pl.pallas_call(
        _mix0_kernel,
        grid=(b, s // tm),
        in_specs=[pl.BlockSpec((1, tm, dm), lambda bi, i: (bi, i, 0)),
                  pl.BlockSpec((1, tm, na), lambda bi, i: (bi, i, 0))] + o_specs + l_specs
                 + [pl.BlockSpec((na + DIL_W, dm), lambda bi, i: (0, 0))],
        out_specs=pl.BlockSpec((1, tm, dm), lambda bi, i: (bi, i, 0)),
        out_shape=jax.ShapeDtypeStruct((b, s, dm), F32),
        scratch_shapes=[pltpu.VMEM((np_, DIL_W // LANES, tm, LANES), F32),
                        pltpu.VMEM((np_, tm, LANES), F32),
                        pltpu.VMEM((tm, DIL_W), BF16)],
        compiler_params=_params(("arbitrary", "arbitrary")),
        name="mix0",
    )(x3, ret, *outs, *lses, w)


def _ffn_kernel(x_ref, *rest, final_norm, mixer_proj):
    if mixer_proj:
        a_ref, wmix_ref = rest[:2]
        rest = rest[2:]
    g_ref, win_ref, cw_ref, cb_ref, wout_ref, fg_ref, o_ref, ubuf, carry, gbuf = rest
    tm = x_ref.shape[1]
    halo = SUBLANES

    @pl.when(pl.program_id(1) == 0)
    def _():
        carry[...] = jnp.zeros_like(carry)

    x = x_ref[0]
    if mixer_proj:
        x = x + _dot(a_ref[0], wmix_ref[...])
    h = _rms(x, g_ref[...]).astype(BF16)

    def conv(part, col):
        cols = slice(col, col + FFN_CF)
        u = _dot(h, win_ref[:, cols])
        ubuf[part, 0:halo, :] = carry[:, cols]
        ubuf[part, halo:halo + tm, :] = u
        carry[:, cols] = u[tm - halo:, :]
        u1 = ubuf[part, halo - 1:halo - 1 + tm, :]
        u2 = ubuf[part, halo - 2:halo - 2 + tm, :]
        return (u * cw_ref[2:3, cols] + u1 * cw_ref[1:2, cols] + u2 * cw_ref[0:1, cols]
                + cb_ref[:, cols])

    for c in range(D_FF // FFN_CF):
        gate = conv(0, c * FFN_CF)
        up = conv(1, D_FF + c * FFN_CF)
        act = 0.5 * gate * (1.0 + lax.erf(gate * (2.0 ** -0.5)))
        gbuf[:, c * FFN_CF:(c + 1) * FFN_CF] = (act * up).astype(BF16)

    y = x + _dot(gbuf[...], wout_ref[...])
    if final_norm:
        y = _rms(y, fg_ref[...])
    o_ref[0] = y


def _ffn(x3, g, w_in, conv_w, conv_b, w_out, final_gain, final_norm, mixer=None):
    b, s, d = x3.shape
    f2 = w_in.shape[1]
    whole = pl.BlockSpec(memory_space=pltpu.VMEM)
    tile = lambda n: pl.BlockSpec((1, FFN_TM, n), lambda bi, t: (bi, t, 0))
    mix_specs = [tile(mixer[0].shape[-1]), whole] if mixer is not None else []
    mix_args = list(mixer) if mixer is not None else []
    return pl.pallas_call(
        functools.partial(_ffn_kernel, final_norm=final_norm, mixer_proj=mixer is not None),
        grid=(b, s // FFN_TM),
        in_specs=[tile(d)] + mix_specs + [whole, whole, whole, whole, whole, whole],
        out_specs=pl.BlockSpec((1, FFN_TM, d), lambda bi, t: (bi, t, 0)),
        out_shape=jax.ShapeDtypeStruct((b, s, d), F32),
        scratch_shapes=[pltpu.VMEM((2, FFN_TM + SUBLANES, FFN_CF), F32),
                        pltpu.VMEM((SUBLANES, f2), F32),
                        pltpu.VMEM((FFN_TM, D_FF), BF16)],
        compiler_params=_params(("arbitrary", "arbitrary")),
        name="ffn_final" if final_norm else "ffn",
    )(x3, *mix_args, g, w_in, conv_w, conv_b, w_out, final_gain)


def _proj1_kernel(x_ref, g_ref, wq_ref, wk_ref, wvt_ref, q_ref, k_ref, vt_ref):
    h = _rms(x_ref[0], g_ref[...]).astype(BF16)
    n = k_ref.shape[-1]
    for c in range(0, n, PROJ_TN):
        cols = slice(c, c + PROJ_TN)
        q_ref[0, 0, cols, :] = (_dot_nt(wq_ref[cols, :], h) * ATT_QSCALE).astype(BF16)
        k_ref[0, :, cols] = _dot(h, wk_ref[:, cols]).astype(BF16)
        vt_ref[0, 0, cols, :] = _dot_nt(wvt_ref[cols, :], h).astype(BF16)


def _proj1(x3, g, wq, wk, wvt):
    b, s, d = x3.shape
    n = wq.shape[1]
    t = ATT_T
    wspec = pl.BlockSpec((d, n), lambda bi, i: (0, 0))
    return pl.pallas_call(
        _proj1_kernel,
        grid=(b, s // t),
        in_specs=[pl.BlockSpec((1, t, d), lambda bi, i: (bi, i, 0)),
                  pl.BlockSpec((1, d), lambda bi, i: (0, 0)),
                  pl.BlockSpec((n, d), lambda bi, i: (0, 0)), wspec,
                  pl.BlockSpec((n, d), lambda bi, i: (0, 0))],
        out_specs=[pl.BlockSpec((1, 1, n, t), lambda bi, i: (bi, i, 0, 0)),
                   pl.BlockSpec((1, t, n), lambda bi, i: (bi, i, 0)),
                   pl.BlockSpec((1, 1, n, t), lambda bi, i: (bi, i, 0, 0))],
        out_shape=[jax.ShapeDtypeStruct((b, s // t, n, t), BF16),
                   jax.ShapeDtypeStruct((b, s, n), BF16),
                   jax.ShapeDtypeStruct((b, s // t, n, t), BF16)],
        compiler_params=_params(("arbitrary", "arbitrary")),
        name="proj1",
    )(x3, g, wq, wk, wvt)


def _diff_kernel(q_ref, k_ref, vt_ref, f_ref, lam_ref, sub_ref, o_ref,
                 qm_ref, km_ref, m_ref, l_ref, l8_ref, acc_ref, nb_ref, p_ref, lamb_ref, *, lam_init):
    i = pl.program_id(2)
    t = ATT_T
    e = DIFF_HEAD_DIM
    half = lax.broadcasted_iota(jnp.int32, (SUBLANES, 2 * e), 1) // e
    sel = (lax.broadcasted_iota(jnp.int32, (SUBLANES, 2 * e), 0) == half).astype(BF16)

    @pl.when(i == 0)
    def _():
        for o in range(ATT_NEAR):
            g = jnp.broadcast_to(f_ref[0, 0:1, o * t:(o + 2) * t], (t, 2 * t))
            nb_ref[o] = pltpu.roll(g, 0, 1, stride=1, stride_axis=0)[:, t:]
        nb_ref[ATT_NEAR] = jnp.zeros((t, t), F32)
        kmax = jnp.zeros((SUBLANES, t), F32)
        for c in range(k_ref.shape[1] // t):
            kk = k_ref[0, c * t:(c + 1) * t, :].astype(F32)
            kmax = jnp.maximum(kmax, _dot_nt(sel, (kk * kk).astype(BF16)))
        for a in range(2):
            km_ref[a] = jnp.broadcast_to(jnp.max(kmax[a:a + 1, :], axis=1, keepdims=True), (1, t))
        lam = (jnp.exp(jnp.sum(lam_ref[0:1, :] * lam_ref[1:2, :], axis=-1, keepdims=True))
               - jnp.exp(jnp.sum(lam_ref[2:3, :] * lam_ref[3:4, :], axis=-1, keepdims=True))
               + lam_init)
        lamb_ref[...] = jnp.broadcast_to(lam, (1, t))

    qt = q_ref[0, 0]
    row = lax.broadcasted_iota(jnp.int32, qt.shape, 0)
    zero = jnp.zeros_like(qt)
    qm_ref[0] = jnp.where(row < e, qt, zero)
    qm_ref[1] = jnp.where(row >= e, qt, zero)
    qf = qt.astype(F32)
    qq = qf * qf
    qn2 = [jnp.sum(qq[a * e:(a + 1) * e], axis=0, keepdims=True) for a in range(2)]
    bias_max = f_ref[0, 1:2, 0:t]
    n_far = jnp.maximum(i - (ATT_NEAR - 1), 0)

    l8_ref[...] = jnp.zeros_like(l8_ref)
    acc_ref[...] = jnp.zeros_like(acc_ref)
    shift = [jnp.sqrt(qn2[a] * km_ref[a]) * ATT_BOUND_SLACK + bias_max for a in range(2)]

    def finalize(acc, l):
        ot = acc[0] * (1.0 / l[0]) - acc[1] * (lamb_ref[...] / l[1])
        ot = ot * lax.rsqrt(jnp.mean(ot * ot, axis=0, keepdims=True) + EPS)
        o_ref[0] = (ot.T * sub_ref[...] * (1.0 - lam_init)).astype(o_ref.dtype)

    def drain(tile, pair):
        acc, l = [], []
        if tile is not None:
            kb = k_ref[0, pl.ds(pl.multiple_of(tile * t, t), t), :]
        for a in range(2):
            total, l8 = acc_ref[a], l8_ref[a]
            if pair is not None:
                slot = pair % 2
                total = (total + _dot(vt_ref[0, 2 * pair], p_ref[slot, a, 0])
                         + _dot(vt_ref[0, 2 * pair + 1], p_ref[slot, a, 1]))
            if tile is not None:
                s = _dot(kb, qm_ref[a]) + nb_ref[jnp.minimum(i - tile, ATT_NEAR)]
                p = jnp.exp2(s - shift[a])
                l8 = l8 + jnp.sum(p.reshape(t // SUBLANES, SUBLANES, t), axis=0)
                total = total + _dot(vt_ref[0, tile], p.astype(BF16))
            if tile is not None and pair is not None:
                acc_ref[a] = total
                total = acc_ref[a]
            acc.append(total)
            l.append(jnp.sum(l8, axis=0, keepdims=True))
            l_ref[a] = l[a]
        finalize(acc, l)

    def probs(u, biased):
        slot = u % 2
        for a in range(2):
            l8 = None
            for tt in range(2):
                j = 2 * u + tt
                kb = k_ref[0, pl.ds(pl.multiple_of(j * t, t), t), :]
                s = _dot(kb, qm_ref[a])
                if biased:
                    s = s + nb_ref[jnp.minimum(i - j, ATT_NEAR)]
                p = jnp.exp2(s - shift[a])
                part = jnp.sum(p.reshape(t // SUBLANES, SUBLANES, t), axis=0)
                l8 = part if l8 is None else l8 + part
                p_ref[slot, a, tt] = p.astype(BF16)
            l8_ref[a] += l8

    def values(u):
        slot = u % 2
        for a in range(2):
            acc_ref[a] += (_dot(vt_ref[0, 2 * u], p_ref[slot, a, 0])
                           + _dot(vt_ref[0, 2 * u + 1], p_ref[slot, a, 1]))

    def far_step(u, carry):
        values(u - 1)
        probs(u, False)
        return carry

    def near_step(u, carry):
        values(u - 1)
        probs(u, True)
        return carry

    n = i + 1
    pairs = n // 2
    far_pairs = n_far // 2

    @pl.when(far_pairs > 0)
    def _():
        probs(0, False)

    @pl.when(jnp.logical_and(far_pairs == 0, pairs > 0))
    def _():
        probs(0, True)

    lax.fori_loop(1, far_pairs, far_step, 0)
    lax.fori_loop(jnp.maximum(far_pairs, 1), pairs, near_step, 0)

    odd = n % 2 == 1

    @pl.when(jnp.logical_and(pairs > 0, jnp.logical_not(odd)))
    def _():
        drain(None, pairs - 1)

    @pl.when(jnp.logical_and(pairs > 0, odd))
    def _():
        drain(n - 1, pairs - 1)

    @pl.when(pairs == 0)
    def _():
        drain(0, None)

    l_min = jnp.min(jnp.minimum(l_ref[0], l_ref[1]))
    l_max = jnp.max(jnp.maximum(l_ref[0], l_ref[1]))
    in_range = jnp.logical_and(l_min >= ATT_MIN_DENOM, l_max <= ATT_MAX_DENOM)

    @pl.when(jnp.logical_not(in_range))
    def _():
        m_ref[...] = jnp.full_like(m_ref, MASKED)
        l_ref[...] = jnp.zeros_like(l_ref)
        acc_ref[...] = jnp.zeros_like(acc_ref)

        def online_tile(j, bias):
            kb = k_ref[0, pl.ds(pl.multiple_of(j * t, t), t), :]
            vtb = vt_ref[0, j]
            for a in range(2):
                s = _dot(kb, qm_ref[a])
                if bias is not None:
                    s = s + bias
                m_old = m_ref[a]
                m_new = jnp.maximum(m_old, jnp.max(s, axis=0, keepdims=True))
                alpha = jnp.exp2(m_old - m_new)
                p = jnp.exp2(s - m_new)
                l_ref[a] = alpha * l_ref[a] + jnp.sum(p, axis=0, keepdims=True)
                acc_ref[a] = alpha * acc_ref[a] + _dot(vtb, p.astype(BF16))
                m_ref[a] = m_new

        def online_far(j, carry):
            online_tile(j, None)
            return carry

        def online_near(j, carry):
            online_tile(j, nb_ref[i - j])
            return carry

        lax.fori_loop(0, n_far, online_far, 0)
        lax.fori_loop(n_far, i + 1, online_near, 0)
        finalize([acc_ref[0], acc_ref[1]], [l_ref[0], l_ref[1]])


def _diff_bias(rel_bias, s):
    t = ATT_T
    n = ATT_NEAR * t
    assert n >= BIAS_FLAT_FROM + t - 1 and s >= n
    dist_bias = rel_bias[_t5_bucket(jnp.arange(n))].astype(F32)
    far = rel_bias[REL_BUCKETS - 1].astype(F32)
    near = (dist_bias - far) * LOG2E
    f = jnp.concatenate([jnp.full((t, far.shape[0]), MASKED, F32), near])
    top = jnp.broadcast_to(jnp.maximum(jnp.max(near, axis=0), 0.0), f.shape)
    return jnp.stack([f.T, top.T], axis=1)


def _diff_attention(q, k, vt, rel_bias, lam4, subln, layer_idx):
    b, s, n = k.shape
    t = ATT_T
    w = 2 * DIFF_HEAD_DIM
    lam_init = 0.8 - 0.6 * math.exp(-0.3 * layer_idx)
    return pl.pallas_call(
        functools.partial(_diff_kernel, lam_init=lam_init),
        grid=(b, DIFF_HEADS, s // t),
        in_specs=[pl.BlockSpec((1, 1, w, t), lambda bi, h, i: (bi, i, h, 0)),
                  pl.BlockSpec((1, s, w), lambda bi, h, i: (bi, 0, h)),
                  pl.BlockSpec((1, s // t, w, t), lambda bi, h, i: (bi, 0, h, 0)),
                  pl.BlockSpec((1, 2, (ATT_NEAR + 1) * t), lambda bi, h, i: (h, 0, 0)),
                  pl.BlockSpec((4, DIFF_HEAD_DIM), lambda bi, h, i: (0, 0)),
                  pl.BlockSpec((1, w), lambda bi, h, i: (0, 0))],
        out_specs=pl.BlockSpec((1, t, w), lambda bi, h, i: (bi, i, h)),
        out_shape=jax.ShapeDtypeStruct((b, s, n), BF16),
        scratch_shapes=[pltpu.VMEM((2, w, t), BF16),
                        pltpu.VMEM((2, 1, t), F32),
                        pltpu.VMEM((2, 1, t), F32),
                        pltpu.VMEM((2, 1, t), F32),
                        pltpu.VMEM((2, SUBLANES, t), F32),
                        pltpu.VMEM((2, w, t), F32),
                        pltpu.VMEM((ATT_NEAR + 1, t, t), F32),
                        pltpu.VMEM((2, 2, 2, t, t), BF16),
                        pltpu.VMEM((1, t), F32)],
        compiler_params=_params(("arbitrary", "arbitrary", "arbitrary")),
        name="diff_attention",
    )(q, k, vt, _diff_bias(rel_bias, s), lam4, subln)


def kernel(x, rel_bias, norm_mix, norm_ffn, norm_final, w_in_ab, ret_gn, w_out_ab, w_in_c,
           lam_q1, lam_k1, lam_q2, lam_k2, diff_subln, w_out_c, w_ffn_in, conv_w, conv_b,
           w_ffn_out):
    b, s, d = x.shape
    depth = norm_mix.shape[0]
    assert depth == 2 and s >= ATT_NEAR * ATT_T and conv_w.shape[1] == CONV_WIDTH == 3
    assert all(s % n == 0 for n in (RET_ROWS, FFN_TM, PROJ_TM, MIX_TM, ATT_T, max(DIL_DILATIONS) * DIL_BLOCK))
    assert DIL_DILATIONS[0] == 1 and all(hi % lo == 0 for lo, hi in zip(DIL_DILATIONS, DIL_DILATIONS[1:]))
    row = lambda v: v.reshape(1, -1)

    proj = _proj0(x, row(norm_mix[0]), w_in_ab[0].astype(BF16))
    ret = _retention(proj[0], row(ret_gn[0]), b, s)
    outs, lses = zip(*[_dilated_pattern(dq, rel_bias, wdw, dil)
                       for dq, (wdw, dil) in zip(proj[1:], DIL_PATTERNS)])
    x3 = _mix0(x, ret, outs, lses, w_out_ab[0].astype(BF16))
    x3 = _ffn(x3, row(norm_ffn[0]), w_ffn_in[0].astype(BF16), conv_w[0],
              row(conv_b[0]), w_ffn_out[0].astype(BF16), row(norm_final), False)

    nqk = DIFF_HEADS * 2 * DIFF_HEAD_DIM
    wc = w_in_c[0].astype(BF16)
    q, k, vt = _proj1(x3, row(norm_mix[1]), wc[:, :nqk].T, wc[:, nqk:2 * nqk], wc[:, 2 * nqk:].T)
    lam4 = jnp.stack([lam_q1[0], lam_k1[0], lam_q2[0], lam_k2[0]]).astype(F32)
    a = _diff_attention(q, k, vt, rel_bias, lam4, row(diff_subln[0]), 1)
    return _ffn(x3, row(norm_ffn[1]), w_ffn_in[1].astype(BF16), conv_w[1], row(conv_b[1]),
                w_ffn_out[1].astype(BF16), row(norm_final), True,
                mixer=(a, w_out_c[0].astype(BF16)))
```
